```python
import math
import jax, jax.numpy as jnp
from jax import lax
import numpy as np

D_MODEL = 1024
BATCH = 8
SEQ = 2048
DEPTH = 1

MIX_WIDTH = D_MODEL
DIFF_WIDTH = MIX_WIDTH // 2
RWKV_WIDTH = MIX_WIDTH - DIFF_WIDTH
DIFF_HEAD_DIM = 64
DIFF_V_DIM = 2 * DIFF_HEAD_DIM
DIFF_HEADS = DIFF_WIDTH // DIFF_V_DIM
ROT_DIM = DIFF_HEAD_DIM // 4
ROPE_THETA = 500000.0
Q_BLOCK = 128
RWKV_HEAD_SIZE = 64
RWKV_HEADS = RWKV_WIDTH // RWKV_HEAD_SIZE
DECAY_LORA = max(32, int(round(1.8 * RWKV_WIDTH ** 0.5 / 32)) * 32)
AAA_LORA = max(32, int(round(1.8 * RWKV_WIDTH ** 0.5 / 32)) * 32)
GATE_LORA = max(32, int(round(0.6 * RWKV_WIDTH ** 0.8 / 32)) * 32)
RWKV_COLS = 3 * RWKV_WIDTH + DECAY_LORA + AAA_LORA + GATE_LORA
IN_COLS = 3 * DIFF_WIDTH + RWKV_COLS
N_EXPERTS = 32
TOP_K = 4
D_FF = D_MODEL
SWIGLU_LIMIT = 7.0
SWIGLU_ALPHA = 1.702
MOE_BLOCK = 512
NORM_EPS = 1e-5
RWKV_GN_EPS = 64e-5

kernel_name = 'hymba_style_diffattn_rwkv7_moe'


def rms_norm(x, w):
    xf = x.astype(jnp.float32)
    y = xf * lax.rsqrt(jnp.mean(xf * xf, axis=-1, keepdims=True) + NORM_EPS)
    return (y * w.astype(jnp.float32)).astype(x.dtype)


def rotary_tables(positions):
    inv_freq = ROPE_THETA ** (-jnp.arange(0, ROT_DIM, 2, dtype=jnp.float32) / ROT_DIM)
    ang = positions.astype(jnp.float32)[..., None] * inv_freq
    return jnp.cos(ang), jnp.sin(ang)


def apply_partial_rotary(t, cos, sin):
    half = ROT_DIM // 2
    c = cos[:, :, None, None, :].astype(t.dtype)
    s = sin[:, :, None, None, :].astype(t.dtype)
    x1 = t[..., :half]
    x2 = t[..., half:ROT_DIM]
    return jnp.concatenate([x1 * c - x2 * s, x2 * c + x1 * s, t[..., ROT_DIM:]], axis=-1)


def diff_attention(q, k, v, lam, subln_w, lambda_init):
    B, S, H = q.shape[0], q.shape[1], q.shape[2]
    nb = S // Q_BLOCK
    scale = DIFF_HEAD_DIM ** -0.5
    qb = q.reshape(B, nb, Q_BLOCK, H, 2, DIFF_HEAD_DIM).transpose(1, 0, 2, 3, 4, 5)
    key_pos = jnp.arange(S)

    def block(args):
        q_blk, i = args
        s = jnp.einsum('bqhmd,bkhmd->bhmqk', q_blk, k).astype(jnp.float32) * scale
        q_pos = i * Q_BLOCK + jnp.arange(Q_BLOCK)
        mask = key_pos[None, :] <= q_pos[:, None]
        s = jnp.where(mask, s, -jnp.inf)
        p = jax.nn.softmax(s, axis=-1)
        a = p[:, :, 0] - lam * p[:, :, 1]
        return jnp.einsum('bhqk,bkhd->bqhd', a.astype(v.dtype), v)

    o = lax.map(block, (qb, jnp.arange(nb)))
    o = o.transpose(1, 0, 2, 3, 4).reshape(B, S, H, DIFF_V_DIM)
    o = rms_norm(o, subln_w) * (1.0 - lambda_init)
    return o.reshape(B, S, H * DIFF_V_DIM)


def rwkv7_time_mix(z, mu, w0, w_up, a0, a_up, g_up, k_k, k_a, r_k, ln_w, ln_b):
    B, S = z.shape[0], z.shape[1]
    H, N, W = RWKV_HEADS, RWKV_HEAD_SIZE, RWKV_WIDTH
    zf = z.astype(jnp.float32)
    shifted = jnp.pad(zf[:, :-1], ((0, 0), (1, 0), (0, 0)))
    zf = zf + mu.astype(jnp.float32) * (shifted - zf)
    r = zf[..., :W]
    k = zf[..., W:2 * W]
    v = zf[..., 2 * W:3 * W]
    o = 3 * W
    wd = zf[..., o:o + DECAY_LORA]
    ad = zf[..., o + DECAY_LORA:o + DECAY_LORA + AAA_LORA]
    gd = zf[..., o + DECAY_LORA + AAA_LORA:]
    f32 = lambda t: t.astype(jnp.float32)
    w = -jax.nn.softplus(-(f32(w0) + jnp.tanh(wd) @ f32(w_up))) - 0.5
    decay = jnp.exp(-jnp.exp(w))
    a = jax.nn.sigmoid(f32(a0) + ad @ f32(a_up))
    g = jax.nn.sigmoid(gd) @ f32(g_up)
    heads = lambda t: t.reshape(B, S, H, N)
    kk = heads(k * f32(k_k))
    kk = kk / jnp.maximum(jnp.linalg.norm(kk, axis=-1, keepdims=True), 1e-12)
    k = k * (1.0 + (a - 1.0) * f32(k_a))
    r, k, v, decay, a = heads(r), heads(k), heads(v), heads(decay), heads(a)

    def step(state, inp):
        r_t, w_t, k_t, v_t, kk_t, a_t = inp
        sa = jnp.einsum('bhij,bhj->bhi', state, -kk_t)
        state = (state * w_t[:, :, None, :]
                 + sa[..., None] * (kk_t * a_t)[:, :, None, :]
                 + v_t[..., None] * k_t[:, :, None, :])
        y_t = jnp.einsum('bhij,bhj->bhi', state, r_t)
        return state, y_t

    tm = lambda t: t.transpose(1, 0, 2, 3)
    state0 = jnp.zeros((B, H, N, N), jnp.float32)
    _, y = lax.scan(step, state0, (tm(r), tm(decay), tm(k), tm(v), tm(kk), tm(a)))
    y = y.transpose(1, 0, 2, 3)
    mean = jnp.mean(y, axis=-1, keepdims=True)
    var = jnp.var(y, axis=-1, keepdims=True)
    yn = ((y - mean) * lax.rsqrt(var + RWKV_GN_EPS)).reshape(B, S, W)
    yn = yn * f32(ln_w) + f32(ln_b)
    bonus = jnp.sum(r * k * f32(r_k), axis=-1, keepdims=True) * v
    out = (yn + bonus.reshape(B, S, W)) * g
    return out.astype(z.dtype)


def clamped_swiglu(h):
    x_glu = jnp.minimum(h[..., ::2], SWIGLU_LIMIT)
    x_lin = jnp.clip(h[..., 1::2], -SWIGLU_LIMIT, SWIGLU_LIMIT)
    return x_glu * jax.nn.sigmoid(SWIGLU_ALPHA * x_glu) * (x_lin + 1.0)


def moe_ffn(h, router_w, router_b, w1, b1, w2, b2):
    B, S, D = h.shape
    T = B * S
    x = h.reshape(T, D)
    logits = (x @ router_w + router_b).astype(jnp.float32)
    top_vals, top_idx = lax.top_k(logits, TOP_K)
    gates = jax.nn.softmax(top_vals, axis=-1)
    n_assign = T * TOP_K
    flat_e = top_idx.reshape(-1)
    flat_tok = jnp.arange(n_assign) // TOP_K
    order = jnp.argsort(flat_e)
    e_sorted = flat_e[order]
    tok_sorted = flat_tok[order]
    gate_sorted = gates.reshape(-1)[order]
    counts = jnp.bincount(flat_e, length=N_EXPERTS)
    starts = jnp.cumsum(counts) - counts
    padded = (counts + MOE_BLOCK - 1) // MOE_BLOCK * MOE_BLOCK
    pad_ends = jnp.cumsum(padded)
    pad_starts = pad_ends - padded
    dest = pad_starts[e_sorted] + (jnp.arange(n_assign) - starts[e_sorted])
    n_blocks = -(-(n_assign + N_EXPERTS * (MOE_BLOCK - 1)) // MOE_BLOCK)
    n_rows = n_blocks * MOE_BLOCK
    x_buf = jnp.zeros((n_rows, D), x.dtype).at[dest].set(x[tok_sorted])
    block_e = jnp.minimum(jnp.searchsorted(pad_ends, jnp.arange(n_blocks) * MOE_BLOCK, side='right'),
                          N_EXPERTS - 1)

    def expert_block(args):
        xb, e = args
        hid = xb @ w1[e] + b1[e]
        return clamped_swiglu(hid) @ w2[e] + b2[e]

    y_buf = lax.map(expert_block, (x_buf.reshape(n_blocks, MOE_BLOCK, D), block_e)).reshape(n_rows, D)
    y = y_buf[dest] * gate_sorted[:, None].astype(y_buf.dtype)
    out = jax.ops.segment_sum(y, tok_sorted, num_segments=T)
    return out.reshape(B, S, D)


def setup_inputs(seed: int = 0) -> dict:
    key = jax.random.key(seed)
    ks = jax.random.split(key, 32)
    nrm = lambda k, shape, scale: jax.random.normal(k, shape, jnp.float32) * scale
    L, D, E = DEPTH, D_MODEL, N_EXPERTS
    offsets = jax.random.randint(ks[1], (BATCH, 1), 0, 4096, dtype=jnp.int32)
    positions = offsets + jnp.arange(SEQ, dtype=jnp.int32)[None, :]
    return {
        'x': nrm(ks[0], (BATCH, SEQ, D), 1.0),
        'positions': positions,
        'attn_norm_w': 1.0 + nrm(ks[2], (L, D), 0.02),
        'w_in': nrm(ks[3], (L, D, IN_COLS), D ** -0.5),
        'diff_lambda_q1': nrm(ks[4], (L, DIFF_HEAD_DIM), 0.1),
        'diff_lambda_k1': nrm(ks[5], (L, DIFF_HEAD_DIM), 0.1),
        'diff_lambda_q2': nrm(ks[6], (L, DIFF_HEAD_DIM), 0.1),
        'diff_lambda_k2': nrm(ks[7], (L, DIFF_HEAD_DIM), 0.1),
        'diff_subln_w': 1.0 + nrm(ks[8], (L, DIFF_V_DIM), 0.02),
        'rwkv_mu': jax.random.uniform(ks[9], (L, RWKV_COLS), jnp.float32),
        'rwkv_w0': jax.random.uniform(ks[10], (L, RWKV_WIDTH), jnp.float32, -6.5, -0.5),
        'rwkv_w_up': nrm(ks[11], (L, DECAY_LORA, RWKV_WIDTH), 0.1 * DECAY_LORA ** -0.5),
        'rwkv_a0': nrm(ks[12], (L, RWKV_WIDTH), 0.1),
        'rwkv_a_up': nrm(ks[13], (L, AAA_LORA, RWKV_WIDTH), 0.1 * AAA_LORA ** -0.5),
        'rwkv_g_up': nrm(ks[14], (L, GATE_LORA, RWKV_WIDTH), GATE_LORA ** -0.5),
        'rwkv_k_k': 0.85 + nrm(ks[15], (L, RWKV_WIDTH), 0.02),
        'rwkv_k_a': 1.0 + nrm(ks[16], (L, RWKV_WIDTH), 0.02),
        'rwkv_r_k': nrm(ks[17], (L, RWKV_HEADS, RWKV_HEAD_SIZE), 0.1),
        'rwkv_ln_w': 1.0 + nrm(ks[18], (L, RWKV_WIDTH), 0.02),
        'rwkv_ln_b': nrm(ks[19], (L, RWKV_WIDTH), 0.01),
        'w_out': nrm(ks[20], (L, MIX_WIDTH, D), MIX_WIDTH ** -0.5),
        'ffn_norm_w': 1.0 + nrm(ks[21], (L, D), 0.02),
        'router_w': nrm(ks[22], (L, D, E), D ** -0.5),
        'router_b': nrm(ks[23], (L, E), 0.01),
        'exp_w1': nrm(ks[24], (L, E, D, 2 * D_FF), D ** -0.5),
        'exp_b1': nrm(ks[25], (L, E, 2 * D_FF), 0.01),
        'exp_w2': nrm(ks[26], (L, E, D_FF, D), D_FF ** -0.5),
        'exp_b2': nrm(ks[27], (L, E, D), 0.01),
        'final_norm_w': 1.0 + nrm(ks[28], (D,), 0.02),
    }


def reference(x, positions, attn_norm_w, w_in, diff_lambda_q1, diff_lambda_k1, diff_lambda_q2,
              diff_lambda_k2, diff_subln_w, rwkv_mu, rwkv_w0, rwkv_w_up, rwkv_a0, rwkv_a_up,
              rwkv_g_up, rwkv_k_k, rwkv_k_a, rwkv_r_k, rwkv_ln_w, rwkv_ln_b, w_out, ffn_norm_w,
              router_w, router_b, exp_w1, exp_b1, exp_w2, exp_b2, final_norm_w):
    B, S = x.shape[0], x.shape[1]
    cos, sin = rotary_tables(positions)
    h = x
    for l in range(DEPTH):
        lambda_init = 0.8 - 0.6 * math.exp(-0.3 * l)
        u = rms_norm(h, attn_norm_w[l])
        z = u @ w_in[l]
        q = z[..., :DIFF_WIDTH].reshape(B, S, DIFF_HEADS, 2, DIFF_HEAD_DIM)
        k = z[..., DIFF_WIDTH:2 * DIFF_WIDTH].reshape(B, S, DIFF_HEADS, 2, DIFF_HEAD_DIM)
        v = z[..., 2 * DIFF_WIDTH:3 * DIFF_WIDTH].reshape(B, S, DIFF_HEADS, DIFF_V_DIM)
        q = apply_partial_rotary(q, cos, sin)
        k = apply_partial_rotary(k, cos, sin)
        lq1, lk1 = diff_lambda_q1[l].astype(jnp.float32), diff_lambda_k1[l].astype(jnp.float32)
        lq2, lk2 = diff_lambda_q2[l].astype(jnp.float32), diff_lambda_k2[l].astype(jnp.float32)
        lam = jnp.exp(jnp.sum(lq1 * lk1)) - jnp.exp(jnp.sum(lq2 * lk2)) + lambda_init
        o_diff = diff_attention(q, k, v, lam, diff_subln_w[l], lambda_init)
        o_rwkv = rwkv7_time_mix(z[..., 3 * DIFF_WIDTH:], rwkv_mu[l], rwkv_w0[l], rwkv_w_up[l],
                                rwkv_a0[l], rwkv_a_up[l], rwkv_g_up[l], rwkv_k_k[l], rwkv_k_a[l],
                                rwkv_r_k[l], rwkv_ln_w[l], rwkv_ln_b[l])
        h = h + jnp.concatenate([o_diff, o_rwkv], axis=-1) @ w_out[l]
        h = h + moe_ffn(rms_norm(h, ffn_norm_w[l]), router_w[l], router_b[l],
                        exp_w1[l], exp_b1[l], exp_w2[l], exp_b2[l])
    return rms_norm(h, final_norm_w)
```

```python
import functools
import math

import jax
import jax.numpy as jnp
import numpy as np
from jax import lax
from jax.experimental import pallas as pl
from jax.experimental.pallas import tpu as pltpu

F32 = jnp.float32
BF16 = jnp.bfloat16
I32 = jnp.int32

DIFF_HEAD_DIM = 64
DIFF_V_DIM = 128
DIFF_HEADS = 4
DIFF_WIDTH = DIFF_HEADS * DIFF_V_DIM
ROT_DIM = 16
ROPE_THETA = 500000.0
RWKV_HEAD = 64
RWKV_HEADS = 8
RWKV_WIDTH = RWKV_HEAD * RWKV_HEADS
DECAY_LORA = 32
AAA_LORA = 32
GATE_LORA = 96
N_EXPERTS = 32
TOP_K = 4
SWIGLU_LIMIT = 7.0
SWIGLU_ALPHA = 1.702
NORM_EPS = 1e-5
RWKV_GN_EPS = 64e-5

LANES = 128
SUBLANES = 8
VMEM_LIMIT = 56 * 1024 * 1024

ROW_TILE = 512
ATTN_TILE = 256
CHUNK = 64
SCAN_TILE = 512
GROUP = 4 * RWKV_HEAD
EXPERT_BLOCK = 256
DISPATCH_TILE = 256
COMBINE_TILE = 128
LORA_PAD = LANES
ZR_COLS = 3 * RWKV_WIDTH + 3 * LORA_PAD


def _cparams(sem):
    return pltpu.CompilerParams(dimension_semantics=sem, vmem_limit_bytes=VMEM_LIMIT)


def _nt(a, b):
    return lax.dot_general(a, b, (((1,), (1,)), ((), ())), preferred_element_type=F32)


def _tn(a, b):
    return lax.dot_general(a, b, (((0,), (0,)), ((), ())), preferred_element_type=F32)


def _dot(a, b):
    return jnp.dot(a, b, preferred_element_type=F32)


def _split_dot(x, w_bf16):
    hi = x.astype(BF16)
    lo = (x - hi.astype(F32)).astype(BF16)
    return _dot(hi, w_bf16) + _dot(lo, w_bf16)


def _inproj_kernel(x_ref, nw_ref, wq_ref, wr_ref, c_ref, sa_ref, sb_ref,
                   q_ref, k_ref, v_ref, zr_ref):
    x = x_ref[...]
    ms = jnp.mean(x * x, axis=-1, keepdims=True)
    u = (x * lax.rsqrt(ms + NORM_EPS) * nw_ref[...]).astype(BF16)
    zq = _dot(u, wq_ref[...])
    c = c_ref[...]
    sa = sa_ref[...]
    sb = sb_ref[...]
    scale = DIFF_HEAD_DIM ** -0.5
    for g in range(2 * DIFF_HEADS):
        zg = zq[:, g * LANES:(g + 1) * LANES]
        rot = zg * c + pltpu.roll(zg, LANES - ROT_DIM // 2, 1) * sa + pltpu.roll(zg, ROT_DIM // 2, 1) * sb
        if g < DIFF_HEADS:
            q_ref[:, g * LANES:(g + 1) * LANES] = (rot * scale).astype(BF16)
        else:
            h = g - DIFF_HEADS
            k_ref[:, h * LANES:(h + 1) * LANES] = rot.astype(BF16)
    v_ref[...] = zq[:, 2 * DIFF_WIDTH:3 * DIFF_WIDTH].astype(BF16)
    zr_ref[...] = _dot(u, wr_ref[...])


def _inproj(x2, nw, wq, wr, ctab, satab, sbtab):
    t, d = x2.shape
    tm = min(ROW_TILE, t)
    row = lambda i: (i, 0)
    fixed = lambda i: (0, 0)
    return pl.pallas_call(
        _inproj_kernel,
        grid=(t // tm,),
        in_specs=[
            pl.BlockSpec((tm, d), row),
            pl.BlockSpec((1, d), fixed),
            pl.BlockSpec(wq.shape, fixed),
            pl.BlockSpec(wr.shape, fixed),
            pl.BlockSpec((tm, LANES), row),
            pl.BlockSpec((tm, LANES), row),
            pl.BlockSpec((tm, LANES), row),
        ],
        out_specs=[
            pl.BlockSpec((tm, DIFF_WIDTH), row),
            pl.BlockSpec((tm, DIFF_WIDTH), row),
            pl.BlockSpec((tm, DIFF_WIDTH), row),
            pl.BlockSpec((tm, ZR_COLS), row),
        ],
        out_shape=[
            jax.ShapeDtypeStruct((t, DIFF_WIDTH), BF16),
            jax.ShapeDtypeStruct((t, DIFF_WIDTH), BF16),
            jax.ShapeDtypeStruct((t, DIFF_WIDTH), BF16),
            jax.ShapeDtypeStruct((t, ZR_COLS), F32),
        ],
        compiler_params=_cparams(("parallel",)),
        name="inproj",
    )(x2, nw, wq, wr, ctab, satab, sbtab)


def _attn_kernel(q_ref, k_ref, v_ref, lq1_ref, lk1_ref, lq2_ref, lk2_ref, sw_ref, o_ref,
                 *, tile, lambda_init):
    qi = pl.program_id(2)
    q = q_ref[0]
    lane = lax.broadcasted_iota(I32, (1, LANES), 1)
    first = lane < DIFF_HEAD_DIM
    zero = jnp.zeros_like(q)
    q1 = jnp.where(first, q, zero)
    q2 = jnp.where(first, zero, q)
    neg = -1e30

    def step(j, carry, masked):
        m1, l1, a1, m2, l2, a2 = carry
        kb = k_ref[0, pl.ds(j * tile, tile), :]
        vb = v_ref[0, pl.ds(j * tile, tile), :]
        s1 = _nt(q1, kb)
        s2 = _nt(q2, kb)
        if masked:
            r = lax.broadcasted_iota(I32, (tile, tile), 0)
            cidx = lax.broadcasted_iota(I32, (tile, tile), 1)
            keep = cidx <= r
            s1 = jnp.where(keep, s1, neg)
            s2 = jnp.where(keep, s2, neg)
        out = []
        for s, m, l, a in ((s1, m1, l1, a1), (s2, m2, l2, a2)):
            mn = jnp.maximum(m, jnp.max(s, axis=-1, keepdims=True))
            alpha = jnp.exp(m - mn)
            p = jnp.exp(s - mn)
            ln = alpha * l + jnp.sum(p, axis=-1, keepdims=True)
            an = alpha * a + _dot(p.astype(BF16), vb)
            out += [mn, ln, an]
        return tuple(out)

    init = (jnp.full((tile, 1), neg, F32), jnp.zeros((tile, 1), F32), jnp.zeros((tile, LANES), F32)) * 2
    carry = lax.fori_loop(0, qi, lambda j, c: step(j, c, False), init)
    m1, l1, a1, m2, l2, a2 = step(qi, carry, True)
    lam = (jnp.exp(jnp.sum(lq1_ref[...] * lk1_ref[...], axis=-1, keepdims=True))
           - jnp.exp(jnp.sum(lq2_ref[...] * lk2_ref[...], axis=-1, keepdims=True)) + lambda_init)
    o = a1 / l1 - lam * (a2 / l2)
    ms = jnp.mean(o * o, axis=-1, keepdims=True)
    o = o * lax.rsqrt(ms + NORM_EPS) * sw_ref[...] * (1.0 - lambda_init)
    o_ref[0] = o.astype(o_ref.dtype)


def _attn(q3, k3, v3, lq1, lk1, lq2, lk2, sw, lambda_init):
    b, s, _ = q3.shape
    tile = min(ATTN_TILE, s)
    qspec = pl.BlockSpec((1, tile, LANES), lambda bi, h, i: (bi, i, h))
    kvspec = pl.BlockSpec((1, s, LANES), lambda bi, h, i: (bi, 0, h))
    vec = lambda n: pl.BlockSpec((1, n), lambda bi, h, i: (0, 0))
    return pl.pallas_call(
        functools.partial(_attn_kernel, tile=tile, lambda_init=lambda_init),
        grid=(b, DIFF_HEADS, s // tile),
        in_specs=[qspec, kvspec, kvspec, vec(DIFF_HEAD_DIM), vec(DIFF_HEAD_DIM), vec(DIFF_HEAD_DIM),
                  vec(DIFF_HEAD_DIM), vec(DIFF_V_DIM)],
        out_specs=qspec,
        out_shape=jax.ShapeDtypeStruct((b, s, DIFF_WIDTH), BF16),
        compiler_params=_cparams(("parallel", "parallel", "parallel")),
        name="attn",
    )(q3, k3, v3, lq1, lk1, lq2, lk2, sw)


def _prep_kernel(z_ref, zp_ref, mu_ref, w0_ref, a0_ref, kk_ref, ka_ref, rk_ref,
                 wup_ref, aup_ref, gup_ref, ones_ref,
                 r_ref, lw_ref, k_ref, v_ref, kkn_ref, b_ref, g_ref, bonus_ref, *, tiles_per_seq):
    i = pl.program_id(0)
    z = z_ref[...]
    tm = z.shape[0]
    rows = lax.broadcasted_iota(I32, (tm, 1), 0)
    prev = zp_ref[SUBLANES - 1:SUBLANES, :]
    prev = jnp.where(i % tiles_per_seq == 0, jnp.zeros_like(prev), prev)
    shifted = jnp.where(rows == 0, prev, pltpu.roll(z, 1, 0))
    zf = z + mu_ref[...] * (shifted - z)
    w = RWKV_WIDTH
    r = zf[:, 0:w]
    k = zf[:, w:2 * w]
    v = zf[:, 2 * w:3 * w]
    wd = zf[:, 3 * w:3 * w + LORA_PAD]
    ad = zf[:, 3 * w + LORA_PAD:3 * w + 2 * LORA_PAD]
    gd = zf[:, 3 * w + 2 * LORA_PAD:3 * w + 3 * LORA_PAD]
    hp = lax.Precision.HIGHEST
    pre = w0_ref[...] + jnp.dot(jnp.tanh(wd), wup_ref[...], precision=hp, preferred_element_type=F32)
    neg = -pre
    softplus = jnp.maximum(neg, 0.0) + jnp.log(1.0 + jnp.exp(-jnp.abs(neg)))
    wlog = -softplus - 0.5
    lw_ref[...] = -jnp.exp(wlog)
    a = jax.nn.sigmoid(a0_ref[...] + jnp.dot(ad, aup_ref[...], precision=hp, preferred_element_type=F32))
    g_ref[...] = jnp.dot(jax.nn.sigmoid(gd), gup_ref[...], precision=hp, preferred_element_type=F32)
    ones = ones_ref[...]
    kk = k * kk_ref[...]
    norm = jnp.sqrt(_split_dot(kk * kk, ones))
    kk = kk / jnp.maximum(norm, 1e-12)
    k = k * (1.0 + (a - 1.0) * ka_ref[...])
    r_ref[...] = r
    k_ref[...] = k
    v_ref[...] = v
    kkn_ref[...] = kk
    b_ref[...] = kk * a
    bonus_ref[...] = _split_dot(r * k * rk_ref[...], ones) * v


def _prep(zr, mu_p, w0, a0, k_k, k_a, rk, wup, aup, gup, ones, seq):
    t = zr.shape[0]
    tm = min(ROW_TILE, seq)
    w = RWKV_WIDTH
    row = lambda i: (i, 0)
    fixed = lambda i: (0, 0)
    per = tm // SUBLANES
    vecw = pl.BlockSpec((1, w), fixed)
    out = pl.BlockSpec((tm, w), row)
    return pl.pallas_call(
        functools.partial(_prep_kernel, tiles_per_seq=seq // tm),
        grid=(t // tm,),
        in_specs=[
            pl.BlockSpec((tm, ZR_COLS), row),
            pl.BlockSpec((SUBLANES, ZR_COLS), lambda i: (jnp.maximum(i * per - 1, 0), 0)),
            pl.BlockSpec((1, ZR_COLS), fixed),
            vecw, vecw, vecw, vecw, vecw,
            pl.BlockSpec((LORA_PAD, w), fixed),
            pl.BlockSpec((LORA_PAD, w), fixed),
            pl.BlockSpec((LORA_PAD, w), fixed),
            pl.BlockSpec((w, w), fixed),
        ],
        out_specs=[out] * 8,
        out_shape=[jax.ShapeDtypeStruct((t, w), F32)] * 8,
        compiler_params=_cparams(("parallel",)),
        name="rwkv_prep",
    )(zr, zr, mu_p, w0, a0, k_k, k_a, rk, wup, aup, gup, ones)


def _scan_kernel(r_ref, lw_ref, k_ref, v_ref, kk_ref, b_ref, y_ref, state_ref, *, n_chunks):
    L = CHUNK
    G = GROUP

    @pl.when(pl.program_id(1) == 0)
    def _():
        state_ref[...] = jnp.zeros_like(state_ref)

    row = lax.broadcasted_iota(I32, (L, G), 0)
    colr = lax.broadcasted_iota(I32, (L, G), 1) & (L - 1)
    strict = (colr < row).astype(F32)
    incl = (colr <= row).astype(F32)
    eye = (colr == row).astype(F32)
    br = lax.broadcasted_iota(I32, (G, G), 0) >> 6
    bc = lax.broadcasted_iota(I32, (G, G), 1) >> 6
    block = (br == bc).astype(F32)
    rows1 = lax.broadcasted_iota(I32, (L, 1), 0)

    def stack4(x):
        return (jnp.concatenate([x, x, x, x], axis=0) * block).astype(BF16)

    def chunk(c, carry):
        base = pl.multiple_of(c * L, L)
        for g in range(RWKV_WIDTH // G):
            lanes = slice(g * G, (g + 1) * G)
            rd = lambda ref: ref[0, pl.ds(base, L), lanes]
            r, lw, k, v, kk, b = rd(r_ref), rd(lw_ref), rd(k_ref), rd(v_ref), rd(kk_ref), rd(b_ref)
            cs = lw
            sh = 1
            while sh < L:
                cs = cs + jnp.where(rows1 >= sh, pltpu.roll(cs, sh, 0), 0.0)
                sh *= 2
            tot = cs[L - 1:L, :]
            w_in = jnp.exp(cs)
            w_ex = jnp.exp(cs - lw)
            w_inv = jnp.exp(-cs)
            w_end = jnp.exp(tot - cs)
            a_hat = -kk * w_ex
            r_hat = r * w_in
            lhs = jnp.concatenate([a_hat, r_hat], axis=0).astype(BF16)
            ab = _nt(lhs, stack4(b * w_inv))
            ak = _nt(lhs, stack4(k * w_inv))
            a_ab = ab[:L] * strict
            a_rb = ab[L:] * incl
            a_ak = ak[:L] * strict
            a_rk = ak[L:] * incl
            t_mat = eye + a_ab
            p_mat = _dot(a_ab.astype(BF16), stack4(a_ab))
            for _ in range(4):
                tp = _dot(jnp.concatenate([t_mat, p_mat], axis=0).astype(BF16), stack4(p_mat))
                t_mat = t_mat + tp[:L]
                p_mat = tp[L:]
            t_mat = t_mat + _dot(t_mat.astype(BF16), stack4(p_mat))
            av = _dot(jnp.concatenate([a_ak, a_rk], axis=0).astype(BF16), stack4(v))
            t_bf = t_mat.astype(BF16)
            w_til = _dot(t_bf, stack4(a_hat))
            u_til = _dot(t_bf, stack4(av[:L]))
            vk = _tn(v.astype(BF16), (k * w_end).astype(BF16))
            s0 = state_ref[g]
            uy = _nt(jnp.concatenate([w_til, r_hat], axis=0).astype(BF16), s0.astype(BF16))
            u = uy[:L] + u_til
            y = uy[L:] + _dot(a_rb.astype(BF16), stack4(u)) + av[L:]
            ub = _tn(u.astype(BF16), (b * w_end).astype(BF16))
            state_ref[g] = s0 * jnp.exp(tot) + (ub + vk) * block
            y_ref[0, pl.ds(base, L), lanes] = y
        return carry

    lax.fori_loop(0, n_chunks, chunk, 0)


def _scan(r3, lw3, k3, v3, kk3, b3):
    bsz, s, w = r3.shape
    ts = min(SCAN_TILE, s)
    spec = pl.BlockSpec((1, ts, w), lambda bi, i: (bi, i, 0))
    return pl.pallas_call(
        functools.partial(_scan_kernel, n_chunks=ts // CHUNK),
        grid=(bsz, s // ts),
        in_specs=[spec] * 6,
        out_specs=spec,
        out_shape=jax.ShapeDtypeStruct((bsz, s, w), F32),
        scratch_shapes=[pltpu.VMEM((w // GROUP, GROUP, GROUP), F32)],
        compiler_params=_cparams(("parallel", "arbitrary")),
        name="rwkv_scan",
    )(r3, lw3, k3, v3, kk3, b3)


def _mix_kernel(od_ref, y_ref, g_ref, bonus_ref, x_ref, lnw_ref, lnb_ref, ones_ref, wo_ref,
                fw_ref, rw_ref, rb_ref,
                h1_ref, xn_ref, sel_ref, idx_ref, gate_ref):
    ones = ones_ref[...]
    y = y_ref[...]
    inv_n = 1.0 / RWKV_HEAD
    mean = _split_dot(y, ones) * inv_n
    d = y - mean
    var = _split_dot(d * d, ones) * inv_n
    yn = d * lax.rsqrt(var + RWKV_GN_EPS) * lnw_ref[...] + lnb_ref[...]
    orw = ((yn + bonus_ref[...]) * g_ref[...]).astype(BF16)
    h1 = (x_ref[...] + _dot(od_ref[...], wo_ref[0:DIFF_WIDTH, :])
          + _dot(orw, wo_ref[DIFF_WIDTH:DIFF_WIDTH + RWKV_WIDTH, :]))
    h1_ref[...] = h1
    ms = jnp.mean(h1 * h1, axis=-1, keepdims=True)
    xn = h1 * lax.rsqrt(ms + NORM_EPS) * fw_ref[...]
    xn_ref[...] = xn
    logits = jnp.dot(xn, rw_ref[...], precision=lax.Precision.HIGHEST,
                     preferred_element_type=F32) + rb_ref[...]
    tm = logits.shape[0]
    lane = lax.broadcasted_iota(I32, (tm, LANES), 1).astype(F32)
    work = logits
    sel = jnp.zeros((tm, LANES), F32)
    idx_l = jnp.zeros((tm, LANES), F32)
    val_l = jnp.zeros((tm, LANES), F32)
    top = None
    for kslot in range(TOP_K):
        m = jnp.max(work, axis=-1, keepdims=True)
        pick = jnp.min(jnp.where(work == m, lane, float(LANES)), axis=-1, keepdims=True)
        hit = lane == pick
        sel = jnp.where(hit, 1.0, sel)
        idx_l = jnp.where(lane == kslot, pick, idx_l)
        if top is None:
            top = m
        val_l = jnp.where(lane == kslot, jnp.exp(m - top), val_l)
        work = jnp.where(hit, -jnp.inf, work)
    sel_ref[...] = sel
    idx_ref[...] = idx_l.astype(I32)
    gate_ref[...] = val_l / jnp.sum(val_l, axis=-1, keepdims=True)


def _mix(od, y, g, bonus, x2, lnw, lnb, ones, wo, fw, rw, rb):
    t, d = x2.shape
    tm = min(ROW_TILE, t)
    w = RWKV_WIDTH
    row = lambda i: (i, 0)
    fixed = lambda i: (0, 0)
    rs = lambda n: pl.BlockSpec((tm, n), row)
    return pl.pallas_call(
        _mix_kernel,
        grid=(t // tm,),
        in_specs=[rs(DIFF_WIDTH), rs(w), rs(w), rs(w), rs(d),
                  pl.BlockSpec((1, w), fixed), pl.BlockSpec((1, w), fixed),
                  pl.BlockSpec((w, w), fixed), pl.BlockSpec(wo.shape, fixed),
                  pl.BlockSpec((1, d), fixed), pl.BlockSpec(rw.shape, fixed),
                  pl.BlockSpec((1, LANES), fixed)],
        out_specs=[rs(d), rs(d), rs(LANES), rs(LANES), rs(LANES)],
        out_shape=[jax.ShapeDtypeStruct((t, d), F32), jax.ShapeDtypeStruct((t, d), F32),
                   jax.ShapeDtypeStruct((t, LANES), F32), jax.ShapeDtypeStruct((t, LANES), I32),
                   jax.ShapeDtypeStruct((t, LANES), F32)],
        compiler_params=_cparams(("parallel",)),
        name="mix_router",
    )(od, y, g, bonus, x2, lnw, lnb, ones, wo, fw, rw, rb)


def _rank_kernel(sel_ref, idx_ref, rank_ref, count_ref, run_ref):
    i = pl.program_id(0)

    @pl.when(i == 0)
    def _():
        run_ref[...] = jnp.zeros_like(run_ref)

    sel = sel_ref[...]
    tt = sel.shape[0]
    r = lax.broadcasted_iota(I32, (tt, tt), 0)
    c = lax.broadcasted_iota(I32, (tt, tt), 1)
    lower = (c < r).astype(BF16)
    run = run_ref[0:1, :]
    before = _dot(lower, sel.astype(BF16)) + run
    lane = lax.broadcasted_iota(I32, (tt, LANES), 1).astype(F32)
    idx = idx_ref[...].astype(F32)
    out = jnp.zeros((tt, LANES), F32)
    for kslot in range(TOP_K):
        e = jnp.sum(jnp.where(lane == kslot, idx, 0.0), axis=-1, keepdims=True)
        rk = jnp.sum(jnp.where(lane == e, before, 0.0), axis=-1, keepdims=True)
        out = jnp.where(lane == kslot, rk, out)
    rank_ref[...] = out.astype(I32)
    run = run + jnp.sum(sel, axis=0, keepdims=True)
    run_ref[...] = jnp.broadcast_to(run, run_ref.shape)
    count_ref[...] = jnp.broadcast_to(run, count_ref.shape).astype(I32)


def _rank(sel, idx_l):
    t = sel.shape[0]
    tt = min(ROW_TILE, t)
    row = lambda i: (i, 0)
    return pl.pallas_call(
        _rank_kernel,
        grid=(t // tt,),
        in_specs=[pl.BlockSpec((tt, LANES), row), pl.BlockSpec((tt, LANES), row)],
        out_specs=[pl.BlockSpec((tt, LANES), row), pl.BlockSpec((SUBLANES, LANES), lambda i: (0, 0))],
        out_shape=[jax.ShapeDtypeStruct((t, LANES), I32), jax.ShapeDtypeStruct((SUBLANES, LANES), I32)],
        scratch_shapes=[pltpu.VMEM((SUBLANES, LANES), F32)],
        compiler_params=_cparams(("arbitrary",)),
        name="moe_rank",
    )(sel, idx_l)


def _dest_kernel(idx_ref, rank_ref, start_ref, dest_ref):
    idx = idx_ref[...].astype(F32)
    tt = idx.shape[0]
    lane = lax.broadcasted_iota(I32, (tt, LANES), 1).astype(F32)
    start = start_ref[0:1, :].astype(F32)
    out = jnp.zeros((tt, LANES), F32)
    for kslot in range(TOP_K):
        e = jnp.sum(jnp.where(lane == kslot, idx, 0.0), axis=-1, keepdims=True)
        st = jnp.sum(jnp.where(lane == e, start, 0.0), axis=-1, keepdims=True)
        out = jnp.where(lane == kslot, st, out)
    dest_ref[...] = out.astype(I32) + rank_ref[...]


def _dest(idx_l, rank_l, start8):
    t = idx_l.shape[0]
    tt = min(ROW_TILE, t)
    row = lambda i: (i, 0)
    return pl.pallas_call(
        _dest_kernel,
        grid=(t // tt,),
        in_specs=[pl.BlockSpec((tt, LANES), row), pl.BlockSpec((tt, LANES), row),
                  pl.BlockSpec((SUBLANES, LANES), lambda i: (0, 0))],
        out_specs=pl.BlockSpec((tt, LANES), row),
        out_shape=jax.ShapeDtypeStruct((t, LANES), I32),
        compiler_params=_cparams(("parallel",)),
        name="moe_dest",
    )(idx_l, rank_l, start8)


def _dispatch_kernel(dest_ref, xn_ref, buf_in_ref, buf_ref, sem, *, tt):
    del buf_in_ref
    i = pl.program_id(0)

    def row_copy(tok, slot):
        d = dest_ref[0, 0, tok * TOP_K + slot]
        return pltpu.make_async_copy(xn_ref.at[pl.ds(i * tt + tok, 1), :], buf_ref.at[pl.ds(d, 1), :], sem)

    def start(tok, carry):
        for slot in range(TOP_K):
            row_copy(tok, slot).start()
        return carry

    def wait(tok, carry):
        for slot in range(TOP_K):
            row_copy(tok, slot).wait()
        return carry

    lax.fori_loop(0, tt, start, 0)
    lax.fori_loop(0, tt, wait, 0)


def _dispatch(dest3, xn, buf0):
    t, d = xn.shape
    n_tiles, _, per = dest3.shape
    tt = per // TOP_K
    return pl.pallas_call(
        functools.partial(_dispatch_kernel, tt=tt),
        grid=(n_tiles,),
        in_specs=[pl.BlockSpec((1, 1, per), lambda i: (i, 0, 0), memory_space=pltpu.SMEM),
                  pl.BlockSpec(memory_space=pl.ANY),
                  pl.BlockSpec(memory_space=pl.ANY)],
        out_specs=pl.BlockSpec(memory_space=pl.ANY),
        out_shape=jax.ShapeDtypeStruct(buf0.shape, buf0.dtype),
        scratch_shapes=[pltpu.SemaphoreType.DMA(())],
        input_output_aliases={2: 0},
        compiler_params=_cparams(("arbitrary",)),
        name="moe_dispatch",
    )(dest3, xn, buf0)


def _expert_kernel(be_ref, nused_ref, x_ref, w1_ref, b1_ref, w2_ref, b2_ref, y_ref):
    i = pl.program_id(0)

    @pl.when(i < nused_ref[0])
    def _():
        ff = w2_ref.shape[1]
        x = x_ref[...].astype(BF16)
        hid = _dot(x, w1_ref[0]) + b1_ref[0]
        glu = jnp.minimum(hid[:, :ff], SWIGLU_LIMIT)
        lin = jnp.clip(hid[:, ff:], -SWIGLU_LIMIT, SWIGLU_LIMIT)
        act = glu * jax.nn.sigmoid(SWIGLU_ALPHA * glu) * (lin + 1.0)
        y_ref[...] = _dot(act.astype(BF16), w2_ref[0]) + b2_ref[0]

    @pl.when(i >= nused_ref[0])
    def _():
        y_ref[...] = jnp.zeros_like(y_ref)


def _experts(block_e, n_used, xbuf, w1, b1, w2, b2):
    n_rows, d = xbuf.shape
    bm = EXPERT_BLOCK
    n_blocks = n_rows // bm
    e, _, ff2 = w1.shape
    ff = ff2 // 2
    blk = lambda i, be, nu: (jnp.minimum(i, nu[0] - 1), 0)
    ex3 = lambda i, be, nu: (be[i], 0, 0)
    grid_spec = pltpu.PrefetchScalarGridSpec(
        num_scalar_prefetch=2,
        grid=(n_blocks,),
        in_specs=[pl.BlockSpec((bm, d), blk),
                  pl.BlockSpec((1, d, ff2), ex3),
                  pl.BlockSpec((1, 1, ff2), ex3),
                  pl.BlockSpec((1, ff, d), ex3),
                  pl.BlockSpec((1, 1, d), ex3)],
        out_specs=pl.BlockSpec((bm, d), lambda i, be, nu: (i, 0)),
    )
    return pl.pallas_call(
        _expert_kernel,
        grid_spec=grid_spec,
        out_shape=jax.ShapeDtypeStruct((n_rows, d), F32),
        compiler_params=_cparams(("arbitrary",)),
        name="moe_experts",
    )(block_e, n_used, xbuf, w1, b1, w2, b2)


def _combine_kernel(dest_ref, gate_ref, h1_ref, fw_ref, ybuf_ref, o_ref, rows_ref, sem, *, tt):
    def row_copy(tok, slot):
        d = dest_ref[0, 0, tok * TOP_K + slot]
        return pltpu.make_async_copy(ybuf_ref.at[pl.ds(d, 1), :], rows_ref.at[slot, pl.ds(tok, 1), :], sem)

    def start(tok, carry):
        for slot in range(TOP_K):
            row_copy(tok, slot).start()
        return carry

    def wait(tok, carry):
        for slot in range(TOP_K):
            row_copy(tok, slot).wait()
        return carry

    lax.fori_loop(0, tt, start, 0)
    lax.fori_loop(0, tt, wait, 0)
    gate = gate_ref[...]
    h = h1_ref[...]
    for slot in range(TOP_K):
        h = h + gate[:, slot:slot + 1] * rows_ref[slot]
    ms = jnp.mean(h * h, axis=-1, keepdims=True)
    o_ref[...] = h * lax.rsqrt(ms + NORM_EPS) * fw_ref[...]


def _combine(dest3, gate_l, h1, fw, ybuf):
    t, d = h1.shape
    n_tiles, _, per = dest3.shape
    tt = per // TOP_K
    row = lambda i: (i, 0)
    return pl.pallas_call(
        functools.partial(_combine_kernel, tt=tt),
        grid=(n_tiles,),
        in_specs=[pl.BlockSpec((1, 1, per), lambda i: (i, 0, 0), memory_space=pltpu.SMEM),
                  pl.BlockSpec((tt, LANES), row),
                  pl.BlockSpec((tt, d), row),
                  pl.BlockSpec((1, d), lambda i: (0, 0)),
                  pl.BlockSpec(memory_space=pl.ANY)],
        out_specs=pl.BlockSpec((tt, d), row),
        out_shape=jax.ShapeDtypeStruct((t, d), F32),
        scratch_shapes=[pltpu.VMEM((TOP_K, tt, d), F32), pltpu.SemaphoreType.DMA(())],
        compiler_params=_cparams(("arbitrary",)),
        name="moe_combine",
    )(dest3, gate_l, h1, fw, ybuf)


def _rotary_tables(positions):
    half = ROT_DIM // 2
    inv_freq = ROPE_THETA ** (-jnp.arange(0, ROT_DIM, 2, dtype=F32) / ROT_DIM)
    ang = positions.astype(F32).reshape(-1, 1) * inv_freq
    cos, sin = jnp.cos(ang), jnp.sin(ang)
    t = ang.shape[0]
    rest = DIFF_HEAD_DIM - ROT_DIM
    tile2 = lambda a: jnp.concatenate([a, a], axis=-1)
    ctab = tile2(jnp.concatenate([cos, cos, jnp.ones((t, rest), F32)], axis=-1))
    satab = tile2(jnp.concatenate([-sin, jnp.zeros((t, half + rest), F32)], axis=-1))
    sbtab = tile2(jnp.concatenate([jnp.zeros((t, half), F32), sin, jnp.zeros((t, rest), F32)], axis=-1))
    return ctab, satab, sbtab


def _pad_rows(a, rows):
    return jnp.concatenate([a, jnp.zeros((rows - a.shape[0],) + a.shape[1:], a.dtype)], axis=0)


def _layer(h, l, tabs, attn_norm_w, w_in, diff_lambda_q1, diff_lambda_k1, diff_lambda_q2, diff_lambda_k2,
           diff_subln_w, rwkv_mu, rwkv_w0, rwkv_w_up, rwkv_a0, rwkv_a_up, rwkv_g_up, rwkv_k_k, rwkv_k_a,
           rwkv_r_k, rwkv_ln_w, rwkv_ln_b, w_out, ffn_norm_w, router_w, router_b, exp_w1, exp_b1,
           exp_w2, exp_b2, final_w):
    bsz, seq, d = h.shape
    t = bsz * seq
    w = RWKV_WIDTH
    lambda_init = 0.8 - 0.6 * math.exp(-0.3 * l)
    x2 = h.reshape(t, d)
    row1 = lambda a: a.reshape(1, -1).astype(F32)

    wi = w_in[l]
    qkv_cols = 3 * DIFF_WIDTH
    wq = wi[:, :qkv_cols].astype(BF16)
    o = qkv_cols + 3 * w
    zcol = lambda n: jnp.zeros((d, n), wi.dtype)
    wr = jnp.concatenate([
        wi[:, qkv_cols:o],
        wi[:, o:o + DECAY_LORA], zcol(LORA_PAD - DECAY_LORA),
        wi[:, o + DECAY_LORA:o + DECAY_LORA + AAA_LORA], zcol(LORA_PAD - AAA_LORA),
        wi[:, o + DECAY_LORA + AAA_LORA:], zcol(LORA_PAD - GATE_LORA)], axis=1).astype(BF16)
    mu = rwkv_mu[l]
    zv = lambda n: jnp.zeros((n,), mu.dtype)
    mu_p = jnp.concatenate([
        mu[:3 * w],
        mu[3 * w:3 * w + DECAY_LORA], zv(LORA_PAD - DECAY_LORA),
        mu[3 * w + DECAY_LORA:3 * w + DECAY_LORA + AAA_LORA], zv(LORA_PAD - AAA_LORA),
        mu[3 * w + DECAY_LORA + AAA_LORA:], zv(LORA_PAD - GATE_LORA)]).reshape(1, -1)

    q, k, v, zr = _inproj(x2, row1(attn_norm_w[l]), wq, wr, *tabs)

    od = _attn(q.reshape(bsz, seq, -1), k.reshape(bsz, seq, -1), v.reshape(bsz, seq, -1),
               row1(diff_lambda_q1[l]), row1(diff_lambda_k1[l]), row1(diff_lambda_q2[l]),
               row1(diff_lambda_k2[l]), row1(diff_subln_w[l]), lambda_init)

    head = np.arange(w) // RWKV_HEAD
    ones = jnp.asarray(head[:, None] == head[None, :], BF16)
    r, lw, kmod, vv, kk, bb, g, bonus = _prep(
        zr, mu_p, row1(rwkv_w0[l]), row1(rwkv_a0[l]), row1(rwkv_k_k[l]), row1(rwkv_k_a[l]),
        row1(rwkv_r_k[l]), _pad_rows(rwkv_w_up[l].astype(F32), LORA_PAD),
        _pad_rows(rwkv_a_up[l].astype(F32), LORA_PAD), _pad_rows(rwkv_g_up[l].astype(F32), LORA_PAD),
        ones, seq)
    s3 = lambda a: a.reshape(bsz, seq, w)
    y = _scan(s3(r), s3(lw), s3(kmod), s3(vv), s3(kk), s3(bb)).reshape(t, w)

    n_e = router_w.shape[-1]
    rw = jnp.concatenate([router_w[l].astype(F32), jnp.zeros((d, LANES - n_e), F32)], axis=1)
    rb = jnp.concatenate([router_b[l].astype(F32), jnp.full((LANES - n_e,), -1e30, F32)]).reshape(1, -1)
    h1, xn, sel, idx_l, gate_l = _mix(
        od.reshape(t, -1), y, g, bonus, x2, row1(rwkv_ln_w[l]), row1(rwkv_ln_b[l]), ones,
        w_out[l].astype(BF16), row1(ffn_norm_w[l]), rw, rb)

    rank_l, count8 = _rank(sel, idx_l)
    bm = EXPERT_BLOCK
    counts = count8[0, :n_e]
    padded = (counts + bm - 1) // bm * bm
    pad_ends = jnp.cumsum(padded)
    pad_starts = pad_ends - padded
    start8 = jnp.zeros((SUBLANES, LANES), I32).at[:, :n_e].set(pad_starts[None, :])
    dest_l = _dest(idx_l, rank_l, start8)
    n_assign = t * TOP_K
    n_blocks = -(-(n_assign + n_e * (bm - 1)) // bm)
    n_rows = n_blocks * bm
    block_e = jnp.minimum(jnp.searchsorted(pad_ends, jnp.arange(n_blocks, dtype=I32) * bm, side='right'),
                          n_e - 1).astype(I32)
    n_used = (pad_ends[-1] // bm).astype(I32).reshape(1)
    dest = dest_l[:, :TOP_K]

    dt = min(DISPATCH_TILE, t)
    xbuf = _dispatch(dest.reshape(t // dt, 1, dt * TOP_K), xn, jnp.zeros((n_rows, d), F32))

    ff2 = exp_w1.shape[-1]
    w1 = jnp.concatenate([exp_w1[l][..., 0::2], exp_w1[l][..., 1::2]], axis=-1).astype(BF16)
    b1 = jnp.concatenate([exp_b1[l][..., 0::2], exp_b1[l][..., 1::2]], axis=-1).astype(F32).reshape(n_e, 1, ff2)
    w2 = exp_w2[l].astype(BF16)
    b2 = exp_b2[l].astype(F32).reshape(n_e, 1, d)
    ybuf = _experts(block_e, n_used, xbuf, w1, b1, w2, b2)

    ct = min(COMBINE_TILE, t)
    out = _combine(dest.reshape(t // ct, 1, ct * TOP_K), gate_l, h1, row1(final_w), ybuf)
    return out.reshape(bsz, seq, d)


def kernel(x, positions, attn_norm_w, w_in, diff_lambda_q1, diff_lambda_k1, diff_lambda_q2, diff_lambda_k2, diff_subln_w, rwkv_mu, rwkv_w0, rwkv_w_up, rwkv_a0, rwkv_a_up, rwkv_g_up, rwkv_k_k, rwkv_k_a, rwkv_r_k, rwkv_ln_w, rwkv_ln_b, w_out, ffn_norm_w, router_w, router_b, exp_w1, exp_b1, exp_w2, exp_b2, final_norm_w):
    depth = w_in.shape[0]
    assert depth == 1, "the final norm is fused into the last (only) layer's combine kernel"
    tabs = _rotary_tables(positions)
    return _layer(x, 0, tabs, attn_norm_w, w_in, diff_lambda_q1, diff_lambda_k1, diff_lambda_q2,
                  diff_lambda_k2, diff_subln_w, rwkv_mu, rwkv_w0, rwkv_w_up, rwkv_a0, rwkv_a_up, rwkv_g_up,
                  rwkv_k_k, rwkv_k_a, rwkv_r_k, rwkv_ln_w, rwkv_ln_b, w_out, ffn_norm_w, router_w, router_b,
                  exp_w1, exp_b1, exp_w2, exp_b2, final_norm_w)
```

```python
import functools
import math

import jax
import jax.numpy as jnp
import numpy as np
from jax import lax
from jax.experimental import pallas as pl
from jax.experimental.pallas import tpu as pltpu

F32 = jnp.float32
BF16 = jnp.bfloat16
I32 = jnp.int32

DIFF_HEAD_DIM = 64
DIFF_V_DIM = 128
DIFF_HEADS = 4
DIFF_WIDTH = DIFF_HEADS * DIFF_V_DIM
ROT_DIM = 16
ROPE_THETA = 500000.0
RWKV_HEAD = 64
RWKV_HEADS = 8
RWKV_WIDTH = RWKV_HEAD * RWKV_HEADS
DECAY_LORA = 32
AAA_LORA = 32
GATE_LORA = 96
N_EXPERTS = 32
TOP_K = 4
SWIGLU_LIMIT = 7.0
SWIGLU_ALPHA = 1.702
NORM_EPS = 1e-5
RWKV_GN_EPS = 64e-5

LANES = 128
SUBLANES = 8
VMEM_LIMIT = 56 * 1024 * 1024

ROW_TILE = 512
ATTN_TILE = 256
CHUNK = 64
SCAN_TILE = 512
GROUP = 4 * RWKV_HEAD
EXPERT_BLOCK = 256
DISPATCH_TILE = 256
COMBINE_TILE = 128
LORA_PAD = LANES
ZR_COLS = 3 * RWKV_WIDTH + 3 * LORA_PAD


def _cparams(sem):
    return pltpu.CompilerParams(dimension_semantics=sem, vmem_limit_bytes=VMEM_LIMIT)


def _nt(a, b):
    return lax.dot_general(a, b, (((1,), (1,)), ((), ())), preferred_element_type=F32)


def _tn(a, b):
    return lax.dot_general(a, b, (((0,), (0,)), ((), ())), preferred_element_type=F32)


def _dot(a, b):
    return jnp.dot(a, b, preferred_element_type=F32)


def _split_dot(x, w_bf16):
    hi = x.astype(BF16)
    lo = (x - hi.astype(F32)).astype(BF16)
    return _dot(hi, w_bf16) + _dot(lo, w_bf16)


def _inproj_kernel(x_ref, nw_ref, wq_ref, wr_ref, c_ref, sa_ref, sb_ref,
                   q_ref, k_ref, v_ref, zr_ref):
    x = x_ref[...]
    ms = jnp.mean(x * x, axis=-1, keepdims=True)
    u = (x * lax.rsqrt(ms + NORM_EPS) * nw_ref[...]).astype(BF16)
    zq = _dot(u, wq_ref[...])
    c = c_ref[...]
    sa = sa_ref[...]
    sb = sb_ref[...]
    scale = DIFF_HEAD_DIM ** -0.5
    for g in range(2 * DIFF_HEADS):
        zg = zq[:, g * LANES:(g + 1) * LANES]
        rot = zg * c + pltpu.roll(zg, LANES - ROT_DIM // 2, 1) * sa + pltpu.roll(zg, ROT_DIM // 2, 1) * sb
        if g < DIFF_HEADS:
            q_ref[:, g * LANES:(g + 1) * LANES] = (rot * scale).astype(BF16)
        else:
            h = g - DIFF_HEADS
            k_ref[:, h * LANES:(h + 1) * LANES] = rot.astype(BF16)
    v_ref[...] = zq[:, 2 * DIFF_WIDTH:3 * DIFF_WIDTH].astype(BF16)
    zr_ref[...] = _dot(u, wr_ref[...])


def _inproj(x2, nw, wq, wr, ctab, satab, sbtab):
    t, d = x2.shape
    tm = min(ROW_TILE, t)
    row = lambda i: (i, 0)
    fixed = lambda i: (0, 0)
    return pl.pallas_call(
        _inproj_kernel,
        grid=(t // tm,),
        in_specs=[
            pl.BlockSpec((tm, d), row),
            pl.BlockSpec((1, d), fixed),
            pl.BlockSpec(wq.shape, fixed),
            pl.BlockSpec(wr.shape, fixed),
            pl.BlockSpec((tm, LANES), row),
            pl.BlockSpec((tm, LANES), row),
            pl.BlockSpec((tm, LANES), row),
        ],
        out_specs=[
            pl.BlockSpec((tm, DIFF_WIDTH), row),
            pl.BlockSpec((tm, DIFF_WIDTH), row),
            pl.BlockSpec((tm, DIFF_WIDTH), row),
            pl.BlockSpec((tm, ZR_COLS), row),
        ],
        out_shape=[
            jax.ShapeDtypeStruct((t, DIFF_WIDTH), BF16),
            jax.ShapeDtypeStruct((t, DIFF_WIDTH), BF16),
            jax.ShapeDtypeStruct((t, DIFF_WIDTH), BF16),
            jax.ShapeDtypeStruct((t, ZR_COLS), F32),
        ],
        compiler_params=_cparams(("parallel",)),
        name="inproj",
    )(x2, nw, wq, wr, ctab, satab, sbtab)


def _attn_kernel(q_ref, k_ref, v_ref, lq1_ref, lk1_ref, lq2_ref, lk2_ref, sw_ref, o_ref,
                 *, tile, lambda_init):
    qi = pl.program_id(2)
    q = q_ref[0]
    lane = lax.broadcasted_iota(I32, (1, LANES), 1)
    first = lane < DIFF_HEAD_DIM
    zero = jnp.zeros_like(q)
    q1 = jnp.where(first, q, zero)
    q2 = jnp.where(first, zero, q)
    neg = -1e30

    def step(j, carry, masked):
        m1, l1, a1, m2, l2, a2 = carry
        kb = k_ref[0, pl.ds(j * tile, tile), :]
        vb = v_ref[0, pl.ds(j * tile, tile), :]
        s1 = _nt(q1, kb)
        s2 = _nt(q2, kb)
        if masked:
            r = lax.broadcasted_iota(I32, (tile, tile), 0)
            cidx = lax.broadcasted_iota(I32, (tile, tile), 1)
            keep = cidx <= r
            s1 = jnp.where(keep, s1, neg)
            s2 = jnp.where(keep, s2, neg)
        out = []
        for s, m, l, a in ((s1, m1, l1, a1), (s2, m2, l2, a2)):
            mn = jnp.maximum(m, jnp.max(s, axis=-1, keepdims=True))
            alpha = jnp.exp(m - mn)
            p = jnp.exp(s - mn)
            ln = alpha * l + jnp.sum(p, axis=-1, keepdims=True)
            an = alpha * a + _dot(p.astype(BF16), vb)
            out += [mn, ln, an]
        return tuple(out)

    init = (jnp.full((tile, 1), neg, F32), jnp.zeros((tile, 1), F32), jnp.zeros((tile, LANES), F32)) * 2
    carry = lax.fori_loop(0, qi, lambda j, c: step(j, c, False), init)
    m1, l1, a1, m2, l2, a2 = step(qi, carry, True)
    lam = (jnp.exp(jnp.sum(lq1_ref[...] * lk1_ref[...], axis=-1, keepdims=True))
           - jnp.exp(jnp.sum(lq2_ref[...] * lk2_ref[...], axis=-1, keepdims=True)) + lambda_init)
    o = a1 / l1 - lam * (a2 / l2)
    ms = jnp.mean(o * o, axis=-1, keepdims=True)
    o = o * lax.rsqrt(ms + NORM_EPS) * sw_ref[...] * (1.0 - lambda_init)
    o_ref[0] = o.astype(o_ref.dtype)


def _attn(q3, k3, v3, lq1, lk1, lq2, lk2, sw, lambda_init):
    b, s, _ = q3.shape
    tile = min(ATTN_TILE, s)
    qspec = pl.BlockSpec((1, tile, LANES), lambda bi, h, i: (bi, i, h))
    kvspec = pl.BlockSpec((1, s, LANES), lambda bi, h, i: (bi, 0, h))
    vec = lambda n: pl.BlockSpec((1, n), lambda bi, h, i: (0, 0))
    return pl.pallas_call(
        functools.partial(_attn_kernel, tile=tile, lambda_init=lambda_init),
        grid=(b, DIFF_HEADS, s // tile),
        in_specs=[qspec, kvspec, kvspec, vec(DIFF_HEAD_DIM), vec(DIFF_HEAD_DIM), vec(DIFF_HEAD_DIM),
                  vec(DIFF_HEAD_DIM), vec(DIFF_V_DIM)],
        out_specs=qspec,
        out_shape=jax.ShapeDtypeStruct((b, s, DIFF_WIDTH), BF16),
        compiler_params=_cparams(("parallel", "parallel", "parallel")),
        name="attn",
    )(q3, k3, v3, lq1, lk1, lq2, lk2, sw)


def _prep_kernel(z_ref, zp_ref, mu_ref, w0_ref, a0_ref, kk_ref, ka_ref, rk_ref,
                 wup_ref, aup_ref, gup_ref, ones_ref,
                 r_ref, lw_ref, k_ref, v_ref, kkn_ref, b_ref, g_ref, bonus_ref, *, tiles_per_seq):
    i = pl.program_id(0)
    z = z_ref[...]
    tm = z.shape[0]
    rows = lax.broadcasted_iota(I32, (tm, 1), 0)
    prev = zp_ref[SUBLANES - 1:SUBLANES, :]
    prev = jnp.where(i % tiles_per_seq == 0, jnp.zeros_like(prev), prev)
    shifted = jnp.where(rows == 0, prev, pltpu.roll(z, 1, 0))
    zf = z + mu_ref[...] * (shifted - z)
    w = RWKV_WIDTH
    r = zf[:, 0:w]
    k = zf[:, w:2 * w]
    v = zf[:, 2 * w:3 * w]
    wd = zf[:, 3 * w:3 * w + LORA_PAD]
    ad = zf[:, 3 * w + LORA_PAD:3 * w + 2 * LORA_PAD]
    gd = zf[:, 3 * w + 2 * LORA_PAD:3 * w + 3 * LORA_PAD]
    hp = lax.Precision.HIGHEST
    pre = w0_ref[...] + jnp.dot(jnp.tanh(wd), wup_ref[...], precision=hp, preferred_element_type=F32)
    neg = -pre
    softplus = jnp.maximum(neg, 0.0) + jnp.log(1.0 + jnp.exp(-jnp.abs(neg)))
    wlog = -softplus - 0.5
    lw_ref[...] = -jnp.exp(wlog)
    a = jax.nn.sigmoid(a0_ref[...] + jnp.dot(ad, aup_ref[...], precision=hp, preferred_element_type=F32))
    g_ref[...] = jnp.dot(jax.nn.sigmoid(gd), gup_ref[...], precision=hp, preferred_element_type=F32)
    ones = ones_ref[...]
    kk = k * kk_ref[...]
    norm = jnp.sqrt(_split_dot(kk * kk, ones))
    kk = kk / jnp.maximum(norm, 1e-12)
    k = k * (1.0 + (a - 1.0) * ka_ref[...])
    r_ref[...] = r
    k_ref[...] = k
    v_ref[...] = v
    kkn_ref[...] = kk
    b_ref[...] = kk * a
    bonus_ref[...] = _split_dot(r * k * rk_ref[...], ones) * v


def _prep(zr, mu_p, w0, a0, k_k, k_a, rk, wup, aup, gup, ones, seq):
    t = zr.shape[0]
    tm = min(ROW_TILE, seq)
    w = RWKV_WIDTH
    row = lambda i: (i, 0)
    fixed = lambda i: (0, 0)
    per = tm // SUBLANES
    vecw = pl.BlockSpec((1, w), fixed)
    out = pl.BlockSpec((tm, w), row)
    return pl.pallas_call(
        functools.partial(_prep_kernel, tiles_per_seq=seq // tm),
        grid=(t // tm,),
        in_specs=[
            pl.BlockSpec((tm, ZR_COLS), row),
            pl.BlockSpec((SUBLANES, ZR_COLS), lambda i: (jnp.maximum(i * per - 1, 0), 0)),
            pl.BlockSpec((1, ZR_COLS), fixed),
            vecw, vecw, vecw, vecw, vecw,
            pl.BlockSpec((LORA_PAD, w), fixed),
            pl.BlockSpec((LORA_PAD, w), fixed),
            pl.BlockSpec((LORA_PAD, w), fixed),
            pl.BlockSpec((w, w), fixed),
        ],
        out_specs=[out] * 8,
        out_shape=[jax.ShapeDtypeStruct((t, w), F32)] * 8,
        compiler_params=_cparams(("parallel",)),
        name="rwkv_prep",
    )(zr, zr, mu_p, w0, a0, k_k, k_a, rk, wup, aup, gup, ones)


def _scan_kernel(r_ref, lw_ref, k_ref, v_ref, kk_ref, b_ref, y_ref, state_ref, *, n_chunks):
    L = CHUNK
    G = GROUP

    @pl.when(pl.program_id(1) == 0)
    def _():
        state_ref[...] = jnp.zeros_like(state_ref)

    row = lax.broadcasted_iota(I32, (L, G), 0)
    colr = lax.broadcasted_iota(I32, (L, G), 1) & (L - 1)
    strict = (colr < row).astype(F32)
    incl = (colr <= row).astype(F32)
    eye = (colr == row).astype(F32)
    br = lax.broadcasted_iota(I32, (G, G), 0) >> 6
    bc = lax.broadcasted_iota(I32, (G, G), 1) >> 6
    block = (br == bc).astype(F32)
    rows1 = lax.broadcasted_iota(I32, (L, 1), 0)

    def stack4(x):
        return (jnp.concatenate([x, x, x, x], axis=0) * block).astype(BF16)

    def chunk(c, carry):
        base = pl.multiple_of(c * L, L)
        for g in range(RWKV_WIDTH // G):
            lanes = slice(g * G, (g + 1) * G)
            rd = lambda ref: ref[0, pl.ds(base, L), lanes]
            r, lw, k, v, kk, b = rd(r_ref), rd(lw_ref), rd(k_ref), rd(v_ref), rd(kk_ref), rd(b_ref)
            cs = lw
            sh = 1
            while sh < L:
                cs = cs + jnp.where(rows1 >= sh, pltpu.roll(cs, sh, 0), 0.0)
                sh *= 2
            tot = cs[L - 1:L, :]
            w_in = jnp.exp(cs)
            w_ex = jnp.exp(cs - lw)
            w_inv = jnp.exp(-cs)
            w_end = jnp.exp(tot - cs)
            a_hat = -kk * w_ex
            r_hat = r * w_in
            lhs = jnp.concatenate([a_hat, r_hat], axis=0).astype(BF16)
            ab = _nt(lhs, stack4(b * w_inv))
            ak = _nt(lhs, stack4(k * w_inv))
            a_ab = ab[:L] * strict
            a_rb = ab[L:] * incl
            a_ak = ak[:L] * strict
            a_rk = ak[L:] * incl
            t_mat = eye + a_ab
            p_mat = _dot(a_ab.astype(BF16), stack4(a_ab))
            for _ in range(4):
                tp = _dot(jnp.concatenate([t_mat, p_mat], axis=0).astype(BF16), stack4(p_mat))
                t_mat = t_mat + tp[:L]
                p_mat = tp[L:]
            t_mat = t_mat + _dot(t_mat.astype(BF16), stack4(p_mat))
            av = _dot(jnp.concatenate([a_ak, a_rk], axis=0).astype(BF16), stack4(v))
            t_bf = t_mat.astype(BF16)
            w_til = _dot(t_bf, stack4(a_hat))
            u_til = _dot(t_bf, stack4(av[:L]))
            vk = _tn(v.astype(BF16), (k * w_end).astype(BF16))
            s0 = state_ref[g]
            uy = _nt(jnp.concatenate([w_til, r_hat], axis=0).astype(BF16), s0.astype(BF16))
            u = uy[:L] + u_til
            y = uy[L:] + _dot(a_rb.astype(BF16), stack4(u)) + av[L:]
            ub = _tn(u.astype(BF16), (b * w_end).astype(BF16))
            state_ref[g] = s0 * jnp.exp(tot) + (ub + vk) * block
            y_ref[0, pl.ds(base, L), lanes] = y
        return carry

    lax.fori_loop(0, n_chunks, chunk, 0)


def _scan(r3, lw3, k3, v3, kk3, b3):
    bsz, s, w = r3.shape
    ts = min(SCAN_TILE, s)
    spec = pl.BlockSpec((1, ts, w), lambda bi, i: (bi, i, 0))
    return pl.pallas_call(
        functools.partial(_scan_kernel, n_chunks=ts // CHUNK),
        grid=(bsz, s // ts),
        in_specs=[spec] * 6,
        out_specs=spec,
        out_shape=jax.ShapeDtypeStruct((bsz, s, w), F32),
        scratch_shapes=[pltpu.VMEM((w // GROUP, GROUP, GROUP), F32)],
        compiler_params=_cparams(("parallel", "arbitrary")),
        name="rwkv_scan",
    )(r3, lw3, k3, v3, kk3, b3)


def _mix_kernel(od_ref, y_ref, g_ref, bonus_ref, x_ref, lnw_ref, lnb_ref, ones_ref, wo_ref,
                fw_ref, rw_ref, rb_ref,
                h1_ref, xn_ref, sel_ref, idx_ref, gate_ref):
    ones = ones_ref[...]
    y = y_ref[...]
    inv_n = 1.0 / RWKV_HEAD
    mean = _split_dot(y, ones) * inv_n
    d = y - mean
    var = _split_dot(d * d, ones) * inv_n
    yn = d * lax.rsqrt(var + RWKV_GN_EPS) * lnw_ref[...] + lnb_ref[...]
    orw = ((yn + bonus_ref[...]) * g_ref[...]).astype(BF16)
    h1 = (x_ref[...] + _dot(od_ref[...], wo_ref[0:DIFF_WIDTH, :])
          + _dot(orw, wo_ref[DIFF_WIDTH:DIFF_WIDTH + RWKV_WIDTH, :]))
    h1_ref[...] = h1
    ms = jnp.mean(h1 * h1, axis=-1, keepdims=True)
    xn = h1 * lax.rsqrt(ms + NORM_EPS) * fw_ref[...]
    xn_ref[...] = xn
    logits = jnp.dot(xn, rw_ref[...], precision=lax.Precision.HIGHEST,
                     preferred_element_type=F32) + rb_ref[...]
    tm = logits.shape[0]
    lane = lax.broadcasted_iota(I32, (tm, LANES), 1).astype(F32)
    work = logits
    sel = jnp.zeros((tm, LANES), F32)
    idx_l = jnp.zeros((tm, LANES), F32)
    val_l = jnp.zeros((tm, LANES), F32)
    top = None
    for kslot in range(TOP_K):
        m = jnp.max(work, axis=-1, keepdims=True)
        pick = jnp.min(jnp.where(work == m, lane, float(LANES)), axis=-1, keepdims=True)
        hit = lane == pick
        sel = jnp.where(hit, 1.0, sel)
        idx_l = jnp.where(lane == kslot, pick, idx_l)
        if top is None:
            top = m
        val_l = jnp.where(lane == kslot, jnp.exp(m - top), val_l)
        work = jnp.where(hit, -jnp.inf, work)
    sel_ref[...] = sel
    idx_ref[...] = idx_l.astype(I32)
    gate_ref[...] = val_l / jnp.sum(val_l, axis=-1, keepdims=True)


def _mix(od, y, g, bonus, x2, lnw, lnb, ones, wo, fw, rw, rb):
    t, d = x2.shape
    tm = min(ROW_TILE, t)
    w = RWKV_WIDTH
    row = lambda i: (i, 0)
    fixed = lambda i: (0, 0)
    rs = lambda n: pl.BlockSpec((tm, n), row)
    return pl.pallas_call(
        _mix_kernel,
        grid=(t // tm,),
        in_specs=[rs(DIFF_WIDTH), rs(w), rs(w), rs(w), rs(d),
                  pl.BlockSpec((1, w), fixed), pl.BlockSpec((1, w), fixed),
                  pl.BlockSpec((w, w), fixed), pl.BlockSpec(wo.shape, fixed),
                  pl.BlockSpec((1, d), fixed), pl.BlockSpec(rw.shape, fixed),
                  pl.BlockSpec((1, LANES), fixed)],
        out_specs=[rs(d), rs(d), rs(LANES), rs(LANES), rs(LANES)],
        out_shape=[jax.ShapeDtypeStruct((t, d), F32), jax.ShapeDtypeStruct((t, d), F32),
                   jax.ShapeDtypeStruct((t, LANES), F32), jax.ShapeDtypeStruct((t, LANES), I32),
                   jax.ShapeDtypeStruct((t, LANES), F32)],
        compiler_params=_cparams(("parallel",)),
        name="mix_router",
    )(od, y, g, bonus, x2, lnw, lnb, ones, wo, fw, rw, rb)


def _rank_kernel(sel_ref, idx_ref, rank_ref, count_ref, run_ref):
    i = pl.program_id(0)

    @pl.when(i == 0)
    def _():
        run_ref[...] = jnp.zeros_like(run_ref)

    sel = sel_ref[...]
    tt = sel.shape[0]
    r = lax.broadcasted_iota(I32, (tt, tt), 0)
    c = lax.broadcasted_iota(I32, (tt, tt), 1)
    lower = (c < r).astype(BF16)
    run = run_ref[0:1, :]
    before = _dot(lower, sel.astype(BF16)) + run
    lane = lax.broadcasted_iota(I32, (tt, LANES), 1).astype(F32)
    idx = idx_ref[...].astype(F32)
    out = jnp.zeros((tt, LANES), F32)
    for kslot in range(TOP_K):
        e = jnp.sum(jnp.where(lane == kslot, idx, 0.0), axis=-1, keepdims=True)
        rk = jnp.sum(jnp.where(lane == e, before, 0.0), axis=-1, keepdims=True)
        out = jnp.where(lane == kslot, rk, out)
    rank_ref[...] = out.astype(I32)
    run = run + jnp.sum(sel, axis=0, keepdims=True)
    run_ref[...] = jnp.broadcast_to(run, run_ref.shape)
    count_ref[...] = jnp.broadcast_to(run, count_ref.shape).astype(I32)


def _rank(sel, idx_l):
    t = sel.shape[0]
    tt = min(ROW_TILE, t)
    row = lambda i: (i, 0)
    return pl.pallas_call(
        _rank_kernel,
        grid=(t // tt,),
        in_specs=[pl.BlockSpec((tt, LANES), row), pl.BlockSpec((tt, LANES), row)],
        out_specs=[pl.BlockSpec((tt, LANES), row), pl.BlockSpec((SUBLANES, LANES), lambda i: (0, 0))],
        out_shape=[jax.ShapeDtypeStruct((t, LANES), I32), jax.ShapeDtypeStruct((SUBLANES, LANES), I32)],
        scratch_shapes=[pltpu.VMEM((SUBLANES, LANES), F32)],
        compiler_params=_cparams(("arbitrary",)),
        name="moe_rank",
    )(sel, idx_l)


def _dest_kernel(idx_ref, rank_ref, start_ref, dest_ref):
    idx = idx_ref[...].astype(F32)
    tt = idx.shape[0]
    lane = lax.broadcasted_iota(I32, (tt, LANES), 1).astype(F32)
    start = start_ref[0:1, :].astype(F32)
    out = jnp.zeros((tt, LANES), F32)
    for kslot in range(TOP_K):
        e = jnp.sum(jnp.where(lane == kslot, idx, 0.0), axis=-1, keepdims=True)
        st = jnp.sum(jnp.where(lane == e, start, 0.0), axis=-1, keepdims=True)
        out = jnp.where(lane == kslot, st, out)
    dest_ref[...] = out.astype(I32) + rank_ref[...]


def _dest(idx_l, rank_l, start8):
    t = idx_l.shape[0]
    tt = min(ROW_TILE, t)
    row = lambda i: (i, 0)
    return pl.pallas_call(
        _dest_kernel,
        grid=(t // tt,),
        in_specs=[pl.BlockSpec((tt, LANES), row), pl.BlockSpec((tt, LANES), row),
                  pl.BlockSpec((SUBLANES, LANES), lambda i: (0, 0))],
        out_specs=pl.BlockSpec((tt, LANES), row),
        out_shape=jax.ShapeDtypeStruct((t, LANES), I32),
        compiler_params=_cparams(("parallel",)),
        name="moe_dest",
    )(idx_l, rank_l, start8)


def _dispatch_kernel(dest_ref, xn_ref, buf_in_ref, buf_ref, sem, *, tt):
    del buf_in_ref

    def row_copy(tok, slot):
        d = dest_ref[0, 0, tok * TOP_K + slot]
        return pltpu.make_async_copy(xn_ref.at[pl.ds(tok, 1), :], buf_ref.at[pl.ds(d, 1), :], sem)

    def start(tok, carry):
        for slot in range(TOP_K):
            row_copy(tok, slot).start()
        return carry

    def wait(tok, carry):
        for slot in range(TOP_K):
            row_copy(tok, slot).wait()
        return carry

    lax.fori_loop(0, tt, start, 0, unroll=4)
    lax.fori_loop(0, tt, wait, 0)


def _dispatch(dest3, xn, buf0):
    t, d = xn.shape
    n_tiles, _, per = dest3.shape
    tt = per // TOP_K
    return pl.pallas_call(
        functools.partial(_dispatch_kernel, tt=tt),
        grid=(n_tiles,),
        in_specs=[pl.BlockSpec((1, 1, per), lambda i: (i, 0, 0), memory_space=pltpu.SMEM),
                  pl.BlockSpec((tt, d), lambda i: (i, 0)),
                  pl.BlockSpec(memory_space=pl.ANY)],
        out_specs=pl.BlockSpec(memory_space=pl.ANY),
        out_shape=jax.ShapeDtypeStruct(buf0.shape, buf0.dtype),
        scratch_shapes=[pltpu.SemaphoreType.DMA(())],
        input_output_aliases={2: 0},
        compiler_params=_cparams(("arbitrary",)),
        name="moe_dispatch",
    )(dest3, xn, buf0)


def _expert_kernel(be_ref, nused_ref, x_ref, w1_ref, b1_ref, w2_ref, b2_ref, y_ref):
    i = pl.program_id(0)

    @pl.when(i < nused_ref[0])
    def _():
        ff = w2_ref.shape[1]
        x = x_ref[...].astype(BF16)
        hid = _dot(x, w1_ref[0]) + b1_ref[0]
        even = (lax.broadcasted_iota(I32, (1, LANES), 1) & 1) == 0

        def act_even(g):
            hg = hid[:, g * LANES:(g + 1) * LANES]
            glu = jnp.minimum(hg, SWIGLU_LIMIT)
            lin = jnp.clip(hg, -SWIGLU_LIMIT, SWIGLU_LIMIT) + 1.0
            return glu * jax.nn.sigmoid(SWIGLU_ALPHA * glu) * pltpu.roll(lin, LANES - 1, 1)

        half = ff // LANES
        act = jnp.concatenate(
            [jnp.where(even, act_even(g), pltpu.roll(act_even(g + half), 1, 1)) for g in range(half)], axis=1)
        y_ref[...] = _dot(act.astype(BF16), w2_ref[0]) + b2_ref[0]

    @pl.when(i >= nused_ref[0])
    def _():
        y_ref[...] = jnp.zeros_like(y_ref)


def _experts(block_e, n_used, xbuf, w1, b1, w2, b2):
    n_rows, d = xbuf.shape
    bm = EXPERT_BLOCK
    n_blocks = n_rows // bm
    e, _, ff2 = w1.shape
    ff = ff2 // 2
    blk = lambda i, be, nu: (jnp.minimum(i, nu[0] - 1), 0)
    ex3 = lambda i, be, nu: (be[i], 0, 0)
    grid_spec = pltpu.PrefetchScalarGridSpec(
        num_scalar_prefetch=2,
        grid=(n_blocks,),
        in_specs=[pl.BlockSpec((bm, d), blk),
                  pl.BlockSpec((1, d, ff2), ex3),
                  pl.BlockSpec((1, 1, ff2), ex3),
                  pl.BlockSpec((1, ff, d), ex3),
                  pl.BlockSpec((1, 1, d), ex3)],
        out_specs=pl.BlockSpec((bm, d), lambda i, be, nu: (i, 0)),
    )
    return pl.pallas_call(
        _expert_kernel,
        grid_spec=grid_spec,
        out_shape=jax.ShapeDtypeStruct((n_rows, d), F32),
        compiler_params=_cparams(("arbitrary",)),
        name="moe_experts",
    )(block_e, n_used, xbuf, w1, b1, w2, b2)


def _combine_kernel(dest_ref, gate_ref, h1_ref, fw_ref, ybuf_ref, o_ref, rows_ref, sem, *, tt):
    def row_copy(tok, slot):
        d = dest_ref[0, 0, tok * TOP_K + slot]
        return pltpu.make_async_copy(ybuf_ref.at[pl.ds(d, 1), :], rows_ref.at[slot, pl.ds(tok, 1), :], sem)

    def start(tok, carry):
        for slot in range(TOP_K):
            row_copy(tok, slot).start()
        return carry

    def wait(tok, carry):
        for slot in range(TOP_K):
            row_copy(tok, slot).wait()
        return carry

    lax.fori_loop(0, tt, start, 0)
    lax.fori_loop(0, tt, wait, 0)
    gate = gate_ref[...]
    h = h1_ref[...]
    for slot in range(TOP_K):
        h = h + gate[:, slot:slot + 1] * rows_ref[slot]
    ms = jnp.mean(h * h, axis=-1, keepdims=True)
    o_ref[...] = h * lax.rsqrt(ms + NORM_EPS) * fw_ref[...]


def _combine(dest3, gate_l, h1, fw, ybuf):
    t, d = h1.shape
    n_tiles, _, per = dest3.shape
    tt = per // TOP_K
    row = lambda i: (i, 0)
    return pl.pallas_call(
        functools.partial(_combine_kernel, tt=tt),
        grid=(n_tiles,),
        in_specs=[pl.BlockSpec((1, 1, per), lambda i: (i, 0, 0), memory_space=pltpu.SMEM),
                  pl.BlockSpec((tt, LANES), row),
                  pl.BlockSpec((tt, d), row),
                  pl.BlockSpec((1, d), lambda i: (0, 0)),
                  pl.BlockSpec(memory_space=pl.ANY)],
        out_specs=pl.BlockSpec((tt, d), row),
        out_shape=jax.ShapeDtypeStruct((t, d), F32),
        scratch_shapes=[pltpu.VMEM((TOP_K, tt, d), F32), pltpu.SemaphoreType.DMA(())],
        compiler_params=_cparams(("arbitrary",)),
        name="moe_combine",
    )(dest3, gate_l, h1, fw, ybuf)


def _rotary_tables(positions):
    half = ROT_DIM // 2
    inv_freq = ROPE_THETA ** (-jnp.arange(0, ROT_DIM, 2, dtype=F32) / ROT_DIM)
    ang = positions.astype(F32).reshape(-1, 1) * inv_freq
    cos = jnp.tile(jnp.cos(ang), (1, LANES // half))
    sin = jnp.tile(jnp.sin(ang), (1, LANES // half))
    dim = np.arange(LANES)[None, :] % DIFF_HEAD_DIM
    ctab = jnp.where(dim < ROT_DIM, cos, 1.0)
    satab = jnp.where(dim < half, -sin, 0.0)
    sbtab = jnp.where((dim >= half) & (dim < ROT_DIM), sin, 0.0)
    return ctab, satab, sbtab


def _pad_rows(a, rows):
    return jnp.concatenate([a, jnp.zeros((rows - a.shape[0],) + a.shape[1:], a.dtype)], axis=0)


def _layer(h, l, tabs, attn_norm_w, w_in, diff_lambda_q1, diff_lambda_k1, diff_lambda_q2, diff_lambda_k2,
           diff_subln_w, rwkv_mu, rwkv_w0, rwkv_w_up, rwkv_a0, rwkv_a_up, rwkv_g_up, rwkv_k_k, rwkv_k_a,
           rwkv_r_k, rwkv_ln_w, rwkv_ln_b, w_out, ffn_norm_w, router_w, router_b, exp_w1, exp_b1,
           exp_w2, exp_b2, final_w):
    bsz, seq, d = h.shape
    t = bsz * seq
    w = RWKV_WIDTH
    lambda_init = 0.8 - 0.6 * math.exp(-0.3 * l)
    x2 = h.reshape(t, d)
    row1 = lambda a: a.reshape(1, -1).astype(F32)

    wi = w_in[l]
    qkv_cols = 3 * DIFF_WIDTH
    wq = wi[:, :qkv_cols].astype(BF16)
    o = qkv_cols + 3 * w
    zcol = lambda n: jnp.zeros((d, n), wi.dtype)
    wr = jnp.concatenate([
        wi[:, qkv_cols:o],
        wi[:, o:o + DECAY_LORA], zcol(LORA_PAD - DECAY_LORA),
        wi[:, o + DECAY_LORA:o + DECAY_LORA + AAA_LORA], zcol(LORA_PAD - AAA_LORA),
        wi[:, o + DECAY_LORA + AAA_LORA:], zcol(LORA_PAD - GATE_LORA)], axis=1).astype(BF16)
    mu = rwkv_mu[l]
    zv = lambda n: jnp.zeros((n,), mu.dtype)
    mu_p = jnp.concatenate([
        mu[:3 * w],
        mu[3 * w:3 * w + DECAY_LORA], zv(LORA_PAD - DECAY_LORA),
        mu[3 * w + DECAY_LORA:3 * w + DECAY_LORA + AAA_LORA], zv(LORA_PAD - AAA_LORA),
        mu[3 * w + DECAY_LORA + AAA_LORA:], zv(LORA_PAD - GATE_LORA)]).reshape(1, -1)

    q, k, v, zr = _inproj(x2, row1(attn_norm_w[l]), wq, wr, *tabs)

    od = _attn(q.reshape(bsz, seq, -1), k.reshape(bsz, seq, -1), v.reshape(bsz, seq, -1),
               row1(diff_lambda_q1[l]), row1(diff_lambda_k1[l]), row1(diff_lambda_q2[l]),
               row1(diff_lambda_k2[l]), row1(diff_subln_w[l]), lambda_init)

    head = np.arange(w) // RWKV_HEAD
    ones = jnp.asarray(head[:, None] == head[None, :], BF16)
    r, lw, kmod, vv, kk, bb, g, bonus = _prep(
        zr, mu_p, row1(rwkv_w0[l]), row1(rwkv_a0[l]), row1(rwkv_k_k[l]), row1(rwkv_k_a[l]),
        row1(rwkv_r_k[l]), _pad_rows(rwkv_w_up[l].astype(F32), LORA_PAD),
        _pad_rows(rwkv_a_up[l].astype(F32), LORA_PAD), _pad_rows(rwkv_g_up[l].astype(F32), LORA_PAD),
        ones, seq)
    s3 = lambda a: a.reshape(bsz, seq, w)
    y = _scan(s3(r), s3(lw), s3(kmod), s3(vv), s3(kk), s3(bb)).reshape(t, w)

    n_e = router_w.shape[-1]
    rw = jnp.concatenate([router_w[l].astype(F32), jnp.zeros((d, LANES - n_e), F32)], axis=1)
    rb = jnp.concatenate([router_b[l].astype(F32), jnp.full((LANES - n_e,), -1e30, F32)]).reshape(1, -1)
    h1, xn, sel, idx_l, gate_l = _mix(
        od.reshape(t, -1), y, g, bonus, x2, row1(rwkv_ln_w[l]), row1(rwkv_ln_b[l]), ones,
        w_out[l].astype(BF16), row1(ffn_norm_w[l]), rw, rb)

    rank_l, count8 = _rank(sel, idx_l)
    bm = EXPERT_BLOCK
    counts = count8[0, :n_e]
    padded = (counts + bm - 1) // bm * bm
    pad_ends = jnp.cumsum(padded)
    pad_starts = pad_ends - padded
    start8 = jnp.zeros((SUBLANES, LANES), I32).at[:, :n_e].set(pad_starts[None, :])
    dest_l = _dest(idx_l, rank_l, start8)
    n_assign = t * TOP_K
    n_blocks = -(-(n_assign + n_e * (bm - 1)) // bm)
    n_rows = n_blocks * bm
    first_row = jnp.arange(n_blocks, dtype=I32) * bm
    block_e = jnp.minimum(jnp.sum(pad_ends[None, :] <= first_row[:, None], axis=1), n_e - 1).astype(I32)
    n_used = (pad_ends[-1] // bm).astype(I32).reshape(1)
    dest = dest_l[:, :TOP_K]

    dt = min(DISPATCH_TILE, t)
    xbuf = _dispatch(dest.reshape(t // dt, 1, dt * TOP_K), xn, jnp.zeros((n_rows, d), F32))

    ff2 = exp_w1.shape[-1]
    ff = ff2 // 2
    w1 = exp_w1[l].astype(BF16)
    b1 = exp_b1[l].astype(F32).reshape(n_e, 1, ff2)
    w2 = exp_w2[l].reshape(n_e, 2, ff // 2, d).transpose(0, 2, 1, 3).reshape(n_e, ff, d).astype(BF16)
    b2 = exp_b2[l].astype(F32).reshape(n_e, 1, d)
    ybuf = _experts(block_e, n_used, xbuf, w1, b1, w2, b2)

    ct = min(COMBINE_TILE, t)
    out = _combine(dest.reshape(t // ct, 1, ct * TOP_K), gate_l, h1, row1(final_w), ybuf)
    return out.reshape(bsz, seq, d)


def kernel(x, positions, attn_norm_w, w_in, diff_lambda_q1, diff_lambda_k1, diff_lambda_q2, diff_lambda_k2, diff_subln_w, rwkv_mu, rwkv_w0, rwkv_w_up, rwkv_a0, rwkv_a_up, rwkv_g_up, rwkv_k_k, rwkv_k_a, rwkv_r_k, rwkv_ln_w, rwkv_ln_b, w_out, ffn_norm_w, router_w, router_b, exp_w1, exp_b1, exp_w2, exp_b2, final_norm_w):
    depth = w_in.shape[0]
    assert depth == 1, "the final norm is fused into the last (only) layer's combine kernel"
    tabs = _rotary_tables(positions)
    return _layer(x, 0, tabs, attn_norm_w, w_in, diff_lambda_q1, diff_lambda_k1, diff_lambda_q2,
                  diff_lambda_k2, diff_subln_w, rwkv_mu, rwkv_w0, rwkv_w_up, rwkv_a0, rwkv_a_up, rwkv_g_up,
                  rwkv_k_k, rwkv_k_a, rwkv_r_k, rwkv_ln_w, rwkv_ln_b, w_out, ffn_norm_w, router_w, router_b,
                  exp_w1, exp_b1, exp_w2, exp_b2, final_norm_w)
```

```python
import functools
import math

import jax
import jax.numpy as jnp
import numpy as np
from jax import lax
from jax.experimental import pallas as pl
from jax.experimental.pallas import tpu as pltpu

F32 = jnp.float32
BF16 = jnp.bfloat16
I32 = jnp.int32

DIFF_HEAD_DIM = 64
DIFF_V_DIM = 128
DIFF_HEADS = 4
DIFF_WIDTH = DIFF_HEADS * DIFF_V_DIM
ROT_DIM = 16
ROPE_THETA = 500000.0
RWKV_HEAD = 64
RWKV_HEADS = 8
RWKV_WIDTH = RWKV_HEAD * RWKV_HEADS
DECAY_LORA = 32
AAA_LORA = 32
GATE_LORA = 96
N_EXPERTS = 32
TOP_K = 4
SWIGLU_LIMIT = 7.0
SWIGLU_ALPHA = 1.702
NORM_EPS = 1e-5
RWKV_GN_EPS = 64e-5

LANES = 128
SUBLANES = 8
VMEM_LIMIT = 56 * 1024 * 1024

ROW_TILE = 512
ATTN_TILE = 512
CHUNK = 64
SCAN_TILE = 512
SCAN_BATCH = 4
GROUP = 4 * RWKV_HEAD
EXPERT_BLOCK = 256
DISPATCH_TILE = 256
COMBINE_TILE = 256
CAST_CHUNKS = 8
LORA_PAD = LANES
ZR_COLS = 3 * RWKV_WIDTH + 3 * LORA_PAD


def _cparams(sem, flags=None):
    return pltpu.CompilerParams(dimension_semantics=sem, vmem_limit_bytes=VMEM_LIMIT, flags=flags)


def _nt(a, b):
    return lax.dot_general(a, b, (((1,), (1,)), ((), ())), preferred_element_type=F32)


def _tn(a, b):
    return lax.dot_general(a, b, (((0,), (0,)), ((), ())), preferred_element_type=F32)


def _dot(a, b):
    return jnp.dot(a, b, preferred_element_type=F32)


def _split_dot(x, w_bf16):
    hi = x.astype(BF16)
    lo = (x - hi.astype(F32)).astype(BF16)
    return _dot(hi, w_bf16) + _dot(lo, w_bf16)


def _inproj_kernel(x_ref, nw_ref, wq_ref, wr_ref, c_ref, sa_ref, sb_ref,
                   q_ref, k_ref, v_ref, zr_ref):
    x = x_ref[...]
    ms = jnp.mean(x * x, axis=-1, keepdims=True)
    u = (x * lax.rsqrt(ms + NORM_EPS) * nw_ref[...]).astype(BF16)
    zq = _dot(u, wq_ref[...])
    c = c_ref[...]
    sa = sa_ref[...]
    sb = sb_ref[...]
    scale = DIFF_HEAD_DIM ** -0.5
    for g in range(2 * DIFF_HEADS):
        zg = zq[:, g * LANES:(g + 1) * LANES]
        rot = zg * c + pltpu.roll(zg, LANES - ROT_DIM // 2, 1) * sa + pltpu.roll(zg, ROT_DIM // 2, 1) * sb
        if g < DIFF_HEADS:
            q_ref[:, g * LANES:(g + 1) * LANES] = (rot * scale).astype(BF16)
        else:
            h = g - DIFF_HEADS
            k_ref[:, h * LANES:(h + 1) * LANES] = rot.astype(BF16)
    v_ref[...] = zq[:, 2 * DIFF_WIDTH:3 * DIFF_WIDTH].astype(BF16)
    zr_ref[...] = _dot(u, wr_ref[...])


def _inproj(x2, nw, wq, wr, ctab, satab, sbtab):
    t, d = x2.shape
    tm = min(ROW_TILE, t)
    row = lambda i: (i, 0)
    fixed = lambda i: (0, 0)
    return pl.pallas_call(
        _inproj_kernel,
        grid=(t // tm,),
        in_specs=[
            pl.BlockSpec((tm, d), row),
            pl.BlockSpec((1, d), fixed),
            pl.BlockSpec(wq.shape, fixed),
            pl.BlockSpec(wr.shape, fixed),
            pl.BlockSpec((tm, LANES), row),
            pl.BlockSpec((tm, LANES), row),
            pl.BlockSpec((tm, LANES), row),
        ],
        out_specs=[
            pl.BlockSpec((tm, DIFF_WIDTH), row),
            pl.BlockSpec((tm, DIFF_WIDTH), row),
            pl.BlockSpec((tm, DIFF_WIDTH), row),
            pl.BlockSpec((tm, ZR_COLS), row),
        ],
        out_shape=[
            jax.ShapeDtypeStruct((t, DIFF_WIDTH), BF16),
            jax.ShapeDtypeStruct((t, DIFF_WIDTH), BF16),
            jax.ShapeDtypeStruct((t, DIFF_WIDTH), BF16),
            jax.ShapeDtypeStruct((t, ZR_COLS), F32),
        ],
        compiler_params=_cparams(("parallel",)),
        name="inproj",
    )(x2, nw, wq, wr, ctab, satab, sbtab)


def _attn_kernel(q_ref, k_ref, v_ref, lq1_ref, lk1_ref, lq2_ref, lk2_ref, sw_ref, o_ref,
                 *, tile, lambda_init):
    qi = pl.program_id(2)
    q = q_ref[0]
    lane = lax.broadcasted_iota(I32, (1, LANES), 1)
    first = lane < DIFF_HEAD_DIM
    zero = jnp.zeros_like(q)
    q1 = jnp.where(first, q, zero)
    q2 = jnp.where(first, zero, q)
    neg = -1e30

    def step(j, carry, masked):
        m1, l1, a1, m2, l2, a2 = carry
        kb = k_ref[0, pl.ds(j * tile, tile), :]
        vb = v_ref[0, pl.ds(j * tile, tile), :]
        s1 = _nt(q1, kb)
        s2 = _nt(q2, kb)
        if masked:
            r = lax.broadcasted_iota(I32, (tile, tile), 0)
            cidx = lax.broadcasted_iota(I32, (tile, tile), 1)
            keep = cidx <= r
            s1 = jnp.where(keep, s1, neg)
            s2 = jnp.where(keep, s2, neg)
        out = []
        for s, m, l, a in ((s1, m1, l1, a1), (s2, m2, l2, a2)):
            mn = jnp.maximum(m, jnp.max(s, axis=-1, keepdims=True))
            alpha = jnp.exp(m - mn)
            p = jnp.exp(s - mn)
            ln = alpha * l + jnp.sum(p, axis=-1, keepdims=True)
            an = alpha * a + _dot(p.astype(BF16), vb)
            out += [mn, ln, an]
        return tuple(out)

    init = (jnp.full((tile, 1), neg, F32), jnp.zeros((tile, 1), F32), jnp.zeros((tile, LANES), F32)) * 2
    carry = lax.fori_loop(0, qi, lambda j, c: step(j, c, False), init)
    m1, l1, a1, m2, l2, a2 = step(qi, carry, True)
    lam = (jnp.exp(jnp.sum(lq1_ref[...] * lk1_ref[...], axis=-1, keepdims=True))
           - jnp.exp(jnp.sum(lq2_ref[...] * lk2_ref[...], axis=-1, keepdims=True)) + lambda_init)
    o = a1 / l1 - lam * (a2 / l2)
    ms = jnp.mean(o * o, axis=-1, keepdims=True)
    o = o * lax.rsqrt(ms + NORM_EPS) * sw_ref[...] * (1.0 - lambda_init)
    o_ref[0] = o.astype(o_ref.dtype)


def _attn(q3, k3, v3, lq1, lk1, lq2, lk2, sw, lambda_init):
    b, s, _ = q3.shape
    tile = min(ATTN_TILE, s)
    qspec = pl.BlockSpec((1, tile, LANES), lambda bi, h, i: (bi, i, h))
    kvspec = pl.BlockSpec((1, s, LANES), lambda bi, h, i: (bi, 0, h))
    vec = lambda n: pl.BlockSpec((1, n), lambda bi, h, i: (0, 0))
    return pl.pallas_call(
        functools.partial(_attn_kernel, tile=tile, lambda_init=lambda_init),
        grid=(b, DIFF_HEADS, s // tile),
        in_specs=[qspec, kvspec, kvspec, vec(DIFF_HEAD_DIM), vec(DIFF_HEAD_DIM), vec(DIFF_HEAD_DIM),
                  vec(DIFF_HEAD_DIM), vec(DIFF_V_DIM)],
        out_specs=qspec,
        out_shape=jax.ShapeDtypeStruct((b, s, DIFF_WIDTH), BF16),
        compiler_params=_cparams(("parallel", "parallel", "parallel")),
        name="attn",
    )(q3, k3, v3, lq1, lk1, lq2, lk2, sw)


def _prep_kernel(z_ref, zp_ref, mu_ref, w0_ref, a0_ref, kk_ref, ka_ref, rk_ref,
                 wup_ref, aup_ref, gup_ref, ones_ref,
                 r_ref, lw_ref, k_ref, v_ref, kkn_ref, b_ref, g_ref, bonus_ref, *, tiles_per_seq):
    i = pl.program_id(0)
    z = z_ref[...]
    tm = z.shape[0]
    rows = lax.broadcasted_iota(I32, (tm, 1), 0)
    prev = zp_ref[SUBLANES - 1:SUBLANES, :]
    prev = jnp.where(i % tiles_per_seq == 0, jnp.zeros_like(prev), prev)
    shifted = jnp.where(rows == 0, prev, pltpu.roll(z, 1, 0))
    zf = z + mu_ref[...] * (shifted - z)
    w = RWKV_WIDTH
    r = zf[:, 0:w]
    k = zf[:, w:2 * w]
    v = zf[:, 2 * w:3 * w]
    wd = zf[:, 3 * w:3 * w + LORA_PAD]
    ad = zf[:, 3 * w + LORA_PAD:3 * w + 2 * LORA_PAD]
    gd = zf[:, 3 * w + 2 * LORA_PAD:3 * w + 3 * LORA_PAD]
    hp = lax.Precision.HIGHEST
    pre = w0_ref[...] + jnp.dot(jnp.tanh(wd), wup_ref[...], precision=hp, preferred_element_type=F32)
    neg = -pre
    softplus = jnp.maximum(neg, 0.0) + jnp.log(1.0 + jnp.exp(-jnp.abs(neg)))
    wlog = -softplus - 0.5
    lw_ref[...] = -jnp.exp(wlog)
    a = jax.nn.sigmoid(a0_ref[...] + jnp.dot(ad, aup_ref[...], precision=hp, preferred_element_type=F32))
    g_ref[...] = jnp.dot(jax.nn.sigmoid(gd), gup_ref[...], precision=hp, preferred_element_type=F32)
    ones = ones_ref[...]
    kk = k * kk_ref[...]
    norm = jnp.sqrt(_split_dot(kk * kk, ones))
    kk = kk / jnp.maximum(norm, 1e-12)
    k = k * (1.0 + (a - 1.0) * ka_ref[...])
    r_ref[...] = r
    k_ref[...] = k
    v_ref[...] = v
    kkn_ref[...] = kk
    b_ref[...] = kk * a
    bonus_ref[...] = _split_dot(r * k * rk_ref[...], ones) * v


def _prep(zr, mu_p, w0, a0, k_k, k_a, rk, wup, aup, gup, ones, seq):
    t = zr.shape[0]
    tm = min(ROW_TILE, seq)
    w = RWKV_WIDTH
    row = lambda i: (i, 0)
    fixed = lambda i: (0, 0)
    per = tm // SUBLANES
    vecw = pl.BlockSpec((1, w), fixed)
    out = pl.BlockSpec((tm, w), row)
    return pl.pallas_call(
        functools.partial(_prep_kernel, tiles_per_seq=seq // tm),
        grid=(t // tm,),
        in_specs=[
            pl.BlockSpec((tm, ZR_COLS), row),
            pl.BlockSpec((SUBLANES, ZR_COLS), lambda i: (jnp.maximum(i * per - 1, 0), 0)),
            pl.BlockSpec((1, ZR_COLS), fixed),
            vecw, vecw, vecw, vecw, vecw,
            pl.BlockSpec((LORA_PAD, w), fixed),
            pl.BlockSpec((LORA_PAD, w), fixed),
            pl.BlockSpec((LORA_PAD, w), fixed),
            pl.BlockSpec((w, w), fixed),
        ],
        out_specs=[out] * 8,
        out_shape=[jax.ShapeDtypeStruct((t, w), F32)] * 8,
        compiler_params=_cparams(("parallel",)),
        name="rwkv_prep",
    )(zr, zr, mu_p, w0, a0, k_k, k_a, rk, wup, aup, gup, ones)


def _scan_kernel(r_ref, lw_ref, k_ref, v_ref, kk_ref, b_ref, y_ref, state_ref,
                 wr_s, ut_s, utt_s, arb_s, pv_s, bh_s, vk_s, wt_s, *, n_chunks):
    L = CHUNK
    G = GROUP
    n_groups = RWKV_WIDTH // G

    @pl.when(pl.program_id(1) == 0)
    def _():
        state_ref[...] = jnp.zeros_like(state_ref)

    row = lax.broadcasted_iota(I32, (L, G), 0)
    colr = lax.broadcasted_iota(I32, (L, G), 1) & (L - 1)
    strict = (colr < row).astype(F32)
    incl = (colr <= row).astype(F32)
    eye = (colr == row).astype(F32)
    eye_l = (lax.broadcasted_iota(I32, (L, L), 0) == lax.broadcasted_iota(I32, (L, L), 1)).astype(BF16)
    br = lax.broadcasted_iota(I32, (G, G), 0) >> 6
    bc = lax.broadcasted_iota(I32, (G, G), 1) >> 6
    block = (br == bc).astype(F32)
    block_bf = block.astype(BF16)
    rows1 = lax.broadcasted_iota(I32, (L, 1), 0)

    def stack4(x):
        xb = x.astype(BF16)
        return jnp.concatenate([xb, xb, xb, xb], axis=0) * block_bf

    def cat(a, b):
        return jnp.concatenate([a, b], axis=0).astype(BF16)

    def precompute(it, carry):
        chains = [(cl, g) for cl in range(SCAN_BATCH) for g in range(n_groups)]
        each = lambda f, *lists: [f(*args) for args in zip(*lists)]

        def load(ref):
            out = []
            for cl, g in chains:
                base = pl.multiple_of((it * SCAN_BATCH + cl) * L, L)
                out.append(ref[0, pl.ds(base, L), g * G:(g + 1) * G])
            return out

        r, lw, k, v, kk, b = load(r_ref), load(lw_ref), load(k_ref), load(v_ref), load(kk_ref), load(b_ref)

        def cumsum(x):
            sh = 1
            while sh < L:
                x = x + jnp.where(rows1 >= sh, pltpu.roll(x, sh, 0), 0.0)
                sh *= 2
            return x

        cs = each(cumsum, lw)
        tot = each(lambda c: c[L - 1:L, :], cs)
        a_hat = each(lambda kk_, c, l: -kk_ * jnp.exp(c - l), kk, cs, lw)
        r_hat = each(lambda r_, c: r_ * jnp.exp(c), r, cs)
        w_inv = each(lambda c: jnp.exp(-c), cs)
        w_end = each(lambda t_, c: jnp.exp(t_ - c), tot, cs)
        lhs = each(cat, a_hat, r_hat)
        ab = each(lambda l_, b_, wi: _nt(l_, stack4(b_ * wi)), lhs, b, w_inv)
        ak = each(lambda l_, k_, wi: _nt(l_, stack4(k_ * wi)), lhs, k, w_inv)
        a_ab = each(lambda x: x[:L] * strict, ab)
        a_rb = each(lambda x: (x[L:] * incl).astype(BF16), ab)
        a_k = each(lambda x: cat(x[:L] * strict, x[L:] * incl), ak)
        t_mat = each(lambda a: eye + a, a_ab)
        p_mat = each(lambda a: _dot(a.astype(BF16), stack4(a)), a_ab)
        for _ in range(4):
            tp = each(lambda t_, p_: _dot(cat(t_, p_), stack4(p_)), t_mat, p_mat)
            t_mat = each(lambda t_, x: t_ + x[:L], t_mat, tp)
            p_mat = each(lambda x: x[L:], tp)
        t_bf = each(lambda t_, p_: (t_ + _dot(t_.astype(BF16), stack4(p_))).astype(BF16), t_mat, p_mat)
        av = each(lambda a, v_: _dot(a, stack4(v_)), a_k, v)
        w_til = each(lambda t_, a: _dot(t_, stack4(a)), t_bf, a_hat)
        u_til = each(lambda t_, x: _dot(t_, stack4(x[:L])), t_bf, av)
        u_til_t = each(lambda u_: _tn(u_.astype(BF16), eye_l), u_til)
        vk = each(lambda v_, k_, we: _tn(v_.astype(BF16), (k_ * we).astype(BF16)) * block, v, k, w_end)
        for n, (cl, g) in enumerate(chains):
            slot = (it * SCAN_BATCH + cl) * n_groups + g
            wr_s[slot] = cat(w_til[n], r_hat[n])
            ut_s[slot] = u_til[n]
            utt_s[slot] = u_til_t[n]
            arb_s[slot] = a_rb[n]
            pv_s[slot] = av[n][L:]
            bh_s[slot] = (b[n] * w_end[n]).astype(BF16)
            vk_s[slot] = vk[n]
            wt_s[slot] = jnp.broadcast_to(jnp.exp(tot[n]), (SUBLANES, G))
        return carry

    lax.fori_loop(0, n_chunks // SCAN_BATCH, precompute, 0)

    def recur(c, carry):
        gs = range(n_groups)
        slots = [c * n_groups + g for g in gs]
        s0 = [state_ref[g] for g in gs]
        s0b = [s.astype(BF16) for s in s0]
        wr = [wr_s[sl] for sl in slots]
        u_t = [_nt(s0b[g], wr[g][:L]) + utt_s[slots[g]] for g in gs]
        ub = [_dot(u_t[g].astype(BF16), bh_s[slots[g]]) for g in gs]
        for g in gs:
            state_ref[g] = s0[g] * wt_s[slots[g]][0:1, :] + ub[g] * block + vk_s[slots[g]]
        uy = [_nt(wr[g], s0b[g]) for g in gs]
        u = [uy[g][:L] + ut_s[slots[g]] for g in gs]
        base = pl.multiple_of(c * L, L)
        for g in gs:
            y = uy[g][L:] + _dot(arb_s[slots[g]], stack4(u[g])) + pv_s[slots[g]]
            y_ref[0, pl.ds(base, L), g * G:(g + 1) * G] = y
        return carry

    lax.fori_loop(0, n_chunks, recur, 0)


def _scan(r3, lw3, k3, v3, kk3, b3):
    bsz, s, w = r3.shape
    ts = min(SCAN_TILE, s)
    n_chunks = ts // CHUNK
    slots = n_chunks * (w // GROUP)
    L, G = CHUNK, GROUP
    spec = pl.BlockSpec((1, ts, w), lambda bi, i: (bi, i, 0))
    return pl.pallas_call(
        functools.partial(_scan_kernel, n_chunks=n_chunks),
        grid=(bsz, s // ts),
        in_specs=[spec] * 6,
        out_specs=spec,
        out_shape=jax.ShapeDtypeStruct((bsz, s, w), F32),
        scratch_shapes=[pltpu.VMEM((w // GROUP, G, G), F32),
                        pltpu.VMEM((slots, 2 * L, G), BF16),
                        pltpu.VMEM((slots, L, G), F32),
                        pltpu.VMEM((slots, G, L), F32),
                        pltpu.VMEM((slots, L, G), BF16),
                        pltpu.VMEM((slots, L, G), F32),
                        pltpu.VMEM((slots, L, G), BF16),
                        pltpu.VMEM((slots, G, G), F32),
                        pltpu.VMEM((slots, SUBLANES, G), F32)],
        compiler_params=_cparams(("parallel", "arbitrary")),
        name="rwkv_scan",
    )(r3, lw3, k3, v3, kk3, b3)


def _mix_kernel(od_ref, y_ref, g_ref, bonus_ref, x_ref, lnw_ref, lnb_ref, ones_ref, wo_ref,
                fw_ref, rw_ref, rb_ref,
                h1_ref, xn_ref, sel_ref, idx_ref, gate_ref):
    ones = ones_ref[...]
    y = y_ref[...]
    inv_n = 1.0 / RWKV_HEAD
    mean = _split_dot(y, ones) * inv_n
    d = y - mean
    var = _split_dot(d * d, ones) * inv_n
    yn = d * lax.rsqrt(var + RWKV_GN_EPS) * lnw_ref[...] + lnb_ref[...]
    orw = ((yn + bonus_ref[...]) * g_ref[...]).astype(BF16)
    h1 = (x_ref[...] + _dot(od_ref[...], wo_ref[0:DIFF_WIDTH, :])
          + _dot(orw, wo_ref[DIFF_WIDTH:DIFF_WIDTH + RWKV_WIDTH, :]))
    h1_ref[...] = h1
    ms = jnp.mean(h1 * h1, axis=-1, keepdims=True)
    xn = h1 * lax.rsqrt(ms + NORM_EPS) * fw_ref[...]
    xn_ref[...] = xn
    logits = jnp.dot(xn, rw_ref[...], precision=lax.Precision.HIGHEST,
                     preferred_element_type=F32) + rb_ref[...]
    tm = logits.shape[0]
    lane = lax.broadcasted_iota(I32, (tm, LANES), 1).astype(F32)
    work = logits
    sel = jnp.zeros((tm, LANES), F32)
    idx_l = jnp.zeros((tm, LANES), F32)
    val_l = jnp.zeros((tm, LANES), F32)
    top = None
    for kslot in range(TOP_K):
        m = jnp.max(work, axis=-1, keepdims=True)
        pick = jnp.min(jnp.where(work == m, lane, float(LANES)), axis=-1, keepdims=True)
        hit = lane == pick
        sel = jnp.where(hit, 1.0, sel)
        idx_l = jnp.where(lane == kslot, pick, idx_l)
        if top is None:
            top = m
        val_l = jnp.where(lane == kslot, jnp.exp(m - top), val_l)
        work = jnp.where(hit, -jnp.inf, work)
    sel_ref[...] = sel
    idx_ref[...] = idx_l.astype(I32)
    gate_ref[...] = val_l / jnp.sum(val_l, axis=-1, keepdims=True)


def _mix(od, y, g, bonus, x2, lnw, lnb, ones, wo, fw, rw, rb):
    t, d = x2.shape
    tm = min(ROW_TILE, t)
    w = RWKV_WIDTH
    row = lambda i: (i, 0)
    fixed = lambda i: (0, 0)
    rs = lambda n: pl.BlockSpec((tm, n), row)
    return pl.pallas_call(
        _mix_kernel,
        grid=(t // tm,),
        in_specs=[rs(DIFF_WIDTH), rs(w), rs(w), rs(w), rs(d),
                  pl.BlockSpec((1, w), fixed), pl.BlockSpec((1, w), fixed),
                  pl.BlockSpec((w, w), fixed), pl.BlockSpec(wo.shape, fixed),
                  pl.BlockSpec((1, d), fixed), pl.BlockSpec(rw.shape, fixed),
                  pl.BlockSpec((1, LANES), fixed)],
        out_specs=[rs(d), rs(d), rs(LANES), rs(LANES), rs(LANES)],
        out_shape=[jax.ShapeDtypeStruct((t, d), F32), jax.ShapeDtypeStruct((t, d), F32),
                   jax.ShapeDtypeStruct((t, LANES), F32), jax.ShapeDtypeStruct((t, LANES), I32),
                   jax.ShapeDtypeStruct((t, LANES), F32)],
        compiler_params=_cparams(("parallel",)),
        name="mix_router",
    )(od, y, g, bonus, x2, lnw, lnb, ones, wo, fw, rw, rb)


def _rank_kernel(sel_ref, idx_ref, rank_ref, count_ref, run_ref):
    i = pl.program_id(0)

    @pl.when(i == 0)
    def _():
        run_ref[...] = jnp.zeros_like(run_ref)

    sel = sel_ref[...]
    tt = sel.shape[0]
    r = lax.broadcasted_iota(I32, (tt, tt), 0)
    c = lax.broadcasted_iota(I32, (tt, tt), 1)
    lower = (c < r).astype(BF16)
    run = run_ref[0:1, :]
    before = _dot(lower, sel.astype(BF16)) + run
    lane = lax.broadcasted_iota(I32, (tt, LANES), 1).astype(F32)
    idx = idx_ref[...].astype(F32)
    out = jnp.zeros((tt, LANES), F32)
    for kslot in range(TOP_K):
        e = jnp.sum(jnp.where(lane == kslot, idx, 0.0), axis=-1, keepdims=True)
        rk = jnp.sum(jnp.where(lane == e, before, 0.0), axis=-1, keepdims=True)
        out = jnp.where(lane == kslot, rk, out)
    rank_ref[...] = out.astype(I32)
    run = run + jnp.sum(sel, axis=0, keepdims=True)
    run_ref[...] = jnp.broadcast_to(run, run_ref.shape)
    count_ref[...] = jnp.broadcast_to(run, count_ref.shape).astype(I32)


def _rank(sel, idx_l):
    t = sel.shape[0]
    tt = min(ROW_TILE, t)
    row = lambda i: (i, 0)
    return pl.pallas_call(
        _rank_kernel,
        grid=(t // tt,),
        in_specs=[pl.BlockSpec((tt, LANES), row), pl.BlockSpec((tt, LANES), row)],
        out_specs=[pl.BlockSpec((tt, LANES), row), pl.BlockSpec((SUBLANES, LANES), lambda i: (0, 0))],
        out_shape=[jax.ShapeDtypeStruct((t, LANES), I32), jax.ShapeDtypeStruct((SUBLANES, LANES), I32)],
        scratch_shapes=[pltpu.VMEM((SUBLANES, LANES), F32)],
        compiler_params=_cparams(("arbitrary",)),
        name="moe_rank",
    )(sel, idx_l)


def _dest_kernel(idx_ref, rank_ref, start_ref, dest_ref):
    idx = idx_ref[...].astype(F32)
    tt = idx.shape[0]
    lane = lax.broadcasted_iota(I32, (tt, LANES), 1).astype(F32)
    start = start_ref[0:1, :].astype(F32)
    out = jnp.zeros((tt, LANES), F32)
    for kslot in range(TOP_K):
        e = jnp.sum(jnp.where(lane == kslot, idx, 0.0), axis=-1, keepdims=True)
        st = jnp.sum(jnp.where(lane == e, start, 0.0), axis=-1, keepdims=True)
        out = jnp.where(lane == kslot, st, out)
    dest_ref[...] = out.astype(I32) + rank_ref[...]


def _dest(idx_l, rank_l, start8):
    t = idx_l.shape[0]
    tt = min(ROW_TILE, t)
    row = lambda i: (i, 0)
    return pl.pallas_call(
        _dest_kernel,
        grid=(t // tt,),
        in_specs=[pl.BlockSpec((tt, LANES), row), pl.BlockSpec((tt, LANES), row),
                  pl.BlockSpec((SUBLANES, LANES), lambda i: (0, 0))],
        out_specs=pl.BlockSpec((tt, LANES), row),
        out_shape=jax.ShapeDtypeStruct((t, LANES), I32),
        compiler_params=_cparams(("parallel",)),
        name="moe_dest",
    )(idx_l, rank_l, start8)


def _dispatch_kernel(dest_ref, xn_ref, buf_in_ref, buf_ref, sem, *, tt):
    del buf_in_ref

    def row_copy(tok, slot):
        d = dest_ref[0, 0, tok * TOP_K + slot]
        return pltpu.make_async_copy(xn_ref.at[pl.ds(tok, 1), :], buf_ref.at[pl.ds(d, 1), :], sem)

    def start(tok, carry):
        for slot in range(TOP_K):
            row_copy(tok, slot).start()
        return carry

    def wait(tok, carry):
        for slot in range(TOP_K):
            row_copy(tok, slot).wait()
        return carry

    lax.fori_loop(0, tt, start, 0, unroll=4)
    lax.fori_loop(0, tt, wait, 0)


def _dispatch(dest3, xn, buf0):
    t, d = xn.shape
    n_tiles, _, per = dest3.shape
    tt = per // TOP_K
    return pl.pallas_call(
        functools.partial(_dispatch_kernel, tt=tt),
        grid=(n_tiles,),
        in_specs=[pl.BlockSpec((1, 1, per), lambda i: (i, 0, 0), memory_space=pltpu.SMEM),
                  pl.BlockSpec((tt, d), lambda i: (i, 0)),
                  pl.BlockSpec(memory_space=pl.ANY)],
        out_specs=pl.BlockSpec(memory_space=pl.ANY),
        out_shape=jax.ShapeDtypeStruct(buf0.shape, buf0.dtype),
        scratch_shapes=[pltpu.SemaphoreType.DMA(())],
        input_output_aliases={2: 0},
        compiler_params=_cparams(("arbitrary",)),
        name="moe_dispatch",
    )(dest3, xn, buf0)


def _expert_kernel(be_ref, first_ref, nused_ref, x_ref, w1_ref, b1_ref, w2_ref, b2_ref, y_ref,
                   w1b_ref, w2f_ref, w2b_ref):
    i = pl.program_id(0)
    ff = w2_ref.shape[1]
    used = i < nused_ref[0]

    @pl.when(jnp.logical_and(used, first_ref[i] == 1))
    def _():
        rows = w1_ref.shape[1] // CAST_CHUNKS

        def cast1(c, carry):
            r0 = pl.multiple_of(c * rows, rows)
            w1b_ref[pl.ds(r0, rows), :] = w1_ref[0, pl.ds(r0, rows), :].astype(BF16)
            return carry

        lax.fori_loop(0, CAST_CHUNKS, cast1, 0)
        for g in range(w2_ref.shape[2] // LANES):
            cols = slice(g * LANES, (g + 1) * LANES)
            w2f_ref[pl.ds(0, ff // 2, stride=2), :] = w2_ref[0, 0:ff // 2, cols]
            w2f_ref[pl.ds(1, ff // 2, stride=2), :] = w2_ref[0, ff // 2:ff, cols]
            w2b_ref[:, cols] = w2f_ref[...].astype(BF16)

    @pl.when(used)
    def _():
        x = x_ref[...].astype(BF16)
        hid = _dot(x, w1b_ref[...]) + b1_ref[0]
        even = (lax.broadcasted_iota(I32, (1, LANES), 1) & 1) == 0

        def act_even(g):
            hg = hid[:, g * LANES:(g + 1) * LANES]
            glu = jnp.minimum(hg, SWIGLU_LIMIT)
            lin = jnp.clip(hg, -SWIGLU_LIMIT, SWIGLU_LIMIT) + 1.0
            return glu * jax.nn.sigmoid(SWIGLU_ALPHA * glu) * pltpu.roll(lin, LANES - 1, 1)

        half = ff // LANES
        act = jnp.concatenate(
            [jnp.where(even, act_even(g), pltpu.roll(act_even(g + half), 1, 1)) for g in range(half)], axis=1)
        y_ref[...] = _dot(act.astype(BF16), w2b_ref[...]) + b2_ref[0]

    @pl.when(jnp.logical_not(used))
    def _():
        y_ref[...] = jnp.zeros_like(y_ref)


def _experts(block_e, first, n_used, xbuf, w1, b1, w2, b2):
    n_rows, d = xbuf.shape
    bm = EXPERT_BLOCK
    n_blocks = n_rows // bm
    e, _, ff2 = w1.shape
    ff = ff2 // 2
    blk = lambda i, be, fi, nu: (jnp.maximum(jnp.minimum(i, nu[0] - 1), 0), 0)
    ex3 = lambda i, be, fi, nu: (be[i], 0, 0)
    grid_spec = pltpu.PrefetchScalarGridSpec(
        num_scalar_prefetch=3,
        grid=(n_blocks,),
        in_specs=[pl.BlockSpec((bm, d), blk),
                  pl.BlockSpec((1, d, ff2), ex3),
                  pl.BlockSpec((1, 1, ff2), ex3),
                  pl.BlockSpec((1, ff, d), ex3),
                  pl.BlockSpec((1, 1, d), ex3)],
        out_specs=pl.BlockSpec((bm, d), lambda i, be, fi, nu: (i, 0)),
        scratch_shapes=[pltpu.VMEM((d, ff2), BF16), pltpu.VMEM((ff, LANES), F32), pltpu.VMEM((ff, d), BF16)],
    )
    return pl.pallas_call(
        _expert_kernel,
        grid_spec=grid_spec,
        out_shape=jax.ShapeDtypeStruct((n_rows, d), F32),
        compiler_params=_cparams(("arbitrary",)),
        name="moe_experts",
    )(block_e, first, n_used, xbuf, w1, b1, w2, b2)


def _combine_kernel(dest_ref, gate_ref, h1_ref, fw_ref, ybuf_ref, o_ref, rows_ref, sem, *, tt):
    def row_copy(tok, slot):
        d = dest_ref[0, 0, tok * TOP_K + slot]
        return pltpu.make_async_copy(ybuf_ref.at[pl.ds(d, 1), :], rows_ref.at[slot, pl.ds(tok, 1), :], sem)

    def start(tok, carry):
        for slot in range(TOP_K):
            row_copy(tok, slot).start()
        return carry

    def wait(tok, carry):
        for slot in range(TOP_K):
            row_copy(tok, slot).wait()
        return carry

    lax.fori_loop(0, tt, start, 0, unroll=4)
    lax.fori_loop(0, tt, wait, 0)
    gate = gate_ref[...]
    h = h1_ref[...]
    for slot in range(TOP_K):
        h = h + gate[:, slot:slot + 1] * rows_ref[slot]
    ms = jnp.mean(h * h, axis=-1, keepdims=True)
    o_ref[...] = h * lax.rsqrt(ms + NORM_EPS) * fw_ref[...]


def _combine(dest3, gate_l, h1, fw, ybuf):
    t, d = h1.shape
    n_tiles, _, per = dest3.shape
    tt = per // TOP_K
    row = lambda i: (i, 0)
    return pl.pallas_call(
        functools.partial(_combine_kernel, tt=tt),
        grid=(n_tiles,),
        in_specs=[pl.BlockSpec((1, 1, per), lambda i: (i, 0, 0), memory_space=pltpu.SMEM),
                  pl.BlockSpec((tt, LANES), row),
                  pl.BlockSpec((tt, d), row),
                  pl.BlockSpec((1, d), lambda i: (0, 0)),
                  pl.BlockSpec(memory_space=pl.ANY)],
        out_specs=pl.BlockSpec((tt, d), row),
        out_shape=jax.ShapeDtypeStruct((t, d), F32),
        scratch_shapes=[pltpu.VMEM((TOP_K, tt, d), F32), pltpu.SemaphoreType.DMA(())],
        compiler_params=_cparams(("arbitrary",)),
        name="moe_combine",
    )(dest3, gate_l, h1, fw, ybuf)


def _rotary_tables(positions):
    half = ROT_DIM // 2
    inv_freq = ROPE_THETA ** (-jnp.arange(0, ROT_DIM, 2, dtype=F32) / ROT_DIM)
    ang = positions.astype(F32).reshape(-1, 1) * inv_freq
    cos = jnp.tile(jnp.cos(ang), (1, LANES // half))
    sin = jnp.tile(jnp.sin(ang), (1, LANES // half))
    dim = np.arange(LANES)[None, :] % DIFF_HEAD_DIM
    ctab = jnp.where(dim < ROT_DIM, cos, 1.0)
    satab = jnp.where(dim < half, -sin, 0.0)
    sbtab = jnp.where((dim >= half) & (dim < ROT_DIM), sin, 0.0)
    return ctab, satab, sbtab


def _pad_rows(a, rows):
    return jnp.concatenate([a, jnp.zeros((rows - a.shape[0],) + a.shape[1:], a.dtype)], axis=0)


def _layer(h, l, tabs, attn_norm_w, w_in, diff_lambda_q1, diff_lambda_k1, diff_lambda_q2, diff_lambda_k2,
           diff_subln_w, rwkv_mu, rwkv_w0, rwkv_w_up, rwkv_a0, rwkv_a_up, rwkv_g_up, rwkv_k_k, rwkv_k_a,
           rwkv_r_k, rwkv_ln_w, rwkv_ln_b, w_out, ffn_norm_w, router_w, router_b, exp_w1, exp_b1,
           exp_w2, exp_b2, final_w):
    bsz, seq, d = h.shape
    t = bsz * seq
    w = RWKV_WIDTH
    lambda_init = 0.8 - 0.6 * math.exp(-0.3 * l)
    x2 = h.reshape(t, d)
    row1 = lambda a: a.reshape(1, -1).astype(F32)

    wi = w_in[l]
    qkv_cols = 3 * DIFF_WIDTH
    wq = wi[:, :qkv_cols].astype(BF16)
    o = qkv_cols + 3 * w
    zcol = lambda n: jnp.zeros((d, n), wi.dtype)
    wr = jnp.concatenate([
        wi[:, qkv_cols:o],
        wi[:, o:o + DECAY_LORA], zcol(LORA_PAD - DECAY_LORA),
        wi[:, o + DECAY_LORA:o + DECAY_LORA + AAA_LORA], zcol(LORA_PAD - AAA_LORA),
        wi[:, o + DECAY_LORA + AAA_LORA:], zcol(LORA_PAD - GATE_LORA)], axis=1).astype(BF16)
    mu = rwkv_mu[l]
    zv = lambda n: jnp.zeros((n,), mu.dtype)
    mu_p = jnp.concatenate([
        mu[:3 * w],
        mu[3 * w:3 * w + DECAY_LORA], zv(LORA_PAD - DECAY_LORA),
        mu[3 * w + DECAY_LORA:3 * w + DECAY_LORA + AAA_LORA], zv(LORA_PAD - AAA_LORA),
        mu[3 * w + DECAY_LORA + AAA_LORA:], zv(LORA_PAD - GATE_LORA)]).reshape(1, -1)

    q, k, v, zr = _inproj(x2, row1(attn_norm_w[l]), wq, wr, *tabs)

    od = _attn(q.reshape(bsz, seq, -1), k.reshape(bsz, seq, -1), v.reshape(bsz, seq, -1),
               row1(diff_lambda_q1[l]), row1(diff_lambda_k1[l]), row1(diff_lambda_q2[l]),
               row1(diff_lambda_k2[l]), row1(diff_subln_w[l]), lambda_init)

    head = np.arange(w) // RWKV_HEAD
    ones = jnp.asarray(head[:, None] == head[None, :], BF16)
    r, lw, kmod, vv, kk, bb, g, bonus = _prep(
        zr, mu_p, row1(rwkv_w0[l]), row1(rwkv_a0[l]), row1(rwkv_k_k[l]), row1(rwkv_k_a[l]),
        row1(rwkv_r_k[l]), _pad_rows(rwkv_w_up[l].astype(F32), LORA_PAD),
        _pad_rows(rwkv_a_up[l].astype(F32), LORA_PAD), _pad_rows(rwkv_g_up[l].astype(F32), LORA_PAD),
        ones, seq)
    s3 = lambda a: a.reshape(bsz, seq, w)
    y = _scan(s3(r), s3(lw), s3(kmod), s3(vv), s3(kk), s3(bb)).reshape(t, w)

    n_e = router_w.shape[-1]
    rw = jnp.concatenate([router_w[l].astype(F32), jnp.zeros((d, LANES - n_e), F32)], axis=1)
    rb = jnp.concatenate([router_b[l].astype(F32), jnp.full((LANES - n_e,), -1e30, F32)]).reshape(1, -1)
    h1, xn, sel, idx_l, gate_l = _mix(
        od.reshape(t, -1), y, g, bonus, x2, row1(rwkv_ln_w[l]), row1(rwkv_ln_b[l]), ones,
        w_out[l].astype(BF16), row1(ffn_norm_w[l]), rw, rb)

    rank_l, count8 = _rank(sel, idx_l)
    bm = EXPERT_BLOCK
    counts = count8[0, :n_e]
    padded = (counts + bm - 1) // bm * bm
    pad_ends = jnp.cumsum(padded)
    pad_starts = pad_ends - padded
    start8 = jnp.zeros((SUBLANES, LANES), I32).at[:, :n_e].set(pad_starts[None, :])
    dest_l = _dest(idx_l, rank_l, start8)
    n_assign = t * TOP_K
    n_blocks = -(-(n_assign + n_e * (bm - 1)) // bm)
    n_rows = n_blocks * bm
    first_row = jnp.arange(n_blocks, dtype=I32) * bm
    block_e = jnp.minimum(jnp.sum(pad_ends[None, :] <= first_row[:, None], axis=1), n_e - 1).astype(I32)
    n_used = (pad_ends[-1] // bm).astype(I32).reshape(1)
    dest = dest_l[:, :TOP_K]

    dt = min(DISPATCH_TILE, t)
    xbuf = _dispatch(dest.reshape(t // dt, 1, dt * TOP_K), xn, jnp.zeros((n_rows, d), F32))

    ff2 = exp_w1.shape[-1]
    b1 = exp_b1[l].astype(F32).reshape(n_e, 1, ff2)
    b2 = exp_b2[l].astype(F32).reshape(n_e, 1, d)
    first = jnp.concatenate([jnp.ones((1,), I32), (block_e[1:] != block_e[:-1]).astype(I32)])
    ybuf = _experts(block_e, first, n_used, xbuf, exp_w1[l].astype(F32), b1, exp_w2[l].astype(F32), b2)

    ct = min(COMBINE_TILE, t)
    out = _combine(dest.reshape(t // ct, 1, ct * TOP_K), gate_l, h1, row1(final_w), ybuf)
    return out.reshape(bsz, seq, d)


def kernel(x, positions, attn_norm_w, w_in, diff_lambda_q1, diff_lambda_k1, diff_lambda_q2, diff_lambda_k2, diff_subln_w, rwkv_mu, rwkv_w0, rwkv_w_up, rwkv_a0, rwkv_a_up, rwkv_g_up, rwkv_k_k, rwkv_k_a, rwkv_r_k, rwkv_ln_w, rwkv_ln_b, w_out, ffn_norm_w, router_w, router_b, exp_w1, exp_b1, exp_w2, exp_b2, final_norm_w):
    depth = w_in.shape[0]
    assert depth == 1, "the final norm is fused into the last (only) layer's combine kernel"
    tabs = _rotary_tables(positions)
    return _layer(x, 0, tabs, attn_norm_w, w_in, diff_lambda_q1, diff_lambda_k1, diff_lambda_q2,
                  diff_lambda_k2, diff_subln_w, rwkv_mu, rwkv_w0, rwkv_w_up, rwkv_a0, rwkv_a_up, rwkv_g_up,
                  rwkv_k_k, rwkv_k_a, rwkv_r_k, rwkv_ln_w, rwkv_ln_b, w_out, ffn_norm_w, router_w, router_b,
                  exp_w1, exp_b1, exp_w2, exp_b2, final_norm_w)
```

```python
import functools
import math

import jax
import jax.numpy as jnp
import numpy as np
from jax import lax
from jax.experimental import pallas as pl
from jax.experimental.pallas import tpu as pltpu

F32 = jnp.float32
BF16 = jnp.bfloat16
I32 = jnp.int32

DIFF_HEAD_DIM = 64
DIFF_V_DIM = 128
DIFF_HEADS = 4
DIFF_WIDTH = DIFF_HEADS * DIFF_V_DIM
ROT_DIM = 16
ROPE_THETA = 500000.0
RWKV_HEAD = 64
RWKV_HEADS = 8
RWKV_WIDTH = RWKV_HEAD * RWKV_HEADS
DECAY_LORA = 32
AAA_LORA = 32
GATE_LORA = 96
N_EXPERTS = 32
TOP_K = 4
SWIGLU_LIMIT = 7.0
SWIGLU_ALPHA = 1.702
NORM_EPS = 1e-5
RWKV_GN_EPS = 64e-5

LANES = 128
SUBLANES = 8
VMEM_LIMIT = 56 * 1024 * 1024

ROW_TILE = 512
ATTN_TILE = 512
CHUNK = 64
SCAN_TILE = 512
SCAN_BATCH = 4
GROUP = 4 * RWKV_HEAD
EXPERT_BLOCK = 512
DISPATCH_TILE = 256
COMBINE_TILE = 256
CAST_CHUNKS = 8
LORA_PAD = LANES
ZR_COLS = 3 * RWKV_WIDTH + 3 * LORA_PAD


def _cparams(sem, flags=None):
    return pltpu.CompilerParams(dimension_semantics=sem, vmem_limit_bytes=VMEM_LIMIT, flags=flags)


def _nt(a, b):
    return lax.dot_general(a, b, (((1,), (1,)), ((), ())), preferred_element_type=F32)


def _tn(a, b):
    return lax.dot_general(a, b, (((0,), (0,)), ((), ())), preferred_element_type=F32)


def _dot(a, b):
    return jnp.dot(a, b, preferred_element_type=F32)


def _split(x):
    hi = x.astype(BF16)
    return hi, (x - hi.astype(F32)).astype(BF16)


def _split_dot(x, w_bf16):
    hi, lo = _split(x)
    return _dot(hi, w_bf16) + _dot(lo, w_bf16)


def _split3_dot(x, w_hi, w_lo):
    hi, lo = _split(x)
    return _dot(hi, w_hi) + (_dot(hi, w_lo) + _dot(lo, w_hi))


def _inproj_kernel(x_ref, nw_ref, wq_ref, wr_ref, c_ref, sa_ref, sb_ref,
                   q_ref, k_ref, v_ref, zr_ref):
    x = x_ref[...]
    ms = jnp.mean(x * x, axis=-1, keepdims=True)
    u = (x * lax.rsqrt(ms + NORM_EPS) * nw_ref[...]).astype(BF16)
    zq = _dot(u, wq_ref[...])
    c = c_ref[...]
    sa = sa_ref[...]
    sb = sb_ref[...]
    scale = DIFF_HEAD_DIM ** -0.5
    for g in range(2 * DIFF_HEADS):
        zg = zq[:, g * LANES:(g + 1) * LANES]
        rot = zg * c + pltpu.roll(zg, LANES - ROT_DIM // 2, 1) * sa + pltpu.roll(zg, ROT_DIM // 2, 1) * sb
        if g < DIFF_HEADS:
            q_ref[:, g * LANES:(g + 1) * LANES] = (rot * scale).astype(BF16)
        else:
            h = g - DIFF_HEADS
            k_ref[:, h * LANES:(h + 1) * LANES] = rot.astype(BF16)
    v_ref[...] = zq[:, 2 * DIFF_WIDTH:3 * DIFF_WIDTH].astype(BF16)
    zr_ref[...] = _dot(u, wr_ref[...])


def _inproj(x2, nw, wq, wr, ctab, satab, sbtab):
    t, d = x2.shape
    tm = min(ROW_TILE, t)
    row = lambda i: (i, 0)
    fixed = lambda i: (0, 0)
    return pl.pallas_call(
        _inproj_kernel,
        grid=(t // tm,),
        in_specs=[
            pl.BlockSpec((tm, d), row),
            pl.BlockSpec((1, d), fixed),
            pl.BlockSpec(wq.shape, fixed),
            pl.BlockSpec(wr.shape, fixed),
            pl.BlockSpec((tm, LANES), row),
            pl.BlockSpec((tm, LANES), row),
            pl.BlockSpec((tm, LANES), row),
        ],
        out_specs=[
            pl.BlockSpec((tm, DIFF_WIDTH), row),
            pl.BlockSpec((tm, DIFF_WIDTH), row),
            pl.BlockSpec((tm, DIFF_WIDTH), row),
            pl.BlockSpec((tm, ZR_COLS), row),
        ],
        out_shape=[
            jax.ShapeDtypeStruct((t, DIFF_WIDTH), BF16),
            jax.ShapeDtypeStruct((t, DIFF_WIDTH), BF16),
            jax.ShapeDtypeStruct((t, DIFF_WIDTH), BF16),
            jax.ShapeDtypeStruct((t, ZR_COLS), F32),
        ],
        compiler_params=_cparams(("parallel",)),
        name="inproj",
    )(x2, nw, wq, wr, ctab, satab, sbtab)


def _attn_kernel(q_ref, k_ref, v_ref, lq1_ref, lk1_ref, lq2_ref, lk2_ref, sw_ref, o_ref,
                 *, tile, lambda_init):
    qi = pl.program_id(2)
    q = q_ref[0]
    lane = lax.broadcasted_iota(I32, (1, LANES), 1)
    first = lane < DIFF_HEAD_DIM
    zero = jnp.zeros_like(q)
    q1 = jnp.where(first, q, zero)
    q2 = jnp.where(first, zero, q)
    neg = -1e30

    def step(j, carry, masked):
        m1, l1, a1, m2, l2, a2 = carry
        kb = k_ref[0, pl.ds(j * tile, tile), :]
        vb = v_ref[0, pl.ds(j * tile, tile), :]
        s1 = _nt(q1, kb)
        s2 = _nt(q2, kb)
        if masked:
            r = lax.broadcasted_iota(I32, (tile, tile), 0)
            cidx = lax.broadcasted_iota(I32, (tile, tile), 1)
            keep = cidx <= r
            s1 = jnp.where(keep, s1, neg)
            s2 = jnp.where(keep, s2, neg)
        out = []
        for s, m, l, a in ((s1, m1, l1, a1), (s2, m2, l2, a2)):
            mn = jnp.maximum(m, jnp.max(s, axis=-1, keepdims=True))
            alpha = jnp.exp(m - mn)
            p = jnp.exp(s - mn)
            ln = alpha * l + jnp.sum(p, axis=-1, keepdims=True)
            an = alpha * a + _dot(p.astype(BF16), vb)
            out += [mn, ln, an]
        return tuple(out)

    init = (jnp.full((tile, 1), neg, F32), jnp.zeros((tile, 1), F32), jnp.zeros((tile, LANES), F32)) * 2
    carry = lax.fori_loop(0, qi, lambda j, c: step(j, c, False), init)
    m1, l1, a1, m2, l2, a2 = step(qi, carry, True)
    lam = (jnp.exp(jnp.sum(lq1_ref[...] * lk1_ref[...], axis=-1, keepdims=True))
           - jnp.exp(jnp.sum(lq2_ref[...] * lk2_ref[...], axis=-1, keepdims=True)) + lambda_init)
    o = a1 / l1 - lam * (a2 / l2)
    ms = jnp.mean(o * o, axis=-1, keepdims=True)
    o = o * lax.rsqrt(ms + NORM_EPS) * sw_ref[...] * (1.0 - lambda_init)
    o_ref[0] = o.astype(o_ref.dtype)


def _attn(q3, k3, v3, lq1, lk1, lq2, lk2, sw, lambda_init):
    b, s, _ = q3.shape
    tile = min(ATTN_TILE, s)
    qspec = pl.BlockSpec((1, tile, LANES), lambda bi, h, i: (bi, i, h))
    kvspec = pl.BlockSpec((1, s, LANES), lambda bi, h, i: (bi, 0, h))
    vec = lambda n: pl.BlockSpec((1, n), lambda bi, h, i: (0, 0))
    return pl.pallas_call(
        functools.partial(_attn_kernel, tile=tile, lambda_init=lambda_init),
        grid=(b, DIFF_HEADS, s // tile),
        in_specs=[qspec, kvspec, kvspec, vec(DIFF_HEAD_DIM), vec(DIFF_HEAD_DIM), vec(DIFF_HEAD_DIM),
                  vec(DIFF_HEAD_DIM), vec(DIFF_V_DIM)],
        out_specs=qspec,
        out_shape=jax.ShapeDtypeStruct((b, s, DIFF_WIDTH), BF16),
        compiler_params=_cparams(("parallel", "parallel", "parallel")),
        name="attn",
    )(q3, k3, v3, lq1, lk1, lq2, lk2, sw)


def _prep_kernel(z_ref, zp_ref, mu_ref, w0_ref, a0_ref, kk_ref, ka_ref, rk_ref,
                 wup_ref, aup_ref, gup_ref, ones_ref,
                 r_ref, lw_ref, k_ref, v_ref, kkn_ref, b_ref, g_ref, bonus_ref, *, tiles_per_seq):
    i = pl.program_id(0)
    z = z_ref[...]
    tm = z.shape[0]
    rows = lax.broadcasted_iota(I32, (tm, 1), 0)
    prev = zp_ref[SUBLANES - 1:SUBLANES, :]
    prev = jnp.where(i % tiles_per_seq == 0, jnp.zeros_like(prev), prev)
    shifted = jnp.where(rows == 0, prev, pltpu.roll(z, 1, 0))
    zf = z + mu_ref[...] * (shifted - z)
    w = RWKV_WIDTH
    r = zf[:, 0:w]
    k = zf[:, w:2 * w]
    v = zf[:, 2 * w:3 * w]
    wd = zf[:, 3 * w:3 * w + LORA_PAD]
    ad = zf[:, 3 * w + LORA_PAD:3 * w + 2 * LORA_PAD]
    gd = zf[:, 3 * w + 2 * LORA_PAD:3 * w + 3 * LORA_PAD]
    pre = w0_ref[...] + _split_dot(jnp.tanh(wd), wup_ref[...])
    neg = -pre
    softplus = jnp.maximum(neg, 0.0) + jnp.log(1.0 + jnp.exp(-jnp.abs(neg)))
    wlog = -softplus - 0.5
    lw_ref[...] = -jnp.exp(wlog)
    a = jax.nn.sigmoid(a0_ref[...] + _split_dot(ad, aup_ref[...]))
    g_ref[...] = _split_dot(jax.nn.sigmoid(gd), gup_ref[...])
    ones = ones_ref[...]
    kk = k * kk_ref[...]
    norm = jnp.sqrt(_split_dot(kk * kk, ones))
    kk = kk / jnp.maximum(norm, 1e-12)
    k = k * (1.0 + (a - 1.0) * ka_ref[...])
    r_ref[...] = r
    k_ref[...] = k
    v_ref[...] = v
    kkn_ref[...] = kk
    b_ref[...] = kk * a
    bonus_ref[...] = _split_dot(r * k * rk_ref[...], ones) * v


def _prep(zr, mu_p, w0, a0, k_k, k_a, rk, wup, aup, gup, ones, seq):
    t = zr.shape[0]
    tm = min(ROW_TILE, seq)
    w = RWKV_WIDTH
    row = lambda i: (i, 0)
    fixed = lambda i: (0, 0)
    per = tm // SUBLANES
    vecw = pl.BlockSpec((1, w), fixed)
    out = pl.BlockSpec((tm, w), row)
    return pl.pallas_call(
        functools.partial(_prep_kernel, tiles_per_seq=seq // tm),
        grid=(t // tm,),
        in_specs=[
            pl.BlockSpec((tm, ZR_COLS), row),
            pl.BlockSpec((SUBLANES, ZR_COLS), lambda i: (jnp.maximum(i * per - 1, 0), 0)),
            pl.BlockSpec((1, ZR_COLS), fixed),
            vecw, vecw, vecw, vecw, vecw,
            pl.BlockSpec((LORA_PAD, w), fixed),
            pl.BlockSpec((LORA_PAD, w), fixed),
            pl.BlockSpec((LORA_PAD, w), fixed),
            pl.BlockSpec((w, w), fixed),
        ],
        out_specs=[out] * 8,
        out_shape=[jax.ShapeDtypeStruct((t, w), F32)] * 8,
        compiler_params=_cparams(("parallel",)),
        name="rwkv_prep",
    )(zr, zr, mu_p, w0, a0, k_k, k_a, rk, wup, aup, gup, ones)


def _scan_kernel(r_ref, lw_ref, k_ref, v_ref, kk_ref, b_ref, y_ref, state_ref,
                 wr_s, ut_s, utt_s, arb_s, pv_s, bh_s, vk_s, wt_s, *, n_chunks):
    L = CHUNK
    G = GROUP
    n_groups = RWKV_WIDTH // G

    @pl.when(pl.program_id(1) == 0)
    def _():
        state_ref[...] = jnp.zeros_like(state_ref)

    row = lax.broadcasted_iota(I32, (L, G), 0)
    colr = lax.broadcasted_iota(I32, (L, G), 1) & (L - 1)
    strict = (colr < row).astype(F32)
    incl = (colr <= row).astype(F32)
    eye = (colr == row).astype(F32)
    eye_l = (lax.broadcasted_iota(I32, (L, L), 0) == lax.broadcasted_iota(I32, (L, L), 1)).astype(BF16)
    br = lax.broadcasted_iota(I32, (G, G), 0) >> 6
    bc = lax.broadcasted_iota(I32, (G, G), 1) >> 6
    block = (br == bc).astype(F32)
    block_bf = block.astype(BF16)
    rows1 = lax.broadcasted_iota(I32, (L, 1), 0)

    def stack4(x):
        xb = x.astype(BF16)
        return jnp.concatenate([xb, xb, xb, xb], axis=0) * block_bf

    def cat(a, b):
        return jnp.concatenate([a, b], axis=0).astype(BF16)

    def precompute(it, carry):
        chains = [(cl, g) for cl in range(SCAN_BATCH) for g in range(n_groups)]
        each = lambda f, *lists: [f(*args) for args in zip(*lists)]

        def load(ref):
            out = []
            for cl, g in chains:
                base = pl.multiple_of((it * SCAN_BATCH + cl) * L, L)
                out.append(ref[0, pl.ds(base, L), g * G:(g + 1) * G])
            return out

        r, lw, k, v, kk, b = load(r_ref), load(lw_ref), load(k_ref), load(v_ref), load(kk_ref), load(b_ref)

        def cumsum(x):
            sh = 1
            while sh < L:
                x = x + jnp.where(rows1 >= sh, pltpu.roll(x, sh, 0), 0.0)
                sh *= 2
            return x

        cs = each(cumsum, lw)
        tot = each(lambda c: c[L - 1:L, :], cs)
        a_hat = each(lambda kk_, c, l: -kk_ * jnp.exp(c - l), kk, cs, lw)
        r_hat = each(lambda r_, c: r_ * jnp.exp(c), r, cs)
        w_inv = each(lambda c: jnp.exp(-c), cs)
        w_end = each(lambda t_, c: jnp.exp(t_ - c), tot, cs)
        lhs = each(cat, a_hat, r_hat)
        ab = each(lambda l_, b_, wi: _nt(l_, stack4(b_ * wi)), lhs, b, w_inv)
        ak = each(lambda l_, k_, wi: _nt(l_, stack4(k_ * wi)), lhs, k, w_inv)
        a_ab = each(lambda x: x[:L] * strict, ab)
        a_rb = each(lambda x: (x[L:] * incl).astype(BF16), ab)
        a_k = each(lambda x: cat(x[:L] * strict, x[L:] * incl), ak)
        t_mat = each(lambda a: eye + a, a_ab)
        p_mat = each(lambda a: _dot(a.astype(BF16), stack4(a)), a_ab)
        for _ in range(4):
            tp = each(lambda t_, p_: _dot(cat(t_, p_), stack4(p_)), t_mat, p_mat)
            t_mat = each(lambda t_, x: t_ + x[:L], t_mat, tp)
            p_mat = each(lambda x: x[L:], tp)
        t_bf = each(lambda t_, p_: (t_ + _dot(t_.astype(BF16), stack4(p_))).astype(BF16), t_mat, p_mat)
        av = each(lambda a, v_: _dot(a, stack4(v_)), a_k, v)
        w_til = each(lambda t_, a: _dot(t_, stack4(a)), t_bf, a_hat)
        u_til = each(lambda t_, x: _dot(t_, stack4(x[:L])), t_bf, av)
        u_til_t = each(lambda u_: _tn(u_.astype(BF16), eye_l), u_til)
        vk = each(lambda v_, k_, we: _tn(v_.astype(BF16), (k_ * we).astype(BF16)) * block, v, k, w_end)
        for n, (cl, g) in enumerate(chains):
            slot = (it * SCAN_BATCH + cl) * n_groups + g
            wr_s[slot] = cat(w_til[n], r_hat[n])
            ut_s[slot] = u_til[n]
            utt_s[slot] = u_til_t[n]
            arb_s[slot] = a_rb[n]
            pv_s[slot] = av[n][L:]
            bh_s[slot] = (b[n] * w_end[n]).astype(BF16)
            vk_s[slot] = vk[n]
            wt_s[slot] = jnp.broadcast_to(jnp.exp(tot[n]), (SUBLANES, G))
        return carry

    lax.fori_loop(0, n_chunks // SCAN_BATCH, precompute, 0)

    def recur(c, carry):
        gs = range(n_groups)
        slots = [c * n_groups + g for g in gs]
        s0 = [state_ref[g] for g in gs]
        s0b = [s.astype(BF16) for s in s0]
        wr = [wr_s[sl] for sl in slots]
        u_t = [_nt(s0b[g], wr[g][:L]) + utt_s[slots[g]] for g in gs]
        ub = [_dot(u_t[g].astype(BF16), bh_s[slots[g]]) for g in gs]
        for g in gs:
            state_ref[g] = s0[g] * wt_s[slots[g]][0:1, :] + ub[g] * block + vk_s[slots[g]]
        uy = [_nt(wr[g], s0b[g]) for g in gs]
        u = [uy[g][:L] + ut_s[slots[g]] for g in gs]
        base = pl.multiple_of(c * L, L)
        for g in gs:
            y = uy[g][L:] + _dot(arb_s[slots[g]], stack4(u[g])) + pv_s[slots[g]]
            y_ref[0, pl.ds(base, L), g * G:(g + 1) * G] = y
        return carry

    lax.fori_loop(0, n_chunks, recur, 0)


def _scan(r3, lw3, k3, v3, kk3, b3):
    bsz, s, w = r3.shape
    ts = min(SCAN_TILE, s)
    n_chunks = ts // CHUNK
    slots = n_chunks * (w // GROUP)
    L, G = CHUNK, GROUP
    spec = pl.BlockSpec((1, ts, w), lambda bi, i: (bi, i, 0))
    return pl.pallas_call(
        functools.partial(_scan_kernel, n_chunks=n_chunks),
        grid=(bsz, s // ts),
        in_specs=[spec] * 6,
        out_specs=spec,
        out_shape=jax.ShapeDtypeStruct((bsz, s, w), F32),
        scratch_shapes=[pltpu.VMEM((w // GROUP, G, G), F32),
                        pltpu.VMEM((slots, 2 * L, G), BF16),
                        pltpu.VMEM((slots, L, G), F32),
                        pltpu.VMEM((slots, G, L), F32),
                        pltpu.VMEM((slots, L, G), BF16),
                        pltpu.VMEM((slots, L, G), F32),
                        pltpu.VMEM((slots, L, G), BF16),
                        pltpu.VMEM((slots, G, G), F32),
                        pltpu.VMEM((slots, SUBLANES, G), F32)],
        compiler_params=_cparams(("parallel", "arbitrary")),
        name="rwkv_scan",
    )(r3, lw3, k3, v3, kk3, b3)


def _mix_kernel(od_ref, y_ref, g_ref, bonus_ref, x_ref, lnw_ref, lnb_ref, ones_ref, wo_ref,
                fw_ref, rw_ref, rb_ref,
                h1_ref, xn_ref, sel_ref, idx_ref, gate_ref):
    ones = ones_ref[...]
    y = y_ref[...]
    inv_n = 1.0 / RWKV_HEAD
    mean = _split_dot(y, ones) * inv_n
    d = y - mean
    var = _split_dot(d * d, ones) * inv_n
    yn = d * lax.rsqrt(var + RWKV_GN_EPS) * lnw_ref[...] + lnb_ref[...]
    orw = ((yn + bonus_ref[...]) * g_ref[...]).astype(BF16)
    h1 = (x_ref[...] + _dot(od_ref[...], wo_ref[0:DIFF_WIDTH, :])
          + _dot(orw, wo_ref[DIFF_WIDTH:DIFF_WIDTH + RWKV_WIDTH, :]))
    h1_ref[...] = h1
    ms = jnp.mean(h1 * h1, axis=-1, keepdims=True)
    xn = h1 * lax.rsqrt(ms + NORM_EPS) * fw_ref[...]
    xn_ref[...] = xn
    logits = _split3_dot(xn, rw_ref[0], rw_ref[1]) + rb_ref[...]
    tm = logits.shape[0]
    lane = lax.broadcasted_iota(I32, (tm, LANES), 1).astype(F32)
    work = logits
    sel = jnp.zeros((tm, LANES), F32)
    idx_l = jnp.zeros((tm, LANES), F32)
    val_l = jnp.zeros((tm, LANES), F32)
    top = None
    for kslot in range(TOP_K):
        m = jnp.max(work, axis=-1, keepdims=True)
        pick = jnp.min(jnp.where(work == m, lane, float(LANES)), axis=-1, keepdims=True)
        hit = lane == pick
        sel = jnp.where(hit, 1.0, sel)
        idx_l = jnp.where(lane == kslot, pick, idx_l)
        if top is None:
            top = m
        val_l = jnp.where(lane == kslot, jnp.exp(m - top), val_l)
        work = jnp.where(hit, -jnp.inf, work)
    sel_ref[...] = sel
    idx_ref[...] = idx_l.astype(I32)
    gate_ref[...] = val_l / jnp.sum(val_l, axis=-1, keepdims=True)


def _mix(od, y, g, bonus, x2, lnw, lnb, ones, wo, fw, rw, rb):
    t, d = x2.shape
    tm = min(ROW_TILE, t)
    w = RWKV_WIDTH
    row = lambda i: (i, 0)
    fixed = lambda i: (0, 0)
    rs = lambda n: pl.BlockSpec((tm, n), row)
    return pl.pallas_call(
        _mix_kernel,
        grid=(t // tm,),
        in_specs=[rs(DIFF_WIDTH), rs(w), rs(w), rs(w), rs(d),
                  pl.BlockSpec((1, w), fixed), pl.BlockSpec((1, w), fixed),
                  pl.BlockSpec((w, w), fixed), pl.BlockSpec(wo.shape, fixed),
                  pl.BlockSpec((1, d), fixed), pl.BlockSpec(rw.shape, lambda i: (0, 0, 0)),
                  pl.BlockSpec((1, LANES), fixed)],
        out_specs=[rs(d), rs(d), rs(LANES), rs(LANES), rs(LANES)],
        out_shape=[jax.ShapeDtypeStruct((t, d), F32), jax.ShapeDtypeStruct((t, d), F32),
                   jax.ShapeDtypeStruct((t, LANES), F32), jax.ShapeDtypeStruct((t, LANES), I32),
                   jax.ShapeDtypeStruct((t, LANES), F32)],
        compiler_params=_cparams(("parallel",)),
        name="mix_router",
    )(od, y, g, bonus, x2, lnw, lnb, ones, wo, fw, rw, rb)


def _rank_kernel(sel_ref, idx_ref, rank_ref, count_ref, run_ref):
    i = pl.program_id(0)

    @pl.when(i == 0)
    def _():
        run_ref[...] = jnp.zeros_like(run_ref)

    sel = sel_ref[...]
    tt = sel.shape[0]
    r = lax.broadcasted_iota(I32, (tt, tt), 0)
    c = lax.broadcasted_iota(I32, (tt, tt), 1)
    lower = (c < r).astype(BF16)
    run = run_ref[0:1, :]
    before = _dot(lower, sel.astype(BF16)) + run
    lane = lax.broadcasted_iota(I32, (tt, LANES), 1).astype(F32)
    idx = idx_ref[...].astype(F32)
    out = jnp.zeros((tt, LANES), F32)
    for kslot in range(TOP_K):
        e = jnp.sum(jnp.where(lane == kslot, idx, 0.0), axis=-1, keepdims=True)
        rk = jnp.sum(jnp.where(lane == e, before, 0.0), axis=-1, keepdims=True)
        out = jnp.where(lane == kslot, rk, out)
    rank_ref[...] = out.astype(I32)
    run = run + jnp.sum(sel, axis=0, keepdims=True)
    run_ref[...] = jnp.broadcast_to(run, run_ref.shape)
    count_ref[...] = jnp.broadcast_to(run, count_ref.shape).astype(I32)


def _rank(sel, idx_l):
    t = sel.shape[0]
    tt = min(ROW_TILE, t)
    row = lambda i: (i, 0)
    return pl.pallas_call(
        _rank_kernel,
        grid=(t // tt,),
        in_specs=[pl.BlockSpec((tt, LANES), row), pl.BlockSpec((tt, LANES), row)],
        out_specs=[pl.BlockSpec((tt, LANES), row), pl.BlockSpec((SUBLANES, LANES), lambda i: (0, 0))],
        out_shape=[jax.ShapeDtypeStruct((t, LANES), I32), jax.ShapeDtypeStruct((SUBLANES, LANES), I32)],
        scratch_shapes=[pltpu.VMEM((SUBLANES, LANES), F32)],
        compiler_params=_cparams(("arbitrary",)),
        name="moe_rank",
    )(sel, idx_l)


def _dest_kernel(idx_ref, rank_ref, start_ref, dest_ref):
    idx = idx_ref[...].astype(F32)
    tt = idx.shape[0]
    lane = lax.broadcasted_iota(I32, (tt, LANES), 1).astype(F32)
    start = start_ref[0:1, :].astype(F32)
    out = jnp.zeros((tt, LANES), F32)
    for kslot in range(TOP_K):
        e = jnp.sum(jnp.where(lane == kslot, idx, 0.0), axis=-1, keepdims=True)
        st = jnp.sum(jnp.where(lane == e, start, 0.0), axis=-1, keepdims=True)
        out = jnp.where(lane == kslot, st, out)
    dest_ref[...] = out.astype(I32) + rank_ref[...]


def _dest(idx_l, rank_l, start8):
    t = idx_l.shape[0]
    tt = min(ROW_TILE, t)
    row = lambda i: (i, 0)
    return pl.pallas_call(
        _dest_kernel,
        grid=(t // tt,),
        in_specs=[pl.BlockSpec((tt, LANES), row), pl.BlockSpec((tt, LANES), row),
                  pl.BlockSpec((SUBLANES, LANES), lambda i: (0, 0))],
        out_specs=pl.BlockSpec((tt, LANES), row),
        out_shape=jax.ShapeDtypeStruct((t, LANES), I32),
        compiler_params=_cparams(("parallel",)),
        name="moe_dest",
    )(idx_l, rank_l, start8)


def _dispatch_kernel(dest_ref, xn_ref, buf_in_ref, buf_ref, sem, *, tt):
    del buf_in_ref

    def row_copy(tok, slot):
        d = dest_ref[0, 0, tok * TOP_K + slot]
        return pltpu.make_async_copy(xn_ref.at[pl.ds(tok, 1), :], buf_ref.at[pl.ds(d, 1), :], sem)

    def start(tok, carry):
        for slot in range(TOP_K):
            row_copy(tok, slot).start()
        return carry

    def wait(tok, carry):
        for slot in range(TOP_K):
            row_copy(tok, slot).wait()
        return carry

    lax.fori_loop(0, tt, start, 0, unroll=4)
    lax.fori_loop(0, tt, wait, 0)


def _dispatch(dest3, xn, buf0):
    t, d = xn.shape
    n_tiles, _, per = dest3.shape
    tt = per // TOP_K
    return pl.pallas_call(
        functools.partial(_dispatch_kernel, tt=tt),
        grid=(n_tiles,),
        in_specs=[pl.BlockSpec((1, 1, per), lambda i: (i, 0, 0), memory_space=pltpu.SMEM),
                  pl.BlockSpec((tt, d), lambda i: (i, 0)),
                  pl.BlockSpec(memory_space=pl.ANY)],
        out_specs=pl.BlockSpec(memory_space=pl.ANY),
        out_shape=jax.ShapeDtypeStruct(buf0.shape, buf0.dtype),
        scratch_shapes=[pltpu.SemaphoreType.DMA(())],
        input_output_aliases={2: 0},
        compiler_params=_cparams(("arbitrary",)),
        name="moe_dispatch",
    )(dest3, xn, buf0)


def _expert_kernel(be_ref, first_ref, ord_ref, next_ref, nused_ref, x_ref, w1_hbm, b1_ref, w2_hbm, b2_ref,
                   y_ref, w1f_ref, w2f_ref, w1b_ref, w2i_ref, w2b_ref, sem):
    i = pl.program_id(0)
    ff = w2f_ref.shape[1]
    used = i < nused_ref[0]
    new_expert = jnp.logical_and(used, first_ref[i] == 1)

    def fetch(expert, slot):
        return (pltpu.make_async_copy(w1_hbm.at[expert], w1f_ref.at[slot], sem.at[0, slot]),
                pltpu.make_async_copy(w2_hbm.at[expert], w2f_ref.at[slot], sem.at[1, slot]))

    @pl.when(i == 0)
    def _():
        for cp in fetch(be_ref[0], 0):
            cp.start()

    @pl.when(jnp.logical_and(new_expert, next_ref[i] >= 0))
    def _():
        for cp in fetch(next_ref[i], 1 - (ord_ref[i] & 1)):
            cp.start()

    @pl.when(new_expert)
    def _():
        slot = ord_ref[i] & 1
        for cp in fetch(be_ref[i], slot):
            cp.wait()
        rows = w1f_ref.shape[1] // CAST_CHUNKS

        def cast1(c, carry):
            r0 = pl.multiple_of(c * rows, rows)
            w1b_ref[pl.ds(r0, rows), :] = w1f_ref[slot, pl.ds(r0, rows), :].astype(BF16)
            return carry

        lax.fori_loop(0, CAST_CHUNKS, cast1, 0)
        for g in range(w2f_ref.shape[2] // LANES):
            cols = slice(g * LANES, (g + 1) * LANES)
            w2i_ref[pl.ds(0, ff // 2, stride=2), :] = w2f_ref[slot, 0:ff // 2, cols]
            w2i_ref[pl.ds(1, ff // 2, stride=2), :] = w2f_ref[slot, ff // 2:ff, cols]
            w2b_ref[:, cols] = w2i_ref[...].astype(BF16)

    @pl.when(used)
    def _():
        x = x_ref[...].astype(BF16)
        hid = _dot(x, w1b_ref[...]) + b1_ref[0]
        even = (lax.broadcasted_iota(I32, (1, LANES), 1) & 1) == 0

        def act_even(g):
            hg = hid[:, g * LANES:(g + 1) * LANES]
            glu = jnp.minimum(hg, SWIGLU_LIMIT)
            lin = jnp.clip(hg, -SWIGLU_LIMIT, SWIGLU_LIMIT) + 1.0
            return glu * jax.nn.sigmoid(SWIGLU_ALPHA * glu) * pltpu.roll(lin, LANES - 1, 1)

        half = ff // LANES
        act = jnp.concatenate(
            [jnp.where(even, act_even(g), pltpu.roll(act_even(g + half), 1, 1)) for g in range(half)], axis=1)
        y_ref[...] = _dot(act.astype(BF16), w2b_ref[...]) + b2_ref[0]

    @pl.when(jnp.logical_not(used))
    def _():
        y_ref[...] = jnp.zeros_like(y_ref)


def _experts(block_e, rows_per_expert, n_used, xbuf, w1, b1, w2, b2):
    n_rows, d = xbuf.shape
    bm = EXPERT_BLOCK
    n_blocks = n_rows // bm
    e, _, ff2 = w1.shape
    ff = ff2 // 2
    has = rows_per_expert > 0
    ids = jnp.arange(e, dtype=I32)
    later = (ids[None, :] > ids[:, None]) & has[None, :]
    next_used = jnp.where(jnp.any(later, axis=1), jnp.argmax(later, axis=1), -1).astype(I32)
    ordinal = (jnp.cumsum(has.astype(I32)) - 1).astype(I32)
    first = jnp.concatenate([jnp.ones((1,), I32), (block_e[1:] != block_e[:-1]).astype(I32)])
    order = ordinal[block_e]
    next_e = next_used[block_e]
    blk =lambda i, be, fi, od, ne, nu: (jnp.maximum(jnp.minimum(i, nu[0] - 1), 0), 0)
    ex3 = lambda i, be, fi, od, ne, nu: (be[i], 0, 0)
    grid_spec = pltpu.PrefetchScalarGridSpec(
        num_scalar_prefetch=5,
        grid=(n_blocks,),
        in_specs=[pl.BlockSpec((bm, d), blk),
                  pl.BlockSpec(memory_space=pl.ANY),
                  pl.BlockSpec((1, 1, ff2), ex3),
                  pl.BlockSpec(memory_space=pl.ANY),
                  pl.BlockSpec((1, 1, d), ex3)],
        out_specs=pl.BlockSpec((bm, d), lambda i, be, fi, od, ne, nu: (i, 0)),
        scratch_shapes=[pltpu.VMEM((2, d, ff2), F32), pltpu.VMEM((2, ff, d), F32),
                        pltpu.VMEM((d, ff2), BF16), pltpu.VMEM((ff, LANES), F32), pltpu.VMEM((ff, d), BF16),
                        pltpu.SemaphoreType.DMA((2, 2))],
    )
    return pl.pallas_call(
        _expert_kernel,
        grid_spec=grid_spec,
        out_shape=jax.ShapeDtypeStruct((n_rows, d), F32),
        compiler_params=_cparams(("arbitrary",)),
        name="moe_experts",
    )(block_e, first, order, next_e, n_used, xbuf, w1, b1, w2, b2)


def _combine_kernel(dest_ref, gate_ref, h1_ref, fw_ref, ybuf_ref, o_ref, rows_ref, sem, *, tt):
    def row_copy(tok, slot):
        d = dest_ref[0, 0, tok * TOP_K + slot]
        return pltpu.make_async_copy(ybuf_ref.at[pl.ds(d, 1), :], rows_ref.at[slot, pl.ds(tok, 1), :], sem)

    def start(tok, carry):
        for slot in range(TOP_K):
            row_copy(tok, slot).start()
        return carry

    def wait(tok, carry):
        for slot in range(TOP_K):
            row_copy(tok, slot).wait()
        return carry

    lax.fori_loop(0, tt, start, 0, unroll=4)
    lax.fori_loop(0, tt, wait, 0)
    gate = gate_ref[...]
    h = h1_ref[...]
    for slot in range(TOP_K):
        h = h + gate[:, slot:slot + 1] * rows_ref[slot]
    ms = jnp.mean(h * h, axis=-1, keepdims=True)
    o_ref[...] = h * lax.rsqrt(ms + NORM_EPS) * fw_ref[...]


def _combine(dest3, gate_l, h1, fw, ybuf):
    t, d = h1.shape
    n_tiles, _, per = dest3.shape
    tt = per // TOP_K
    row = lambda i: (i, 0)
    return pl.pallas_call(
        functools.partial(_combine_kernel, tt=tt),
        grid=(n_tiles,),
        in_specs=[pl.BlockSpec((1, 1, per), lambda i: (i, 0, 0), memory_space=pltpu.SMEM),
                  pl.BlockSpec((tt, LANES), row),
                  pl.BlockSpec((tt, d), row),
                  pl.BlockSpec((1, d), lambda i: (0, 0)),
                  pl.BlockSpec(memory_space=pl.ANY)],
        out_specs=pl.BlockSpec((tt, d), row),
        out_shape=jax.ShapeDtypeStruct((t, d), F32),
        scratch_shapes=[pltpu.VMEM((TOP_K, tt, d), F32), pltpu.SemaphoreType.DMA(())],
        compiler_params=_cparams(("arbitrary",)),
        name="moe_combine",
    )(dest3, gate_l, h1, fw, ybuf)


def _rotary_tables(positions):
    half = ROT_DIM // 2
    inv_freq = ROPE_THETA ** (-jnp.arange(0, ROT_DIM, 2, dtype=F32) / ROT_DIM)
    ang = positions.astype(F32).reshape(-1, 1) * inv_freq
    cos = jnp.tile(jnp.cos(ang), (1, LANES // half))
    sin = jnp.tile(jnp.sin(ang), (1, LANES // half))
    dim = np.arange(LANES)[None, :] % DIFF_HEAD_DIM
    ctab = jnp.where(dim < ROT_DIM, cos, 1.0)
    satab = jnp.where(dim < half, -sin, 0.0)
    sbtab = jnp.where((dim >= half) & (dim < ROT_DIM), sin, 0.0)
    return ctab, satab, sbtab


def _pad_rows(a, rows):
    return jnp.concatenate([a, jnp.zeros((rows - a.shape[0],) + a.shape[1:], a.dtype)], axis=0)


def _layer(h, l, tabs, attn_norm_w, w_in, diff_lambda_q1, diff_lambda_k1, diff_lambda_q2, diff_lambda_k2,
           diff_subln_w, rwkv_mu, rwkv_w0, rwkv_w_up, rwkv_a0, rwkv_a_up, rwkv_g_up, rwkv_k_k, rwkv_k_a,
           rwkv_r_k, rwkv_ln_w, rwkv_ln_b, w_out, ffn_norm_w, router_w, router_b, exp_w1, exp_b1,
           exp_w2, exp_b2, final_w):
    bsz, seq, d = h.shape
    t = bsz * seq
    w = RWKV_WIDTH
    lambda_init = 0.8 - 0.6 * math.exp(-0.3 * l)
    x2 = h.reshape(t, d)
    row1 = lambda a: a.reshape(1, -1).astype(F32)

    wi = w_in[l]
    qkv_cols = 3 * DIFF_WIDTH
    wq = wi[:, :qkv_cols].astype(BF16)
    o = qkv_cols + 3 * w
    zcol = lambda n: jnp.zeros((d, n), wi.dtype)
    wr = jnp.concatenate([
        wi[:, qkv_cols:o],
        wi[:, o:o + DECAY_LORA], zcol(LORA_PAD - DECAY_LORA),
        wi[:, o + DECAY_LORA:o + DECAY_LORA + AAA_LORA], zcol(LORA_PAD - AAA_LORA),
        wi[:, o + DECAY_LORA + AAA_LORA:], zcol(LORA_PAD - GATE_LORA)], axis=1).astype(BF16)
    mu = rwkv_mu[l]
    zv = lambda n: jnp.zeros((n,), mu.dtype)
    mu_p = jnp.concatenate([
        mu[:3 * w],
        mu[3 * w:3 * w + DECAY_LORA], zv(LORA_PAD - DECAY_LORA),
        mu[3 * w + DECAY_LORA:3 * w + DECAY_LORA + AAA_LORA], zv(LORA_PAD - AAA_LORA),
        mu[3 * w + DECAY_LORA + AAA_LORA:], zv(LORA_PAD - GATE_LORA)]).reshape(1, -1)

    q, k, v, zr = _inproj(x2, row1(attn_norm_w[l]), wq, wr, *tabs)

    od = _attn(q.reshape(bsz, seq, -1), k.reshape(bsz, seq, -1), v.reshape(bsz, seq, -1),
               row1(diff_lambda_q1[l]), row1(diff_lambda_k1[l]), row1(diff_lambda_q2[l]),
               row1(diff_lambda_k2[l]), row1(diff_subln_w[l]), lambda_init)

    head = np.arange(w) // RWKV_HEAD
    ones = jnp.asarray(head[:, None] == head[None, :], BF16)
    r, lw, kmod, vv, kk, bb, g, bonus = _prep(
        zr, mu_p, row1(rwkv_w0[l]), row1(rwkv_a0[l]), row1(rwkv_k_k[l]), row1(rwkv_k_a[l]),
        row1(rwkv_r_k[l]), _pad_rows(rwkv_w_up[l].astype(BF16), LORA_PAD),
        _pad_rows(rwkv_a_up[l].astype(BF16), LORA_PAD), _pad_rows(rwkv_g_up[l].astype(BF16), LORA_PAD),
        ones, seq)
    s3 = lambda a: a.reshape(bsz, seq, w)
    y = _scan(s3(r), s3(lw), s3(kmod), s3(vv), s3(kk), s3(bb)).reshape(t, w)

    n_e = router_w.shape[-1]
    rw = jnp.concatenate([router_w[l].astype(F32), jnp.zeros((d, LANES - n_e), F32)], axis=1)
    rw_hi = rw.astype(BF16)
    rw = jnp.stack([rw_hi, (rw - rw_hi.astype(F32)).astype(BF16)])
    rb = jnp.concatenate([router_b[l].astype(F32), jnp.full((LANES - n_e,), -1e30, F32)]).reshape(1, -1)
    h1, xn, sel, idx_l, gate_l = _mix(
        od.reshape(t, -1), y, g, bonus, x2, row1(rwkv_ln_w[l]), row1(rwkv_ln_b[l]), ones,
        w_out[l].astype(BF16), row1(ffn_norm_w[l]), rw, rb)

    rank_l, count8 = _rank(sel, idx_l)
    bm = EXPERT_BLOCK
    counts = count8[0, :n_e]
    padded = (counts + bm - 1) // bm * bm
    pad_ends = jnp.cumsum(padded)
    pad_starts = pad_ends - padded
    start8 = jnp.zeros((SUBLANES, LANES), I32).at[:, :n_e].set(pad_starts[None, :])
    dest_l = _dest(idx_l, rank_l, start8)
    n_assign = t * TOP_K
    n_blocks = -(-(n_assign + n_e * (bm - 1)) // bm)
    n_rows = n_blocks * bm
    first_row = jnp.arange(n_blocks, dtype=I32) * bm
    block_e = jnp.minimum(jnp.sum(pad_ends[None, :] <= first_row[:, None], axis=1), n_e - 1).astype(I32)
    n_used = (pad_ends[-1] // bm).astype(I32).reshape(1)
    dest = dest_l[:, :TOP_K]

    dt = min(DISPATCH_TILE, t)
    xbuf = _dispatch(dest.reshape(t // dt, 1, dt * TOP_K), xn, jnp.zeros((n_rows, d), F32))

    ff2 = exp_w1.shape[-1]
    b1 = exp_b1[l].astype(F32).reshape(n_e, 1, ff2)
    b2 = exp_b2[l].astype(F32).reshape(n_e, 1, d)
    ybuf = _experts(block_e, padded, n_used, xbuf, exp_w1[l].astype(F32), b1, exp_w2[l].astype(F32), b2)

    ct = min(COMBINE_TILE, t)
    out = _combine(dest.reshape(t // ct, 1, ct * TOP_K), gate_l, h1, row1(final_w), ybuf)
    return out.reshape(bsz, seq, d)


def kernel(x, positions, attn_norm_w, w_in, diff_lambda_q1, diff_lambda_k1, diff_lambda_q2, diff_lambda_k2, diff_subln_w, rwkv_mu, rwkv_w0, rwkv_w_up, rwkv_a0, rwkv_a_up, rwkv_g_up, rwkv_k_k, rwkv_k_a, rwkv_r_k, rwkv_ln_w, rwkv_ln_b, w_out, ffn_norm_w, router_w, router_b, exp_w1, exp_b1, exp_w2, exp_b2, final_norm_w):
    depth = w_in.shape[0]
    assert depth == 1, "the final norm is fused into the last (only) layer's combine kernel"
    tabs = _rotary_tables(positions)
    return _layer(x, 0, tabs, attn_norm_w, w_in, diff_lambda_q1, diff_lambda_k1, diff_lambda_q2,
                  diff_lambda_k2, diff_subln_w, rwkv_mu, rwkv_w0, rwkv_w_up, rwkv_a0, rwkv_a_up, rwkv_g_up,
                  rwkv_k_k, rwkv_k_a, rwkv_r_k, rwkv_ln_w, rwkv_ln_b, w_out, ffn_norm_w, router_w, router_b,
                  exp_w1, exp_b1, exp_w2, exp_b2, final_norm_w)
```

```python
import functools
import math

import jax
import jax.numpy as jnp
import numpy as np
from jax import lax
from jax.experimental import pallas as pl
from jax.experimental.pallas import tpu as pltpu

F32 = jnp.float32
BF16 = jnp.bfloat16
I32 = jnp.int32

DIFF_HEAD_DIM = 64
DIFF_V_DIM = 128
DIFF_HEADS = 4
DIFF_WIDTH = DIFF_HEADS * DIFF_V_DIM
ROT_DIM = 16
ROPE_THETA = 500000.0
RWKV_HEAD = 64
RWKV_HEADS = 8
RWKV_WIDTH = RWKV_HEAD * RWKV_HEADS
DECAY_LORA = 32
AAA_LORA = 32
GATE_LORA = 96
N_EXPERTS = 32
TOP_K = 4
SWIGLU_LIMIT = 7.0
SWIGLU_ALPHA = 1.702
NORM_EPS = 1e-5
RWKV_GN_EPS = 64e-5

LANES = 128
SUBLANES = 8
VMEM_LIMIT = 56 * 1024 * 1024

ROW_TILE = 512
ATTN_TILE = 512
CHUNK = 64
SCAN_TILE = 512
SCAN_BATCH = 4
GROUP = 4 * RWKV_HEAD
EXPERT_BLOCK = 512
DISPATCH_TILE = 256
COMBINE_TILE = 256
CAST_CHUNKS = 8
LORA_PAD = LANES
ZR_COLS = 3 * RWKV_WIDTH + 3 * LORA_PAD


def _cparams(sem, flags=None):
    return pltpu.CompilerParams(dimension_semantics=sem, vmem_limit_bytes=VMEM_LIMIT, flags=flags)


def _nt(a, b):
    return lax.dot_general(a, b, (((1,), (1,)), ((), ())), preferred_element_type=F32)


def _tn(a, b):
    return lax.dot_general(a, b, (((0,), (0,)), ((), ())), preferred_element_type=F32)


def _dot(a, b):
    return jnp.dot(a, b, preferred_element_type=F32)


def _split(x):
    hi = x.astype(BF16)
    return hi, (x - hi.astype(F32)).astype(BF16)


def _split_dot(x, w_bf16):
    hi, lo = _split(x)
    return _dot(hi, w_bf16) + _dot(lo, w_bf16)


def _split3_dot(x, w_hi, w_lo):
    hi, lo = _split(x)
    return _dot(hi, w_hi) + (_dot(hi, w_lo) + _dot(lo, w_hi))


def _inproj_kernel(x_ref, nw_ref, wq_ref, wr_ref, c_ref, sa_ref, sb_ref,
                   q_ref, k_ref, v_ref, zr_ref):
    x = x_ref[...]
    ms = jnp.mean(x * x, axis=-1, keepdims=True)
    u = (x * lax.rsqrt(ms + NORM_EPS) * nw_ref[...]).astype(BF16)
    zq = _dot(u, wq_ref[...])
    c = c_ref[...]
    sa = sa_ref[...]
    sb = sb_ref[...]
    scale = DIFF_HEAD_DIM ** -0.5
    for g in range(2 * DIFF_HEADS):
        zg = zq[:, g * LANES:(g + 1) * LANES]
        rot = zg * c + pltpu.roll(zg, LANES - ROT_DIM // 2, 1) * sa + pltpu.roll(zg, ROT_DIM // 2, 1) * sb
        if g < DIFF_HEADS:
            q_ref[:, g * LANES:(g + 1) * LANES] = (rot * scale).astype(BF16)
        else:
            h = g - DIFF_HEADS
            k_ref[:, h * LANES:(h + 1) * LANES] = rot.astype(BF16)
    v_ref[...] = zq[:, 2 * DIFF_WIDTH:3 * DIFF_WIDTH].astype(BF16)
    zr_ref[...] = _dot(u, wr_ref[...])


def _inproj(x2, nw, wq, wr, ctab, satab, sbtab):
    t, d = x2.shape
    tm = min(ROW_TILE, t)
    row = lambda i: (i, 0)
    fixed = lambda i: (0, 0)
    return pl.pallas_call(
        _inproj_kernel,
        grid=(t // tm,),
        in_specs=[
            pl.BlockSpec((tm, d), row),
            pl.BlockSpec((1, d), fixed),
            pl.BlockSpec(wq.shape, fixed),
            pl.BlockSpec(wr.shape, fixed),
            pl.BlockSpec((tm, LANES), row),
            pl.BlockSpec((tm, LANES), row),
            pl.BlockSpec((tm, LANES), row),
        ],
        out_specs=[
            pl.BlockSpec((tm, DIFF_WIDTH), row),
            pl.BlockSpec((tm, DIFF_WIDTH), row),
            pl.BlockSpec((tm, DIFF_WIDTH), row),
            pl.BlockSpec((tm, ZR_COLS), row),
        ],
        out_shape=[
            jax.ShapeDtypeStruct((t, DIFF_WIDTH), BF16),
            jax.ShapeDtypeStruct((t, DIFF_WIDTH), BF16),
            jax.ShapeDtypeStruct((t, DIFF_WIDTH), BF16),
            jax.ShapeDtypeStruct((t, ZR_COLS), F32),
        ],
        compiler_params=_cparams(("parallel",)),
        name="inproj",
    )(x2, nw, wq, wr, ctab, satab, sbtab)


def _attn_kernel(q_ref, k_ref, v_ref, lq1_ref, lk1_ref, lq2_ref, lk2_ref, sw_ref, o_ref,
                 *, tile, lambda_init):
    qi = pl.program_id(2)
    q = q_ref[0]
    lane = lax.broadcasted_iota(I32, (1, LANES), 1)
    first = lane < DIFF_HEAD_DIM
    zero = jnp.zeros_like(q)
    q1 = jnp.where(first, q, zero)
    q2 = jnp.where(first, zero, q)
    neg = -1e30

    def step(j, carry, masked):
        m1, l1, a1, m2, l2, a2 = carry
        kb = k_ref[0, pl.ds(j * tile, tile), :]
        vb = v_ref[0, pl.ds(j * tile, tile), :]
        s1 = _nt(q1, kb)
        s2 = _nt(q2, kb)
        if masked:
            r = lax.broadcasted_iota(I32, (tile, tile), 0)
            cidx = lax.broadcasted_iota(I32, (tile, tile), 1)
            keep = cidx <= r
            s1 = jnp.where(keep, s1, neg)
            s2 = jnp.where(keep, s2, neg)
        out = []
        for s, m, l, a in ((s1, m1, l1, a1), (s2, m2, l2, a2)):
            mn = jnp.maximum(m, jnp.max(s, axis=-1, keepdims=True))
            alpha = jnp.exp(m - mn)
            p = jnp.exp(s - mn)
            ln = alpha * l + jnp.sum(p, axis=-1, keepdims=True)
            an = alpha * a + _dot(p.astype(BF16), vb)
            out += [mn, ln, an]
        return tuple(out)

    init = (jnp.full((tile, 1), neg, F32), jnp.zeros((tile, 1), F32), jnp.zeros((tile, LANES), F32)) * 2
    carry = lax.fori_loop(0, qi, lambda j, c: step(j, c, False), init)
    m1, l1, a1, m2, l2, a2 = step(qi, carry, True)
    lam = (jnp.exp(jnp.sum(lq1_ref[...] * lk1_ref[...], axis=-1, keepdims=True))
           - jnp.exp(jnp.sum(lq2_ref[...] * lk2_ref[...], axis=-1, keepdims=True)) + lambda_init)
    o = a1 / l1 - lam * (a2 / l2)
    ms = jnp.mean(o * o, axis=-1, keepdims=True)
    o = o * lax.rsqrt(ms + NORM_EPS) * sw_ref[...] * (1.0 - lambda_init)
    o_ref[0] = o.astype(o_ref.dtype)


def _attn(q3, k3, v3, lq1, lk1, lq2, lk2, sw, lambda_init):
    b, s, _ = q3.shape
    tile = min(ATTN_TILE, s)
    qspec = pl.BlockSpec((1, tile, LANES), lambda bi, h, i: (bi, i, h))
    kvspec = pl.BlockSpec((1, s, LANES), lambda bi, h, i: (bi, 0, h))
    vec = lambda n: pl.BlockSpec((1, n), lambda bi, h, i: (0, 0))
    return pl.pallas_call(
        functools.partial(_attn_kernel, tile=tile, lambda_init=lambda_init),
        grid=(b, DIFF_HEADS, s // tile),
        in_specs=[qspec, kvspec, kvspec, vec(DIFF_HEAD_DIM), vec(DIFF_HEAD_DIM), vec(DIFF_HEAD_DIM),
                  vec(DIFF_HEAD_DIM), vec(DIFF_V_DIM)],
        out_specs=qspec,
        out_shape=jax.ShapeDtypeStruct((b, s, DIFF_WIDTH), BF16),
        compiler_params=_cparams(("parallel", "parallel", "parallel")),
        name="attn",
    )(q3, k3, v3, lq1, lk1, lq2, lk2, sw)


def _prep_kernel(z_ref, zp_ref, mu_ref, w0_ref, a0_ref, kk_ref, ka_ref, rk_ref,
                 wup_ref, aup_ref, gup_ref, ones_ref,
                 r_ref, lw_ref, k_ref, v_ref, kkn_ref, b_ref, g_ref, bonus_ref, *, tiles_per_seq):
    i = pl.program_id(0)
    z = z_ref[...]
    tm = z.shape[0]
    rows = lax.broadcasted_iota(I32, (tm, 1), 0)
    prev = zp_ref[SUBLANES - 1:SUBLANES, :]
    prev = jnp.where(i % tiles_per_seq == 0, jnp.zeros_like(prev), prev)
    shifted = jnp.where(rows == 0, prev, pltpu.roll(z, 1, 0))
    zf = z + mu_ref[...] * (shifted - z)
    w = RWKV_WIDTH
    r = zf[:, 0:w]
    k = zf[:, w:2 * w]
    v = zf[:, 2 * w:3 * w]
    wd = zf[:, 3 * w:3 * w + LORA_PAD]
    ad = zf[:, 3 * w + LORA_PAD:3 * w + 2 * LORA_PAD]
    gd = zf[:, 3 * w + 2 * LORA_PAD:3 * w + 3 * LORA_PAD]
    pre = w0_ref[...] + _split_dot(jnp.tanh(wd), wup_ref[...])
    neg = -pre
    softplus = jnp.maximum(neg, 0.0) + jnp.log(1.0 + jnp.exp(-jnp.abs(neg)))
    wlog = -softplus - 0.5
    lw_ref[...] = -jnp.exp(wlog)
    a = jax.nn.sigmoid(a0_ref[...] + _split_dot(ad, aup_ref[...]))
    g_ref[...] = _split_dot(jax.nn.sigmoid(gd), gup_ref[...])
    ones = ones_ref[...]
    kk = k * kk_ref[...]
    norm = jnp.sqrt(_split_dot(kk * kk, ones))
    kk = kk / jnp.maximum(norm, 1e-12)
    k = k * (1.0 + (a - 1.0) * ka_ref[...])
    r_ref[...] = r
    k_ref[...] = k
    v_ref[...] = v
    kkn_ref[...] = kk
    b_ref[...] = kk * a
    bonus_ref[...] = _split_dot(r * k * rk_ref[...], ones) * v


def _prep(zr, mu_p, w0, a0, k_k, k_a, rk, wup, aup, gup, ones, seq):
    t = zr.shape[0]
    tm = min(ROW_TILE, seq)
    w = RWKV_WIDTH
    row = lambda i: (i, 0)
    fixed = lambda i: (0, 0)
    per = tm // SUBLANES
    vecw = pl.BlockSpec((1, w), fixed)
    out = pl.BlockSpec((tm, w), row)
    return pl.pallas_call(
        functools.partial(_prep_kernel, tiles_per_seq=seq // tm),
        grid=(t // tm,),
        in_specs=[
            pl.BlockSpec((tm, ZR_COLS), row),
            pl.BlockSpec((SUBLANES, ZR_COLS), lambda i: (jnp.maximum(i * per - 1, 0), 0)),
            pl.BlockSpec((1, ZR_COLS), fixed),
            vecw, vecw, vecw, vecw, vecw,
            pl.BlockSpec((LORA_PAD, w), fixed),
            pl.BlockSpec((LORA_PAD, w), fixed),
            pl.BlockSpec((LORA_PAD, w), fixed),
            pl.BlockSpec((w, w), fixed),
        ],
        out_specs=[out] * 8,
        out_shape=[jax.ShapeDtypeStruct((t, w), F32)] * 8,
        compiler_params=_cparams(("parallel",)),
        name="rwkv_prep",
    )(zr, zr, mu_p, w0, a0, k_k, k_a, rk, wup, aup, gup, ones)


def _scan_kernel(r_ref, lw_ref, k_ref, v_ref, kk_ref, b_ref, y_ref, state_ref,
                 wr_s, ut_s, utt_s, arb_s, pv_s, bh_s, vk_s, wt_s, *, n_chunks):
    L = CHUNK
    G = GROUP
    n_groups = RWKV_WIDTH // G

    @pl.when(pl.program_id(1) == 0)
    def _():
        state_ref[...] = jnp.zeros_like(state_ref)

    row = lax.broadcasted_iota(I32, (L, G), 0)
    colr = lax.broadcasted_iota(I32, (L, G), 1) & (L - 1)
    strict = (colr < row).astype(F32)
    incl = (colr <= row).astype(F32)
    eye = (colr == row).astype(F32)
    eye_l = (lax.broadcasted_iota(I32, (L, L), 0) == lax.broadcasted_iota(I32, (L, L), 1)).astype(BF16)
    br = lax.broadcasted_iota(I32, (G, G), 0) >> 6
    bc = lax.broadcasted_iota(I32, (G, G), 1) >> 6
    block = (br == bc).astype(F32)
    block_bf = block.astype(BF16)
    rows1 = lax.broadcasted_iota(I32, (L, 1), 0)

    def stack4(x):
        xb = x.astype(BF16)
        return jnp.concatenate([xb, xb, xb, xb], axis=0) * block_bf

    def cat(a, b):
        return jnp.concatenate([a, b], axis=0).astype(BF16)

    def precompute(it, carry):
        chains = [(cl, g) for cl in range(SCAN_BATCH) for g in range(n_groups)]
        each = lambda f, *lists: [f(*args) for args in zip(*lists)]

        def load(ref):
            out = []
            for cl, g in chains:
                base = pl.multiple_of((it * SCAN_BATCH + cl) * L, L)
                out.append(ref[0, pl.ds(base, L), g * G:(g + 1) * G])
            return out

        r, lw, k, v, kk, b = load(r_ref), load(lw_ref), load(k_ref), load(v_ref), load(kk_ref), load(b_ref)

        def cumsum(x):
            sh = 1
            while sh < L:
                x = x + jnp.where(rows1 >= sh, pltpu.roll(x, sh, 0), 0.0)
                sh *= 2
            return x

        cs = each(cumsum, lw)
        tot = each(lambda c: c[L - 1:L, :], cs)
        a_hat = each(lambda kk_, c, l: -kk_ * jnp.exp(c - l), kk, cs, lw)
        r_hat = each(lambda r_, c: r_ * jnp.exp(c), r, cs)
        w_inv = each(lambda c: jnp.exp(-c), cs)
        w_end = each(lambda t_, c: jnp.exp(t_ - c), tot, cs)
        lhs = each(cat, a_hat, r_hat)
        ab = each(lambda l_, b_, wi: _nt(l_, stack4(b_ * wi)), lhs, b, w_inv)
        ak = each(lambda l_, k_, wi: _nt(l_, stack4(k_ * wi)), lhs, k, w_inv)
        a_ab = each(lambda x: x[:L] * strict, ab)
        a_rb = each(lambda x: (x[L:] * incl).astype(BF16), ab)
        a_k = each(lambda x: cat(x[:L] * strict, x[L:] * incl), ak)
        t_mat = each(lambda a: eye + a, a_ab)
        p_mat = each(lambda a: _dot(a.astype(BF16), stack4(a)), a_ab)
        for _ in range(4):
            tp = each(lambda t_, p_: _dot(cat(t_, p_), stack4(p_)), t_mat, p_mat)
            t_mat = each(lambda t_, x: t_ + x[:L], t_mat, tp)
            p_mat = each(lambda x: x[L:], tp)
        t_bf = each(lambda t_, p_: (t_ + _dot(t_.astype(BF16), stack4(p_))).astype(BF16), t_mat, p_mat)
        av = each(lambda a, v_: _dot(a, stack4(v_)), a_k, v)
        w_til = each(lambda t_, a: _dot(t_, stack4(a)), t_bf, a_hat)
        u_til = each(lambda t_, x: _dot(t_, stack4(x[:L])), t_bf, av)
        u_til_t = each(lambda u_: _tn(u_.astype(BF16), eye_l), u_til)
        vk = each(lambda v_, k_, we: _tn(v_.astype(BF16), (k_ * we).astype(BF16)) * block, v, k, w_end)
        for n, (cl, g) in enumerate(chains):
            slot = (it * SCAN_BATCH + cl) * n_groups + g
            wr_s[slot] = cat(w_til[n], r_hat[n])
            ut_s[slot] = u_til[n]
            utt_s[slot] = u_til_t[n]
            arb_s[slot] = a_rb[n]
            pv_s[slot] = av[n][L:]
            bh_s[slot] = (b[n] * w_end[n]).astype(BF16)
            vk_s[slot] = vk[n]
            wt_s[slot] = jnp.broadcast_to(jnp.exp(tot[n]), (SUBLANES, G))
        return carry

    lax.fori_loop(0, n_chunks // SCAN_BATCH, precompute, 0)

    def recur(c, carry):
        gs = range(n_groups)
        slots = [c * n_groups + g for g in gs]
        s0 = [state_ref[g] for g in gs]
        s0b = [s.astype(BF16) for s in s0]
        wr = [wr_s[sl] for sl in slots]
        u_t = [_nt(s0b[g], wr[g][:L]) + utt_s[slots[g]] for g in gs]
        ub = [_dot(u_t[g].astype(BF16), bh_s[slots[g]]) for g in gs]
        for g in gs:
            state_ref[g] = s0[g] * wt_s[slots[g]][0:1, :] + ub[g] * block + vk_s[slots[g]]
        uy = [_nt(wr[g], s0b[g]) for g in gs]
        u = [uy[g][:L] + ut_s[slots[g]] for g in gs]
        base = pl.multiple_of(c * L, L)
        for g in gs:
            y = uy[g][L:] + _dot(arb_s[slots[g]], stack4(u[g])) + pv_s[slots[g]]
            y_ref[0, pl.ds(base, L), g * G:(g + 1) * G] = y
        return carry

    lax.fori_loop(0, n_chunks, recur, 0)


def _scan(r3, lw3, k3, v3, kk3, b3):
    bsz, s, w = r3.shape
    ts = min(SCAN_TILE, s)
    n_chunks = ts // CHUNK
    slots = n_chunks * (w // GROUP)
    L, G = CHUNK, GROUP
    spec = pl.BlockSpec((1, ts, w), lambda bi, i: (bi, i, 0))
    return pl.pallas_call(
        functools.partial(_scan_kernel, n_chunks=n_chunks),
        grid=(bsz, s // ts),
        in_specs=[spec] * 6,
        out_specs=spec,
        out_shape=jax.ShapeDtypeStruct((bsz, s, w), F32),
        scratch_shapes=[pltpu.VMEM((w // GROUP, G, G), F32),
                        pltpu.VMEM((slots, 2 * L, G), BF16),
                        pltpu.VMEM((slots, L, G), F32),
                        pltpu.VMEM((slots, G, L), F32),
                        pltpu.VMEM((slots, L, G), BF16),
                        pltpu.VMEM((slots, L, G), F32),
                        pltpu.VMEM((slots, L, G), BF16),
                        pltpu.VMEM((slots, G, G), F32),
                        pltpu.VMEM((slots, SUBLANES, G), F32)],
        compiler_params=_cparams(("parallel", "arbitrary")),
        name="rwkv_scan",
    )(r3, lw3, k3, v3, kk3, b3)


def _mix_kernel(od_ref, y_ref, g_ref, bonus_ref, x_ref, lnw_ref, lnb_ref, ones_ref, wo_ref,
                fw_ref, rw_ref, rb_ref,
                h1_ref, xn_ref, sel_ref, idx_ref, gate_ref, cnt_ref):
    ones = ones_ref[...]
    y = y_ref[...]
    inv_n = 1.0 / RWKV_HEAD
    mean = _split_dot(y, ones) * inv_n
    d = y - mean
    var = _split_dot(d * d, ones) * inv_n
    yn = d * lax.rsqrt(var + RWKV_GN_EPS) * lnw_ref[...] + lnb_ref[...]
    orw = ((yn + bonus_ref[...]) * g_ref[...]).astype(BF16)
    h1 = (x_ref[...] + _dot(od_ref[...], wo_ref[0:DIFF_WIDTH, :])
          + _dot(orw, wo_ref[DIFF_WIDTH:DIFF_WIDTH + RWKV_WIDTH, :]))
    h1_ref[...] = h1
    ms = jnp.mean(h1 * h1, axis=-1, keepdims=True)
    xn = h1 * lax.rsqrt(ms + NORM_EPS) * fw_ref[...]
    xn_ref[...] = xn.astype(xn_ref.dtype)
    logits = _split3_dot(xn, rw_ref[0], rw_ref[1]) + rb_ref[...]
    tm = logits.shape[0]
    lane = lax.broadcasted_iota(I32, (tm, LANES), 1).astype(F32)
    work = logits
    sel = jnp.zeros((tm, LANES), F32)
    idx_l = jnp.zeros((tm, LANES), F32)
    val_l = jnp.zeros((tm, LANES), F32)
    top = None
    for kslot in range(TOP_K):
        m = jnp.max(work, axis=-1, keepdims=True)
        pick = jnp.min(jnp.where(work == m, lane, float(LANES)), axis=-1, keepdims=True)
        hit = lane == pick
        sel = jnp.where(hit, 1.0, sel)
        idx_l = jnp.where(lane == kslot, pick, idx_l)
        if top is None:
            top = m
        val_l = jnp.where(lane == kslot, jnp.exp(m - top), val_l)
        work = jnp.where(hit, -jnp.inf, work)
    sel_ref[...] = sel
    idx_ref[...] = idx_l.astype(I32)
    gate_ref[...] = val_l / jnp.sum(val_l, axis=-1, keepdims=True)
    cnt_ref[...] = jnp.broadcast_to(jnp.sum(sel, axis=0, keepdims=True), cnt_ref.shape).astype(I32)


def _mix(od, y, g, bonus, x2, lnw, lnb, ones, wo, fw, rw, rb):
    t, d = x2.shape
    tm = min(ROW_TILE, t)
    w = RWKV_WIDTH
    row = lambda i: (i, 0)
    fixed = lambda i: (0, 0)
    rs = lambda n: pl.BlockSpec((tm, n), row)
    return pl.pallas_call(
        _mix_kernel,
        grid=(t // tm,),
        in_specs=[rs(DIFF_WIDTH), rs(w), rs(w), rs(w), rs(d),
                  pl.BlockSpec((1, w), fixed), pl.BlockSpec((1, w), fixed),
                  pl.BlockSpec((w, w), fixed), pl.BlockSpec(wo.shape, fixed),
                  pl.BlockSpec((1, d), fixed), pl.BlockSpec(rw.shape, lambda i: (0, 0, 0)),
                  pl.BlockSpec((1, LANES), fixed)],
        out_specs=[rs(d), rs(d), rs(LANES), rs(LANES), rs(LANES), pl.BlockSpec((SUBLANES, LANES), row)],
        out_shape=[jax.ShapeDtypeStruct((t, d), F32), jax.ShapeDtypeStruct((t, d), BF16),
                   jax.ShapeDtypeStruct((t, LANES), F32), jax.ShapeDtypeStruct((t, LANES), I32),
                   jax.ShapeDtypeStruct((t, LANES), F32),
                   jax.ShapeDtypeStruct((t // tm * SUBLANES, LANES), I32)],
        compiler_params=_cparams(("parallel",)),
        name="mix_router",
    )(od, y, g, bonus, x2, lnw, lnb, ones, wo, fw, rw, rb)


def _slot_positions(sel, idx_l, lstart):
    tt = sel.shape[0]
    r = lax.broadcasted_iota(I32, (tt, tt), 0)
    c = lax.broadcasted_iota(I32, (tt, tt), 1)
    lower = (c < r).astype(BF16)
    where_to = _dot(lower, sel.astype(BF16)) + lstart
    lane = lax.broadcasted_iota(I32, (tt, LANES), 1).astype(F32)
    idx = idx_l.astype(F32)
    pos = jnp.full((tt, LANES), -1.0, F32)
    for kslot in range(TOP_K):
        e = jnp.sum(jnp.where(lane == kslot, idx, 0.0), axis=-1, keepdims=True)
        p = jnp.sum(jnp.where(lane == e, where_to, 0.0), axis=-1, keepdims=True)
        pos = jnp.where(lane == kslot, p, pos)
    return pos


def _piece(ref, q):
    return ref.at[pl.ds(pl.multiple_of(q * SUBLANES, SUBLANES), SUBLANES), :]


def _dispatch_kernel(np_ref, fill_ref, nfill_ref, gdst_ref, sel_ref, idx_ref, lstart_ref, xn_ref,
                     pos_ref, buf_ref, xs_ref, zero_ref, sem, fill_sem, *, n_slots):
    i = pl.program_id(0)
    pos = _slot_positions(sel_ref[...], idx_ref[...], lstart_ref[0:1, :].astype(F32))
    pos_ref[...] = pos
    tt = pos.shape[0]
    pos_t = pos.T.astype(I32)
    s_iota = lax.broadcasted_iota(I32, (n_slots, tt), 0)
    perm = jnp.zeros((n_slots, tt), F32)
    for kslot in range(TOP_K):
        perm = perm + jnp.where(s_iota == pos_t[kslot:kslot + 1, :], 1.0, 0.0)
    xs_ref[...] = _dot(perm.astype(BF16), xn_ref[...])

    def copy(q):
        return pltpu.make_async_copy(_piece(xs_ref, q), _piece(buf_ref, gdst_ref[0, 0, q]), sem)

    n = np_ref[i]
    lax.fori_loop(0, n, lambda q, c: (copy(q).start(), c)[1], 0)

    @pl.when(i == 0)
    def _():
        zero_ref[...] = jnp.zeros_like(zero_ref)

        def fill(j):
            return pltpu.make_async_copy(zero_ref, _piece(buf_ref, fill_ref[j]), fill_sem)

        lax.fori_loop(0, nfill_ref[0], lambda j, c: (fill(j).start(), c)[1], 0)
        lax.fori_loop(0, nfill_ref[0], lambda j, c: (fill(j).wait(), c)[1], 0)

    lax.fori_loop(0, n, lambda q, c: (copy(q).wait(), c)[1], 0)


def _dispatch(npieces, fill, nfill, gdst3, sel, idx_l, lstart8, xn, n_rows):
    t, d = xn.shape
    n_tiles, _, n_pieces = gdst3.shape
    tt = t // n_tiles
    n_slots = n_pieces * SUBLANES
    row = lambda i, *_: (i, 0)
    grid_spec = pltpu.PrefetchScalarGridSpec(
        num_scalar_prefetch=3,
        grid=(n_tiles,),
        in_specs=[pl.BlockSpec((1, 1, n_pieces), lambda i, *_: (i, 0, 0), memory_space=pltpu.SMEM),
                  pl.BlockSpec((tt, LANES), row),
                  pl.BlockSpec((tt, LANES), row),
                  pl.BlockSpec((SUBLANES, LANES), row),
                  pl.BlockSpec((tt, d), row)],
        out_specs=[pl.BlockSpec((tt, LANES), row), pl.BlockSpec(memory_space=pl.ANY)],
        scratch_shapes=[pltpu.VMEM((n_slots, d), F32), pltpu.VMEM((SUBLANES, d), F32),
                        pltpu.SemaphoreType.DMA(()), pltpu.SemaphoreType.DMA(())],
    )
    return pl.pallas_call(
        functools.partial(_dispatch_kernel, n_slots=n_slots),
        grid_spec=grid_spec,
        out_shape=[jax.ShapeDtypeStruct((t, LANES), F32), jax.ShapeDtypeStruct((n_rows, d), F32)],
        compiler_params=_cparams(("arbitrary",)),
        name="moe_dispatch",
    )(npieces, fill, nfill, gdst3, sel, idx_l, lstart8, xn)


def _expert_kernel(be_ref, first_ref, ord_ref, next_ref, nused_ref, x_ref, w1_hbm, b1_ref, w2_hbm, b2_ref,
                   y_ref, w1f_ref, w2f_ref, w1b_ref, w2i_ref, w2b_ref, sem):
    i = pl.program_id(0)
    ff = w2f_ref.shape[1]
    used = i < nused_ref[0]
    new_expert = jnp.logical_and(used, first_ref[i] == 1)

    def fetch(expert, slot):
        return (pltpu.make_async_copy(w1_hbm.at[expert], w1f_ref.at[slot], sem.at[0, slot]),
                pltpu.make_async_copy(w2_hbm.at[expert], w2f_ref.at[slot], sem.at[1, slot]))

    @pl.when(i == 0)
    def _():
        for cp in fetch(be_ref[0], 0):
            cp.start()

    @pl.when(jnp.logical_and(new_expert, next_ref[i] >= 0))
    def _():
        for cp in fetch(next_ref[i], 1 - (ord_ref[i] & 1)):
            cp.start()

    @pl.when(new_expert)
    def _():
        slot = ord_ref[i] & 1
        for cp in fetch(be_ref[i], slot):
            cp.wait()
        rows = w1f_ref.shape[1] // CAST_CHUNKS

        def cast1(c, carry):
            r0 = pl.multiple_of(c * rows, rows)
            w1b_ref[pl.ds(r0, rows), :] = w1f_ref[slot, pl.ds(r0, rows), :].astype(BF16)
            return carry

        lax.fori_loop(0, CAST_CHUNKS, cast1, 0)
        for g in range(w2f_ref.shape[2] // LANES):
            cols = slice(g * LANES, (g + 1) * LANES)
            w2i_ref[pl.ds(0, ff // 2, stride=2), :] = w2f_ref[slot, 0:ff // 2, cols]
            w2i_ref[pl.ds(1, ff // 2, stride=2), :] = w2f_ref[slot, ff // 2:ff, cols]
            w2b_ref[:, cols] = w2i_ref[...].astype(BF16)

    @pl.when(used)
    def _():
        x = x_ref[...].astype(BF16)
        hid = _dot(x, w1b_ref[...]) + b1_ref[0]
        even = (lax.broadcasted_iota(I32, (1, LANES), 1) & 1) == 0

        def act_even(g):
            hg = hid[:, g * LANES:(g + 1) * LANES]
            glu = jnp.minimum(hg, SWIGLU_LIMIT)
            lin = jnp.clip(hg, -SWIGLU_LIMIT, SWIGLU_LIMIT) + 1.0
            return glu * jax.nn.sigmoid(SWIGLU_ALPHA * glu) * pltpu.roll(lin, LANES - 1, 1)

        half = ff // LANES
        act = jnp.concatenate(
            [jnp.where(even, act_even(g), pltpu.roll(act_even(g + half), 1, 1)) for g in range(half)], axis=1)
        y_ref[...] = _dot(act.astype(BF16), w2b_ref[...]) + b2_ref[0]

    @pl.when(jnp.logical_not(used))
    def _():
        y_ref[...] = jnp.zeros_like(y_ref)


def _experts(block_e, rows_per_expert, n_used, xbuf, w1, b1, w2, b2):
    n_rows, d = xbuf.shape
    bm = EXPERT_BLOCK
    n_blocks = n_rows // bm
    e, _, ff2 = w1.shape
    ff = ff2 // 2
    has = rows_per_expert > 0
    ids = jnp.arange(e, dtype=I32)
    later = (ids[None, :] > ids[:, None]) & has[None, :]
    next_used = jnp.where(jnp.any(later, axis=1), jnp.argmax(later, axis=1), -1).astype(I32)
    ordinal = (jnp.cumsum(has.astype(I32)) - 1).astype(I32)
    first = jnp.concatenate([jnp.ones((1,), I32), (block_e[1:] != block_e[:-1]).astype(I32)])
    order = ordinal[block_e]
    next_e = next_used[block_e]
    blk =lambda i, be, fi, od, ne, nu: (jnp.maximum(jnp.minimum(i, nu[0] - 1), 0), 0)
    ex3 = lambda i, be, fi, od, ne, nu: (be[i], 0, 0)
    grid_spec = pltpu.PrefetchScalarGridSpec(
        num_scalar_prefetch=5,
        grid=(n_blocks,),
        in_specs=[pl.BlockSpec((bm, d), blk),
                  pl.BlockSpec(memory_space=pl.ANY),
                  pl.BlockSpec((1, 1, ff2), ex3),
                  pl.BlockSpec(memory_space=pl.ANY),
                  pl.BlockSpec((1, 1, d), ex3)],
        out_specs=pl.BlockSpec((bm, d), lambda i, be, fi, od, ne, nu: (i, 0)),
        scratch_shapes=[pltpu.VMEM((2, d, ff2), F32), pltpu.VMEM((2, ff, d), F32),
                        pltpu.VMEM((d, ff2), BF16), pltpu.VMEM((ff, LANES), F32), pltpu.VMEM((ff, d), BF16),
                        pltpu.SemaphoreType.DMA((2, 2))],
    )
    return pl.pallas_call(
        _expert_kernel,
        grid_spec=grid_spec,
        out_shape=jax.ShapeDtypeStruct((n_rows, d), F32),
        compiler_params=_cparams(("arbitrary",)),
        name="moe_experts",
    )(block_e, first, order, next_e, n_used, xbuf, w1, b1, w2, b2)


def _combine_kernel(np_ref, gdst_ref, pos_ref, gate_ref, h1_ref, fw_ref, ybuf_ref, o_ref, ys_ref, sem,
                    *, n_slots):
    i = pl.program_id(0)
    n = np_ref[i]

    def copy(q):
        return pltpu.make_async_copy(_piece(ybuf_ref, gdst_ref[0, 0, q]), _piece(ys_ref, q), sem)

    lax.fori_loop(0, n, lambda q, c: (copy(q).start(), c)[1], 0)

    def zero(q, carry):
        _piece(ys_ref, q)[...] = jnp.zeros((SUBLANES, ys_ref.shape[1]), ys_ref.dtype)
        return carry

    lax.fori_loop(n, n_slots // SUBLANES, zero, 0)
    pos = pos_ref[...].astype(I32)
    gate = gate_ref[...]
    tt = pos.shape[0]
    s_iota = lax.broadcasted_iota(I32, (tt, n_slots), 1)
    weight = jnp.zeros((tt, n_slots), F32)
    for kslot in range(TOP_K):
        weight = weight + jnp.where(s_iota == pos[:, kslot:kslot + 1], gate[:, kslot:kslot + 1], 0.0)
    lax.fori_loop(0, n, lambda q, c: (copy(q).wait(), c)[1], 0)
    h = h1_ref[...] + _dot(weight.astype(BF16), ys_ref[...].astype(BF16))
    ms = jnp.mean(h * h, axis=-1, keepdims=True)
    o_ref[...] = h * lax.rsqrt(ms + NORM_EPS) * fw_ref[...]


def _combine(npieces, gdst3, pos_l, gate_l, h1, fw, ybuf):
    t, d = h1.shape
    n_tiles, _, n_pieces = gdst3.shape
    tt = t // n_tiles
    n_slots = n_pieces * SUBLANES
    row = lambda i, *_: (i, 0)
    grid_spec = pltpu.PrefetchScalarGridSpec(
        num_scalar_prefetch=1,
        grid=(n_tiles,),
        in_specs=[pl.BlockSpec((1, 1, n_pieces), lambda i, *_: (i, 0, 0), memory_space=pltpu.SMEM),
                  pl.BlockSpec((tt, LANES), row),
                  pl.BlockSpec((tt, LANES), row),
                  pl.BlockSpec((tt, d), row),
                  pl.BlockSpec((1, d), lambda i, *_: (0, 0)),
                  pl.BlockSpec(memory_space=pl.ANY)],
        out_specs=pl.BlockSpec((tt, d), row),
        scratch_shapes=[pltpu.VMEM((n_slots, d), F32), pltpu.SemaphoreType.DMA(())],
    )
    return pl.pallas_call(
        functools.partial(_combine_kernel, n_slots=n_slots),
        grid_spec=grid_spec,
        out_shape=jax.ShapeDtypeStruct((t, d), F32),
        compiler_params=_cparams(("arbitrary",)),
        name="moe_combine",
    )(npieces, gdst3, pos_l, gate_l, h1, fw, ybuf)


def _rotary_tables(positions):
    half = ROT_DIM // 2
    inv_freq = ROPE_THETA ** (-jnp.arange(0, ROT_DIM, 2, dtype=F32) / ROT_DIM)
    ang = positions.astype(F32).reshape(-1, 1) * inv_freq
    cos = jnp.tile(jnp.cos(ang), (1, LANES // half))
    sin = jnp.tile(jnp.sin(ang), (1, LANES // half))
    dim = np.arange(LANES)[None, :] % DIFF_HEAD_DIM
    ctab = jnp.where(dim < ROT_DIM, cos, 1.0)
    satab = jnp.where(dim < half, -sin, 0.0)
    sbtab = jnp.where((dim >= half) & (dim < ROT_DIM), sin, 0.0)
    return ctab, satab, sbtab


def _pad_rows(a, rows):
    return jnp.concatenate([a, jnp.zeros((rows - a.shape[0],) + a.shape[1:], a.dtype)], axis=0)


def _layer(h, l, tabs, attn_norm_w, w_in, diff_lambda_q1, diff_lambda_k1, diff_lambda_q2, diff_lambda_k2,
           diff_subln_w, rwkv_mu, rwkv_w0, rwkv_w_up, rwkv_a0, rwkv_a_up, rwkv_g_up, rwkv_k_k, rwkv_k_a,
           rwkv_r_k, rwkv_ln_w, rwkv_ln_b, w_out, ffn_norm_w, router_w, router_b, exp_w1, exp_b1,
           exp_w2, exp_b2, final_w):
    bsz, seq, d = h.shape
    t = bsz * seq
    w = RWKV_WIDTH
    lambda_init = 0.8 - 0.6 * math.exp(-0.3 * l)
    x2 = h.reshape(t, d)
    row1 = lambda a: a.reshape(1, -1).astype(F32)

    wi = w_in[l]
    qkv_cols = 3 * DIFF_WIDTH
    wq = wi[:, :qkv_cols].astype(BF16)
    o = qkv_cols + 3 * w
    zcol = lambda n: jnp.zeros((d, n), wi.dtype)
    wr = jnp.concatenate([
        wi[:, qkv_cols:o],
        wi[:, o:o + DECAY_LORA], zcol(LORA_PAD - DECAY_LORA),
        wi[:, o + DECAY_LORA:o + DECAY_LORA + AAA_LORA], zcol(LORA_PAD - AAA_LORA),
        wi[:, o + DECAY_LORA + AAA_LORA:], zcol(LORA_PAD - GATE_LORA)], axis=1).astype(BF16)
    mu = rwkv_mu[l]
    zv = lambda n: jnp.zeros((n,), mu.dtype)
    mu_p = jnp.concatenate([
        mu[:3 * w],
        mu[3 * w:3 * w + DECAY_LORA], zv(LORA_PAD - DECAY_LORA),
        mu[3 * w + DECAY_LORA:3 * w + DECAY_LORA + AAA_LORA], zv(LORA_PAD - AAA_LORA),
        mu[3 * w + DECAY_LORA + AAA_LORA:], zv(LORA_PAD - GATE_LORA)]).reshape(1, -1)

    q, k, v, zr = _inproj(x2, row1(attn_norm_w[l]), wq, wr, *tabs)

    od = _attn(q.reshape(bsz, seq, -1), k.reshape(bsz, seq, -1), v.reshape(bsz, seq, -1),
               row1(diff_lambda_q1[l]), row1(diff_lambda_k1[l]), row1(diff_lambda_q2[l]),
               row1(diff_lambda_k2[l]), row1(diff_subln_w[l]), lambda_init)

    head = np.arange(w) // RWKV_HEAD
    ones = jnp.asarray(head[:, None] == head[None, :], BF16)
    r, lw, kmod, vv, kk, bb, g, bonus = _prep(
        zr, mu_p, row1(rwkv_w0[l]), row1(rwkv_a0[l]), row1(rwkv_k_k[l]), row1(rwkv_k_a[l]),
        row1(rwkv_r_k[l]), _pad_rows(rwkv_w_up[l].astype(BF16), LORA_PAD),
        _pad_rows(rwkv_a_up[l].astype(BF16), LORA_PAD), _pad_rows(rwkv_g_up[l].astype(BF16), LORA_PAD),
        ones, seq)
    s3 = lambda a: a.reshape(bsz, seq, w)
    y = _scan(s3(r), s3(lw), s3(kmod), s3(vv), s3(kk), s3(bb)).reshape(t, w)

    n_e = router_w.shape[-1]
    rw = jnp.concatenate([router_w[l].astype(F32), jnp.zeros((d, LANES - n_e), F32)], axis=1)
    rw_hi = rw.astype(BF16)
    rw = jnp.stack([rw_hi, (rw - rw_hi.astype(F32)).astype(BF16)])
    rb = jnp.concatenate([router_b[l].astype(F32), jnp.full((LANES - n_e,), -1e30, F32)]).reshape(1, -1)
    h1, xn, sel, idx_l, gate_l, cnt8 = _mix(
        od.reshape(t, -1), y, g, bonus, x2, row1(rwkv_ln_w[l]), row1(rwkv_ln_b[l]), ones,
        w_out[l].astype(BF16), row1(ffn_norm_w[l]), rw, rb)

    bm = EXPERT_BLOCK
    pc = SUBLANES
    tm = min(ROW_TILE, t)
    n_tiles = t // tm
    cnt = cnt8.reshape(n_tiles, SUBLANES, LANES)[:, 0, :n_e]
    seg = (cnt + pc - 1) // pc * pc
    lend = jnp.cumsum(seg, axis=1)
    lstart = lend - seg
    rows_e = jnp.sum(seg, axis=0)
    padded = (rows_e + bm - 1) // bm * bm
    pad_ends = jnp.cumsum(padded)
    gstart = (pad_ends - padded)[None, :] + jnp.cumsum(seg, axis=0) - seg
    n_slots = tm * TOP_K + n_e * pc
    n_pieces = n_slots // pc
    n_blocks = -(-(t * TOP_K + n_tiles * n_e * (pc - 1) + n_e * (bm - pc)) // bm)
    n_rows = n_blocks * bm
    piece_row = jnp.arange(n_pieces, dtype=I32) * pc
    piece_e = jnp.minimum(jnp.sum(lend[:, None, :] <= piece_row[None, :, None], axis=-1), n_e - 1)
    take = lambda a: jnp.take_along_axis(a, piece_e, axis=1)
    gdst = ((take(gstart) + piece_row[None, :] - take(lstart)) // pc).astype(I32)
    gdst3 = jnp.clip(gdst, 0, n_rows // pc - 1).reshape(n_tiles, 1, n_pieces)
    npieces = (lend[:, -1] // pc).astype(I32)
    lstart8 = jnp.zeros((n_tiles, SUBLANES, LANES), I32).at[:, :, :n_e].set(lstart[:, None, :])
    lstart8 = lstart8.reshape(n_tiles * SUBLANES, LANES)
    gap_start = jnp.concatenate([pad_ends - padded + rows_e, pad_ends[-1:]]) // pc
    gap_len = jnp.concatenate([padded - rows_e, n_rows - pad_ends[-1:]]) // pc
    gap_end = jnp.cumsum(gap_len)
    max_fill = n_e * (bm // pc - 1) + (n_rows - t * TOP_K) // pc
    j = jnp.arange(max_fill, dtype=I32)
    gap = jnp.minimum(jnp.sum(gap_end[None, :] <= j[:, None], axis=1), n_e)
    fill = jnp.clip(gap_start[gap] + j - (gap_end - gap_len)[gap], 0, n_rows // pc - 1).astype(I32)
    nfill = gap_end[-1:].astype(I32)
    first_row = jnp.arange(n_blocks, dtype=I32) * bm
    block_e = jnp.minimum(jnp.sum(pad_ends[None, :] <= first_row[:, None], axis=1), n_e - 1).astype(I32)
    n_used = (pad_ends[-1] // bm).astype(I32).reshape(1)

    pos_l, xbuf = _dispatch(npieces, fill, nfill, gdst3, sel, idx_l, lstart8, xn, n_rows)

    ff2 = exp_w1.shape[-1]
    b1 = exp_b1[l].astype(F32).reshape(n_e, 1, ff2)
    b2 = exp_b2[l].astype(F32).reshape(n_e, 1, d)
    ybuf = _experts(block_e, padded, n_used, xbuf, exp_w1[l].astype(F32), b1, exp_w2[l].astype(F32), b2)

    out = _combine(npieces, gdst3, pos_l, gate_l, h1, row1(final_w), ybuf)
    return out.reshape(bsz, seq, d)


def kernel(x, positions, attn_norm_w, w_in, diff_lambda_q1, diff_lambda_k1, diff_lambda_q2, diff_lambda_k2, diff_subln_w, rwkv_mu, rwkv_w0, rwkv_w_up, rwkv_a0, rwkv_a_up, rwkv_g_up, rwkv_k_k, rwkv_k_a, rwkv_r_k, rwkv_ln_w, rwkv_ln_b, w_out, ffn_norm_w, router_w, router_b, exp_w1, exp_b1, exp_w2, exp_b2, final_norm_w):
    depth = w_in.shape[0]
    assert depth == 1, "the final norm is fused into the last (only) layer's combine kernel"
    tabs = _rotary_tables(positions)
    return _layer(x, 0, tabs, attn_norm_w, w_in, diff_lambda_q1, diff_lambda_k1, diff_lambda_q2,
                  diff_lambda_k2, diff_subln_w, rwkv_mu, rwkv_w0, rwkv_w_up, rwkv_a0, rwkv_a_up, rwkv_g_up,
                  rwkv_k_k, rwkv_k_a, rwkv_r_k, rwkv_ln_w, rwkv_ln_b, w_out, ffn_norm_w, router_w, router_b,
                  exp_w1, exp_b1, exp_w2, exp_b2, final_norm_w)
```

```python
import functools
import math

import jax
import jax.numpy as jnp
import numpy as np
from jax import lax
from jax.experimental import pallas as pl
from jax.experimental.pallas import tpu as pltpu

F32 = jnp.float32
BF16 = jnp.bfloat16
I32 = jnp.int32

DIFF_HEAD_DIM = 64
DIFF_V_DIM = 128
DIFF_HEADS = 4
DIFF_WIDTH = DIFF_HEADS * DIFF_V_DIM
ROT_DIM = 16
ROPE_THETA = 500000.0
RWKV_HEAD = 64
RWKV_HEADS = 8
RWKV_WIDTH = RWKV_HEAD * RWKV_HEADS
DECAY_LORA = 32
AAA_LORA = 32
GATE_LORA = 96
N_EXPERTS = 32
TOP_K = 4
SWIGLU_LIMIT = 7.0
SWIGLU_ALPHA = 1.702
NORM_EPS = 1e-5
RWKV_GN_EPS = 64e-5

LANES = 128
SUBLANES = 8
VMEM_LIMIT = 56 * 1024 * 1024

ROW_TILE = 512
ATTN_TILE = 512
CHUNK = 64
SCAN_TILE = 512
SCAN_BATCH = 4
GROUP = 4 * RWKV_HEAD
EXPERT_BLOCK = 512
DISPATCH_TILE = 256
COMBINE_TILE = 256
CAST_CHUNKS = 8
LORA_PAD = LANES
ZR_COLS = 3 * RWKV_WIDTH + 3 * LORA_PAD


def _cparams(sem, flags=None):
    return pltpu.CompilerParams(dimension_semantics=sem, vmem_limit_bytes=VMEM_LIMIT, flags=flags)


def _nt(a, b):
    return lax.dot_general(a, b, (((1,), (1,)), ((), ())), preferred_element_type=F32)


def _tn(a, b):
    return lax.dot_general(a, b, (((0,), (0,)), ((), ())), preferred_element_type=F32)


def _dot(a, b):
    return jnp.dot(a, b, preferred_element_type=F32)


def _split(x):
    hi = x.astype(BF16)
    return hi, (x - hi.astype(F32)).astype(BF16)


def _split_dot(x, w_bf16):
    hi, lo = _split(x)
    return _dot(hi, w_bf16) + _dot(lo, w_bf16)


def _split3_dot(x, w_hi, w_lo):
    hi, lo = _split(x)
    return _dot(hi, w_hi) + (_dot(hi, w_lo) + _dot(lo, w_hi))


def _inproj_kernel(x_ref, nw_ref, wq_ref, wr_ref, c_ref, sa_ref, sb_ref,
                   q_ref, k_ref, v_ref, zr_ref):
    x = x_ref[...]
    ms = jnp.mean(x * x, axis=-1, keepdims=True)
    u = (x * lax.rsqrt(ms + NORM_EPS) * nw_ref[...]).astype(BF16)
    zq = _dot(u, wq_ref[...])
    c = c_ref[...]
    sa = sa_ref[...]
    sb = sb_ref[...]
    scale = DIFF_HEAD_DIM ** -0.5
    for g in range(2 * DIFF_HEADS):
        zg = zq[:, g * LANES:(g + 1) * LANES]
        rot = zg * c + pltpu.roll(zg, LANES - ROT_DIM // 2, 1) * sa + pltpu.roll(zg, ROT_DIM // 2, 1) * sb
        if g < DIFF_HEADS:
            q_ref[:, g * LANES:(g + 1) * LANES] = (rot * scale).astype(BF16)
        else:
            h = g - DIFF_HEADS
            k_ref[:, h * LANES:(h + 1) * LANES] = rot.astype(BF16)
    v_ref[...] = zq[:, 2 * DIFF_WIDTH:3 * DIFF_WIDTH].astype(BF16)
    zr_ref[...] = _dot(u, wr_ref[...])


def _inproj(x2, nw, wq, wr, ctab, satab, sbtab):
    t, d = x2.shape
    tm = min(ROW_TILE, t)
    row = lambda i: (i, 0)
    fixed = lambda i: (0, 0)
    return pl.pallas_call(
        _inproj_kernel,
        grid=(t // tm,),
        in_specs=[
            pl.BlockSpec((tm, d), row),
            pl.BlockSpec((1, d), fixed),
            pl.BlockSpec(wq.shape, fixed),
            pl.BlockSpec(wr.shape, fixed),
            pl.BlockSpec((tm, LANES), row),
            pl.BlockSpec((tm, LANES), row),
            pl.BlockSpec((tm, LANES), row),
        ],
        out_specs=[
            pl.BlockSpec((tm, DIFF_WIDTH), row),
            pl.BlockSpec((tm, DIFF_WIDTH), row),
            pl.BlockSpec((tm, DIFF_WIDTH), row),
            pl.BlockSpec((tm, ZR_COLS), row),
        ],
        out_shape=[
            jax.ShapeDtypeStruct((t, DIFF_WIDTH), BF16),
            jax.ShapeDtypeStruct((t, DIFF_WIDTH), BF16),
            jax.ShapeDtypeStruct((t, DIFF_WIDTH), BF16),
            jax.ShapeDtypeStruct((t, ZR_COLS), F32),
        ],
        compiler_params=_cparams(("parallel",)),
        name="inproj",
    )(x2, nw, wq, wr, ctab, satab, sbtab)


def _attn_kernel(q_ref, k_ref, v_ref, lq1_ref, lk1_ref, lq2_ref, lk2_ref, sw_ref, o_ref,
                 *, tile, lambda_init):
    lane = lax.broadcasted_iota(I32, (1, LANES), 1)
    first = lane < DIFF_HEAD_DIM
    neg = -1e30
    lam = (jnp.exp(jnp.sum(lq1_ref[...] * lk1_ref[...], axis=-1, keepdims=True))
           - jnp.exp(jnp.sum(lq2_ref[...] * lk2_ref[...], axis=-1, keepdims=True)) + lambda_init)
    r = lax.broadcasted_iota(I32, (tile, tile), 0)
    c = lax.broadcasted_iota(I32, (tile, tile), 1)
    keep = c <= r
    for i in range(q_ref.shape[1] // tile):
        q = q_ref[0, i * tile:(i + 1) * tile, :]
        zero = jnp.zeros_like(q)
        n = (i + 1) * tile
        kb = k_ref[0, 0:n, :]
        vb = v_ref[0, 0:n, :]
        outs = []
        for qm in (jnp.where(first, q, zero), jnp.where(first, zero, q)):
            s = _nt(qm, kb)
            diag = jnp.where(keep, s[:, i * tile:], neg)
            s = diag if i == 0 else jnp.concatenate([s[:, :i * tile], diag], axis=1)
            p = jnp.exp(s - jnp.max(s, axis=-1, keepdims=True))
            outs.append(_dot(p.astype(BF16), vb) / jnp.sum(p, axis=-1, keepdims=True))
        o = outs[0] - lam * outs[1]
        ms = jnp.mean(o * o, axis=-1, keepdims=True)
        o = o * lax.rsqrt(ms + NORM_EPS) * sw_ref[...] * (1.0 - lambda_init)
        o_ref[0, i * tile:(i + 1) * tile, :] = o.astype(o_ref.dtype)


def _attn(q3, k3, v3, lq1, lk1, lq2, lk2, sw, lambda_init):
    b, s, _ = q3.shape
    tile = min(ATTN_TILE, s)
    spec = pl.BlockSpec((1, s, LANES), lambda bi, h: (bi, 0, h))
    vec = lambda n: pl.BlockSpec((1, n), lambda bi, h: (0, 0))
    return pl.pallas_call(
        functools.partial(_attn_kernel, tile=tile, lambda_init=lambda_init),
        grid=(b, DIFF_HEADS),
        in_specs=[spec, spec, spec, vec(DIFF_HEAD_DIM), vec(DIFF_HEAD_DIM), vec(DIFF_HEAD_DIM),
                  vec(DIFF_HEAD_DIM), vec(DIFF_V_DIM)],
        out_specs=spec,
        out_shape=jax.ShapeDtypeStruct((b, s, DIFF_WIDTH), BF16),
        compiler_params=_cparams(("parallel", "parallel")),
        name="attn",
    )(q3, k3, v3, lq1, lk1, lq2, lk2, sw)


def _prep_kernel(z_ref, zp_ref, mu_ref, w0_ref, a0_ref, kk_ref, ka_ref, rk_ref,
                 wup_ref, aup_ref, gup_ref, ones_ref,
                 r_ref, lw_ref, k_ref, v_ref, kkn_ref, b_ref, g_ref, bonus_ref, *, tiles_per_seq):
    i = pl.program_id(0)
    z = z_ref[...]
    tm = z.shape[0]
    rows = lax.broadcasted_iota(I32, (tm, 1), 0)
    prev = zp_ref[SUBLANES - 1:SUBLANES, :]
    prev = jnp.where(i % tiles_per_seq == 0, jnp.zeros_like(prev), prev)
    shifted = jnp.where(rows == 0, prev, pltpu.roll(z, 1, 0))
    zf = z + mu_ref[...] * (shifted - z)
    w = RWKV_WIDTH
    r = zf[:, 0:w]
    k = zf[:, w:2 * w]
    v = zf[:, 2 * w:3 * w]
    wd = zf[:, 3 * w:3 * w + LORA_PAD]
    ad = zf[:, 3 * w + LORA_PAD:3 * w + 2 * LORA_PAD]
    gd = zf[:, 3 * w + 2 * LORA_PAD:3 * w + 3 * LORA_PAD]
    pre = w0_ref[...] + _split_dot(jnp.tanh(wd), wup_ref[...])
    neg = -pre
    softplus = jnp.maximum(neg, 0.0) + jnp.log(1.0 + jnp.exp(-jnp.abs(neg)))
    wlog = -softplus - 0.5
    lw_ref[...] = -jnp.exp(wlog)
    a = jax.nn.sigmoid(a0_ref[...] + _split_dot(ad, aup_ref[...]))
    g_ref[...] = _split_dot(jax.nn.sigmoid(gd), gup_ref[...])
    ones = ones_ref[...]
    kk = k * kk_ref[...]
    norm = jnp.sqrt(_split_dot(kk * kk, ones))
    kk = kk / jnp.maximum(norm, 1e-12)
    k = k * (1.0 + (a - 1.0) * ka_ref[...])
    r_ref[...] = r
    k_ref[...] = k
    v_ref[...] = v
    kkn_ref[...] = kk
    b_ref[...] = kk * a
    bonus_ref[...] = _split_dot(r * k * rk_ref[...], ones) * v


def _prep(zr, mu_p, w0, a0, k_k, k_a, rk, wup, aup, gup, ones, seq):
    t = zr.shape[0]
    tm = min(ROW_TILE, seq)
    w = RWKV_WIDTH
    row = lambda i: (i, 0)
    fixed = lambda i: (0, 0)
    per = tm // SUBLANES
    vecw = pl.BlockSpec((1, w), fixed)
    out = pl.BlockSpec((tm, w), row)
    return pl.pallas_call(
        functools.partial(_prep_kernel, tiles_per_seq=seq // tm),
        grid=(t // tm,),
        in_specs=[
            pl.BlockSpec((tm, ZR_COLS), row),
            pl.BlockSpec((SUBLANES, ZR_COLS), lambda i: (jnp.maximum(i * per - 1, 0), 0)),
            pl.BlockSpec((1, ZR_COLS), fixed),
            vecw, vecw, vecw, vecw, vecw,
            pl.BlockSpec((LORA_PAD, w), fixed),
            pl.BlockSpec((LORA_PAD, w), fixed),
            pl.BlockSpec((LORA_PAD, w), fixed),
            pl.BlockSpec((w, w), fixed),
        ],
        out_specs=[out] * 8,
        out_shape=[jax.ShapeDtypeStruct((t, w), F32)] * 8,
        compiler_params=_cparams(("parallel",)),
        name="rwkv_prep",
    )(zr, zr, mu_p, w0, a0, k_k, k_a, rk, wup, aup, gup, ones)


def _scan_kernel(r_ref, lw_ref, k_ref, v_ref, kk_ref, b_ref, y_ref, state_ref,
                 wr_s, ut_s, utt_s, arb_s, pv_s, bh_s, vk_s, wt_s, *, n_chunks):
    L = CHUNK
    G = GROUP
    n_groups = RWKV_WIDTH // G

    @pl.when(pl.program_id(1) == 0)
    def _():
        state_ref[...] = jnp.zeros_like(state_ref)

    row = lax.broadcasted_iota(I32, (L, G), 0)
    colr = lax.broadcasted_iota(I32, (L, G), 1) & (L - 1)
    strict = (colr < row).astype(F32)
    incl = (colr <= row).astype(F32)
    eye = (colr == row).astype(F32)
    eye_l = (lax.broadcasted_iota(I32, (L, L), 0) == lax.broadcasted_iota(I32, (L, L), 1)).astype(BF16)
    br = lax.broadcasted_iota(I32, (G, G), 0) >> 6
    bc = lax.broadcasted_iota(I32, (G, G), 1) >> 6
    block = (br == bc).astype(F32)
    block_bf = block.astype(BF16)
    rows1 = lax.broadcasted_iota(I32, (L, 1), 0)

    def stack4(x):
        xb = x.astype(BF16)
        return jnp.concatenate([xb, xb, xb, xb], axis=0) * block_bf

    def cat(a, b):
        return jnp.concatenate([a, b], axis=0).astype(BF16)

    def precompute(it, carry):
        chains = [(cl, g) for cl in range(SCAN_BATCH) for g in range(n_groups)]
        each = lambda f, *lists: [f(*args) for args in zip(*lists)]

        def load(ref):
            out = []
            for cl, g in chains:
                base = pl.multiple_of((it * SCAN_BATCH + cl) * L, L)
                out.append(ref[0, pl.ds(base, L), g * G:(g + 1) * G])
            return out

        r, lw, k, v, kk, b = load(r_ref), load(lw_ref), load(k_ref), load(v_ref), load(kk_ref), load(b_ref)

        def cumsum(x):
            sh = 1
            while sh < L:
                x = x + jnp.where(rows1 >= sh, pltpu.roll(x, sh, 0), 0.0)
                sh *= 2
            return x

        cs = each(cumsum, lw)
        tot = each(lambda c: c[L - 1:L, :], cs)
        a_hat = each(lambda kk_, c, l: -kk_ * jnp.exp(c - l), kk, cs, lw)
        r_hat = each(lambda r_, c: r_ * jnp.exp(c), r, cs)
        w_inv = each(lambda c: jnp.exp(-c), cs)
        w_end = each(lambda t_, c: jnp.exp(t_ - c), tot, cs)
        lhs = each(cat, a_hat, r_hat)
        ab = each(lambda l_, b_, wi: _nt(l_, stack4(b_ * wi)), lhs, b, w_inv)
        ak = each(lambda l_, k_, wi: _nt(l_, stack4(k_ * wi)), lhs, k, w_inv)
        a_ab = each(lambda x: x[:L] * strict, ab)
        a_rb = each(lambda x: (x[L:] * incl).astype(BF16), ab)
        a_k = each(lambda x: cat(x[:L] * strict, x[L:] * incl), ak)
        t_mat = each(lambda a: eye + a, a_ab)
        p_mat = each(lambda a: _dot(a.astype(BF16), stack4(a)), a_ab)
        for _ in range(4):
            tp = each(lambda t_, p_: _dot(cat(t_, p_), stack4(p_)), t_mat, p_mat)
            t_mat = each(lambda t_, x: t_ + x[:L], t_mat, tp)
            p_mat = each(lambda x: x[L:], tp)
        t_bf = each(lambda t_, p_: (t_ + _dot(t_.astype(BF16), stack4(p_))).astype(BF16), t_mat, p_mat)
        av = each(lambda a, v_: _dot(a, stack4(v_)), a_k, v)
        w_til = each(lambda t_, a: _dot(t_, stack4(a)), t_bf, a_hat)
        u_til = each(lambda t_, x: _dot(t_, stack4(x[:L])), t_bf, av)
        u_til_t = each(lambda u_: _tn(u_.astype(BF16), eye_l), u_til)
        vk = each(lambda v_, k_, we: _tn(v_.astype(BF16), (k_ * we).astype(BF16)) * block, v, k, w_end)
        for n, (cl, g) in enumerate(chains):
            slot = (it * SCAN_BATCH + cl) * n_groups + g
            wr_s[slot] = cat(w_til[n], r_hat[n])
            ut_s[slot] = u_til[n]
            utt_s[slot] = u_til_t[n]
            arb_s[slot] = a_rb[n]
            pv_s[slot] = av[n][L:]
            bh_s[slot] = (b[n] * w_end[n]).astype(BF16)
            vk_s[slot] = vk[n]
            wt_s[slot] = jnp.broadcast_to(jnp.exp(tot[n]), (SUBLANES, G))
        return carry

    lax.fori_loop(0, n_chunks // SCAN_BATCH, precompute, 0)

    def recur(c, carry):
        gs = range(n_groups)
        slots = [c * n_groups + g for g in gs]
        s0 = [state_ref[g] for g in gs]
        s0b = [s.astype(BF16) for s in s0]
        wr = [wr_s[sl] for sl in slots]
        u_t = [_nt(s0b[g], wr[g][:L]) + utt_s[slots[g]] for g in gs]
        ub = [_dot(u_t[g].astype(BF16), bh_s[slots[g]]) for g in gs]
        for g in gs:
            state_ref[g] = s0[g] * wt_s[slots[g]][0:1, :] + ub[g] * block + vk_s[slots[g]]
        uy = [_nt(wr[g], s0b[g]) for g in gs]
        u = [uy[g][:L] + ut_s[slots[g]] for g in gs]
        base = pl.multiple_of(c * L, L)
        for g in gs:
            y = uy[g][L:] + _dot(arb_s[slots[g]], stack4(u[g])) + pv_s[slots[g]]
            y_ref[0, pl.ds(base, L), g * G:(g + 1) * G] = y
        return carry

    lax.fori_loop(0, n_chunks, recur, 0)


def _scan(r3, lw3, k3, v3, kk3, b3):
    bsz, s, w = r3.shape
    ts = min(SCAN_TILE, s)
    n_chunks = ts // CHUNK
    slots = n_chunks * (w // GROUP)
    L, G = CHUNK, GROUP
    spec = pl.BlockSpec((1, ts, w), lambda bi, i: (bi, i, 0))
    return pl.pallas_call(
        functools.partial(_scan_kernel, n_chunks=n_chunks),
        grid=(bsz, s // ts),
        in_specs=[spec] * 6,
        out_specs=spec,
        out_shape=jax.ShapeDtypeStruct((bsz, s, w), F32),
        scratch_shapes=[pltpu.VMEM((w // GROUP, G, G), F32),
                        pltpu.VMEM((slots, 2 * L, G), BF16),
                        pltpu.VMEM((slots, L, G), F32),
                        pltpu.VMEM((slots, G, L), F32),
                        pltpu.VMEM((slots, L, G), BF16),
                        pltpu.VMEM((slots, L, G), F32),
                        pltpu.VMEM((slots, L, G), BF16),
                        pltpu.VMEM((slots, G, G), F32),
                        pltpu.VMEM((slots, SUBLANES, G), F32)],
        compiler_params=_cparams(("parallel", "arbitrary")),
        name="rwkv_scan",
    )(r3, lw3, k3, v3, kk3, b3)


def _mix_kernel(od_ref, y_ref, g_ref, bonus_ref, x_ref, lnw_ref, lnb_ref, ones_ref, wo_ref,
                fw_ref, rw_ref, rb_ref,
                h1_ref, xn_ref, sel_ref, idx_ref, gate_ref, cnt_ref):
    ones = ones_ref[...]
    y = y_ref[...]
    inv_n = 1.0 / RWKV_HEAD
    mean = _split_dot(y, ones) * inv_n
    d = y - mean
    var = _split_dot(d * d, ones) * inv_n
    yn = d * lax.rsqrt(var + RWKV_GN_EPS) * lnw_ref[...] + lnb_ref[...]
    orw = ((yn + bonus_ref[...]) * g_ref[...]).astype(BF16)
    h1 = (x_ref[...] + _dot(od_ref[...], wo_ref[0:DIFF_WIDTH, :])
          + _dot(orw, wo_ref[DIFF_WIDTH:DIFF_WIDTH + RWKV_WIDTH, :]))
    h1_ref[...] = h1
    ms = jnp.mean(h1 * h1, axis=-1, keepdims=True)
    xn = h1 * lax.rsqrt(ms + NORM_EPS) * fw_ref[...]
    xn_ref[...] = xn.astype(xn_ref.dtype)
    logits = _split3_dot(xn, rw_ref[0], rw_ref[1]) + rb_ref[...]
    tm = logits.shape[0]
    lane = lax.broadcasted_iota(I32, (tm, LANES), 1).astype(F32)
    work = logits
    sel = jnp.zeros((tm, LANES), F32)
    idx_l = jnp.zeros((tm, LANES), F32)
    val_l = jnp.zeros((tm, LANES), F32)
    top = None
    for kslot in range(TOP_K):
        m = jnp.max(work, axis=-1, keepdims=True)
        pick = jnp.min(jnp.where(work == m, lane, float(LANES)), axis=-1, keepdims=True)
        hit = lane == pick
        sel = jnp.where(hit, 1.0, sel)
        idx_l = jnp.where(lane == kslot, pick, idx_l)
        if top is None:
            top = m
        val_l = jnp.where(lane == kslot, jnp.exp(m - top), val_l)
        work = jnp.where(hit, -jnp.inf, work)
    sel_ref[...] = sel
    idx_ref[...] = idx_l.astype(I32)
    gate_ref[...] = val_l / jnp.sum(val_l, axis=-1, keepdims=True)
    cnt_ref[...] = jnp.broadcast_to(jnp.sum(sel, axis=0, keepdims=True), cnt_ref.shape).astype(I32)


def _mix(od, y, g, bonus, x2, lnw, lnb, ones, wo, fw, rw, rb):
    t, d = x2.shape
    tm = min(ROW_TILE, t)
    w = RWKV_WIDTH
    row = lambda i: (i, 0)
    fixed = lambda i: (0, 0)
    rs = lambda n: pl.BlockSpec((tm, n), row)
    return pl.pallas_call(
        _mix_kernel,
        grid=(t // tm,),
        in_specs=[rs(DIFF_WIDTH), rs(w), rs(w), rs(w), rs(d),
                  pl.BlockSpec((1, w), fixed), pl.BlockSpec((1, w), fixed),
                  pl.BlockSpec((w, w), fixed), pl.BlockSpec(wo.shape, fixed),
                  pl.BlockSpec((1, d), fixed), pl.BlockSpec(rw.shape, lambda i: (0, 0, 0)),
                  pl.BlockSpec((1, LANES), fixed)],
        out_specs=[rs(d), rs(d), rs(LANES), rs(LANES), rs(LANES), pl.BlockSpec((SUBLANES, LANES), row)],
        out_shape=[jax.ShapeDtypeStruct((t, d), F32), jax.ShapeDtypeStruct((t, d), BF16),
                   jax.ShapeDtypeStruct((t, LANES), F32), jax.ShapeDtypeStruct((t, LANES), I32),
                   jax.ShapeDtypeStruct((t, LANES), F32),
                   jax.ShapeDtypeStruct((t // tm * SUBLANES, LANES), I32)],
        compiler_params=_cparams(("parallel",)),
        name="mix_router",
    )(od, y, g, bonus, x2, lnw, lnb, ones, wo, fw, rw, rb)


def _slot_positions(sel, idx_l, lstart):
    tt = sel.shape[0]
    r = lax.broadcasted_iota(I32, (tt, tt), 0)
    c = lax.broadcasted_iota(I32, (tt, tt), 1)
    lower = (c < r).astype(BF16)
    where_to = _dot(lower, sel.astype(BF16)) + lstart
    lane = lax.broadcasted_iota(I32, (tt, LANES), 1).astype(F32)
    idx = idx_l.astype(F32)
    pos = jnp.full((tt, LANES), -1.0, F32)
    for kslot in range(TOP_K):
        e = jnp.sum(jnp.where(lane == kslot, idx, 0.0), axis=-1, keepdims=True)
        p = jnp.sum(jnp.where(lane == e, where_to, 0.0), axis=-1, keepdims=True)
        pos = jnp.where(lane == kslot, p, pos)
    return pos


def _piece(ref, q):
    return ref.at[pl.ds(pl.multiple_of(q * SUBLANES, SUBLANES), SUBLANES), :]


def _dispatch_kernel(np_ref, fill_ref, nfill_ref, gdst_ref, sel_ref, idx_ref, lstart_ref, xn_ref,
                     pos_ref, buf_ref, xs_ref, zero_ref, sem, fill_sem, *, n_slots):
    i = pl.program_id(0)
    pos = _slot_positions(sel_ref[...], idx_ref[...], lstart_ref[0:1, :].astype(F32))
    pos_ref[...] = pos
    tt = pos.shape[0]
    pos_t = pos.T.astype(I32)
    s_iota = lax.broadcasted_iota(I32, (n_slots, tt), 0)
    perm = jnp.zeros((n_slots, tt), F32)
    for kslot in range(TOP_K):
        perm = perm + jnp.where(s_iota == pos_t[kslot:kslot + 1, :], 1.0, 0.0)
    xs_ref[...] = _dot(perm.astype(BF16), xn_ref[...])

    def copy(q):
        return pltpu.make_async_copy(_piece(xs_ref, q), _piece(buf_ref, gdst_ref[0, 0, q]), sem)

    n = np_ref[i]
    lax.fori_loop(0, n, lambda q, c: (copy(q).start(), c)[1], 0)

    @pl.when(i == 0)
    def _():
        zero_ref[...] = jnp.zeros_like(zero_ref)

        def fill(j):
            return pltpu.make_async_copy(zero_ref, _piece(buf_ref, fill_ref[j]), fill_sem)

        lax.fori_loop(0, nfill_ref[0], lambda j, c: (fill(j).start(), c)[1], 0)
        lax.fori_loop(0, nfill_ref[0], lambda j, c: (fill(j).wait(), c)[1], 0)

    lax.fori_loop(0, n, lambda q, c: (copy(q).wait(), c)[1], 0)


def _dispatch(npieces, fill, nfill, gdst3, sel, idx_l, lstart8, xn, n_rows):
    t, d = xn.shape
    n_tiles, _, n_pieces = gdst3.shape
    tt = t // n_tiles
    n_slots = n_pieces * SUBLANES
    row = lambda i, *_: (i, 0)
    grid_spec = pltpu.PrefetchScalarGridSpec(
        num_scalar_prefetch=3,
        grid=(n_tiles,),
        in_specs=[pl.BlockSpec((1, 1, n_pieces), lambda i, *_: (i, 0, 0), memory_space=pltpu.SMEM),
                  pl.BlockSpec((tt, LANES), row),
                  pl.BlockSpec((tt, LANES), row),
                  pl.BlockSpec((SUBLANES, LANES), row),
                  pl.BlockSpec((tt, d), row)],
        out_specs=[pl.BlockSpec((tt, LANES), row), pl.BlockSpec(memory_space=pl.ANY)],
        scratch_shapes=[pltpu.VMEM((n_slots, d), F32), pltpu.VMEM((SUBLANES, d), F32),
                        pltpu.SemaphoreType.DMA(()), pltpu.SemaphoreType.DMA(())],
    )
    return pl.pallas_call(
        functools.partial(_dispatch_kernel, n_slots=n_slots),
        grid_spec=grid_spec,
        out_shape=[jax.ShapeDtypeStruct((t, LANES), F32), jax.ShapeDtypeStruct((n_rows, d), F32)],
        compiler_params=_cparams(("arbitrary",)),
        name="moe_dispatch",
    )(npieces, fill, nfill, gdst3, sel, idx_l, lstart8, xn)


def _expert_kernel(be_ref, first_ref, ord_ref, next_ref, nused_ref, x_ref, w1_hbm, b1_ref, w2_hbm, b2_ref,
                   y_ref, w1f_ref, w2f_ref, w1b_ref, w2i_ref, w2b_ref, sem):
    i = pl.program_id(0)
    ff = w2f_ref.shape[1]
    used = i < nused_ref[0]
    new_expert = jnp.logical_and(used, first_ref[i] == 1)

    def fetch(expert, slot):
        return (pltpu.make_async_copy(w1_hbm.at[expert], w1f_ref.at[slot], sem.at[0, slot]),
                pltpu.make_async_copy(w2_hbm.at[expert], w2f_ref.at[slot], sem.at[1, slot]))

    @pl.when(i == 0)
    def _():
        for cp in fetch(be_ref[0], 0):
            cp.start()

    @pl.when(jnp.logical_and(new_expert, next_ref[i] >= 0))
    def _():
        for cp in fetch(next_ref[i], 1 - (ord_ref[i] & 1)):
            cp.start()

    @pl.when(new_expert)
    def _():
        slot = ord_ref[i] & 1
        for cp in fetch(be_ref[i], slot):
            cp.wait()
        rows = w1f_ref.shape[1] // CAST_CHUNKS

        def cast1(c, carry):
            r0 = pl.multiple_of(c * rows, rows)
            w1b_ref[pl.ds(r0, rows), :] = w1f_ref[slot, pl.ds(r0, rows), :].astype(BF16)
            return carry

        lax.fori_loop(0, CAST_CHUNKS, cast1, 0)
        for g in range(w2f_ref.shape[2] // LANES):
            cols = slice(g * LANES, (g + 1) * LANES)
            w2i_ref[pl.ds(0, ff // 2, stride=2), :] = w2f_ref[slot, 0:ff // 2, cols]
            w2i_ref[pl.ds(1, ff // 2, stride=2), :] = w2f_ref[slot, ff // 2:ff, cols]
            w2b_ref[:, cols] = w2i_ref[...].astype(BF16)

    @pl.when(used)
    def _():
        x = x_ref[...].astype(BF16)
        hid = _dot(x, w1b_ref[...]) + b1_ref[0]
        even = (lax.broadcasted_iota(I32, (1, LANES), 1) & 1) == 0

        def act_even(g):
            hg = hid[:, g * LANES:(g + 1) * LANES]
            glu = jnp.minimum(hg, SWIGLU_LIMIT)
            lin = jnp.clip(hg, -SWIGLU_LIMIT, SWIGLU_LIMIT) + 1.0
            return glu * jax.nn.sigmoid(SWIGLU_ALPHA * glu) * pltpu.roll(lin, LANES - 1, 1)

        half = ff // LANES
        act = jnp.concatenate(
            [jnp.where(even, act_even(g), pltpu.roll(act_even(g + half), 1, 1)) for g in range(half)], axis=1)
        y_ref[...] = _dot(act.astype(BF16), w2b_ref[...]) + b2_ref[0]

    @pl.when(jnp.logical_not(used))
    def _():
        y_ref[...] = jnp.zeros_like(y_ref)


def _experts(block_e, rows_per_expert, n_used, xbuf, w1, b1, w2, b2):
    n_rows, d = xbuf.shape
    bm = EXPERT_BLOCK
    n_blocks = n_rows // bm
    e, _, ff2 = w1.shape
    ff = ff2 // 2
    has = rows_per_expert > 0
    ids = jnp.arange(e, dtype=I32)
    later = (ids[None, :] > ids[:, None]) & has[None, :]
    next_used = jnp.where(jnp.any(later, axis=1), jnp.argmax(later, axis=1), -1).astype(I32)
    ordinal = (jnp.cumsum(has.astype(I32)) - 1).astype(I32)
    first = jnp.concatenate([jnp.ones((1,), I32), (block_e[1:] != block_e[:-1]).astype(I32)])
    of_block = block_e[:, None] == ids[None, :]
    order = jnp.sum(jnp.where(of_block, ordinal[None, :], 0), axis=1).astype(I32)
    next_e = jnp.sum(jnp.where(of_block, next_used[None, :], 0), axis=1).astype(I32)
    blk =lambda i, be, fi, od, ne, nu: (jnp.maximum(jnp.minimum(i, nu[0] - 1), 0), 0)
    ex3 = lambda i, be, fi, od, ne, nu: (be[i], 0, 0)
    grid_spec = pltpu.PrefetchScalarGridSpec(
        num_scalar_prefetch=5,
        grid=(n_blocks,),
        in_specs=[pl.BlockSpec((bm, d), blk),
                  pl.BlockSpec(memory_space=pl.ANY),
                  pl.BlockSpec((1, 1, ff2), ex3),
                  pl.BlockSpec(memory_space=pl.ANY),
                  pl.BlockSpec((1, 1, d), ex3)],
        out_specs=pl.BlockSpec((bm, d), lambda i, be, fi, od, ne, nu: (i, 0)),
        scratch_shapes=[pltpu.VMEM((2, d, ff2), F32), pltpu.VMEM((2, ff, d), F32),
                        pltpu.VMEM((d, ff2), BF16), pltpu.VMEM((ff, LANES), F32), pltpu.VMEM((ff, d), BF16),
                        pltpu.SemaphoreType.DMA((2, 2))],
    )
    return pl.pallas_call(
        _expert_kernel,
        grid_spec=grid_spec,
        out_shape=jax.ShapeDtypeStruct((n_rows, d), F32),
        compiler_params=_cparams(("arbitrary",)),
        name="moe_experts",
    )(block_e, first, order, next_e, n_used, xbuf, w1, b1, w2, b2)


def _combine_kernel(np_ref, gdst_ref, pos_ref, gate_ref, h1_ref, fw_ref, ybuf_ref, o_ref, ys_ref, sem,
                    *, n_slots):
    i = pl.program_id(0)
    n = np_ref[i]

    def copy(q):
        return pltpu.make_async_copy(_piece(ybuf_ref, gdst_ref[0, 0, q]), _piece(ys_ref, q), sem)

    lax.fori_loop(0, n, lambda q, c: (copy(q).start(), c)[1], 0)

    def zero(q, carry):
        _piece(ys_ref, q)[...] = jnp.zeros((SUBLANES, ys_ref.shape[1]), ys_ref.dtype)
        return carry

    lax.fori_loop(n, n_slots // SUBLANES, zero, 0)
    pos = pos_ref[...].astype(I32)
    gate = gate_ref[...]
    tt = pos.shape[0]
    s_iota = lax.broadcasted_iota(I32, (tt, n_slots), 1)
    weight = jnp.zeros((tt, n_slots), F32)
    for kslot in range(TOP_K):
        weight = weight + jnp.where(s_iota == pos[:, kslot:kslot + 1], gate[:, kslot:kslot + 1], 0.0)
    lax.fori_loop(0, n, lambda q, c: (copy(q).wait(), c)[1], 0)
    h = h1_ref[...] + _dot(weight.astype(BF16), ys_ref[...].astype(BF16))
    ms = jnp.mean(h * h, axis=-1, keepdims=True)
    o_ref[...] = h * lax.rsqrt(ms + NORM_EPS) * fw_ref[...]


def _combine(npieces, gdst3, pos_l, gate_l, h1, fw, ybuf):
    t, d = h1.shape
    n_tiles, _, n_pieces = gdst3.shape
    tt = t // n_tiles
    n_slots = n_pieces * SUBLANES
    row = lambda i, *_: (i, 0)
    grid_spec = pltpu.PrefetchScalarGridSpec(
        num_scalar_prefetch=1,
        grid=(n_tiles,),
        in_specs=[pl.BlockSpec((1, 1, n_pieces), lambda i, *_: (i, 0, 0), memory_space=pltpu.SMEM),
                  pl.BlockSpec((tt, LANES), row),
                  pl.BlockSpec((tt, LANES), row),
                  pl.BlockSpec((tt, d), row),
                  pl.BlockSpec((1, d), lambda i, *_: (0, 0)),
                  pl.BlockSpec(memory_space=pl.ANY)],
        out_specs=pl.BlockSpec((tt, d), row),
        scratch_shapes=[pltpu.VMEM((n_slots, d), F32), pltpu.SemaphoreType.DMA(())],
    )
    return pl.pallas_call(
        functools.partial(_combine_kernel, n_slots=n_slots),
        grid_spec=grid_spec,
        out_shape=jax.ShapeDtypeStruct((t, d), F32),
        compiler_params=_cparams(("arbitrary",)),
        name="moe_combine",
    )(npieces, gdst3, pos_l, gate_l, h1, fw, ybuf)


def _rotary_tables(positions):
    half = ROT_DIM // 2
    inv_freq = ROPE_THETA ** (-jnp.arange(0, ROT_DIM, 2, dtype=F32) / ROT_DIM)
    ang = positions.astype(F32).reshape(-1, 1) * inv_freq
    cos = jnp.tile(jnp.cos(ang), (1, LANES // half))
    sin = jnp.tile(jnp.sin(ang), (1, LANES // half))
    dim = np.arange(LANES)[None, :] % DIFF_HEAD_DIM
    ctab = jnp.where(dim < ROT_DIM, cos, 1.0)
    satab = jnp.where(dim < half, -sin, 0.0)
    sbtab = jnp.where((dim >= half) & (dim < ROT_DIM), sin, 0.0)
    return ctab, satab, sbtab


def _pad_rows(a, rows):
    return jnp.concatenate([a, jnp.zeros((rows - a.shape[0],) + a.shape[1:], a.dtype)], axis=0)


def _layer(h, l, tabs, attn_norm_w, w_in, diff_lambda_q1, diff_lambda_k1, diff_lambda_q2, diff_lambda_k2,
           diff_subln_w, rwkv_mu, rwkv_w0, rwkv_w_up, rwkv_a0, rwkv_a_up, rwkv_g_up, rwkv_k_k, rwkv_k_a,
           rwkv_r_k, rwkv_ln_w, rwkv_ln_b, w_out, ffn_norm_w, router_w, router_b, exp_w1, exp_b1,
           exp_w2, exp_b2, final_w):
    bsz, seq, d = h.shape
    t = bsz * seq
    w = RWKV_WIDTH
    lambda_init = 0.8 - 0.6 * math.exp(-0.3 * l)
    x2 = h.reshape(t, d)
    row1 = lambda a: a.reshape(1, -1).astype(F32)

    wi = w_in[l]
    qkv_cols = 3 * DIFF_WIDTH
    wq = wi[:, :qkv_cols].astype(BF16)
    o = qkv_cols + 3 * w
    zcol = lambda n: jnp.zeros((d, n), wi.dtype)
    wr = jnp.concatenate([
        wi[:, qkv_cols:o],
        wi[:, o:o + DECAY_LORA], zcol(LORA_PAD - DECAY_LORA),
        wi[:, o + DECAY_LORA:o + DECAY_LORA + AAA_LORA], zcol(LORA_PAD - AAA_LORA),
        wi[:, o + DECAY_LORA + AAA_LORA:], zcol(LORA_PAD - GATE_LORA)], axis=1).astype(BF16)
    mu = rwkv_mu[l]
    zv = lambda n: jnp.zeros((n,), mu.dtype)
    mu_p = jnp.concatenate([
        mu[:3 * w],
        mu[3 * w:3 * w + DECAY_LORA], zv(LORA_PAD - DECAY_LORA),
        mu[3 * w + DECAY_LORA:3 * w + DECAY_LORA + AAA_LORA], zv(LORA_PAD - AAA_LORA),
        mu[3 * w + DECAY_LORA + AAA_LORA:], zv(LORA_PAD - GATE_LORA)]).reshape(1, -1)

    q, k, v, zr = _inproj(x2, row1(attn_norm_w[l]), wq, wr, *tabs)

    od = _attn(q.reshape(bsz, seq, -1), k.reshape(bsz, seq, -1), v.reshape(bsz, seq, -1),
               row1(diff_lambda_q1[l]), row1(diff_lambda_k1[l]), row1(diff_lambda_q2[l]),
               row1(diff_lambda_k2[l]), row1(diff_subln_w[l]), lambda_init)

    head = np.arange(w) // RWKV_HEAD
    ones = jnp.asarray(head[:, None] == head[None, :], BF16)
    r, lw, kmod, vv, kk, bb, g, bonus = _prep(
        zr, mu_p, row1(rwkv_w0[l]), row1(rwkv_a0[l]), row1(rwkv_k_k[l]), row1(rwkv_k_a[l]),
        row1(rwkv_r_k[l]), _pad_rows(rwkv_w_up[l].astype(BF16), LORA_PAD),
        _pad_rows(rwkv_a_up[l].astype(BF16), LORA_PAD), _pad_rows(rwkv_g_up[l].astype(BF16), LORA_PAD),
        ones, seq)
    s3 = lambda a: a.reshape(bsz, seq, w)
    y = _scan(s3(r), s3(lw), s3(kmod), s3(vv), s3(kk), s3(bb)).reshape(t, w)

    n_e = router_w.shape[-1]
    rw = jnp.concatenate([router_w[l].astype(F32), jnp.zeros((d, LANES - n_e), F32)], axis=1)
    rw_hi = rw.astype(BF16)
    rw = jnp.stack([rw_hi, (rw - rw_hi.astype(F32)).astype(BF16)])
    rb = jnp.concatenate([router_b[l].astype(F32), jnp.full((LANES - n_e,), -1e30, F32)]).reshape(1, -1)
    h1, xn, sel, idx_l, gate_l, cnt8 = _mix(
        od.reshape(t, -1), y, g, bonus, x2, row1(rwkv_ln_w[l]), row1(rwkv_ln_b[l]), ones,
        w_out[l].astype(BF16), row1(ffn_norm_w[l]), rw, rb)

    bm = EXPERT_BLOCK
    pc = SUBLANES
    tm = min(ROW_TILE, t)
    n_tiles = t // tm
    cnt = cnt8.reshape(n_tiles, SUBLANES, LANES)[:, 0, :n_e]
    seg = (cnt + pc - 1) // pc * pc
    lend = jnp.cumsum(seg, axis=1)
    lstart = lend - seg
    rows_e = jnp.sum(seg, axis=0)
    padded = (rows_e + bm - 1) // bm * bm
    pad_ends = jnp.cumsum(padded)
    gstart = (pad_ends - padded)[None, :] + jnp.cumsum(seg, axis=0) - seg
    n_slots = tm * TOP_K + n_e * pc
    n_pieces = n_slots // pc
    n_blocks = -(-(t * TOP_K + n_tiles * n_e * (pc - 1) + n_e * (bm - pc)) // bm)
    n_rows = n_blocks * bm
    piece_row = jnp.arange(n_pieces, dtype=I32) * pc
    piece_e = jnp.minimum(jnp.sum(lend[:, None, :] <= piece_row[None, :, None], axis=-1), n_e - 1)
    pick = piece_e[:, :, None] == jnp.arange(n_e, dtype=I32)[None, None, :]
    take = lambda a: jnp.sum(jnp.where(pick, a[:, None, :], 0), axis=-1)
    gdst = ((take(gstart) + piece_row[None, :] - take(lstart)) // pc).astype(I32)
    gdst3 = jnp.clip(gdst, 0, n_rows // pc - 1).reshape(n_tiles, 1, n_pieces)
    npieces = (lend[:, -1] // pc).astype(I32)
    lstart8 = jnp.zeros((n_tiles, SUBLANES, LANES), I32).at[:, :, :n_e].set(lstart[:, None, :])
    lstart8 = lstart8.reshape(n_tiles * SUBLANES, LANES)
    gap_start = jnp.concatenate([pad_ends - padded + rows_e, pad_ends[-1:]]) // pc
    gap_len = jnp.concatenate([padded - rows_e, n_rows - pad_ends[-1:]]) // pc
    gap_end = jnp.cumsum(gap_len)
    max_fill = n_e * (bm // pc - 1) + (n_rows - t * TOP_K) // pc
    j = jnp.arange(max_fill, dtype=I32)
    gap = jnp.minimum(jnp.sum(gap_end[None, :] <= j[:, None], axis=1), n_e)
    in_gap = gap[:, None] == jnp.arange(n_e + 1, dtype=I32)[None, :]
    shift = jnp.sum(jnp.where(in_gap, (gap_start - gap_end + gap_len)[None, :], 0), axis=1)
    fill = jnp.clip(j + shift, 0, n_rows // pc - 1).astype(I32)
    nfill = gap_end[-1:].astype(I32)
    first_row = jnp.arange(n_blocks, dtype=I32) * bm
    block_e = jnp.minimum(jnp.sum(pad_ends[None, :] <= first_row[:, None], axis=1), n_e - 1).astype(I32)
    n_used = (pad_ends[-1] // bm).astype(I32).reshape(1)

    pos_l, xbuf = _dispatch(npieces, fill, nfill, gdst3, sel, idx_l, lstart8, xn, n_rows)

    ff2 = exp_w1.shape[-1]
    b1 = exp_b1[l].astype(F32).reshape(n_e, 1, ff2)
    b2 = exp_b2[l].astype(F32).reshape(n_e, 1, d)
    ybuf = _experts(block_e, padded, n_used, xbuf, exp_w1[l].astype(F32), b1, exp_w2[l].astype(F32), b2)

    out = _combine(npieces, gdst3, pos_l, gate_l, h1, row1(final_w), ybuf)
    return out.reshape(bsz, seq, d)


def kernel(x, positions, attn_norm_w, w_in, diff_lambda_q1, diff_lambda_k1, diff_lambda_q2, diff_lambda_k2, diff_subln_w, rwkv_mu, rwkv_w0, rwkv_w_up, rwkv_a0, rwkv_a_up, rwkv_g_up, rwkv_k_k, rwkv_k_a, rwkv_r_k, rwkv_ln_w, rwkv_ln_b, w_out, ffn_norm_w, router_w, router_b, exp_w1, exp_b1, exp_w2, exp_b2, final_norm_w):
    depth = w_in.shape[0]
    assert depth == 1, "the final norm is fused into the last (only) layer's combine kernel"
    tabs = _rotary_tables(positions)
    return _layer(x, 0, tabs, attn_norm_w, w_in, diff_lambda_q1, diff_lambda_k1, diff_lambda_q2,
                  diff_lambda_k2, diff_subln_w, rwkv_mu, rwkv_w0, rwkv_w_up, rwkv_a0, rwkv_a_up, rwkv_g_up,
                  rwkv_k_k, rwkv_k_a, rwkv_r_k, rwkv_ln_w, rwkv_ln_b, w_out, ffn_norm_w, router_w, router_b,
                  exp_w1, exp_b1, exp_w2, exp_b2, final_norm_w)
```

```python
import functools
import math

import jax
import jax.numpy as jnp
import numpy as np
from jax import lax
from jax.experimental import pallas as pl
from jax.experimental.pallas import tpu as pltpu

F32 = jnp.float32
BF16 = jnp.bfloat16
I32 = jnp.int32

DIFF_HEAD_DIM = 64
DIFF_V_DIM = 128
DIFF_HEADS = 4
DIFF_WIDTH = DIFF_HEADS * DIFF_V_DIM
ROT_DIM = 16
ROPE_THETA = 500000.0
RWKV_HEAD = 64
RWKV_HEADS = 8
RWKV_WIDTH = RWKV_HEAD * RWKV_HEADS
DECAY_LORA = 32
AAA_LORA = 32
GATE_LORA = 96
N_EXPERTS = 32
TOP_K = 4
SWIGLU_LIMIT = 7.0
SWIGLU_ALPHA = 1.702
NORM_EPS = 1e-5
RWKV_GN_EPS = 64e-5

LANES = 128
SUBLANES = 8
VMEM_LIMIT = 56 * 1024 * 1024

ROW_TILE = 512
ATTN_TILE = 512
CHUNK = 64
SCAN_TILE = 512
SCAN_BATCH = 4
GROUP = 4 * RWKV_HEAD
EXPERT_BLOCK = 512
MOE_TILE = 256
CAST_CHUNKS = 8
LORA_PAD = LANES
ZR_COLS = 3 * RWKV_WIDTH + 3 * LORA_PAD


def _cparams(sem, flags=None):
    return pltpu.CompilerParams(dimension_semantics=sem, vmem_limit_bytes=VMEM_LIMIT, flags=flags)


def _nt(a, b):
    return lax.dot_general(a, b, (((1,), (1,)), ((), ())), preferred_element_type=F32)


def _tn(a, b):
    return lax.dot_general(a, b, (((0,), (0,)), ((), ())), preferred_element_type=F32)


def _dot(a, b):
    return jnp.dot(a, b, preferred_element_type=F32)


def _split(x):
    hi = x.astype(BF16)
    return hi, (x - hi.astype(F32)).astype(BF16)


def _split_dot(x, w_bf16):
    hi, lo = _split(x)
    return _dot(hi, w_bf16) + _dot(lo, w_bf16)


def _split3_dot(x, w_hi, w_lo):
    hi, lo = _split(x)
    return _dot(hi, w_hi) + (_dot(hi, w_lo) + _dot(lo, w_hi))


def _inproj_kernel(x_ref, nw_ref, wq_ref, wr_ref, c_ref, sa_ref, sb_ref,
                   mu_ref, w0_ref, a0_ref, kk_ref, ka_ref, rk_ref, wup_ref, aup_ref, gup_ref, ones_ref,
                   q_ref, k_ref, v_ref, r_ref, lw_ref, rk_out_ref, rv_ref, kkn_ref, b_ref, g_ref, bonus_ref,
                   prev_ref, *, tiles_per_seq):
    x = x_ref[...]
    ms = jnp.mean(x * x, axis=-1, keepdims=True)
    u = (x * lax.rsqrt(ms + NORM_EPS) * nw_ref[...]).astype(BF16)
    zq = _dot(u, wq_ref[...])
    c = c_ref[...]
    sa = sa_ref[...]
    sb = sb_ref[...]
    scale = DIFF_HEAD_DIM ** -0.5
    for g in range(2 * DIFF_HEADS):
        zg = zq[:, g * LANES:(g + 1) * LANES]
        rot = zg * c + pltpu.roll(zg, LANES - ROT_DIM // 2, 1) * sa + pltpu.roll(zg, ROT_DIM // 2, 1) * sb
        if g < DIFF_HEADS:
            q_ref[:, g * LANES:(g + 1) * LANES] = (rot * scale).astype(BF16)
        else:
            h = g - DIFF_HEADS
            k_ref[:, h * LANES:(h + 1) * LANES] = rot.astype(BF16)
    v_ref[...] = zq[:, 2 * DIFF_WIDTH:3 * DIFF_WIDTH].astype(BF16)

    i = pl.program_id(0)
    z = _dot(u, wr_ref[...])
    tm = z.shape[0]
    rows = lax.broadcasted_iota(I32, (tm, 1), 0)

    @pl.when(i % tiles_per_seq == 0)
    def _():
        prev_ref[...] = jnp.zeros_like(prev_ref)

    shifted = jnp.where(rows == 0, prev_ref[SUBLANES - 1:SUBLANES, :], pltpu.roll(z, 1, 0))
    prev_ref[...] = z[tm - SUBLANES:tm, :]
    zf = z + mu_ref[...] * (shifted - z)
    w = RWKV_WIDTH
    r = zf[:, 0:w]
    k = zf[:, w:2 * w]
    v = zf[:, 2 * w:3 * w]
    wd = zf[:, 3 * w:3 * w + LORA_PAD]
    ad = zf[:, 3 * w + LORA_PAD:3 * w + 2 * LORA_PAD]
    gd = zf[:, 3 * w + 2 * LORA_PAD:3 * w + 3 * LORA_PAD]
    pre = w0_ref[...] + _split_dot(jnp.tanh(wd), wup_ref[...])
    neg = -pre
    softplus = jnp.maximum(neg, 0.0) + jnp.log(1.0 + jnp.exp(-jnp.abs(neg)))
    wlog = -softplus - 0.5
    lw_ref[...] = -jnp.exp(wlog)
    a = jax.nn.sigmoid(a0_ref[...] + _split_dot(ad, aup_ref[...]))
    g_ref[...] = _split_dot(jax.nn.sigmoid(gd), gup_ref[...]).astype(g_ref.dtype)
    ones = ones_ref[...]
    kk = k * kk_ref[...]
    norm = jnp.sqrt(_split_dot(kk * kk, ones))
    kk = kk / jnp.maximum(norm, 1e-12)
    k = k * (1.0 + (a - 1.0) * ka_ref[...])
    r_ref[...] = r.astype(r_ref.dtype)
    rk_out_ref[...] = k.astype(rk_out_ref.dtype)
    rv_ref[...] = v.astype(rv_ref.dtype)
    kkn_ref[...] = kk.astype(kkn_ref.dtype)
    b_ref[...] = (kk * a).astype(b_ref.dtype)
    bonus_ref[...] = (_split_dot(r * k * rk_ref[...], ones) * v).astype(bonus_ref.dtype)


def _inproj(x2, nw, wq, wr, ctab, satab, sbtab, mu_p, w0, a0, k_k, k_a, rk, wup, aup, gup, ones, seq):
    t, d = x2.shape
    tm = min(ROW_TILE, seq)
    w = RWKV_WIDTH
    row = lambda i: (i, 0)
    fixed = lambda i: (0, 0)
    vecw = pl.BlockSpec((1, w), fixed)
    lora = pl.BlockSpec((LORA_PAD, w), fixed)
    attn_out = pl.BlockSpec((tm, DIFF_WIDTH), row)
    feat_out = pl.BlockSpec((tm, w), row)
    feat = lambda dt: jax.ShapeDtypeStruct((t, w), dt)
    return pl.pallas_call(
        functools.partial(_inproj_kernel, tiles_per_seq=seq // tm),
        grid=(t // tm,),
        in_specs=[
            pl.BlockSpec((tm, d), row),
            pl.BlockSpec((1, d), fixed),
            pl.BlockSpec(wq.shape, fixed),
            pl.BlockSpec(wr.shape, fixed),
            pl.BlockSpec((tm, LANES), row),
            pl.BlockSpec((tm, LANES), row),
            pl.BlockSpec((tm, LANES), row),
            pl.BlockSpec((1, ZR_COLS), fixed),
            vecw, vecw, vecw, vecw, vecw, lora, lora, lora,
            pl.BlockSpec((w, w), fixed),
        ],
        out_specs=[attn_out] * 3 + [feat_out] * 8,
        out_shape=[jax.ShapeDtypeStruct((t, DIFF_WIDTH), BF16)] * 3
        + [feat(BF16), feat(F32), feat(BF16), feat(BF16), feat(BF16), feat(BF16), feat(BF16), feat(BF16)],
        scratch_shapes=[pltpu.VMEM((SUBLANES, ZR_COLS), F32)],
        compiler_params=_cparams(("arbitrary",)),
        name="inproj",
    )(x2, nw, wq, wr, ctab, satab, sbtab, mu_p, w0, a0, k_k, k_a, rk, wup, aup, gup, ones)


def _attn_kernel(q_ref, k_ref, v_ref, lq1_ref, lk1_ref, lq2_ref, lk2_ref, sw_ref, o_ref,
                 *, tile, lambda_init):
    lane = lax.broadcasted_iota(I32, (1, LANES), 1)
    first = lane < DIFF_HEAD_DIM
    neg = -1e30
    lam = (jnp.exp(jnp.sum(lq1_ref[...] * lk1_ref[...], axis=-1, keepdims=True))
           - jnp.exp(jnp.sum(lq2_ref[...] * lk2_ref[...], axis=-1, keepdims=True)) + lambda_init)
    r = lax.broadcasted_iota(I32, (tile, tile), 0)
    c = lax.broadcasted_iota(I32, (tile, tile), 1)
    keep = c <= r
    for i in range(q_ref.shape[1] // tile):
        q = q_ref[0, i * tile:(i + 1) * tile, :]
        zero = jnp.zeros_like(q)
        n = (i + 1) * tile
        kb = k_ref[0, 0:n, :]
        vb = v_ref[0, 0:n, :]
        outs = []
        for qm in (jnp.where(first, q, zero), jnp.where(first, zero, q)):
            s = _nt(qm, kb)
            diag = jnp.where(keep, s[:, i * tile:], neg)
            s = diag if i == 0 else jnp.concatenate([s[:, :i * tile], diag], axis=1)
            p = jnp.exp(s - jnp.max(s, axis=-1, keepdims=True))
            outs.append(_dot(p.astype(BF16), vb) / jnp.sum(p, axis=-1, keepdims=True))
        o = outs[0] - lam * outs[1]
        ms = jnp.mean(o * o, axis=-1, keepdims=True)
        o = o * lax.rsqrt(ms + NORM_EPS) * sw_ref[...] * (1.0 - lambda_init)
        o_ref[0, i * tile:(i + 1) * tile, :] = o.astype(o_ref.dtype)


def _attn(q3, k3, v3, lq1, lk1, lq2, lk2, sw, lambda_init):
    b, s, _ = q3.shape
    tile = min(ATTN_TILE, s)
    spec = pl.BlockSpec((1, s, LANES), lambda bi, h: (bi, 0, h))
    vec = lambda n: pl.BlockSpec((1, n), lambda bi, h: (0, 0))
    return pl.pallas_call(
        functools.partial(_attn_kernel, tile=tile, lambda_init=lambda_init),
        grid=(b, DIFF_HEADS),
        in_specs=[spec, spec, spec, vec(DIFF_HEAD_DIM), vec(DIFF_HEAD_DIM), vec(DIFF_HEAD_DIM),
                  vec(DIFF_HEAD_DIM), vec(DIFF_V_DIM)],
        out_specs=spec,
        out_shape=jax.ShapeDtypeStruct((b, s, DIFF_WIDTH), BF16),
        compiler_params=_cparams(("parallel", "parallel")),
        name="attn",
    )(q3, k3, v3, lq1, lk1, lq2, lk2, sw)


def _scan_kernel(r_ref, lw_ref, k_ref, v_ref, kk_ref, b_ref, y_ref, state_ref,
                 wr_s, ut_s, utt_s, arb_s, pv_s, bh_s, vk_s, wt_s, *, n_chunks):
    L = CHUNK
    G = GROUP
    n_groups = RWKV_WIDTH // G

    @pl.when(pl.program_id(1) == 0)
    def _():
        state_ref[...] = jnp.zeros_like(state_ref)

    row = lax.broadcasted_iota(I32, (L, G), 0)
    colr = lax.broadcasted_iota(I32, (L, G), 1) & (L - 1)
    strict = (colr < row).astype(F32)
    incl = (colr <= row).astype(F32)
    eye = (colr == row).astype(F32)
    eye_l = (lax.broadcasted_iota(I32, (L, L), 0) == lax.broadcasted_iota(I32, (L, L), 1)).astype(BF16)
    br = lax.broadcasted_iota(I32, (G, G), 0) >> 6
    bc = lax.broadcasted_iota(I32, (G, G), 1) >> 6
    block = (br == bc).astype(F32)
    block_bf = block.astype(BF16)
    rows1 = lax.broadcasted_iota(I32, (L, 1), 0)

    def stack4(x):
        xb = x.astype(BF16)
        return jnp.concatenate([xb, xb, xb, xb], axis=0) * block_bf

    def cat(a, b):
        return jnp.concatenate([a, b], axis=0).astype(BF16)

    def precompute(it, carry):
        chains = [(cl, g) for cl in range(SCAN_BATCH) for g in range(n_groups)]
        each = lambda f, *lists: [f(*args) for args in zip(*lists)]

        def load(ref):
            out = []
            for cl, g in chains:
                base = pl.multiple_of((it * SCAN_BATCH + cl) * L, L)
                out.append(ref[0, pl.ds(base, L), g * G:(g + 1) * G].astype(F32))
            return out

        r, lw, k, v, kk, b = load(r_ref), load(lw_ref), load(k_ref), load(v_ref), load(kk_ref), load(b_ref)

        def cumsum(x):
            sh = 1
            while sh < L:
                x = x + jnp.where(rows1 >= sh, pltpu.roll(x, sh, 0), 0.0)
                sh *= 2
            return x

        cs = each(cumsum, lw)
        tot = each(lambda c: c[L - 1:L, :], cs)
        a_hat = each(lambda kk_, c, l: -kk_ * jnp.exp(c - l), kk, cs, lw)
        r_hat = each(lambda r_, c: r_ * jnp.exp(c), r, cs)
        w_inv = each(lambda c: jnp.exp(-c), cs)
        w_end = each(lambda t_, c: jnp.exp(t_ - c), tot, cs)
        lhs = each(cat, a_hat, r_hat)
        ab = each(lambda l_, b_, wi: _nt(l_, stack4(b_ * wi)), lhs, b, w_inv)
        ak = each(lambda l_, k_, wi: _nt(l_, stack4(k_ * wi)), lhs, k, w_inv)
        a_ab = each(lambda x: x[:L] * strict, ab)
        a_rb = each(lambda x: (x[L:] * incl).astype(BF16), ab)
        a_k = each(lambda x: cat(x[:L] * strict, x[L:] * incl), ak)
        t_mat = each(lambda a: eye + a, a_ab)
        p_mat = each(lambda a: _dot(a.astype(BF16), stack4(a)), a_ab)
        for _ in range(4):
            tp = each(lambda t_, p_: _dot(cat(t_, p_), stack4(p_)), t_mat, p_mat)
            t_mat = each(lambda t_, x: t_ + x[:L], t_mat, tp)
            p_mat = each(lambda x: x[L:], tp)
        t_bf = each(lambda t_, p_: (t_ + _dot(t_.astype(BF16), stack4(p_))).astype(BF16), t_mat, p_mat)
        av = each(lambda a, v_: _dot(a, stack4(v_)), a_k, v)
        w_til = each(lambda t_, a: _dot(t_, stack4(a)), t_bf, a_hat)
        u_til = each(lambda t_, x: _dot(t_, stack4(x[:L])), t_bf, av)
        u_til_t = each(lambda u_: _tn(u_.astype(BF16), eye_l), u_til)
        vk = each(lambda v_, k_, we: _tn(v_.astype(BF16), (k_ * we).astype(BF16)) * block, v, k, w_end)
        for n, (cl, g) in enumerate(chains):
            slot = (it * SCAN_BATCH + cl) * n_groups + g
            wr_s[slot] = cat(w_til[n], r_hat[n])
            ut_s[slot] = u_til[n]
            utt_s[slot] = u_til_t[n]
            arb_s[slot] = a_rb[n]
            pv_s[slot] = av[n][L:]
            bh_s[slot] = (b[n] * w_end[n]).astype(BF16)
            vk_s[slot] = vk[n]
            wt_s[slot] = jnp.broadcast_to(jnp.exp(tot[n]), (SUBLANES, G))
        return carry

    lax.fori_loop(0, n_chunks // SCAN_BATCH, precompute, 0)

    def recur(c, carry):
        gs = range(n_groups)
        slots = [c * n_groups + g for g in gs]
        s0 = [state_ref[g] for g in gs]
        s0b = [s.astype(BF16) for s in s0]
        wr = [wr_s[sl] for sl in slots]
        u_t = [_nt(s0b[g], wr[g][:L]) + utt_s[slots[g]] for g in gs]
        ub = [_dot(u_t[g].astype(BF16), bh_s[slots[g]]) for g in gs]
        for g in gs:
            state_ref[g] = s0[g] * wt_s[slots[g]][0:1, :] + ub[g] * block + vk_s[slots[g]]
        uy = [_nt(wr[g], s0b[g]) for g in gs]
        u = [uy[g][:L] + ut_s[slots[g]] for g in gs]
        base = pl.multiple_of(c * L, L)
        for g in gs:
            y = uy[g][L:] + _dot(arb_s[slots[g]], stack4(u[g])) + pv_s[slots[g]]
            y_ref[0, pl.ds(base, L), g * G:(g + 1) * G] = y
        return carry

    lax.fori_loop(0, n_chunks, recur, 0)


def _scan(r3, lw3, k3, v3, kk3, b3):
    bsz, s, w = r3.shape
    ts = min(SCAN_TILE, s)
    n_chunks = ts // CHUNK
    slots = n_chunks * (w // GROUP)
    L, G = CHUNK, GROUP
    spec = pl.BlockSpec((1, ts, w), lambda bi, i: (bi, i, 0))
    return pl.pallas_call(
        functools.partial(_scan_kernel, n_chunks=n_chunks),
        grid=(bsz, s // ts),
        in_specs=[spec] * 6,
        out_specs=spec,
        out_shape=jax.ShapeDtypeStruct((bsz, s, w), F32),
        scratch_shapes=[pltpu.VMEM((w // GROUP, G, G), F32),
                        pltpu.VMEM((slots, 2 * L, G), BF16),
                        pltpu.VMEM((slots, L, G), F32),
                        pltpu.VMEM((slots, G, L), F32),
                        pltpu.VMEM((slots, L, G), BF16),
                        pltpu.VMEM((slots, L, G), F32),
                        pltpu.VMEM((slots, L, G), BF16),
                        pltpu.VMEM((slots, G, G), F32),
                        pltpu.VMEM((slots, SUBLANES, G), F32)],
        compiler_params=_cparams(("parallel", "arbitrary")),
        name="rwkv_scan",
    )(r3, lw3, k3, v3, kk3, b3)


def _mix_kernel(od_ref, y_ref, g_ref, bonus_ref, x_ref, lnw_ref, lnb_ref, ones_ref, wo_ref,
                fw_ref, rw_ref, rb_ref,
                h1_ref, xn_ref, sel_ref, idx_ref, gate_ref, cnt_ref):
    ones = ones_ref[...]
    y = y_ref[...]
    inv_n = 1.0 / RWKV_HEAD
    mean = _split_dot(y, ones) * inv_n
    d = y - mean
    var = _split_dot(d * d, ones) * inv_n
    yn = d * lax.rsqrt(var + RWKV_GN_EPS) * lnw_ref[...] + lnb_ref[...]
    orw = ((yn + bonus_ref[...]) * g_ref[...]).astype(BF16)
    h1 = (x_ref[...] + _dot(od_ref[...], wo_ref[0:DIFF_WIDTH, :])
          + _dot(orw, wo_ref[DIFF_WIDTH:DIFF_WIDTH + RWKV_WIDTH, :]))
    h1_ref[...] = h1
    ms = jnp.mean(h1 * h1, axis=-1, keepdims=True)
    xn = h1 * lax.rsqrt(ms + NORM_EPS) * fw_ref[...]
    xn_ref[...] = xn.astype(xn_ref.dtype)
    logits = _split3_dot(xn, rw_ref[0], rw_ref[1]) + rb_ref[...]
    tm = logits.shape[0]
    lane = lax.broadcasted_iota(I32, (tm, LANES), 1).astype(F32)
    work = logits
    sel = jnp.zeros((tm, LANES), F32)
    idx_l = jnp.zeros((tm, LANES), F32)
    val_l = jnp.zeros((tm, LANES), F32)
    top = None
    for kslot in range(TOP_K):
        m = jnp.max(work, axis=-1, keepdims=True)
        pick = jnp.min(jnp.where(work == m, lane, float(LANES)), axis=-1, keepdims=True)
        hit = lane == pick
        sel = jnp.where(hit, 1.0, sel)
        idx_l = jnp.where(lane == kslot, pick, idx_l)
        if top is None:
            top = m
        val_l = jnp.where(lane == kslot, jnp.exp(m - top), val_l)
        work = jnp.where(hit, -jnp.inf, work)
    sel_ref[...] = sel
    idx_ref[...] = idx_l.astype(I32)
    gate_ref[...] = val_l / jnp.sum(val_l, axis=-1, keepdims=True)
    for part in range(tm // MOE_TILE):
        count = jnp.sum(sel[part * MOE_TILE:(part + 1) * MOE_TILE], axis=0, keepdims=True)
        cnt_ref[part * SUBLANES:(part + 1) * SUBLANES, :] = jnp.broadcast_to(count, (SUBLANES, LANES)).astype(I32)


def _mix(od, y, g, bonus, x2, lnw, lnb, ones, wo, fw, rw, rb):
    t, d = x2.shape
    tm = min(ROW_TILE, t)
    w = RWKV_WIDTH
    row = lambda i: (i, 0)
    fixed = lambda i: (0, 0)
    rs = lambda n: pl.BlockSpec((tm, n), row)
    return pl.pallas_call(
        _mix_kernel,
        grid=(t // tm,),
        in_specs=[rs(DIFF_WIDTH), rs(w), rs(w), rs(w), rs(d),
                  pl.BlockSpec((1, w), fixed), pl.BlockSpec((1, w), fixed),
                  pl.BlockSpec((w, w), fixed), pl.BlockSpec(wo.shape, fixed),
                  pl.BlockSpec((1, d), fixed), pl.BlockSpec(rw.shape, lambda i: (0, 0, 0)),
                  pl.BlockSpec((1, LANES), fixed)],
        out_specs=[rs(d), rs(d), rs(LANES), rs(LANES), rs(LANES),
                   pl.BlockSpec((tm // MOE_TILE * SUBLANES, LANES), row)],
        out_shape=[jax.ShapeDtypeStruct((t, d), F32), jax.ShapeDtypeStruct((t, d), BF16),
                   jax.ShapeDtypeStruct((t, LANES), F32), jax.ShapeDtypeStruct((t, LANES), I32),
                   jax.ShapeDtypeStruct((t, LANES), F32),
                   jax.ShapeDtypeStruct((t // MOE_TILE * SUBLANES, LANES), I32)],
        compiler_params=_cparams(("parallel",)),
        name="mix_router",
    )(od, y, g, bonus, x2, lnw, lnb, ones, wo, fw, rw, rb)


def _slot_positions(sel, idx_l, lstart):
    tt = sel.shape[0]
    r = lax.broadcasted_iota(I32, (tt, tt), 0)
    c = lax.broadcasted_iota(I32, (tt, tt), 1)
    lower = (c < r).astype(BF16)
    where_to = _dot(lower, sel.astype(BF16)) + lstart
    lane = lax.broadcasted_iota(I32, (tt, LANES), 1).astype(F32)
    idx = idx_l.astype(F32)
    pos = jnp.full((tt, LANES), -1.0, F32)
    for kslot in range(TOP_K):
        e = jnp.sum(jnp.where(lane == kslot, idx, 0.0), axis=-1, keepdims=True)
        p = jnp.sum(jnp.where(lane == e, where_to, 0.0), axis=-1, keepdims=True)
        pos = jnp.where(lane == kslot, p, pos)
    return pos


def _piece(ref, q):
    return ref.at[pl.ds(pl.multiple_of(q * SUBLANES, SUBLANES), SUBLANES), :]


def _dispatch_kernel(np_ref, fill_ref, nfill_ref, gdst_ref, sel_ref, idx_ref, lstart_ref, xn_ref,
                     pos_ref, buf_ref, xs_ref, zero_ref, sem, fill_sem, *, n_slots):
    i = pl.program_id(0)
    pos = _slot_positions(sel_ref[...], idx_ref[...], lstart_ref[0:1, :].astype(F32))
    pos_ref[...] = pos
    tt = pos.shape[0]
    pos_t = pos.T.astype(I32)
    s_iota = lax.broadcasted_iota(I32, (n_slots, tt), 0)
    perm = jnp.zeros((n_slots, tt), F32)
    for kslot in range(TOP_K):
        perm = perm + jnp.where(s_iota == pos_t[kslot:kslot + 1, :], 1.0, 0.0)
    xs_ref[...] = _dot(perm.astype(BF16), xn_ref[...])

    def copy(q):
        return pltpu.make_async_copy(_piece(xs_ref, q), _piece(buf_ref, gdst_ref[0, 0, q]), sem)

    n = np_ref[i]
    lax.fori_loop(0, n, lambda q, c: (copy(q).start(), c)[1], 0)

    @pl.when(i == 0)
    def _():
        zero_ref[...] = jnp.zeros_like(zero_ref)

        def fill(j):
            return pltpu.make_async_copy(zero_ref, _piece(buf_ref, fill_ref[j]), fill_sem)

        lax.fori_loop(0, nfill_ref[0], lambda j, c: (fill(j).start(), c)[1], 0)
        lax.fori_loop(0, nfill_ref[0], lambda j, c: (fill(j).wait(), c)[1], 0)

    lax.fori_loop(0, n, lambda q, c: (copy(q).wait(), c)[1], 0)


def _dispatch(npieces, fill, nfill, gdst3, sel, idx_l, lstart8, xn, n_rows):
    t, d = xn.shape
    n_tiles, _, n_pieces = gdst3.shape
    tt = t // n_tiles
    n_slots = n_pieces * SUBLANES
    row = lambda i, *_: (i, 0)
    grid_spec = pltpu.PrefetchScalarGridSpec(
        num_scalar_prefetch=3,
        grid=(n_tiles,),
        in_specs=[pl.BlockSpec((1, 1, n_pieces), lambda i, *_: (i, 0, 0), memory_space=pltpu.SMEM),
                  pl.BlockSpec((tt, LANES), row),
                  pl.BlockSpec((tt, LANES), row),
                  pl.BlockSpec((SUBLANES, LANES), row),
                  pl.BlockSpec((tt, d), row)],
        out_specs=[pl.BlockSpec((tt, LANES), row), pl.BlockSpec(memory_space=pl.ANY)],
        scratch_shapes=[pltpu.VMEM((n_slots, d), F32), pltpu.VMEM((SUBLANES, d), F32),
                        pltpu.SemaphoreType.DMA(()), pltpu.SemaphoreType.DMA(())],
    )
    return pl.pallas_call(
        functools.partial(_dispatch_kernel, n_slots=n_slots),
        grid_spec=grid_spec,
        out_shape=[jax.ShapeDtypeStruct((t, LANES), F32), jax.ShapeDtypeStruct((n_rows, d), F32)],
        compiler_params=_cparams(("arbitrary",)),
        name="moe_dispatch",
    )(npieces, fill, nfill, gdst3, sel, idx_l, lstart8, xn)


def _expert_kernel(be_ref, first_ref, ord_ref, next_ref, nused_ref, x_ref, w1_hbm, b1_ref, w2_hbm, b2_ref,
                   y_ref, w1f_ref, w2f_ref, w1b_ref, w2i_ref, w2b_ref, sem):
    i = pl.program_id(0)
    ff = w2f_ref.shape[1]
    used = i < nused_ref[0]
    new_expert = jnp.logical_and(used, first_ref[i] == 1)

    def fetch(expert, slot):
        return (pltpu.make_async_copy(w1_hbm.at[expert], w1f_ref.at[slot], sem.at[0, slot]),
                pltpu.make_async_copy(w2_hbm.at[expert], w2f_ref.at[slot], sem.at[1, slot]))

    @pl.when(i == 0)
    def _():
        for cp in fetch(be_ref[0], 0):
            cp.start()

    @pl.when(jnp.logical_and(new_expert, next_ref[i] >= 0))
    def _():
        for cp in fetch(next_ref[i], 1 - (ord_ref[i] & 1)):
            cp.start()

    @pl.when(new_expert)
    def _():
        slot = ord_ref[i] & 1
        for cp in fetch(be_ref[i], slot):
            cp.wait()
        rows = w1f_ref.shape[1] // CAST_CHUNKS

        def cast1(c, carry):
            r0 = pl.multiple_of(c * rows, rows)
            w1b_ref[pl.ds(r0, rows), :] = w1f_ref[slot, pl.ds(r0, rows), :].astype(BF16)
            return carry

        lax.fori_loop(0, CAST_CHUNKS, cast1, 0)
        for g in range(w2f_ref.shape[2] // LANES):
            cols = slice(g * LANES, (g + 1) * LANES)
            w2i_ref[pl.ds(0, ff // 2, stride=2), :] = w2f_ref[slot, 0:ff // 2, cols]
            w2i_ref[pl.ds(1, ff // 2, stride=2), :] = w2f_ref[slot, ff // 2:ff, cols]
            w2b_ref[:, cols] = w2i_ref[...].astype(BF16)

    @pl.when(used)
    def _():
        x = x_ref[...].astype(BF16)
        hid = _dot(x, w1b_ref[...]) + b1_ref[0]
        even = (lax.broadcasted_iota(I32, (1, LANES), 1) & 1) == 0

        def act_even(g):
            hg = hid[:, g * LANES:(g + 1) * LANES]
            glu = jnp.minimum(hg, SWIGLU_LIMIT)
            lin = jnp.clip(hg, -SWIGLU_LIMIT, SWIGLU_LIMIT) + 1.0
            return glu * jax.nn.sigmoid(SWIGLU_ALPHA * glu) * pltpu.roll(lin, LANES - 1, 1)

        half = ff // LANES
        act = jnp.concatenate(
            [jnp.where(even, act_even(g), pltpu.roll(act_even(g + half), 1, 1)) for g in range(half)], axis=1)
        y_ref[...] = _dot(act.astype(BF16), w2b_ref[...]) + b2_ref[0]

    @pl.when(jnp.logical_not(used))
    def _():
        y_ref[...] = jnp.zeros_like(y_ref)


def _experts(block_e, rows_per_expert, n_used, xbuf, w1, b1, w2, b2):
    n_rows, d = xbuf.shape
    bm = EXPERT_BLOCK
    n_blocks = n_rows // bm
    e, _, ff2 = w1.shape
    ff = ff2 // 2
    has = rows_per_expert > 0
    ids = jnp.arange(e, dtype=I32)
    later = (ids[None, :] > ids[:, None]) & has[None, :]
    next_used = jnp.where(jnp.any(later, axis=1), jnp.argmax(later, axis=1), -1).astype(I32)
    ordinal = (jnp.cumsum(has.astype(I32)) - 1).astype(I32)
    first = jnp.concatenate([jnp.ones((1,), I32), (block_e[1:] != block_e[:-1]).astype(I32)])
    of_block = block_e[:, None] == ids[None, :]
    order = jnp.sum(jnp.where(of_block, ordinal[None, :], 0), axis=1).astype(I32)
    next_e = jnp.sum(jnp.where(of_block, next_used[None, :], 0), axis=1).astype(I32)
    blk =lambda i, be, fi, od, ne, nu: (jnp.maximum(jnp.minimum(i, nu[0] - 1), 0), 0)
    ex3 = lambda i, be, fi, od, ne, nu: (be[i], 0, 0)
    grid_spec = pltpu.PrefetchScalarGridSpec(
        num_scalar_prefetch=5,
        grid=(n_blocks,),
        in_specs=[pl.BlockSpec((bm, d), blk),
                  pl.BlockSpec(memory_space=pl.ANY),
                  pl.BlockSpec((1, 1, ff2), ex3),
                  pl.BlockSpec(memory_space=pl.ANY),
                  pl.BlockSpec((1, 1, d), ex3)],
        out_specs=pl.BlockSpec((bm, d), lambda i, be, fi, od, ne, nu: (i, 0)),
        scratch_shapes=[pltpu.VMEM((2, d, ff2), F32), pltpu.VMEM((2, ff, d), F32),
                        pltpu.VMEM((d, ff2), BF16), pltpu.VMEM((ff, LANES), F32), pltpu.VMEM((ff, d), BF16),
                        pltpu.SemaphoreType.DMA((2, 2))],
    )
    return pl.pallas_call(
        _expert_kernel,
        grid_spec=grid_spec,
        out_shape=jax.ShapeDtypeStruct((n_rows, d), F32),
        compiler_params=_cparams(("arbitrary",)),
        name="moe_experts",
    )(block_e, first, order, next_e, n_used, xbuf, w1, b1, w2, b2)


def _combine_kernel(np_ref, gdst_ref, pos_ref, gate_ref, h1_ref, fw_ref, ybuf_ref, o_ref, ys_ref, sem,
                    *, n_slots):
    i = pl.program_id(0)
    n = np_ref[i]

    def copy(q):
        return pltpu.make_async_copy(_piece(ybuf_ref, gdst_ref[0, 0, q]), _piece(ys_ref, q), sem)

    lax.fori_loop(0, n, lambda q, c: (copy(q).start(), c)[1], 0)

    def zero(q, carry):
        _piece(ys_ref, q)[...] = jnp.zeros((SUBLANES, ys_ref.shape[1]), ys_ref.dtype)
        return carry

    lax.fori_loop(n, n_slots // SUBLANES, zero, 0)
    pos = pos_ref[...].astype(I32)
    gate = gate_ref[...]
    tt = pos.shape[0]
    s_iota = lax.broadcasted_iota(I32, (tt, n_slots), 1)
    weight = jnp.zeros((tt, n_slots), F32)
    for kslot in range(TOP_K):
        weight = weight + jnp.where(s_iota == pos[:, kslot:kslot + 1], gate[:, kslot:kslot + 1], 0.0)
    lax.fori_loop(0, n, lambda q, c: (copy(q).wait(), c)[1], 0)
    h = h1_ref[...] + _dot(weight.astype(BF16), ys_ref[...].astype(BF16))
    ms = jnp.mean(h * h, axis=-1, keepdims=True)
    o_ref[...] = h * lax.rsqrt(ms + NORM_EPS) * fw_ref[...]


def _combine(npieces, gdst3, pos_l, gate_l, h1, fw, ybuf):
    t, d = h1.shape
    n_tiles, _, n_pieces = gdst3.shape
    tt = t // n_tiles
    n_slots = n_pieces * SUBLANES
    row = lambda i, *_: (i, 0)
    grid_spec = pltpu.PrefetchScalarGridSpec(
        num_scalar_prefetch=1,
        grid=(n_tiles,),
        in_specs=[pl.BlockSpec((1, 1, n_pieces), lambda i, *_: (i, 0, 0), memory_space=pltpu.SMEM),
                  pl.BlockSpec((tt, LANES), row),
                  pl.BlockSpec((tt, LANES), row),
                  pl.BlockSpec((tt, d), row),
                  pl.BlockSpec((1, d), lambda i, *_: (0, 0)),
                  pl.BlockSpec(memory_space=pl.ANY)],
        out_specs=pl.BlockSpec((tt, d), row),
        scratch_shapes=[pltpu.VMEM((n_slots, d), F32), pltpu.SemaphoreType.DMA(())],
    )
    return pl.pallas_call(
        functools.partial(_combine_kernel, n_slots=n_slots),
        grid_spec=grid_spec,
        out_shape=jax.ShapeDtypeStruct((t, d), F32),
        compiler_params=_cparams(("arbitrary",)),
        name="moe_combine",
    )(npieces, gdst3, pos_l, gate_l, h1, fw, ybuf)


def _rotary_tables(positions):
    half = ROT_DIM // 2
    inv_freq = ROPE_THETA ** (-jnp.arange(0, ROT_DIM, 2, dtype=F32) / ROT_DIM)
    ang = positions.astype(F32).reshape(-1, 1) * inv_freq
    cos = jnp.tile(jnp.cos(ang), (1, LANES // half))
    sin = jnp.tile(jnp.sin(ang), (1, LANES // half))
    dim = np.arange(LANES)[None, :] % DIFF_HEAD_DIM
    ctab = jnp.where(dim < ROT_DIM, cos, 1.0)
    satab = jnp.where(dim < half, -sin, 0.0)
    sbtab = jnp.where((dim >= half) & (dim < ROT_DIM), sin, 0.0)
    return ctab, satab, sbtab


def _pad_rows(a, rows):
    return jnp.concatenate([a, jnp.zeros((rows - a.shape[0],) + a.shape[1:], a.dtype)], axis=0)


def _layer(h, l, tabs, attn_norm_w, w_in, diff_lambda_q1, diff_lambda_k1, diff_lambda_q2, diff_lambda_k2,
           diff_subln_w, rwkv_mu, rwkv_w0, rwkv_w_up, rwkv_a0, rwkv_a_up, rwkv_g_up, rwkv_k_k, rwkv_k_a,
           rwkv_r_k, rwkv_ln_w, rwkv_ln_b, w_out, ffn_norm_w, router_w, router_b, exp_w1, exp_b1,
           exp_w2, exp_b2, final_w):
    bsz, seq, d = h.shape
    t = bsz * seq
    w = RWKV_WIDTH
    lambda_init = 0.8 - 0.6 * math.exp(-0.3 * l)
    x2 = h.reshape(t, d)
    row1 = lambda a: a.reshape(1, -1).astype(F32)

    wi = w_in[l]
    qkv_cols = 3 * DIFF_WIDTH
    wq = wi[:, :qkv_cols].astype(BF16)
    o = qkv_cols + 3 * w
    zcol = lambda n: jnp.zeros((d, n), wi.dtype)
    wr = jnp.concatenate([
        wi[:, qkv_cols:o],
        wi[:, o:o + DECAY_LORA], zcol(LORA_PAD - DECAY_LORA),
        wi[:, o + DECAY_LORA:o + DECAY_LORA + AAA_LORA], zcol(LORA_PAD - AAA_LORA),
        wi[:, o + DECAY_LORA + AAA_LORA:], zcol(LORA_PAD - GATE_LORA)], axis=1).astype(BF16)
    mu = rwkv_mu[l]
    zv = lambda n: jnp.zeros((n,), mu.dtype)
    mu_p = jnp.concatenate([
        mu[:3 * w],
        mu[3 * w:3 * w + DECAY_LORA], zv(LORA_PAD - DECAY_LORA),
        mu[3 * w + DECAY_LORA:3 * w + DECAY_LORA + AAA_LORA], zv(LORA_PAD - AAA_LORA),
        mu[3 * w + DECAY_LORA + AAA_LORA:], zv(LORA_PAD - GATE_LORA)]).reshape(1, -1)

    head = np.arange(w) // RWKV_HEAD
    ones = jnp.asarray(head[:, None] == head[None, :], BF16)
    q, k, v, r, lw, kmod, vv, kk, bb, g, bonus = _inproj(
        x2, row1(attn_norm_w[l]), wq, wr, *tabs,
        mu_p, row1(rwkv_w0[l]), row1(rwkv_a0[l]), row1(rwkv_k_k[l]), row1(rwkv_k_a[l]),
        row1(rwkv_r_k[l]), _pad_rows(rwkv_w_up[l].astype(BF16), LORA_PAD),
        _pad_rows(rwkv_a_up[l].astype(BF16), LORA_PAD), _pad_rows(rwkv_g_up[l].astype(BF16), LORA_PAD),
        ones, seq)

    od = _attn(q.reshape(bsz, seq, -1), k.reshape(bsz, seq, -1), v.reshape(bsz, seq, -1),
               row1(diff_lambda_q1[l]), row1(diff_lambda_k1[l]), row1(diff_lambda_q2[l]),
               row1(diff_lambda_k2[l]), row1(diff_subln_w[l]), lambda_init)

    s3 = lambda a: a.reshape(bsz, seq, w)
    y = _scan(s3(r), s3(lw), s3(kmod), s3(vv), s3(kk), s3(bb)).reshape(t, w)

    n_e = router_w.shape[-1]
    rw = jnp.concatenate([router_w[l].astype(F32), jnp.zeros((d, LANES - n_e), F32)], axis=1)
    rw_hi = rw.astype(BF16)
    rw = jnp.stack([rw_hi, (rw - rw_hi.astype(F32)).astype(BF16)])
    rb = jnp.concatenate([router_b[l].astype(F32), jnp.full((LANES - n_e,), -1e30, F32)]).reshape(1, -1)
    h1, xn, sel, idx_l, gate_l, cnt8 = _mix(
        od.reshape(t, -1), y, g, bonus, x2, row1(rwkv_ln_w[l]), row1(rwkv_ln_b[l]), ones,
        w_out[l].astype(BF16), row1(ffn_norm_w[l]), rw, rb)

    bm = EXPERT_BLOCK
    pc = SUBLANES
    tm = MOE_TILE
    n_tiles = t // tm
    cnt = cnt8.reshape(n_tiles, SUBLANES, LANES)[:, 0, :n_e]
    seg = (cnt + pc - 1) // pc * pc
    lend = jnp.cumsum(seg, axis=1)
    lstart = lend - seg
    rows_e = jnp.sum(seg, axis=0)
    padded = (rows_e + bm - 1) // bm * bm
    pad_ends = jnp.cumsum(padded)
    gstart = (pad_ends - padded)[None, :] + jnp.cumsum(seg, axis=0) - seg
    n_slots = tm * TOP_K + n_e * pc
    n_pieces = n_slots // pc
    n_blocks = -(-(t * TOP_K + n_tiles * n_e * (pc - 1) + n_e * (bm - pc)) // bm)
    n_rows = n_blocks * bm
    piece_row = jnp.arange(n_pieces, dtype=I32) * pc
    piece_e = jnp.minimum(jnp.sum(lend[:, None, :] <= piece_row[None, :, None], axis=-1), n_e - 1)
    pick = piece_e[:, :, None] == jnp.arange(n_e, dtype=I32)[None, None, :]
    take = lambda a: jnp.sum(jnp.where(pick, a[:, None, :], 0), axis=-1)
    gdst = ((take(gstart) + piece_row[None, :] - take(lstart)) // pc).astype(I32)
    gdst3 = jnp.clip(gdst, 0, n_rows // pc - 1).reshape(n_tiles, 1, n_pieces)
    npieces = (lend[:, -1] // pc).astype(I32)
    lstart8 = jnp.zeros((n_tiles, SUBLANES, LANES), I32).at[:, :, :n_e].set(lstart[:, None, :])
    lstart8 = lstart8.reshape(n_tiles * SUBLANES, LANES)
    gap_start = jnp.concatenate([pad_ends - padded + rows_e, pad_ends[-1:]]) // pc
    gap_len = jnp.concatenate([padded - rows_e, n_rows - pad_ends[-1:]]) // pc
    gap_end = jnp.cumsum(gap_len)
    max_fill = n_e * (bm // pc - 1) + (n_rows - t * TOP_K) // pc
    j = jnp.arange(max_fill, dtype=I32)
    gap = jnp.minimum(jnp.sum(gap_end[None, :] <= j[:, None], axis=1), n_e)
    in_gap = gap[:, None] == jnp.arange(n_e + 1, dtype=I32)[None, :]
    shift = jnp.sum(jnp.where(in_gap, (gap_start - gap_end + gap_len)[None, :], 0), axis=1)
    fill = jnp.clip(j + shift, 0, n_rows // pc - 1).astype(I32)
    nfill = gap_end[-1:].astype(I32)
    first_row = jnp.arange(n_blocks, dtype=I32) * bm
    block_e = jnp.minimum(jnp.sum(pad_ends[None, :] <= first_row[:, None], axis=1), n_e - 1).astype(I32)
    n_used = (pad_ends[-1] // bm).astype(I32).reshape(1)

    pos_l, xbuf = _dispatch(npieces, fill, nfill, gdst3, sel, idx_l, lstart8, xn, n_rows)

    ff2 = exp_w1.shape[-1]
    b1 = exp_b1[l].astype(F32).reshape(n_e, 1, ff2)
    b2 = exp_b2[l].astype(F32).reshape(n_e, 1, d)
    ybuf = _experts(block_e, padded, n_used, xbuf, exp_w1[l].astype(F32), b1, exp_w2[l].astype(F32), b2)

    out = _combine(npieces, gdst3, pos_l, gate_l, h1, row1(final_w), ybuf)
    return out.reshape(bsz, seq, d)


def kernel(x, positions, attn_norm_w, w_in, diff_lambda_q1, diff_lambda_k1, diff_lambda_q2, diff_lambda_k2, diff_subln_w, rwkv_mu, rwkv_w0, rwkv_w_up, rwkv_a0, rwkv_a_up, rwkv_g_up, rwkv_k_k, rwkv_k_a, rwkv_r_k, rwkv_ln_w, rwkv_ln_b, w_out, ffn_norm_w, router_w, router_b, exp_w1, exp_b1, exp_w2, exp_b2, final_norm_w):
    depth = w_in.shape[0]
    assert depth == 1, "the final norm is fused into the last (only) layer's combine kernel"
    tabs = _rotary_tables(positions)
    return _layer(x, 0, tabs, attn_norm_w, w_in, diff_lambda_q1, diff_lambda_k1, diff_lambda_q2,
                  diff_lambda_k2, diff_subln_w, rwkv_mu, rwkv_w0, rwkv_w_up, rwkv_a0, rwkv_a_up, rwkv_g_up,
                  rwkv_k_k, rwkv_k_a, rwkv_r_k, rwkv_ln_w, rwkv_ln_b, w_out, ffn_norm_w, router_w, router_b,
                  exp_w1, exp_b1, exp_w2, exp_b2, final_norm_w)
```

```python
import functools
import math

import jax
import jax.numpy as jnp
import numpy as np
from jax import lax
from jax.experimental import pallas as pl
from jax.experimental.pallas import tpu as pltpu

F32 = jnp.float32
BF16 = jnp.bfloat16
I32 = jnp.int32

DIFF_HEAD_DIM = 64
DIFF_V_DIM = 128
DIFF_HEADS = 4
DIFF_WIDTH = DIFF_HEADS * DIFF_V_DIM
ROT_DIM = 16
ROPE_THETA = 500000.0
RWKV_HEAD = 64
RWKV_HEADS = 8
RWKV_WIDTH = RWKV_HEAD * RWKV_HEADS
DECAY_LORA = 32
AAA_LORA = 32
GATE_LORA = 96
N_EXPERTS = 32
TOP_K = 4
SWIGLU_LIMIT = 7.0
SWIGLU_ALPHA = 1.702
NORM_EPS = 1e-5
RWKV_GN_EPS = 64e-5

LANES = 128
SUBLANES = 8
VMEM_LIMIT = 56 * 1024 * 1024

ROW_TILE = 512
ATTN_TILE = 512
CHUNK = 64
SCAN_TILE = 512
SCAN_BATCH = 4
GROUP = 4 * RWKV_HEAD
EXPERT_BLOCK = 512
MOE_TILE = 256
CAST_CHUNKS = 8
LORA_PAD = LANES
ZR_COLS = 3 * RWKV_WIDTH + 3 * LORA_PAD


def _cparams(sem, flags=None):
    return pltpu.CompilerParams(dimension_semantics=sem, vmem_limit_bytes=VMEM_LIMIT, flags=flags)


def _nt(a, b):
    return lax.dot_general(a, b, (((1,), (1,)), ((), ())), preferred_element_type=F32)


def _tn(a, b):
    return lax.dot_general(a, b, (((0,), (0,)), ((), ())), preferred_element_type=F32)


def _dot(a, b):
    return jnp.dot(a, b, preferred_element_type=F32)


def _split(x):
    hi = x.astype(BF16)
    return hi, (x - hi.astype(F32)).astype(BF16)


def _split_dot(x, w_bf16):
    hi, lo = _split(x)
    return _dot(hi, w_bf16) + _dot(lo, w_bf16)


def _split3_dot(x, w_hi, w_lo):
    hi, lo = _split(x)
    return _dot(hi, w_hi) + (_dot(hi, w_lo) + _dot(lo, w_hi))


def _inproj_kernel(x_ref, nw_ref, wq_ref, wr_ref, c_ref, sa_ref, sb_ref,
                   mu_ref, w0_ref, a0_ref, kk_ref, ka_ref, rk_ref, wup_ref, aup_ref, gup_ref, ones_ref,
                   q_ref, k_ref, v_ref, r_ref, lw_ref, rk_out_ref, rv_ref, kkn_ref, b_ref, g_ref, bonus_ref,
                   prev_ref, *, tiles_per_seq):
    x = x_ref[...]
    ms = jnp.mean(x * x, axis=-1, keepdims=True)
    u = (x * lax.rsqrt(ms + NORM_EPS) * nw_ref[...]).astype(BF16)
    zq = _dot(u, wq_ref[...])
    c = c_ref[...]
    sa = sa_ref[...]
    sb = sb_ref[...]
    scale = DIFF_HEAD_DIM ** -0.5
    for g in range(2 * DIFF_HEADS):
        zg = zq[:, g * LANES:(g + 1) * LANES]
        rot = zg * c + pltpu.roll(zg, LANES - ROT_DIM // 2, 1) * sa + pltpu.roll(zg, ROT_DIM // 2, 1) * sb
        if g < DIFF_HEADS:
            q_ref[:, g * LANES:(g + 1) * LANES] = (rot * scale).astype(BF16)
        else:
            h = g - DIFF_HEADS
            k_ref[:, h * LANES:(h + 1) * LANES] = rot.astype(BF16)
    v_ref[...] = zq[:, 2 * DIFF_WIDTH:3 * DIFF_WIDTH].astype(BF16)

    i = pl.program_id(0)
    z = _dot(u, wr_ref[...])
    tm = z.shape[0]
    rows = lax.broadcasted_iota(I32, (tm, 1), 0)

    @pl.when(i % tiles_per_seq == 0)
    def _():
        prev_ref[...] = jnp.zeros_like(prev_ref)

    shifted = jnp.where(rows == 0, prev_ref[SUBLANES - 1:SUBLANES, :], pltpu.roll(z, 1, 0))
    prev_ref[...] = z[tm - SUBLANES:tm, :]
    zf = z + mu_ref[...] * (shifted - z)
    w = RWKV_WIDTH
    r = zf[:, 0:w]
    k = zf[:, w:2 * w]
    v = zf[:, 2 * w:3 * w]
    wd = zf[:, 3 * w:3 * w + LORA_PAD]
    ad = zf[:, 3 * w + LORA_PAD:3 * w + 2 * LORA_PAD]
    gd = zf[:, 3 * w + 2 * LORA_PAD:3 * w + 3 * LORA_PAD]
    pre = w0_ref[...] + _split_dot(jnp.tanh(wd), wup_ref[...])
    neg = -pre
    softplus = jnp.maximum(neg, 0.0) + jnp.log(1.0 + jnp.exp(-jnp.abs(neg)))
    wlog = -softplus - 0.5
    lw_ref[...] = -jnp.exp(wlog)
    a = jax.nn.sigmoid(a0_ref[...] + _split_dot(ad, aup_ref[...]))
    g_ref[...] = _split_dot(jax.nn.sigmoid(gd), gup_ref[...]).astype(g_ref.dtype)
    ones = ones_ref[...]
    kk = k * kk_ref[...]
    norm = jnp.sqrt(_split_dot(kk * kk, ones))
    kk = kk / jnp.maximum(norm, 1e-12)
    k = k * (1.0 + (a - 1.0) * ka_ref[...])
    r_ref[...] = r.astype(r_ref.dtype)
    rk_out_ref[...] = k.astype(rk_out_ref.dtype)
    rv_ref[...] = v.astype(rv_ref.dtype)
    kkn_ref[...] = kk.astype(kkn_ref.dtype)
    b_ref[...] = (kk * a).astype(b_ref.dtype)
    bonus_ref[...] = (_split_dot(r * k * rk_ref[...], ones) * v).astype(bonus_ref.dtype)


def _inproj(x2, nw, wq, wr, ctab, satab, sbtab, mu_p, w0, a0, k_k, k_a, rk, wup, aup, gup, ones, seq):
    t, d = x2.shape
    tm = min(ROW_TILE, seq)
    w = RWKV_WIDTH
    row = lambda i: (i, 0)
    fixed = lambda i: (0, 0)
    vecw = pl.BlockSpec((1, w), fixed)
    lora = pl.BlockSpec((LORA_PAD, w), fixed)
    attn_out = pl.BlockSpec((tm, DIFF_WIDTH), row)
    feat_out = pl.BlockSpec((tm, w), row)
    feat = lambda dt: jax.ShapeDtypeStruct((t, w), dt)
    return pl.pallas_call(
        functools.partial(_inproj_kernel, tiles_per_seq=seq // tm),
        grid=(t // tm,),
        in_specs=[
            pl.BlockSpec((tm, d), row),
            pl.BlockSpec((1, d), fixed),
            pl.BlockSpec(wq.shape, fixed),
            pl.BlockSpec(wr.shape, fixed),
            pl.BlockSpec((tm, LANES), row),
            pl.BlockSpec((tm, LANES), row),
            pl.BlockSpec((tm, LANES), row),
            pl.BlockSpec((1, ZR_COLS), fixed),
            vecw, vecw, vecw, vecw, vecw, lora, lora, lora,
            pl.BlockSpec((w, w), fixed),
        ],
        out_specs=[attn_out] * 3 + [feat_out] * 8,
        out_shape=[jax.ShapeDtypeStruct((t, DIFF_WIDTH), BF16)] * 3
        + [feat(BF16), feat(F32), feat(BF16), feat(BF16), feat(BF16), feat(BF16), feat(BF16), feat(BF16)],
        scratch_shapes=[pltpu.VMEM((SUBLANES, ZR_COLS), F32)],
        compiler_params=_cparams(("arbitrary",)),
        name="inproj",
    )(x2, nw, wq, wr, ctab, satab, sbtab, mu_p, w0, a0, k_k, k_a, rk, wup, aup, gup, ones)


def _attn_kernel(q_ref, k_ref, v_ref, lq1_ref, lk1_ref, lq2_ref, lk2_ref, sw_ref, o_ref,
                 *, tile, lambda_init):
    lane = lax.broadcasted_iota(I32, (1, LANES), 1)
    first = lane < DIFF_HEAD_DIM
    neg = -1e30
    lam = (jnp.exp(jnp.sum(lq1_ref[...] * lk1_ref[...], axis=-1, keepdims=True))
           - jnp.exp(jnp.sum(lq2_ref[...] * lk2_ref[...], axis=-1, keepdims=True)) + lambda_init)
    r = lax.broadcasted_iota(I32, (tile, tile), 0)
    c = lax.broadcasted_iota(I32, (tile, tile), 1)
    keep = c <= r
    for i in range(q_ref.shape[1] // tile):
        q = q_ref[0, i * tile:(i + 1) * tile, :]
        zero = jnp.zeros_like(q)
        n = (i + 1) * tile
        kb = k_ref[0, 0:n, :]
        vb = v_ref[0, 0:n, :]
        outs = []
        for qm in (jnp.where(first, q, zero), jnp.where(first, zero, q)):
            s = _nt(qm, kb)
            diag = jnp.where(keep, s[:, i * tile:], neg)
            s = diag if i == 0 else jnp.concatenate([s[:, :i * tile], diag], axis=1)
            p = jnp.exp(s - jnp.max(s, axis=-1, keepdims=True))
            outs.append(_dot(p.astype(BF16), vb) / jnp.sum(p, axis=-1, keepdims=True))
        o = outs[0] - lam * outs[1]
        ms = jnp.mean(o * o, axis=-1, keepdims=True)
        o = o * lax.rsqrt(ms + NORM_EPS) * sw_ref[...] * (1.0 - lambda_init)
        o_ref[0, i * tile:(i + 1) * tile, :] = o.astype(o_ref.dtype)


def _attn(q3, k3, v3, lq1, lk1, lq2, lk2, sw, lambda_init):
    b, s, _ = q3.shape
    tile = min(ATTN_TILE, s)
    spec = pl.BlockSpec((1, s, LANES), lambda bi, h: (bi, 0, h))
    vec = lambda n: pl.BlockSpec((1, n), lambda bi, h: (0, 0))
    return pl.pallas_call(
        functools.partial(_attn_kernel, tile=tile, lambda_init=lambda_init),
        grid=(b, DIFF_HEADS),
        in_specs=[spec, spec, spec, vec(DIFF_HEAD_DIM), vec(DIFF_HEAD_DIM), vec(DIFF_HEAD_DIM),
                  vec(DIFF_HEAD_DIM), vec(DIFF_V_DIM)],
        out_specs=spec,
        out_shape=jax.ShapeDtypeStruct((b, s, DIFF_WIDTH), BF16),
        compiler_params=_cparams(("parallel", "parallel")),
        name="attn",
    )(q3, k3, v3, lq1, lk1, lq2, lk2, sw)


def _scan_kernel(r_ref, lw_ref, k_ref, v_ref, kk_ref, b_ref, y_ref, state_ref,
                 wr_s, ut_s, utt_s, arb_s, pv_s, bh_s, vk_s, wt_s, *, n_chunks):
    L = CHUNK
    G = GROUP
    n_groups = RWKV_WIDTH // G

    @pl.when(pl.program_id(1) == 0)
    def _():
        state_ref[...] = jnp.zeros_like(state_ref)

    row = lax.broadcasted_iota(I32, (L, G), 0)
    colr = lax.broadcasted_iota(I32, (L, G), 1) & (L - 1)
    strict = (colr < row).astype(F32)
    incl = (colr <= row).astype(F32)
    eye = (colr == row).astype(F32)
    eye_l = (lax.broadcasted_iota(I32, (L, L), 0) == lax.broadcasted_iota(I32, (L, L), 1)).astype(BF16)
    br = lax.broadcasted_iota(I32, (G, G), 0) >> 6
    bc = lax.broadcasted_iota(I32, (G, G), 1) >> 6
    block = (br == bc).astype(F32)
    block_bf = block.astype(BF16)
    rows1 = lax.broadcasted_iota(I32, (L, 1), 0)

    def stack4(x):
        xb = x.astype(BF16)
        return jnp.concatenate([xb, xb, xb, xb], axis=0) * block_bf

    def cat(a, b):
        return jnp.concatenate([a, b], axis=0).astype(BF16)

    def precompute(it, carry):
        chains = [(cl, g) for cl in range(SCAN_BATCH) for g in range(n_groups)]
        each = lambda f, *lists: [f(*args) for args in zip(*lists)]

        def load(ref):
            out = []
            for cl, g in chains:
                base = pl.multiple_of((it * SCAN_BATCH + cl) * L, L)
                out.append(ref[0, pl.ds(base, L), g * G:(g + 1) * G].astype(F32))
            return out

        r, lw, k, v, kk, b = load(r_ref), load(lw_ref), load(k_ref), load(v_ref), load(kk_ref), load(b_ref)

        def cumsum(x):
            sh = 1
            while sh < L:
                x = x + jnp.where(rows1 >= sh, pltpu.roll(x, sh, 0), 0.0)
                sh *= 2
            return x

        cs = each(cumsum, lw)
        tot = each(lambda c: c[L - 1:L, :], cs)
        a_hat = each(lambda kk_, c, l: -kk_ * jnp.exp(c - l), kk, cs, lw)
        r_hat = each(lambda r_, c: r_ * jnp.exp(c), r, cs)
        w_inv = each(lambda c: jnp.exp(-c), cs)
        w_end = each(lambda t_, c: jnp.exp(t_ - c), tot, cs)
        lhs = each(cat, a_hat, r_hat)
        ab = each(lambda l_, b_, wi: _nt(l_, stack4(b_ * wi)), lhs, b, w_inv)
        ak = each(lambda l_, k_, wi: _nt(l_, stack4(k_ * wi)), lhs, k, w_inv)
        a_ab = each(lambda x: x[:L] * strict, ab)
        a_rb = each(lambda x: (x[L:] * incl).astype(BF16), ab)
        a_k = each(lambda x: cat(x[:L] * strict, x[L:] * incl), ak)
        t_mat = each(lambda a: eye + a, a_ab)
        p_mat = each(lambda a: _dot(a.astype(BF16), stack4(a)), a_ab)
        for _ in range(4):
            tp = each(lambda t_, p_: _dot(cat(t_, p_), stack4(p_)), t_mat, p_mat)
            t_mat = each(lambda t_, x: t_ + x[:L], t_mat, tp)
            p_mat = each(lambda x: x[L:], tp)
        t_bf = each(lambda t_, p_: (t_ + _dot(t_.astype(BF16), stack4(p_))).astype(BF16), t_mat, p_mat)
        av = each(lambda a, v_: _dot(a, stack4(v_)), a_k, v)
        w_til = each(lambda t_, a: _dot(t_, stack4(a)), t_bf, a_hat)
        u_til = each(lambda t_, x: _dot(t_, stack4(x[:L])), t_bf, av)
        u_til_t = each(lambda u_: _tn(u_.astype(BF16), eye_l), u_til)
        vk = each(lambda v_, k_, we: _tn(v_.astype(BF16), (k_ * we).astype(BF16)) * block, v, k, w_end)
        for n, (cl, g) in enumerate(chains):
            slot = (it * SCAN_BATCH + cl) * n_groups + g
            wr_s[slot] = cat(w_til[n], r_hat[n])
            ut_s[slot] = u_til[n]
            utt_s[slot] = u_til_t[n]
            arb_s[slot] = a_rb[n]
            pv_s[slot] = av[n][L:]
            bh_s[slot] = (b[n] * w_end[n]).astype(BF16)
            vk_s[slot] = vk[n]
            wt_s[slot] = jnp.broadcast_to(jnp.exp(tot[n]), (SUBLANES, G))
        return carry

    lax.fori_loop(0, n_chunks // SCAN_BATCH, precompute, 0)

    def recur(c, carry):
        gs = range(n_groups)
        slots = [c * n_groups + g for g in gs]
        s0 = [state_ref[g] for g in gs]
        s0b = [s.astype(BF16) for s in s0]
        wr = [wr_s[sl] for sl in slots]
        u_t = [_nt(s0b[g], wr[g][:L]) + utt_s[slots[g]] for g in gs]
        ub = [_dot(u_t[g].astype(BF16), bh_s[slots[g]]) for g in gs]
        for g in gs:
            state_ref[g] = s0[g] * wt_s[slots[g]][0:1, :] + ub[g] * block + vk_s[slots[g]]
        uy = [_nt(wr[g], s0b[g]) for g in gs]
        u = [uy[g][:L] + ut_s[slots[g]] for g in gs]
        base = pl.multiple_of(c * L, L)
        for g in gs:
            y = uy[g][L:] + _dot(arb_s[slots[g]], stack4(u[g])) + pv_s[slots[g]]
            y_ref[0, pl.ds(base, L), g * G:(g + 1) * G] = y
        return carry

    lax.fori_loop(0, n_chunks, recur, 0)


def _scan(r3, lw3, k3, v3, kk3, b3):
    bsz, s, w = r3.shape
    ts = min(SCAN_TILE, s)
    n_chunks = ts // CHUNK
    slots = n_chunks * (w // GROUP)
    L, G = CHUNK, GROUP
    spec = pl.BlockSpec((1, ts, w), lambda bi, i: (bi, i, 0))
    return pl.pallas_call(
        functools.partial(_scan_kernel, n_chunks=n_chunks),
        grid=(bsz, s // ts),
        in_specs=[spec] * 6,
        out_specs=spec,
        out_shape=jax.ShapeDtypeStruct((bsz, s, w), F32),
        scratch_shapes=[pltpu.VMEM((w // GROUP, G, G), F32),
                        pltpu.VMEM((slots, 2 * L, G), BF16),
                        pltpu.VMEM((slots, L, G), F32),
                        pltpu.VMEM((slots, G, L), F32),
                        pltpu.VMEM((slots, L, G), BF16),
                        pltpu.VMEM((slots, L, G), F32),
                        pltpu.VMEM((slots, L, G), BF16),
                        pltpu.VMEM((slots, G, G), F32),
                        pltpu.VMEM((slots, SUBLANES, G), F32)],
        compiler_params=_cparams(("parallel", "arbitrary")),
        name="rwkv_scan",
    )(r3, lw3, k3, v3, kk3, b3)


def _mix_kernel(od_ref, y_ref, g_ref, bonus_ref, x_ref, lnw_ref, lnb_ref, ones_ref, wo_ref,
                fw_ref, rw_ref, rb_ref,
                h1_ref, xn_ref, sel_ref, idx_ref, gate_ref, cnt_ref):
    ones = ones_ref[...]
    y = y_ref[...]
    inv_n = 1.0 / RWKV_HEAD
    mean = _split_dot(y, ones) * inv_n
    d = y - mean
    var = _split_dot(d * d, ones) * inv_n
    yn = d * lax.rsqrt(var + RWKV_GN_EPS) * lnw_ref[...] + lnb_ref[...]
    orw = ((yn + bonus_ref[...]) * g_ref[...]).astype(BF16)
    h1 = (x_ref[...] + _dot(od_ref[...], wo_ref[0:DIFF_WIDTH, :])
          + _dot(orw, wo_ref[DIFF_WIDTH:DIFF_WIDTH + RWKV_WIDTH, :]))
    h1_ref[...] = h1
    ms = jnp.mean(h1 * h1, axis=-1, keepdims=True)
    xn = h1 * lax.rsqrt(ms + NORM_EPS) * fw_ref[...]
    xn_ref[...] = xn.astype(xn_ref.dtype)
    logits = _split3_dot(xn, rw_ref[0], rw_ref[1]) + rb_ref[...]
    tm = logits.shape[0]
    lane = lax.broadcasted_iota(I32, (tm, LANES), 1).astype(F32)
    work = logits
    sel = jnp.zeros((tm, LANES), F32)
    idx_l = jnp.zeros((tm, LANES), F32)
    val_l = jnp.zeros((tm, LANES), F32)
    top = None
    for kslot in range(TOP_K):
        m = jnp.max(work, axis=-1, keepdims=True)
        pick = jnp.min(jnp.where(work == m, lane, float(LANES)), axis=-1, keepdims=True)
        hit = lane == pick
        sel = jnp.where(hit, 1.0, sel)
        idx_l = jnp.where(lane == kslot, pick, idx_l)
        if top is None:
            top = m
        val_l = jnp.where(lane == kslot, jnp.exp(m - top), val_l)
        work = jnp.where(hit, -jnp.inf, work)
    sel_ref[...] = sel
    idx_ref[...] = idx_l.astype(I32)
    gate_ref[...] = val_l / jnp.sum(val_l, axis=-1, keepdims=True)
    for part in range(tm // MOE_TILE):
        count = jnp.sum(sel[part * MOE_TILE:(part + 1) * MOE_TILE], axis=0, keepdims=True)
        cnt_ref[part * SUBLANES:(part + 1) * SUBLANES, :] = jnp.broadcast_to(count, (SUBLANES, LANES)).astype(I32)


def _mix(od, y, g, bonus, x2, lnw, lnb, ones, wo, fw, rw, rb):
    t, d = x2.shape
    tm = min(ROW_TILE, t)
    w = RWKV_WIDTH
    row = lambda i: (i, 0)
    fixed = lambda i: (0, 0)
    rs = lambda n: pl.BlockSpec((tm, n), row)
    return pl.pallas_call(
        _mix_kernel,
        grid=(t // tm,),
        in_specs=[rs(DIFF_WIDTH), rs(w), rs(w), rs(w), rs(d),
                  pl.BlockSpec((1, w), fixed), pl.BlockSpec((1, w), fixed),
                  pl.BlockSpec((w, w), fixed), pl.BlockSpec(wo.shape, fixed),
                  pl.BlockSpec((1, d), fixed), pl.BlockSpec(rw.shape, lambda i: (0, 0, 0)),
                  pl.BlockSpec((1, LANES), fixed)],
        out_specs=[rs(d), rs(d), rs(LANES), rs(LANES), rs(LANES),
                   pl.BlockSpec((tm // MOE_TILE * SUBLANES, LANES), row)],
        out_shape=[jax.ShapeDtypeStruct((t, d), F32), jax.ShapeDtypeStruct((t, d), BF16),
                   jax.ShapeDtypeStruct((t, LANES), F32), jax.ShapeDtypeStruct((t, LANES), I32),
                   jax.ShapeDtypeStruct((t, LANES), F32),
                   jax.ShapeDtypeStruct((t // MOE_TILE * SUBLANES, LANES), I32)],
        compiler_params=_cparams(("parallel",)),
        name="mix_router",
    )(od, y, g, bonus, x2, lnw, lnb, ones, wo, fw, rw, rb)


def _slot_positions(sel, idx_l, lstart):
    tt = sel.shape[0]
    r = lax.broadcasted_iota(I32, (tt, tt), 0)
    c = lax.broadcasted_iota(I32, (tt, tt), 1)
    lower = (c < r).astype(BF16)
    where_to = _dot(lower, sel.astype(BF16)) + lstart
    lane = lax.broadcasted_iota(I32, (tt, LANES), 1).astype(F32)
    idx = idx_l.astype(F32)
    pos = jnp.full((tt, LANES), -1.0, F32)
    for kslot in range(TOP_K):
        e = jnp.sum(jnp.where(lane == kslot, idx, 0.0), axis=-1, keepdims=True)
        p = jnp.sum(jnp.where(lane == e, where_to, 0.0), axis=-1, keepdims=True)
        pos = jnp.where(lane == kslot, p, pos)
    return pos


def _piece(ref, q):
    return ref.at[pl.ds(pl.multiple_of(q * SUBLANES, SUBLANES), SUBLANES), :]


def _dispatch_kernel(np_ref, fill_ref, nfill_ref, gdst_ref, sel_ref, idx_ref, lstart_ref, xn_ref,
                     pos_ref, buf_ref, xs_ref, zero_ref, sem, fill_sem, *, n_slots):
    i = pl.program_id(0)
    last = pl.num_programs(0) - 1
    slot = i % 2

    def copy(q, sl):
        return pltpu.make_async_copy(_piece(xs_ref.at[sl], q), _piece(buf_ref, gdst_ref[0, 0, q]), sem.at[sl])

    def drain(count, sl):
        lax.fori_loop(0, count, lambda q, c: (copy(0, sl).wait(), c)[1], 0)

    @pl.when(i >= 2)
    def _():
        drain(np_ref[jnp.maximum(i - 2, 0)], slot)

    pos = _slot_positions(sel_ref[...], idx_ref[...], lstart_ref[0:1, :].astype(F32))
    pos_ref[...] = pos
    tt = pos.shape[0]
    pos_t = pos.T.astype(I32)
    s_iota = lax.broadcasted_iota(I32, (n_slots, tt), 0)
    perm = jnp.zeros((n_slots, tt), F32)
    for kslot in range(TOP_K):
        perm = perm + jnp.where(s_iota == pos_t[kslot:kslot + 1, :], 1.0, 0.0)
    xs_ref[slot] = _dot(perm.astype(BF16), xn_ref[...])
    lax.fori_loop(0, np_ref[i], lambda q, c: (copy(q, slot).start(), c)[1], 0)

    @pl.when(i == 0)
    def _():
        zero_ref[...] = jnp.zeros_like(zero_ref)

        def fill(j):
            return pltpu.make_async_copy(zero_ref, _piece(buf_ref, fill_ref[j]), fill_sem)

        lax.fori_loop(0, nfill_ref[0], lambda j, c: (fill(j).start(), c)[1], 0)
        lax.fori_loop(0, nfill_ref[0], lambda j, c: (fill(j).wait(), c)[1], 0)

    @pl.when(i == last)
    def _():
        @pl.when(i >= 1)
        def _():
            drain(np_ref[jnp.maximum(i - 1, 0)], 1 - slot)

        drain(np_ref[i], slot)


def _dispatch(npieces, fill, nfill, gdst3, sel, idx_l, lstart8, xn, n_rows):
    t, d = xn.shape
    n_tiles, _, n_pieces = gdst3.shape
    tt = t // n_tiles
    n_slots = n_pieces * SUBLANES
    row = lambda i, *_: (i, 0)
    grid_spec = pltpu.PrefetchScalarGridSpec(
        num_scalar_prefetch=3,
        grid=(n_tiles,),
        in_specs=[pl.BlockSpec((1, 1, n_pieces), lambda i, *_: (i, 0, 0), memory_space=pltpu.SMEM),
                  pl.BlockSpec((tt, LANES), row),
                  pl.BlockSpec((tt, LANES), row),
                  pl.BlockSpec((SUBLANES, LANES), row),
                  pl.BlockSpec((tt, d), row)],
        out_specs=[pl.BlockSpec((tt, LANES), row), pl.BlockSpec(memory_space=pl.ANY)],
        scratch_shapes=[pltpu.VMEM((2, n_slots, d), F32), pltpu.VMEM((SUBLANES, d), F32),
                        pltpu.SemaphoreType.DMA((2,)), pltpu.SemaphoreType.DMA(())],
    )
    return pl.pallas_call(
        functools.partial(_dispatch_kernel, n_slots=n_slots),
        grid_spec=grid_spec,
        out_shape=[jax.ShapeDtypeStruct((t, LANES), F32), jax.ShapeDtypeStruct((n_rows, d), F32)],
        compiler_params=_cparams(("arbitrary",)),
        name="moe_dispatch",
    )(npieces, fill, nfill, gdst3, sel, idx_l, lstart8, xn)


def _expert_kernel(be_ref, first_ref, ord_ref, next_ref, nused_ref, x_ref, w1_hbm, b1_ref, w2_hbm, b2_ref,
                   y_ref, w1f_ref, w2f_ref, w1b_ref, w2i_ref, w2b_ref, sem):
    i = pl.program_id(0)
    ff = w2f_ref.shape[1]
    used = i < nused_ref[0]
    new_expert = jnp.logical_and(used, first_ref[i] == 1)

    def fetch(expert, slot):
        return (pltpu.make_async_copy(w1_hbm.at[expert], w1f_ref.at[slot], sem.at[0, slot]),
                pltpu.make_async_copy(w2_hbm.at[expert], w2f_ref.at[slot], sem.at[1, slot]))

    @pl.when(i == 0)
    def _():
        for cp in fetch(be_ref[0], 0):
            cp.start()

    @pl.when(jnp.logical_and(new_expert, next_ref[i] >= 0))
    def _():
        for cp in fetch(next_ref[i], 1 - (ord_ref[i] & 1)):
            cp.start()

    @pl.when(new_expert)
    def _():
        slot = ord_ref[i] & 1
        for cp in fetch(be_ref[i], slot):
            cp.wait()
        rows = w1f_ref.shape[1] // CAST_CHUNKS

        def cast1(c, carry):
            r0 = pl.multiple_of(c * rows, rows)
            w1b_ref[pl.ds(r0, rows), :] = w1f_ref[slot, pl.ds(r0, rows), :].astype(BF16)
            return carry

        lax.fori_loop(0, CAST_CHUNKS, cast1, 0)
        for g in range(w2f_ref.shape[2] // LANES):
            cols = slice(g * LANES, (g + 1) * LANES)
            w2i_ref[pl.ds(0, ff // 2, stride=2), :] = w2f_ref[slot, 0:ff // 2, cols]
            w2i_ref[pl.ds(1, ff // 2, stride=2), :] = w2f_ref[slot, ff // 2:ff, cols]
            w2b_ref[:, cols] = w2i_ref[...].astype(BF16)

    @pl.when(used)
    def _():
        x = x_ref[...].astype(BF16)
        hid = _dot(x, w1b_ref[...]) + b1_ref[0]
        even = (lax.broadcasted_iota(I32, (1, LANES), 1) & 1) == 0

        def act_even(g):
            hg = hid[:, g * LANES:(g + 1) * LANES]
            glu = jnp.minimum(hg, SWIGLU_LIMIT)
            lin = jnp.clip(hg, -SWIGLU_LIMIT, SWIGLU_LIMIT) + 1.0
            return glu * jax.nn.sigmoid(SWIGLU_ALPHA * glu) * pltpu.roll(lin, LANES - 1, 1)

        half = ff // LANES
        act = jnp.concatenate(
            [jnp.where(even, act_even(g), pltpu.roll(act_even(g + half), 1, 1)) for g in range(half)], axis=1)
        y_ref[...] = _dot(act.astype(BF16), w2b_ref[...]) + b2_ref[0]

    @pl.when(jnp.logical_not(used))
    def _():
        y_ref[...] = jnp.zeros_like(y_ref)


def _experts(block_e, rows_per_expert, n_used, xbuf, w1, b1, w2, b2):
    n_rows, d = xbuf.shape
    bm = EXPERT_BLOCK
    n_blocks = n_rows // bm
    e, _, ff2 = w1.shape
    ff = ff2 // 2
    has = rows_per_expert > 0
    ids = jnp.arange(e, dtype=I32)
    later = (ids[None, :] > ids[:, None]) & has[None, :]
    next_used = jnp.where(jnp.any(later, axis=1), jnp.argmax(later, axis=1), -1).astype(I32)
    ordinal = (jnp.cumsum(has.astype(I32)) - 1).astype(I32)
    first = jnp.concatenate([jnp.ones((1,), I32), (block_e[1:] != block_e[:-1]).astype(I32)])
    of_block = block_e[:, None] == ids[None, :]
    order = jnp.sum(jnp.where(of_block, ordinal[None, :], 0), axis=1).astype(I32)
    next_e = jnp.sum(jnp.where(of_block, next_used[None, :], 0), axis=1).astype(I32)
    blk =lambda i, be, fi, od, ne, nu: (jnp.maximum(jnp.minimum(i, nu[0] - 1), 0), 0)
    ex3 = lambda i, be, fi, od, ne, nu: (be[i], 0, 0)
    grid_spec = pltpu.PrefetchScalarGridSpec(
        num_scalar_prefetch=5,
        grid=(n_blocks,),
        in_specs=[pl.BlockSpec((bm, d), blk),
                  pl.BlockSpec(memory_space=pl.ANY),
                  pl.BlockSpec((1, 1, ff2), ex3),
                  pl.BlockSpec(memory_space=pl.ANY),
                  pl.BlockSpec((1, 1, d), ex3)],
        out_specs=pl.BlockSpec((bm, d), lambda i, be, fi, od, ne, nu: (i, 0)),
        scratch_shapes=[pltpu.VMEM((2, d, ff2), F32), pltpu.VMEM((2, ff, d), F32),
                        pltpu.VMEM((d, ff2), BF16), pltpu.VMEM((ff, LANES), F32), pltpu.VMEM((ff, d), BF16),
                        pltpu.SemaphoreType.DMA((2, 2))],
    )
    return pl.pallas_call(
        _expert_kernel,
        grid_spec=grid_spec,
        out_shape=jax.ShapeDtypeStruct((n_rows, d), F32),
        compiler_params=_cparams(("arbitrary",)),
        name="moe_experts",
    )(block_e, first, order, next_e, n_used, xbuf, w1, b1, w2, b2)


def _combine_kernel(np_ref, gdst_ref, gnext_ref, pos_ref, gate_ref, h1_ref, fw_ref, ybuf_ref, o_ref, ys_ref, sem,
                    *, n_slots):
    i = pl.program_id(0)
    last = pl.num_programs(0) - 1
    slot = i % 2

    def fetch(table_ref, tile, sl):
        count = np_ref[tile]

        def copy(q):
            return pltpu.make_async_copy(_piece(ybuf_ref, table_ref[0, 0, q]), _piece(ys_ref.at[sl], q), sem.at[sl])

        lax.fori_loop(0, count, lambda q, c: (copy(q).start(), c)[1], 0)

        def zero(q, carry):
            _piece(ys_ref.at[sl], q)[...] = jnp.zeros((SUBLANES, ys_ref.shape[2]), ys_ref.dtype)
            return carry

        lax.fori_loop(count, n_slots // SUBLANES, zero, 0)

    @pl.when(i == 0)
    def _():
        fetch(gdst_ref, i, slot)

    @pl.when(i < last)
    def _():
        fetch(gnext_ref, jnp.minimum(i + 1, last), 1 - slot)

    n = np_ref[i]

    def copy(q):
        return pltpu.make_async_copy(_piece(ybuf_ref, gdst_ref[0, 0, q]), _piece(ys_ref.at[slot], q), sem.at[slot])

    pos = pos_ref[...].astype(I32)
    gate = gate_ref[...]
    tt = pos.shape[0]
    s_iota = lax.broadcasted_iota(I32, (tt, n_slots), 1)
    weight = jnp.zeros((tt, n_slots), F32)
    for kslot in range(TOP_K):
        weight = weight + jnp.where(s_iota == pos[:, kslot:kslot + 1], gate[:, kslot:kslot + 1], 0.0)
    lax.fori_loop(0, n, lambda q, c: (copy(q).wait(), c)[1], 0)
    h = h1_ref[...] + _dot(weight.astype(BF16), ys_ref[slot].astype(BF16))
    ms = jnp.mean(h * h, axis=-1, keepdims=True)
    o_ref[...] = h * lax.rsqrt(ms + NORM_EPS) * fw_ref[...]


def _combine(npieces, gdst3, pos_l, gate_l, h1, fw, ybuf):
    t, d = h1.shape
    n_tiles, _, n_pieces = gdst3.shape
    tt = t // n_tiles
    n_slots = n_pieces * SUBLANES
    row = lambda i, *_: (i, 0)
    grid_spec = pltpu.PrefetchScalarGridSpec(
        num_scalar_prefetch=1,
        grid=(n_tiles,),
        in_specs=[pl.BlockSpec((1, 1, n_pieces), lambda i, *_: (i, 0, 0), memory_space=pltpu.SMEM),
                  pl.BlockSpec((1, 1, n_pieces), lambda i, *_: (jnp.minimum(i + 1, n_tiles - 1), 0, 0),
                               memory_space=pltpu.SMEM),
                  pl.BlockSpec((tt, LANES), row),
                  pl.BlockSpec((tt, LANES), row),
                  pl.BlockSpec((tt, d), row),
                  pl.BlockSpec((1, d), lambda i, *_: (0, 0)),
                  pl.BlockSpec(memory_space=pl.ANY)],
        out_specs=pl.BlockSpec((tt, d), row),
        scratch_shapes=[pltpu.VMEM((2, n_slots, d), F32), pltpu.SemaphoreType.DMA((2,))],
    )
    return pl.pallas_call(
        functools.partial(_combine_kernel, n_slots=n_slots),
        grid_spec=grid_spec,
        out_shape=jax.ShapeDtypeStruct((t, d), F32),
        compiler_params=_cparams(("arbitrary",)),
        name="moe_combine",
    )(npieces, gdst3, gdst3, pos_l, gate_l, h1, fw, ybuf)


def _rotary_tables(positions):
    half = ROT_DIM // 2
    inv_freq = ROPE_THETA ** (-jnp.arange(0, ROT_DIM, 2, dtype=F32) / ROT_DIM)
    ang = positions.astype(F32).reshape(-1, 1) * inv_freq
    cos = jnp.tile(jnp.cos(ang), (1, LANES // half))
    sin = jnp.tile(jnp.sin(ang), (1, LANES // half))
    dim = np.arange(LANES)[None, :] % DIFF_HEAD_DIM
    ctab = jnp.where(dim < ROT_DIM, cos, 1.0)
    satab = jnp.where(dim < half, -sin, 0.0)
    sbtab = jnp.where((dim >= half) & (dim < ROT_DIM), sin, 0.0)
    return ctab, satab, sbtab


def _pad_rows(a, rows):
    return jnp.concatenate([a, jnp.zeros((rows - a.shape[0],) + a.shape[1:], a.dtype)], axis=0)


def _layer(h, l, tabs, attn_norm_w, w_in, diff_lambda_q1, diff_lambda_k1, diff_lambda_q2, diff_lambda_k2,
           diff_subln_w, rwkv_mu, rwkv_w0, rwkv_w_up, rwkv_a0, rwkv_a_up, rwkv_g_up, rwkv_k_k, rwkv_k_a,
           rwkv_r_k, rwkv_ln_w, rwkv_ln_b, w_out, ffn_norm_w, router_w, router_b, exp_w1, exp_b1,
           exp_w2, exp_b2, final_w):
    bsz, seq, d = h.shape
    t = bsz * seq
    w = RWKV_WIDTH
    lambda_init = 0.8 - 0.6 * math.exp(-0.3 * l)
    x2 = h.reshape(t, d)
    row1 = lambda a: a.reshape(1, -1).astype(F32)

    wi = w_in[l]
    qkv_cols = 3 * DIFF_WIDTH
    wq = wi[:, :qkv_cols].astype(BF16)
    o = qkv_cols + 3 * w
    zcol = lambda n: jnp.zeros((d, n), wi.dtype)
    wr = jnp.concatenate([
        wi[:, qkv_cols:o],
        wi[:, o:o + DECAY_LORA], zcol(LORA_PAD - DECAY_LORA),
        wi[:, o + DECAY_LORA:o + DECAY_LORA + AAA_LORA], zcol(LORA_PAD - AAA_LORA),
        wi[:, o + DECAY_LORA + AAA_LORA:], zcol(LORA_PAD - GATE_LORA)], axis=1).astype(BF16)
    mu = rwkv_mu[l]
    zv = lambda n: jnp.zeros((n,), mu.dtype)
    mu_p = jnp.concatenate([
        mu[:3 * w],
        mu[3 * w:3 * w + DECAY_LORA], zv(LORA_PAD - DECAY_LORA),
        mu[3 * w + DECAY_LORA:3 * w + DECAY_LORA + AAA_LORA], zv(LORA_PAD - AAA_LORA),
        mu[3 * w + DECAY_LORA + AAA_LORA:], zv(LORA_PAD - GATE_LORA)]).reshape(1, -1)

    head = np.arange(w) // RWKV_HEAD
    ones = jnp.asarray(head[:, None] == head[None, :], BF16)
    q, k, v, r, lw, kmod, vv, kk, bb, g, bonus = _inproj(
        x2, row1(attn_norm_w[l]), wq, wr, *tabs,
        mu_p, row1(rwkv_w0[l]), row1(rwkv_a0[l]), row1(rwkv_k_k[l]), row1(rwkv_k_a[l]),
        row1(rwkv_r_k[l]), _pad_rows(rwkv_w_up[l].astype(BF16), LORA_PAD),
        _pad_rows(rwkv_a_up[l].astype(BF16), LORA_PAD), _pad_rows(rwkv_g_up[l].astype(BF16), LORA_PAD),
        ones, seq)

    od = _attn(q.reshape(bsz, seq, -1), k.reshape(bsz, seq, -1), v.reshape(bsz, seq, -1),
               row1(diff_lambda_q1[l]), row1(diff_lambda_k1[l]), row1(diff_lambda_q2[l]),
               row1(diff_lambda_k2[l]), row1(diff_subln_w[l]), lambda_init)

    s3 = lambda a: a.reshape(bsz, seq, w)
    y = _scan(s3(r), s3(lw), s3(kmod), s3(vv), s3(kk), s3(bb)).reshape(t, w)

    n_e = router_w.shape[-1]
    rw = jnp.concatenate([router_w[l].astype(F32), jnp.zeros((d, LANES - n_e), F32)], axis=1)
    rw_hi = rw.astype(BF16)
    rw = jnp.stack([rw_hi, (rw - rw_hi.astype(F32)).astype(BF16)])
    rb = jnp.concatenate([router_b[l].astype(F32), jnp.full((LANES - n_e,), -1e30, F32)]).reshape(1, -1)
    h1, xn, sel, idx_l, gate_l, cnt8 = _mix(
        od.reshape(t, -1), y, g, bonus, x2, row1(rwkv_ln_w[l]), row1(rwkv_ln_b[l]), ones,
        w_out[l].astype(BF16), row1(ffn_norm_w[l]), rw, rb)

    bm = EXPERT_BLOCK
    pc = SUBLANES
    tm = MOE_TILE
    n_tiles = t // tm
    cnt = cnt8.reshape(n_tiles, SUBLANES, LANES)[:, 0, :n_e]
    seg = (cnt + pc - 1) // pc * pc
    lend = jnp.cumsum(seg, axis=1)
    lstart = lend - seg
    rows_e = jnp.sum(seg, axis=0)
    padded = (rows_e + bm - 1) // bm * bm
    pad_ends = jnp.cumsum(padded)
    gstart = (pad_ends - padded)[None, :] + jnp.cumsum(seg, axis=0) - seg
    n_slots = tm * TOP_K + n_e * pc
    n_pieces = n_slots // pc
    n_blocks = -(-(t * TOP_K + n_tiles * n_e * (pc - 1) + n_e * (bm - pc)) // bm)
    n_rows = n_blocks * bm
    piece_row = jnp.arange(n_pieces, dtype=I32) * pc
    piece_e = jnp.minimum(jnp.sum(lend[:, None, :] <= piece_row[None, :, None], axis=-1), n_e - 1)
    pick = piece_e[:, :, None] == jnp.arange(n_e, dtype=I32)[None, None, :]
    take = lambda a: jnp.sum(jnp.where(pick, a[:, None, :], 0), axis=-1)
    gdst = ((take(gstart) + piece_row[None, :] - take(lstart)) // pc).astype(I32)
    gdst3 = jnp.clip(gdst, 0, n_rows // pc - 1).reshape(n_tiles, 1, n_pieces)
    npieces = (lend[:, -1] // pc).astype(I32)
    lstart8 = jnp.zeros((n_tiles, SUBLANES, LANES), I32).at[:, :, :n_e].set(lstart[:, None, :])
    lstart8 = lstart8.reshape(n_tiles * SUBLANES, LANES)
    gap_start = jnp.concatenate([pad_ends - padded + rows_e, pad_ends[-1:]]) // pc
    gap_len = jnp.concatenate([padded - rows_e, n_rows - pad_ends[-1:]]) // pc
    gap_end = jnp.cumsum(gap_len)
    max_fill = n_e * (bm // pc - 1) + (n_rows - t * TOP_K) // pc
    j = jnp.arange(max_fill, dtype=I32)
    gap = jnp.minimum(jnp.sum(gap_end[None, :] <= j[:, None], axis=1), n_e)
    in_gap = gap[:, None] == jnp.arange(n_e + 1, dtype=I32)[None, :]
    shift = jnp.sum(jnp.where(in_gap, (gap_start - gap_end + gap_len)[None, :], 0), axis=1)
    fill = jnp.clip(j + shift, 0, n_rows // pc - 1).astype(I32)
    nfill = gap_end[-1:].astype(I32)
    first_row = jnp.arange(n_blocks, dtype=I32) * bm
    block_e = jnp.minimum(jnp.sum(pad_ends[None, :] <= first_row[:, None], axis=1), n_e - 1).astype(I32)
    n_used = (pad_ends[-1] // bm).astype(I32).reshape(1)

    pos_l, xbuf = _dispatch(npieces, fill, nfill, gdst3, sel, idx_l, lstart8, xn, n_rows)

    ff2 = exp_w1.shape[-1]
    b1 = exp_b1[l].astype(F32).reshape(n_e, 1, ff2)
    b2 = exp_b2[l].astype(F32).reshape(n_e, 1, d)
    ybuf = _experts(block_e, padded, n_used, xbuf, exp_w1[l].astype(F32), b1, exp_w2[l].astype(F32), b2)

    out = _combine(npieces, gdst3, pos_l, gate_l, h1, row1(final_w), ybuf)
    return out.reshape(bsz, seq, d)


def kernel(x, positions, attn_norm_w, w_in, diff_lambda_q1, diff_lambda_k1, diff_lambda_q2, diff_lambda_k2, diff_subln_w, rwkv_mu, rwkv_w0, rwkv_w_up, rwkv_a0, rwkv_a_up, rwkv_g_up, rwkv_k_k, rwkv_k_a, rwkv_r_k, rwkv_ln_w, rwkv_ln_b, w_out, ffn_norm_w, router_w, router_b, exp_w1, exp_b1, exp_w2, exp_b2, final_norm_w):
    depth = w_in.shape[0]
    assert depth == 1, "the final norm is fused into the last (only) layer's combine kernel"
    tabs = _rotary_tables(positions)
    return _layer(x, 0, tabs, attn_norm_w, w_in, diff_lambda_q1, diff_lambda_k1, diff_lambda_q2,
                  diff_lambda_k2, diff_subln_w, rwkv_mu, rwkv_w0, rwkv_w_up, rwkv_a0, rwkv_a_up, rwkv_g_up,
                  rwkv_k_k, rwkv_k_a, rwkv_r_k, rwkv_ln_w, rwkv_ln_b, w_out, ffn_norm_w, router_w, router_b,
                  exp_w1, exp_b1, exp_w2, exp_b2, final_norm_w)
```

```python
import functools
import math

import jax
import jax.numpy as jnp
import numpy as np
from jax import lax
from jax.experimental import pallas as pl
from jax.experimental.pallas import tpu as pltpu

F32 = jnp.float32
BF16 = jnp.bfloat16
I32 = jnp.int32
U32 = jnp.uint32

DIFF_HEAD_DIM = 64
DIFF_V_DIM = 128
DIFF_HEADS = 4
DIFF_WIDTH = DIFF_HEADS * DIFF_V_DIM
ROT_DIM = 16
ROPE_THETA = 500000.0
RWKV_HEAD = 64
RWKV_HEADS = 8
RWKV_WIDTH = RWKV_HEAD * RWKV_HEADS
DECAY_LORA = 32
AAA_LORA = 32
GATE_LORA = 96
N_EXPERTS = 32
TOP_K = 4
SWIGLU_LIMIT = 7.0
SWIGLU_ALPHA = 1.702
NORM_EPS = 1e-5
RWKV_GN_EPS = 64e-5

LANES = 128
SUBLANES = 8
VMEM_LIMIT = 56 * 1024 * 1024

ROW_TILE = 512
ATTN_TILE = 512
CHUNK = 64
SCAN_TILE = 512
SCAN_BATCH = 4
GROUP = 4 * RWKV_HEAD
EXPERT_BLOCK = 512
MOE_TILE = 256
CAST_CHUNKS = 8
LORA_PAD = LANES
ZR_COLS = 3 * RWKV_WIDTH + 3 * LORA_PAD


def _cparams(sem, flags=None):
    return pltpu.CompilerParams(dimension_semantics=sem, vmem_limit_bytes=VMEM_LIMIT, flags=flags)


def _nt(a, b):
    return lax.dot_general(a, b, (((1,), (1,)), ((), ())), preferred_element_type=F32)


def _tn(a, b):
    return lax.dot_general(a, b, (((0,), (0,)), ((), ())), preferred_element_type=F32)


def _dot(a, b):
    return jnp.dot(a, b, preferred_element_type=F32)


def _pack_halves(x):
    n = x.shape[1] // 2
    bits = lax.bitcast_convert_type(x, U32)
    return (bits[:, :n] & jnp.uint32(0xFFFF0000)) | (bits[:, n:] >> 16)


def _unpack_halves(p):
    hi = lax.bitcast_convert_type(p & jnp.uint32(0xFFFF0000), F32)
    lo = lax.bitcast_convert_type(p << 16, F32)
    return jnp.concatenate([hi, lo], axis=1).astype(BF16)


def _split(x):
    hi = x.astype(BF16)
    return hi, (x - hi.astype(F32)).astype(BF16)


def _split_dot(x, w_bf16):
    hi, lo = _split(x)
    return _dot(hi, w_bf16) + _dot(lo, w_bf16)


def _split3_dot(x, w_hi, w_lo):
    hi, lo = _split(x)
    return _dot(hi, w_hi) + (_dot(hi, w_lo) + _dot(lo, w_hi))


def _inproj_kernel(x_ref, nw_ref, wq_ref, wr_ref, c_ref, sa_ref, sb_ref,
                   mu_ref, w0_ref, a0_ref, kk_ref, ka_ref, rk_ref, wup_ref, aup_ref, gup_ref, ones_ref,
                   q_ref, k_ref, v_ref, r_ref, lw_ref, rk_out_ref, rv_ref, kkn_ref, b_ref, g_ref, bonus_ref,
                   prev_ref, *, tiles_per_seq):
    x = x_ref[...]
    ms = jnp.mean(x * x, axis=-1, keepdims=True)
    u = (x * lax.rsqrt(ms + NORM_EPS) * nw_ref[...]).astype(BF16)
    zq = _dot(u, wq_ref[...])
    c = c_ref[...]
    sa = sa_ref[...]
    sb = sb_ref[...]
    scale = DIFF_HEAD_DIM ** -0.5
    for g in range(2 * DIFF_HEADS):
        zg = zq[:, g * LANES:(g + 1) * LANES]
        rot = zg * c + pltpu.roll(zg, LANES - ROT_DIM // 2, 1) * sa + pltpu.roll(zg, ROT_DIM // 2, 1) * sb
        if g < DIFF_HEADS:
            q_ref[:, g * LANES:(g + 1) * LANES] = (rot * scale).astype(BF16)
        else:
            h = g - DIFF_HEADS
            k_ref[:, h * LANES:(h + 1) * LANES] = rot.astype(BF16)
    v_ref[...] = zq[:, 2 * DIFF_WIDTH:3 * DIFF_WIDTH].astype(BF16)

    i = pl.program_id(0)
    z = _dot(u, wr_ref[...])
    tm = z.shape[0]
    rows = lax.broadcasted_iota(I32, (tm, 1), 0)

    @pl.when(i % tiles_per_seq == 0)
    def _():
        prev_ref[...] = jnp.zeros_like(prev_ref)

    shifted = jnp.where(rows == 0, prev_ref[SUBLANES - 1:SUBLANES, :], pltpu.roll(z, 1, 0))
    prev_ref[...] = z[tm - SUBLANES:tm, :]
    zf = z + mu_ref[...] * (shifted - z)
    w = RWKV_WIDTH
    r = zf[:, 0:w]
    k = zf[:, w:2 * w]
    v = zf[:, 2 * w:3 * w]
    wd = zf[:, 3 * w:3 * w + LORA_PAD]
    ad = zf[:, 3 * w + LORA_PAD:3 * w + 2 * LORA_PAD]
    gd = zf[:, 3 * w + 2 * LORA_PAD:3 * w + 3 * LORA_PAD]
    pre = w0_ref[...] + _split_dot(jnp.tanh(wd), wup_ref[...])
    neg = -pre
    softplus = jnp.maximum(neg, 0.0) + jnp.log(1.0 + jnp.exp(-jnp.abs(neg)))
    wlog = -softplus - 0.5
    lw_ref[...] = -jnp.exp(wlog)
    a = jax.nn.sigmoid(a0_ref[...] + _split_dot(ad, aup_ref[...]))
    g_ref[...] = _split_dot(jax.nn.sigmoid(gd), gup_ref[...]).astype(g_ref.dtype)
    ones = ones_ref[...]
    kk = k * kk_ref[...]
    norm = jnp.sqrt(_split_dot(kk * kk, ones))
    kk = kk / jnp.maximum(norm, 1e-12)
    k = k * (1.0 + (a - 1.0) * ka_ref[...])
    r_ref[...] = r.astype(r_ref.dtype)
    rk_out_ref[...] = k.astype(rk_out_ref.dtype)
    rv_ref[...] = v.astype(rv_ref.dtype)
    kkn_ref[...] = kk.astype(kkn_ref.dtype)
    b_ref[...] = (kk * a).astype(b_ref.dtype)
    bonus_ref[...] = (_split_dot(r * k * rk_ref[...], ones) * v).astype(bonus_ref.dtype)


def _inproj(x2, nw, wq, wr, ctab, satab, sbtab, mu_p, w0, a0, k_k, k_a, rk, wup, aup, gup, ones, seq):
    t, d = x2.shape
    tm = min(ROW_TILE, seq)
    w = RWKV_WIDTH
    row = lambda i: (i, 0)
    fixed = lambda i: (0, 0)
    vecw = pl.BlockSpec((1, w), fixed)
    lora = pl.BlockSpec((LORA_PAD, w), fixed)
    attn_out = pl.BlockSpec((tm, DIFF_WIDTH), row)
    feat_out = pl.BlockSpec((tm, w), row)
    feat = lambda dt: jax.ShapeDtypeStruct((t, w), dt)
    return pl.pallas_call(
        functools.partial(_inproj_kernel, tiles_per_seq=seq // tm),
        grid=(t // tm,),
        in_specs=[
            pl.BlockSpec((tm, d), row),
            pl.BlockSpec((1, d), fixed),
            pl.BlockSpec(wq.shape, fixed),
            pl.BlockSpec(wr.shape, fixed),
            pl.BlockSpec((tm, LANES), row),
            pl.BlockSpec((tm, LANES), row),
            pl.BlockSpec((tm, LANES), row),
            pl.BlockSpec((1, ZR_COLS), fixed),
            vecw, vecw, vecw, vecw, vecw, lora, lora, lora,
            pl.BlockSpec((w, w), fixed),
        ],
        out_specs=[attn_out] * 3 + [feat_out] * 8,
        out_shape=[jax.ShapeDtypeStruct((t, DIFF_WIDTH), BF16)] * 3
        + [feat(BF16), feat(F32), feat(BF16), feat(BF16), feat(BF16), feat(BF16), feat(BF16), feat(BF16)],
        scratch_shapes=[pltpu.VMEM((SUBLANES, ZR_COLS), F32)],
        compiler_params=_cparams(("arbitrary",)),
        name="inproj",
    )(x2, nw, wq, wr, ctab, satab, sbtab, mu_p, w0, a0, k_k, k_a, rk, wup, aup, gup, ones)


def _attn_kernel(q_ref, k_ref, v_ref, lq1_ref, lk1_ref, lq2_ref, lk2_ref, sw_ref, o_ref,
                 *, tile, lambda_init):
    lane = lax.broadcasted_iota(I32, (1, LANES), 1)
    first = lane < DIFF_HEAD_DIM
    neg = -1e30
    lam = (jnp.exp(jnp.sum(lq1_ref[...] * lk1_ref[...], axis=-1, keepdims=True))
           - jnp.exp(jnp.sum(lq2_ref[...] * lk2_ref[...], axis=-1, keepdims=True)) + lambda_init)
    r = lax.broadcasted_iota(I32, (tile, tile), 0)
    c = lax.broadcasted_iota(I32, (tile, tile), 1)
    keep = c <= r
    for i in range(q_ref.shape[1] // tile):
        q = q_ref[0, i * tile:(i + 1) * tile, :]
        zero = jnp.zeros_like(q)
        n = (i + 1) * tile
        kb = k_ref[0, 0:n, :]
        vb = v_ref[0, 0:n, :]
        outs = []
        for qm in (jnp.where(first, q, zero), jnp.where(first, zero, q)):
            s = _nt(qm, kb)
            diag = jnp.where(keep, s[:, i * tile:], neg)
            s = diag if i == 0 else jnp.concatenate([s[:, :i * tile], diag], axis=1)
            p = jnp.exp(s - jnp.max(s, axis=-1, keepdims=True))
            outs.append(_dot(p.astype(BF16), vb) / jnp.sum(p, axis=-1, keepdims=True))
        o = outs[0] - lam * outs[1]
        ms = jnp.mean(o * o, axis=-1, keepdims=True)
        o = o * lax.rsqrt(ms + NORM_EPS) * sw_ref[...] * (1.0 - lambda_init)
        o_ref[0, i * tile:(i + 1) * tile, :] = o.astype(o_ref.dtype)


def _attn(q3, k3, v3, lq1, lk1, lq2, lk2, sw, lambda_init):
    b, s, _ = q3.shape
    tile = min(ATTN_TILE, s)
    spec = pl.BlockSpec((1, s, LANES), lambda bi, h: (bi, 0, h))
    vec = lambda n: pl.BlockSpec((1, n), lambda bi, h: (0, 0))
    return pl.pallas_call(
        functools.partial(_attn_kernel, tile=tile, lambda_init=lambda_init),
        grid=(b, DIFF_HEADS),
        in_specs=[spec, spec, spec, vec(DIFF_HEAD_DIM), vec(DIFF_HEAD_DIM), vec(DIFF_HEAD_DIM),
                  vec(DIFF_HEAD_DIM), vec(DIFF_V_DIM)],
        out_specs=spec,
        out_shape=jax.ShapeDtypeStruct((b, s, DIFF_WIDTH), BF16),
        compiler_params=_cparams(("parallel", "parallel")),
        name="attn",
    )(q3, k3, v3, lq1, lk1, lq2, lk2, sw)


def _scan_kernel(r_ref, lw_ref, k_ref, v_ref, kk_ref, b_ref, y_ref, state_ref,
                 wr_s, ut_s, utt_s, arb_s, pv_s, bh_s, vk_s, wt_s, *, n_chunks):
    L = CHUNK
    G = GROUP
    n_groups = RWKV_WIDTH // G

    @pl.when(pl.program_id(1) == 0)
    def _():
        state_ref[...] = jnp.zeros_like(state_ref)

    row = lax.broadcasted_iota(I32, (L, G), 0)
    colr = lax.broadcasted_iota(I32, (L, G), 1) & (L - 1)
    strict = (colr < row).astype(F32)
    incl = (colr <= row).astype(F32)
    eye = (colr == row).astype(F32)
    eye_l = (lax.broadcasted_iota(I32, (L, L), 0) == lax.broadcasted_iota(I32, (L, L), 1)).astype(BF16)
    br = lax.broadcasted_iota(I32, (G, G), 0) >> 6
    bc = lax.broadcasted_iota(I32, (G, G), 1) >> 6
    block = (br == bc).astype(F32)
    block_bf = block.astype(BF16)
    rows1 = lax.broadcasted_iota(I32, (L, 1), 0)

    def stack4(x):
        xb = x.astype(BF16)
        return jnp.concatenate([xb, xb, xb, xb], axis=0) * block_bf

    def cat(a, b):
        return jnp.concatenate([a, b], axis=0).astype(BF16)

    def precompute(it, carry):
        chains = [(cl, g) for cl in range(SCAN_BATCH) for g in range(n_groups)]
        each = lambda f, *lists: [f(*args) for args in zip(*lists)]

        def load(ref):
            out = []
            for cl, g in chains:
                base = pl.multiple_of((it * SCAN_BATCH + cl) * L, L)
                out.append(ref[0, pl.ds(base, L), g * G:(g + 1) * G].astype(F32))
            return out

        r, lw, k, v, kk, b = load(r_ref), load(lw_ref), load(k_ref), load(v_ref), load(kk_ref), load(b_ref)

        def cumsum(x):
            sh = 1
            while sh < L:
                x = x + jnp.where(rows1 >= sh, pltpu.roll(x, sh, 0), 0.0)
                sh *= 2
            return x

        cs = each(cumsum, lw)
        tot = each(lambda c: c[L - 1:L, :], cs)
        a_hat = each(lambda kk_, c, l: -kk_ * jnp.exp(c - l), kk, cs, lw)
        r_hat = each(lambda r_, c: r_ * jnp.exp(c), r, cs)
        w_inv = each(lambda c: jnp.exp(-c), cs)
        w_end = each(lambda t_, c: jnp.exp(t_ - c), tot, cs)
        lhs = each(cat, a_hat, r_hat)
        ab = each(lambda l_, b_, wi: _nt(l_, stack4(b_ * wi)), lhs, b, w_inv)
        ak = each(lambda l_, k_, wi: _nt(l_, stack4(k_ * wi)), lhs, k, w_inv)
        a_ab = each(lambda x: x[:L] * strict, ab)
        a_rb = each(lambda x: (x[L:] * incl).astype(BF16), ab)
        a_k = each(lambda x: cat(x[:L] * strict, x[L:] * incl), ak)
        t_mat = each(lambda a: eye + a, a_ab)
        p_mat = each(lambda a: _dot(a.astype(BF16), stack4(a)), a_ab)
        for _ in range(4):
            tp = each(lambda t_, p_: _dot(cat(t_, p_), stack4(p_)), t_mat, p_mat)
            t_mat = each(lambda t_, x: t_ + x[:L], t_mat, tp)
            p_mat = each(lambda x: x[L:], tp)
        t_bf = each(lambda t_, p_: (t_ + _dot(t_.astype(BF16), stack4(p_))).astype(BF16), t_mat, p_mat)
        av = each(lambda a, v_: _dot(a, stack4(v_)), a_k, v)
        w_til = each(lambda t_, a: _dot(t_, stack4(a)), t_bf, a_hat)
        u_til = each(lambda t_, x: _dot(t_, stack4(x[:L])), t_bf, av)
        u_til_t = each(lambda u_: _tn(u_.astype(BF16), eye_l), u_til)
        vk = each(lambda v_, k_, we: _tn(v_.astype(BF16), (k_ * we).astype(BF16)) * block, v, k, w_end)
        for n, (cl, g) in enumerate(chains):
            slot = (it * SCAN_BATCH + cl) * n_groups + g
            wr_s[slot] = cat(w_til[n], r_hat[n])
            ut_s[slot] = u_til[n]
            utt_s[slot] = u_til_t[n]
            arb_s[slot] = a_rb[n]
            pv_s[slot] = av[n][L:]
            bh_s[slot] = (b[n] * w_end[n]).astype(BF16)
            vk_s[slot] = vk[n]
            wt_s[slot] = jnp.broadcast_to(jnp.exp(tot[n]), (SUBLANES, G))
        return carry

    lax.fori_loop(0, n_chunks // SCAN_BATCH, precompute, 0)

    def recur(c, carry):
        gs = range(n_groups)
        slots = [c * n_groups + g for g in gs]
        s0 = [state_ref[g] for g in gs]
        s0b = [s.astype(BF16) for s in s0]
        wr = [wr_s[sl] for sl in slots]
        u_t = [_nt(s0b[g], wr[g][:L]) + utt_s[slots[g]] for g in gs]
        ub = [_dot(u_t[g].astype(BF16), bh_s[slots[g]]) for g in gs]
        for g in gs:
            state_ref[g] = s0[g] * wt_s[slots[g]][0:1, :] + ub[g] * block + vk_s[slots[g]]
        uy = [_nt(wr[g], s0b[g]) for g in gs]
        u = [uy[g][:L] + ut_s[slots[g]] for g in gs]
        base = pl.multiple_of(c * L, L)
        for g in gs:
            y = uy[g][L:] + _dot(arb_s[slots[g]], stack4(u[g])) + pv_s[slots[g]]
            y_ref[0, pl.ds(base, L), g * G:(g + 1) * G] = y
        return carry

    lax.fori_loop(0, n_chunks, recur, 0)


def _scan(r3, lw3, k3, v3, kk3, b3):
    bsz, s, w = r3.shape
    ts = min(SCAN_TILE, s)
    n_chunks = ts // CHUNK
    slots = n_chunks * (w // GROUP)
    L, G = CHUNK, GROUP
    spec = pl.BlockSpec((1, ts, w), lambda bi, i: (bi, i, 0))
    return pl.pallas_call(
        functools.partial(_scan_kernel, n_chunks=n_chunks),
        grid=(bsz, s // ts),
        in_specs=[spec] * 6,
        out_specs=spec,
        out_shape=jax.ShapeDtypeStruct((bsz, s, w), F32),
        scratch_shapes=[pltpu.VMEM((w // GROUP, G, G), F32),
                        pltpu.VMEM((slots, 2 * L, G), BF16),
                        pltpu.VMEM((slots, L, G), F32),
                        pltpu.VMEM((slots, G, L), F32),
                        pltpu.VMEM((slots, L, G), BF16),
                        pltpu.VMEM((slots, L, G), F32),
                        pltpu.VMEM((slots, L, G), BF16),
                        pltpu.VMEM((slots, G, G), F32),
                        pltpu.VMEM((slots, SUBLANES, G), F32)],
        compiler_params=_cparams(("parallel", "arbitrary")),
        name="rwkv_scan",
    )(r3, lw3, k3, v3, kk3, b3)


def _mix_kernel(od_ref, y_ref, g_ref, bonus_ref, x_ref, lnw_ref, lnb_ref, ones_ref, wo_ref,
                fw_ref, rw_ref, rb_ref,
                h1_ref, xn_ref, sel_ref, idx_ref, gate_ref, cnt_ref):
    ones = ones_ref[...]
    y = y_ref[...]
    inv_n = 1.0 / RWKV_HEAD
    mean = _split_dot(y, ones) * inv_n
    d = y - mean
    var = _split_dot(d * d, ones) * inv_n
    yn = d * lax.rsqrt(var + RWKV_GN_EPS) * lnw_ref[...] + lnb_ref[...]
    orw = ((yn + bonus_ref[...]) * g_ref[...]).astype(BF16)
    h1 = (x_ref[...] + _dot(od_ref[...], wo_ref[0:DIFF_WIDTH, :])
          + _dot(orw, wo_ref[DIFF_WIDTH:DIFF_WIDTH + RWKV_WIDTH, :]))
    h1_ref[...] = h1
    ms = jnp.mean(h1 * h1, axis=-1, keepdims=True)
    xn = h1 * lax.rsqrt(ms + NORM_EPS) * fw_ref[...]
    xn_ref[...] = xn.astype(xn_ref.dtype)
    logits = _split3_dot(xn, rw_ref[0], rw_ref[1]) + rb_ref[...]
    tm = logits.shape[0]
    lane = lax.broadcasted_iota(I32, (tm, LANES), 1).astype(F32)
    work = logits
    sel = jnp.zeros((tm, LANES), F32)
    idx_l = jnp.zeros((tm, LANES), F32)
    val_l = jnp.zeros((tm, LANES), F32)
    top = None
    for kslot in range(TOP_K):
        m = jnp.max(work, axis=-1, keepdims=True)
        pick = jnp.min(jnp.where(work == m, lane, float(LANES)), axis=-1, keepdims=True)
        hit = lane == pick
        sel = jnp.where(hit, 1.0, sel)
        idx_l = jnp.where(lane == kslot, pick, idx_l)
        if top is None:
            top = m
        val_l = jnp.where(lane == kslot, jnp.exp(m - top), val_l)
        work = jnp.where(hit, -jnp.inf, work)
    sel_ref[...] = sel
    idx_ref[...] = idx_l.astype(I32)
    gate_ref[...] = val_l / jnp.sum(val_l, axis=-1, keepdims=True)
    for part in range(tm // MOE_TILE):
        count = jnp.sum(sel[part * MOE_TILE:(part + 1) * MOE_TILE], axis=0, keepdims=True)
        cnt_ref[part * SUBLANES:(part + 1) * SUBLANES, :] = jnp.broadcast_to(count, (SUBLANES, LANES)).astype(I32)


def _mix(od, y, g, bonus, x2, lnw, lnb, ones, wo, fw, rw, rb):
    t, d = x2.shape
    tm = min(ROW_TILE, t)
    w = RWKV_WIDTH
    row = lambda i: (i, 0)
    fixed = lambda i: (0, 0)
    rs = lambda n: pl.BlockSpec((tm, n), row)
    return pl.pallas_call(
        _mix_kernel,
        grid=(t // tm,),
        in_specs=[rs(DIFF_WIDTH), rs(w), rs(w), rs(w), rs(d),
                  pl.BlockSpec((1, w), fixed), pl.BlockSpec((1, w), fixed),
                  pl.BlockSpec((w, w), fixed), pl.BlockSpec(wo.shape, fixed),
                  pl.BlockSpec((1, d), fixed), pl.BlockSpec(rw.shape, lambda i: (0, 0, 0)),
                  pl.BlockSpec((1, LANES), fixed)],
        out_specs=[rs(d), rs(d), rs(LANES), rs(LANES), rs(LANES),
                   pl.BlockSpec((tm // MOE_TILE * SUBLANES, LANES), row)],
        out_shape=[jax.ShapeDtypeStruct((t, d), F32), jax.ShapeDtypeStruct((t, d), BF16),
                   jax.ShapeDtypeStruct((t, LANES), F32), jax.ShapeDtypeStruct((t, LANES), I32),
                   jax.ShapeDtypeStruct((t, LANES), F32),
                   jax.ShapeDtypeStruct((t // MOE_TILE * SUBLANES, LANES), I32)],
        compiler_params=_cparams(("parallel",)),
        name="mix_router",
    )(od, y, g, bonus, x2, lnw, lnb, ones, wo, fw, rw, rb)


def _slot_positions(sel, idx_l, lstart):
    tt = sel.shape[0]
    r = lax.broadcasted_iota(I32, (tt, tt), 0)
    c = lax.broadcasted_iota(I32, (tt, tt), 1)
    lower = (c < r).astype(BF16)
    where_to = _dot(lower, sel.astype(BF16)) + lstart
    lane = lax.broadcasted_iota(I32, (tt, LANES), 1).astype(F32)
    idx = idx_l.astype(F32)
    pos = jnp.full((tt, LANES), -1.0, F32)
    for kslot in range(TOP_K):
        e = jnp.sum(jnp.where(lane == kslot, idx, 0.0), axis=-1, keepdims=True)
        p = jnp.sum(jnp.where(lane == e, where_to, 0.0), axis=-1, keepdims=True)
        pos = jnp.where(lane == kslot, p, pos)
    return pos


def _piece(ref, q):
    return ref.at[pl.ds(pl.multiple_of(q * SUBLANES, SUBLANES), SUBLANES), :]


def _dispatch_kernel(np_ref, fill_ref, nfill_ref, gdst_ref, sel_ref, idx_ref, lstart_ref, xn_ref,
                     pos_ref, buf_ref, xs_ref, zero_ref, sem, fill_sem, *, n_slots):
    i = pl.program_id(0)
    last = pl.num_programs(0) - 1
    slot = i % 2

    def copy(q, sl):
        return pltpu.make_async_copy(_piece(xs_ref.at[sl], q), _piece(buf_ref, gdst_ref[0, 0, q]), sem.at[sl])

    def drain(count, sl):
        lax.fori_loop(0, count, lambda q, c: (copy(0, sl).wait(), c)[1], 0)

    @pl.when(i >= 2)
    def _():
        drain(np_ref[jnp.maximum(i - 2, 0)], slot)

    pos = _slot_positions(sel_ref[...], idx_ref[...], lstart_ref[0:1, :].astype(F32))
    pos_ref[...] = pos
    tt = pos.shape[0]
    pos_t = pos.T.astype(I32)
    s_iota = lax.broadcasted_iota(I32, (n_slots, tt), 0)
    perm = jnp.zeros((n_slots, tt), F32)
    for kslot in range(TOP_K):
        perm = perm + jnp.where(s_iota == pos_t[kslot:kslot + 1, :], 1.0, 0.0)
    xs_ref[slot] = _pack_halves(_dot(perm.astype(BF16), xn_ref[...]))
    lax.fori_loop(0, np_ref[i], lambda q, c: (copy(q, slot).start(), c)[1], 0)

    @pl.when(i == 0)
    def _():
        zero_ref[...] = jnp.zeros_like(zero_ref)

        def fill(j):
            return pltpu.make_async_copy(zero_ref, _piece(buf_ref, fill_ref[j]), fill_sem)

        lax.fori_loop(0, nfill_ref[0], lambda j, c: (fill(j).start(), c)[1], 0)
        lax.fori_loop(0, nfill_ref[0], lambda j, c: (fill(j).wait(), c)[1], 0)

    @pl.when(i == last)
    def _():
        @pl.when(i >= 1)
        def _():
            drain(np_ref[jnp.maximum(i - 1, 0)], 1 - slot)

        drain(np_ref[i], slot)


def _dispatch(npieces, fill, nfill, gdst3, sel, idx_l, lstart8, xn, n_rows):
    t, d = xn.shape
    n_tiles, _, n_pieces = gdst3.shape
    tt = t // n_tiles
    n_slots = n_pieces * SUBLANES
    row = lambda i, *_: (i, 0)
    grid_spec = pltpu.PrefetchScalarGridSpec(
        num_scalar_prefetch=3,
        grid=(n_tiles,),
        in_specs=[pl.BlockSpec((1, 1, n_pieces), lambda i, *_: (i, 0, 0), memory_space=pltpu.SMEM),
                  pl.BlockSpec((tt, LANES), row),
                  pl.BlockSpec((tt, LANES), row),
                  pl.BlockSpec((SUBLANES, LANES), row),
                  pl.BlockSpec((tt, d), row)],
        out_specs=[pl.BlockSpec((tt, LANES), row), pl.BlockSpec(memory_space=pl.ANY)],
        scratch_shapes=[pltpu.VMEM((2, n_slots, d // 2), U32), pltpu.VMEM((SUBLANES, d // 2), U32),
                        pltpu.SemaphoreType.DMA((2,)), pltpu.SemaphoreType.DMA(())],
    )
    return pl.pallas_call(
        functools.partial(_dispatch_kernel, n_slots=n_slots),
        grid_spec=grid_spec,
        out_shape=[jax.ShapeDtypeStruct((t, LANES), F32), jax.ShapeDtypeStruct((n_rows, d // 2), U32)],
        compiler_params=_cparams(("arbitrary",)),
        name="moe_dispatch",
    )(npieces, fill, nfill, gdst3, sel, idx_l, lstart8, xn)


def _expert_kernel(be_ref, first_ref, ord_ref, next_ref, nused_ref, x_ref, w1_hbm, b1_ref, w2_hbm, b2_ref,
                   y_ref, w1f_ref, w2f_ref, w1b_ref, w2i_ref, w2b_ref, sem):
    i = pl.program_id(0)
    ff = w2f_ref.shape[1]
    used = i < nused_ref[0]
    new_expert = jnp.logical_and(used, first_ref[i] == 1)

    def fetch(expert, slot):
        return (pltpu.make_async_copy(w1_hbm.at[expert], w1f_ref.at[slot], sem.at[0, slot]),
                pltpu.make_async_copy(w2_hbm.at[expert], w2f_ref.at[slot], sem.at[1, slot]))

    @pl.when(i == 0)
    def _():
        for cp in fetch(be_ref[0], 0):
            cp.start()

    @pl.when(jnp.logical_and(new_expert, next_ref[i] >= 0))
    def _():
        for cp in fetch(next_ref[i], 1 - (ord_ref[i] & 1)):
            cp.start()

    @pl.when(new_expert)
    def _():
        slot = ord_ref[i] & 1
        for cp in fetch(be_ref[i], slot):
            cp.wait()
        rows = w1f_ref.shape[1] // CAST_CHUNKS

        def cast1(c, carry):
            r0 = pl.multiple_of(c * rows, rows)
            w1b_ref[pl.ds(r0, rows), :] = w1f_ref[slot, pl.ds(r0, rows), :].astype(BF16)
            return carry

        lax.fori_loop(0, CAST_CHUNKS, cast1, 0)
        for g in range(w2f_ref.shape[2] // LANES):
            cols = slice(g * LANES, (g + 1) * LANES)
            w2i_ref[pl.ds(0, ff // 2, stride=2), :] = w2f_ref[slot, 0:ff // 2, cols]
            w2i_ref[pl.ds(1, ff // 2, stride=2), :] = w2f_ref[slot, ff // 2:ff, cols]
            w2b_ref[:, cols] = w2i_ref[...].astype(BF16)

    @pl.when(used)
    def _():
        x = _unpack_halves(x_ref[...])
        hid = _dot(x, w1b_ref[...]) + b1_ref[0]
        even = (lax.broadcasted_iota(I32, (1, LANES), 1) & 1) == 0

        def act_even(g):
            hg = hid[:, g * LANES:(g + 1) * LANES]
            glu = jnp.minimum(hg, SWIGLU_LIMIT)
            lin = jnp.clip(hg, -SWIGLU_LIMIT, SWIGLU_LIMIT) + 1.0
            return glu * jax.nn.sigmoid(SWIGLU_ALPHA * glu) * pltpu.roll(lin, LANES - 1, 1)

        half = ff // LANES
        act = jnp.concatenate(
            [jnp.where(even, act_even(g), pltpu.roll(act_even(g + half), 1, 1)) for g in range(half)], axis=1)
        y = _dot(act.astype(BF16), w2b_ref[...]) + b2_ref[0]
        y_ref[...] = _pack_halves(y.astype(BF16).astype(F32))

    @pl.when(jnp.logical_not(used))
    def _():
        y_ref[...] = jnp.zeros_like(y_ref)


def _experts(block_e, rows_per_expert, n_used, xbuf, w1, b1, w2, b2):
    n_rows = xbuf.shape[0]
    bm = EXPERT_BLOCK
    n_blocks = n_rows // bm
    e, d, ff2 = w1.shape
    ff = ff2 // 2
    has = rows_per_expert > 0
    ids = jnp.arange(e, dtype=I32)
    later = (ids[None, :] > ids[:, None]) & has[None, :]
    next_used = jnp.where(jnp.any(later, axis=1), jnp.argmax(later, axis=1), -1).astype(I32)
    ordinal = (jnp.cumsum(has.astype(I32)) - 1).astype(I32)
    first = jnp.concatenate([jnp.ones((1,), I32), (block_e[1:] != block_e[:-1]).astype(I32)])
    of_block = block_e[:, None] == ids[None, :]
    order = jnp.sum(jnp.where(of_block, ordinal[None, :], 0), axis=1).astype(I32)
    next_e = jnp.sum(jnp.where(of_block, next_used[None, :], 0), axis=1).astype(I32)
    blk =lambda i, be, fi, od, ne, nu: (jnp.maximum(jnp.minimum(i, nu[0] - 1), 0), 0)
    ex3 = lambda i, be, fi, od, ne, nu: (be[i], 0, 0)
    grid_spec = pltpu.PrefetchScalarGridSpec(
        num_scalar_prefetch=5,
        grid=(n_blocks,),
        in_specs=[pl.BlockSpec((bm, d // 2), blk),
                  pl.BlockSpec(memory_space=pl.ANY),
                  pl.BlockSpec((1, 1, ff2), ex3),
                  pl.BlockSpec(memory_space=pl.ANY),
                  pl.BlockSpec((1, 1, d), ex3)],
        out_specs=pl.BlockSpec((bm, d // 2), lambda i, be, fi, od, ne, nu: (i, 0)),
        scratch_shapes=[pltpu.VMEM((2, d, ff2), F32), pltpu.VMEM((2, ff, d), F32),
                        pltpu.VMEM((d, ff2), BF16), pltpu.VMEM((ff, LANES), F32), pltpu.VMEM((ff, d), BF16),
                        pltpu.SemaphoreType.DMA((2, 2))],
    )
    return pl.pallas_call(
        _expert_kernel,
        grid_spec=grid_spec,
        out_shape=jax.ShapeDtypeStruct((n_rows, d // 2), U32),
        compiler_params=_cparams(("arbitrary",)),
        name="moe_experts",
    )(block_e, first, order, next_e, n_used, xbuf, w1, b1, w2, b2)


def _combine_kernel(np_ref, gdst_ref, gnext_ref, pos_ref, gate_ref, h1_ref, fw_ref, ybuf_ref, o_ref, ys_ref, sem,
                    *, n_slots):
    i = pl.program_id(0)
    last = pl.num_programs(0) - 1
    slot = i % 2

    def fetch(table_ref, tile, sl):
        count = np_ref[tile]

        def copy(q):
            return pltpu.make_async_copy(_piece(ybuf_ref, table_ref[0, 0, q]), _piece(ys_ref.at[sl], q), sem.at[sl])

        lax.fori_loop(0, count, lambda q, c: (copy(q).start(), c)[1], 0)

        def zero(q, carry):
            _piece(ys_ref.at[sl], q)[...] = jnp.zeros((SUBLANES, ys_ref.shape[2]), ys_ref.dtype)
            return carry

        lax.fori_loop(count, n_slots // SUBLANES, zero, 0)

    @pl.when(i == 0)
    def _():
        fetch(gdst_ref, i, slot)

    @pl.when(i < last)
    def _():
        fetch(gnext_ref, jnp.minimum(i + 1, last), 1 - slot)

    n = np_ref[i]

    def copy(q):
        return pltpu.make_async_copy(_piece(ybuf_ref, gdst_ref[0, 0, q]), _piece(ys_ref.at[slot], q), sem.at[slot])

    pos = pos_ref[...].astype(I32)
    gate = gate_ref[...]
    tt = pos.shape[0]
    s_iota = lax.broadcasted_iota(I32, (tt, n_slots), 1)
    weight = jnp.zeros((tt, n_slots), F32)
    for kslot in range(TOP_K):
        weight = weight + jnp.where(s_iota == pos[:, kslot:kslot + 1], gate[:, kslot:kslot + 1], 0.0)
    lax.fori_loop(0, n, lambda q, c: (copy(q).wait(), c)[1], 0)
    h = h1_ref[...] + _dot(weight.astype(BF16), _unpack_halves(ys_ref[slot]))
    ms = jnp.mean(h * h, axis=-1, keepdims=True)
    o_ref[...] = h * lax.rsqrt(ms + NORM_EPS) * fw_ref[...]


def _combine(npieces, gdst3, pos_l, gate_l, h1, fw, ybuf):
    t, d = h1.shape
    n_tiles, _, n_pieces = gdst3.shape
    tt = t // n_tiles
    n_slots = n_pieces * SUBLANES
    row = lambda i, *_: (i, 0)
    grid_spec = pltpu.PrefetchScalarGridSpec(
        num_scalar_prefetch=1,
        grid=(n_tiles,),
        in_specs=[pl.BlockSpec((1, 1, n_pieces), lambda i, *_: (i, 0, 0), memory_space=pltpu.SMEM),
                  pl.BlockSpec((1, 1, n_pieces), lambda i, *_: (jnp.minimum(i + 1, n_tiles - 1), 0, 0),
                               memory_space=pltpu.SMEM),
                  pl.BlockSpec((tt, LANES), row),
                  pl.BlockSpec((tt, LANES), row),
                  pl.BlockSpec((tt, d), row),
                  pl.BlockSpec((1, d), lambda i, *_: (0, 0)),
                  pl.BlockSpec(memory_space=pl.ANY)],
        out_specs=pl.BlockSpec((tt, d), row),
        scratch_shapes=[pltpu.VMEM((2, n_slots, d // 2), U32), pltpu.SemaphoreType.DMA((2,))],
    )
    return pl.pallas_call(
        functools.partial(_combine_kernel, n_slots=n_slots),
        grid_spec=grid_spec,
        out_shape=jax.ShapeDtypeStruct((t, d), F32),
        compiler_params=_cparams(("arbitrary",)),
        name="moe_combine",
    )(npieces, gdst3, gdst3, pos_l, gate_l, h1, fw, ybuf)


def _rotary_tables(positions):
    half = ROT_DIM // 2
    inv_freq = ROPE_THETA ** (-jnp.arange(0, ROT_DIM, 2, dtype=F32) / ROT_DIM)
    ang = positions.astype(F32).reshape(-1, 1) * inv_freq
    cos = jnp.tile(jnp.cos(ang), (1, LANES // half))
    sin = jnp.tile(jnp.sin(ang), (1, LANES // half))
    dim = np.arange(LANES)[None, :] % DIFF_HEAD_DIM
    ctab = jnp.where(dim < ROT_DIM, cos, 1.0)
    satab = jnp.where(dim < half, -sin, 0.0)
    sbtab = jnp.where((dim >= half) & (dim < ROT_DIM), sin, 0.0)
    return ctab, satab, sbtab


def _pad_rows(a, rows):
    return jnp.concatenate([a, jnp.zeros((rows - a.shape[0],) + a.shape[1:], a.dtype)], axis=0)


def _layer(h, l, tabs, attn_norm_w, w_in, diff_lambda_q1, diff_lambda_k1, diff_lambda_q2, diff_lambda_k2,
           diff_subln_w, rwkv_mu, rwkv_w0, rwkv_w_up, rwkv_a0, rwkv_a_up, rwkv_g_up, rwkv_k_k, rwkv_k_a,
           rwkv_r_k, rwkv_ln_w, rwkv_ln_b, w_out, ffn_norm_w, router_w, router_b, exp_w1, exp_b1,
           exp_w2, exp_b2, final_w):
    bsz, seq, d = h.shape
    t = bsz * seq
    w = RWKV_WIDTH
    lambda_init = 0.8 - 0.6 * math.exp(-0.3 * l)
    x2 = h.reshape(t, d)
    row1 = lambda a: a.reshape(1, -1).astype(F32)

    wi = w_in[l]
    qkv_cols = 3 * DIFF_WIDTH
    wq = wi[:, :qkv_cols].astype(BF16)
    o = qkv_cols + 3 * w
    zcol = lambda n: jnp.zeros((d, n), wi.dtype)
    wr = jnp.concatenate([
        wi[:, qkv_cols:o],
        wi[:, o:o + DECAY_LORA], zcol(LORA_PAD - DECAY_LORA),
        wi[:, o + DECAY_LORA:o + DECAY_LORA + AAA_LORA], zcol(LORA_PAD - AAA_LORA),
        wi[:, o + DECAY_LORA + AAA_LORA:], zcol(LORA_PAD - GATE_LORA)], axis=1).astype(BF16)
    mu = rwkv_mu[l]
    zv = lambda n: jnp.zeros((n,), mu.dtype)
    mu_p = jnp.concatenate([
        mu[:3 * w],
        mu[3 * w:3 * w + DECAY_LORA], zv(LORA_PAD - DECAY_LORA),
        mu[3 * w + DECAY_LORA:3 * w + DECAY_LORA + AAA_LORA], zv(LORA_PAD - AAA_LORA),
        mu[3 * w + DECAY_LORA + AAA_LORA:], zv(LORA_PAD - GATE_LORA)]).reshape(1, -1)

    head = np.arange(w) // RWKV_HEAD
    ones = jnp.asarray(head[:, None] == head[None, :], BF16)
    q, k, v, r, lw, kmod, vv, kk, bb, g, bonus = _inproj(
        x2, row1(attn_norm_w[l]), wq, wr, *tabs,
        mu_p, row1(rwkv_w0[l]), row1(rwkv_a0[l]), row1(rwkv_k_k[l]), row1(rwkv_k_a[l]),
        row1(rwkv_r_k[l]), _pad_rows(rwkv_w_up[l].astype(BF16), LORA_PAD),
        _pad_rows(rwkv_a_up[l].astype(BF16), LORA_PAD), _pad_rows(rwkv_g_up[l].astype(BF16), LORA_PAD),
        ones, seq)

    od = _attn(q.reshape(bsz, seq, -1), k.reshape(bsz, seq, -1), v.reshape(bsz, seq, -1),
               row1(diff_lambda_q1[l]), row1(diff_lambda_k1[l]), row1(diff_lambda_q2[l]),
               row1(diff_lambda_k2[l]), row1(diff_subln_w[l]), lambda_init)

    s3 = lambda a: a.reshape(bsz, seq, w)
    y = _scan(s3(r), s3(lw), s3(kmod), s3(vv), s3(kk), s3(bb)).reshape(t, w)

    n_e = router_w.shape[-1]
    rw = jnp.concatenate([router_w[l].astype(F32), jnp.zeros((d, LANES - n_e), F32)], axis=1)
    rw_hi = rw.astype(BF16)
    rw = jnp.stack([rw_hi, (rw - rw_hi.astype(F32)).astype(BF16)])
    rb = jnp.concatenate([router_b[l].astype(F32), jnp.full((LANES - n_e,), -1e30, F32)]).reshape(1, -1)
    h1, xn, sel, idx_l, gate_l, cnt8 = _mix(
        od.reshape(t, -1), y, g, bonus, x2, row1(rwkv_ln_w[l]), row1(rwkv_ln_b[l]), ones,
        w_out[l].astype(BF16), row1(ffn_norm_w[l]), rw, rb)

    bm = EXPERT_BLOCK
    pc = SUBLANES
    tm = MOE_TILE
    n_tiles = t // tm
    cnt = cnt8.reshape(n_tiles, SUBLANES, LANES)[:, 0, :n_e]
    seg = (cnt + pc - 1) // pc * pc
    lend = jnp.cumsum(seg, axis=1)
    lstart = lend - seg
    rows_e = jnp.sum(seg, axis=0)
    padded = (rows_e + bm - 1) // bm * bm
    pad_ends = jnp.cumsum(padded)
    gstart = (pad_ends - padded)[None, :] + jnp.cumsum(seg, axis=0) - seg
    n_slots = tm * TOP_K + n_e * pc
    n_pieces = n_slots // pc
    n_blocks = -(-(t * TOP_K + n_tiles * n_e * (pc - 1) + n_e * (bm - pc)) // bm)
    n_rows = n_blocks * bm
    piece_row = jnp.arange(n_pieces, dtype=I32) * pc
    piece_e = jnp.minimum(jnp.sum(lend[:, None, :] <= piece_row[None, :, None], axis=-1), n_e - 1)
    pick = piece_e[:, :, None] == jnp.arange(n_e, dtype=I32)[None, None, :]
    take = lambda a: jnp.sum(jnp.where(pick, a[:, None, :], 0), axis=-1)
    gdst = ((take(gstart) + piece_row[None, :] - take(lstart)) // pc).astype(I32)
    gdst3 = jnp.clip(gdst, 0, n_rows // pc - 1).reshape(n_tiles, 1, n_pieces)
    npieces = (lend[:, -1] // pc).astype(I32)
    lstart8 = jnp.zeros((n_tiles, SUBLANES, LANES), I32).at[:, :, :n_e].set(lstart[:, None, :])
    lstart8 = lstart8.reshape(n_tiles * SUBLANES, LANES)
    gap_start = jnp.concatenate([pad_ends - padded + rows_e, pad_ends[-1:]]) // pc
    gap_len = jnp.concatenate([padded - rows_e, n_rows - pad_ends[-1:]]) // pc
    gap_end = jnp.cumsum(gap_len)
    max_fill = n_e * (bm // pc - 1) + (n_rows - t * TOP_K) // pc
    j = jnp.arange(max_fill, dtype=I32)
    gap = jnp.minimum(jnp.sum(gap_end[None, :] <= j[:, None], axis=1), n_e)
    in_gap = gap[:, None] == jnp.arange(n_e + 1, dtype=I32)[None, :]
    shift = jnp.sum(jnp.where(in_gap, (gap_start - gap_end + gap_len)[None, :], 0), axis=1)
    fill = jnp.clip(j + shift, 0, n_rows // pc - 1).astype(I32)
    nfill = gap_end[-1:].astype(I32)
    first_row = jnp.arange(n_blocks, dtype=I32) * bm
    block_e = jnp.minimum(jnp.sum(pad_ends[None, :] <= first_row[:, None], axis=1), n_e - 1).astype(I32)
    n_used = (pad_ends[-1] // bm).astype(I32).reshape(1)

    pos_l, xbuf = _dispatch(npieces, fill, nfill, gdst3, sel, idx_l, lstart8, xn, n_rows)

    ff2 = exp_w1.shape[-1]
    b1 = exp_b1[l].astype(F32).reshape(n_e, 1, ff2)
    b2 = exp_b2[l].astype(F32).reshape(n_e, 1, d)
    ybuf = _experts(block_e, padded, n_used, xbuf, exp_w1[l].astype(F32), b1, exp_w2[l].astype(F32), b2)

    out = _combine(npieces, gdst3, pos_l, gate_l, h1, row1(final_w), ybuf)
    return out.reshape(bsz, seq, d)


def kernel(x, positions, attn_norm_w, w_in, diff_lambda_q1, diff_lambda_k1, diff_lambda_q2, diff_lambda_k2, diff_subln_w, rwkv_mu, rwkv_w0, rwkv_w_up, rwkv_a0, rwkv_a_up, rwkv_g_up, rwkv_k_k, rwkv_k_a, rwkv_r_k, rwkv_ln_w, rwkv_ln_b, w_out, ffn_norm_w, router_w, router_b, exp_w1, exp_b1, exp_w2, exp_b2, final_norm_w):
    depth = w_in.shape[0]
    assert depth == 1, "the final norm is fused into the last (only) layer's combine kernel"
    tabs = _rotary_tables(positions)
    return _layer(x, 0, tabs, attn_norm_w, w_in, diff_lambda_q1, diff_lambda_k1, diff_lambda_q2,
                  diff_lambda_k2, diff_subln_w, rwkv_mu, rwkv_w0, rwkv_w_up, rwkv_a0, rwkv_a_up, rwkv_g_up,
                  rwkv_k_k, rwkv_k_a, rwkv_r_k, rwkv_ln_w, rwkv_ln_b, w_out, ffn_norm_w, router_w, router_b,
                  exp_w1, exp_b1, exp_w2, exp_b2, final_norm_w)
```

```python
import functools
import math

import jax
import jax.numpy as jnp
import numpy as np
from jax import lax
from jax.experimental import pallas as pl
from jax.experimental.pallas import tpu as pltpu

F32 = jnp.float32
BF16 = jnp.bfloat16
I32 = jnp.int32
U32 = jnp.uint32

DIFF_HEAD_DIM = 64
DIFF_V_DIM = 128
DIFF_HEADS = 4
DIFF_WIDTH = DIFF_HEADS * DIFF_V_DIM
ROT_DIM = 16
ROPE_THETA = 500000.0
RWKV_HEAD = 64
RWKV_HEADS = 8
RWKV_WIDTH = RWKV_HEAD * RWKV_HEADS
DECAY_LORA = 32
AAA_LORA = 32
GATE_LORA = 96
N_EXPERTS = 32
TOP_K = 4
SWIGLU_LIMIT = 7.0
SWIGLU_ALPHA = 1.702
NORM_EPS = 1e-5
RWKV_GN_EPS = 64e-5

LANES = 128
SUBLANES = 8
VMEM_LIMIT = 56 * 1024 * 1024

ROW_TILE = 512
ATTN_TILE = 512
CHUNK = 64
SCAN_TILE = 512
SCAN_BATCH = 8
GROUP = 4 * RWKV_HEAD
EXPERT_BLOCK = 512
MOE_TILE = 256
CAST_CHUNKS = 8
LORA_PAD = LANES
ZR_COLS = 3 * RWKV_WIDTH + 3 * LORA_PAD


def _cparams(sem, flags=None):
    return pltpu.CompilerParams(dimension_semantics=sem, vmem_limit_bytes=VMEM_LIMIT, flags=flags)


def _nt(a, b):
    return lax.dot_general(a, b, (((1,), (1,)), ((), ())), preferred_element_type=F32)


def _tn(a, b):
    return lax.dot_general(a, b, (((0,), (0,)), ((), ())), preferred_element_type=F32)


def _dot(a, b):
    return jnp.dot(a, b, preferred_element_type=F32)


def _pack_halves(x):
    n = x.shape[1] // 2
    bits = lax.bitcast_convert_type(x, U32)
    return (bits[:, :n] & jnp.uint32(0xFFFF0000)) | (bits[:, n:] >> 16)


def _unpack_halves(p):
    hi = lax.bitcast_convert_type(p & jnp.uint32(0xFFFF0000), F32)
    lo = lax.bitcast_convert_type(p << 16, F32)
    return jnp.concatenate([hi, lo], axis=1).astype(BF16)


def _split(x):
    hi = x.astype(BF16)
    return hi, (x - hi.astype(F32)).astype(BF16)


def _split_dot(x, w_bf16):
    hi, lo = _split(x)
    return _dot(hi, w_bf16) + _dot(lo, w_bf16)


def _split3_dot(x, w_hi, w_lo):
    hi, lo = _split(x)
    return _dot(hi, w_hi) + (_dot(hi, w_lo) + _dot(lo, w_hi))


def _inproj_kernel(x_ref, nw_ref, wq_ref, wr_ref, c_ref, sa_ref, sb_ref,
                   mu_ref, w0_ref, a0_ref, kk_ref, ka_ref, rk_ref, wup_ref, aup_ref, gup_ref, ones_ref,
                   q_ref, k_ref, v_ref, r_ref, lw_ref, rk_out_ref, rv_ref, kkn_ref, b_ref, g_ref, bonus_ref,
                   prev_ref, *, tiles_per_seq):
    x = x_ref[...]
    ms = jnp.mean(x * x, axis=-1, keepdims=True)
    u = (x * lax.rsqrt(ms + NORM_EPS) * nw_ref[...]).astype(BF16)
    zq = _dot(u, wq_ref[...])
    c = c_ref[...]
    sa = sa_ref[...]
    sb = sb_ref[...]
    scale = DIFF_HEAD_DIM ** -0.5
    for g in range(2 * DIFF_HEADS):
        zg = zq[:, g * LANES:(g + 1) * LANES]
        rot = zg * c + pltpu.roll(zg, LANES - ROT_DIM // 2, 1) * sa + pltpu.roll(zg, ROT_DIM // 2, 1) * sb
        if g < DIFF_HEADS:
            q_ref[:, g * LANES:(g + 1) * LANES] = (rot * scale).astype(BF16)
        else:
            h = g - DIFF_HEADS
            k_ref[:, h * LANES:(h + 1) * LANES] = rot.astype(BF16)
    v_ref[...] = zq[:, 2 * DIFF_WIDTH:3 * DIFF_WIDTH].astype(BF16)

    i = pl.program_id(0)
    z = _dot(u, wr_ref[...])
    tm = z.shape[0]
    rows = lax.broadcasted_iota(I32, (tm, 1), 0)

    @pl.when(i % tiles_per_seq == 0)
    def _():
        prev_ref[...] = jnp.zeros_like(prev_ref)

    shifted = jnp.where(rows == 0, prev_ref[SUBLANES - 1:SUBLANES, :], pltpu.roll(z, 1, 0))
    prev_ref[...] = z[tm - SUBLANES:tm, :]
    zf = z + mu_ref[...] * (shifted - z)
    w = RWKV_WIDTH
    r = zf[:, 0:w]
    k = zf[:, w:2 * w]
    v = zf[:, 2 * w:3 * w]
    wd = zf[:, 3 * w:3 * w + LORA_PAD]
    ad = zf[:, 3 * w + LORA_PAD:3 * w + 2 * LORA_PAD]
    gd = zf[:, 3 * w + 2 * LORA_PAD:3 * w + 3 * LORA_PAD]
    pre = w0_ref[...] + _split_dot(jnp.tanh(wd), wup_ref[...])
    neg = -pre
    softplus = jnp.maximum(neg, 0.0) + jnp.log(1.0 + jnp.exp(-jnp.abs(neg)))
    wlog = -softplus - 0.5
    lw_ref[...] = -jnp.exp(wlog)
    a = jax.nn.sigmoid(a0_ref[...] + _split_dot(ad, aup_ref[...]))
    g_ref[...] = _split_dot(jax.nn.sigmoid(gd), gup_ref[...]).astype(g_ref.dtype)
    ones = ones_ref[...]
    kk = k * kk_ref[...]
    norm = jnp.sqrt(_split_dot(kk * kk, ones))
    kk = kk / jnp.maximum(norm, 1e-12)
    k = k * (1.0 + (a - 1.0) * ka_ref[...])
    r_ref[...] = r.astype(r_ref.dtype)
    rk_out_ref[...] = k.astype(rk_out_ref.dtype)
    rv_ref[...] = v.astype(rv_ref.dtype)
    kkn_ref[...] = kk.astype(kkn_ref.dtype)
    b_ref[...] = (kk * a).astype(b_ref.dtype)
    bonus_ref[...] = (_split_dot(r * k * rk_ref[...], ones) * v).astype(bonus_ref.dtype)


def _inproj(x2, nw, wq, wr, ctab, satab, sbtab, mu_p, w0, a0, k_k, k_a, rk, wup, aup, gup, ones, seq):
    t, d = x2.shape
    tm = min(ROW_TILE, seq)
    w = RWKV_WIDTH
    row = lambda i: (i, 0)
    fixed = lambda i: (0, 0)
    vecw = pl.BlockSpec((1, w), fixed)
    lora = pl.BlockSpec((LORA_PAD, w), fixed)
    attn_out = pl.BlockSpec((tm, DIFF_WIDTH), row)
    feat_out = pl.BlockSpec((tm, w), row)
    feat = lambda dt: jax.ShapeDtypeStruct((t, w), dt)
    return pl.pallas_call(
        functools.partial(_inproj_kernel, tiles_per_seq=seq // tm),
        grid=(t // tm,),
        in_specs=[
            pl.BlockSpec((tm, d), row),
            pl.BlockSpec((1, d), fixed),
            pl.BlockSpec(wq.shape, fixed),
            pl.BlockSpec(wr.shape, fixed),
            pl.BlockSpec((tm, LANES), row),
            pl.BlockSpec((tm, LANES), row),
            pl.BlockSpec((tm, LANES), row),
            pl.BlockSpec((1, ZR_COLS), fixed),
            vecw, vecw, vecw, vecw, vecw, lora, lora, lora,
            pl.BlockSpec((w, w), fixed),
        ],
        out_specs=[attn_out] * 3 + [feat_out] * 8,
        out_shape=[jax.ShapeDtypeStruct((t, DIFF_WIDTH), BF16)] * 3
        + [feat(BF16), feat(F32), feat(BF16), feat(BF16), feat(BF16), feat(BF16), feat(BF16), feat(BF16)],
        scratch_shapes=[pltpu.VMEM((SUBLANES, ZR_COLS), F32)],
        compiler_params=_cparams(("arbitrary",)),
        name="inproj",
    )(x2, nw, wq, wr, ctab, satab, sbtab, mu_p, w0, a0, k_k, k_a, rk, wup, aup, gup, ones)


def _attn_kernel(q_ref, k_ref, v_ref, lq1_ref, lk1_ref, lq2_ref, lk2_ref, sw_ref, o_ref,
                 *, tile, lambda_init):
    lane = lax.broadcasted_iota(I32, (1, LANES), 1)
    first = lane < DIFF_HEAD_DIM
    neg = -1e30
    lam = (jnp.exp(jnp.sum(lq1_ref[...] * lk1_ref[...], axis=-1, keepdims=True))
           - jnp.exp(jnp.sum(lq2_ref[...] * lk2_ref[...], axis=-1, keepdims=True)) + lambda_init)
    r = lax.broadcasted_iota(I32, (tile, tile), 0)
    c = lax.broadcasted_iota(I32, (tile, tile), 1)
    keep = c <= r
    for i in range(q_ref.shape[1] // tile):
        q = q_ref[0, i * tile:(i + 1) * tile, :]
        zero = jnp.zeros_like(q)
        n = (i + 1) * tile
        kb = k_ref[0, 0:n, :]
        vb = v_ref[0, 0:n, :]
        outs = []
        for qm in (jnp.where(first, q, zero), jnp.where(first, zero, q)):
            s = _nt(qm, kb)
            diag = jnp.where(keep, s[:, i * tile:], neg)
            s = diag if i == 0 else jnp.concatenate([s[:, :i * tile], diag], axis=1)
            p = jnp.exp(s - jnp.max(s, axis=-1, keepdims=True))
            outs.append(_dot(p.astype(BF16), vb) / jnp.sum(p, axis=-1, keepdims=True))
        o = outs[0] - lam * outs[1]
        ms = jnp.mean(o * o, axis=-1, keepdims=True)
        o = o * lax.rsqrt(ms + NORM_EPS) * sw_ref[...] * (1.0 - lambda_init)
        o_ref[0, i * tile:(i + 1) * tile, :] = o.astype(o_ref.dtype)


def _attn(q3, k3, v3, lq1, lk1, lq2, lk2, sw, lambda_init):
    b, s, _ = q3.shape
    tile = min(ATTN_TILE, s)
    spec = pl.BlockSpec((1, s, LANES), lambda bi, h: (bi, 0, h))
    vec = lambda n: pl.BlockSpec((1, n), lambda bi, h: (0, 0))
    return pl.pallas_call(
        functools.partial(_attn_kernel, tile=tile, lambda_init=lambda_init),
        grid=(b, DIFF_HEADS),
        in_specs=[spec, spec, spec, vec(DIFF_HEAD_DIM), vec(DIFF_HEAD_DIM), vec(DIFF_HEAD_DIM),
                  vec(DIFF_HEAD_DIM), vec(DIFF_V_DIM)],
        out_specs=spec,
        out_shape=jax.ShapeDtypeStruct((b, s, DIFF_WIDTH), BF16),
        compiler_params=_cparams(("parallel", "parallel")),
        name="attn",
    )(q3, k3, v3, lq1, lk1, lq2, lk2, sw)


def _scan_kernel(r_ref, lw_ref, k_ref, v_ref, kk_ref, b_ref, y_ref, state_ref,
                 wr_s, ut_s, utt_s, arb_s, pv_s, bh_s, vk_s, wt_s, *, n_chunks):
    L = CHUNK
    G = GROUP
    n_groups = RWKV_WIDTH // G

    @pl.when(pl.program_id(1) == 0)
    def _():
        state_ref[...] = jnp.zeros_like(state_ref)

    row = lax.broadcasted_iota(I32, (L, G), 0)
    colr = lax.broadcasted_iota(I32, (L, G), 1) & (L - 1)
    strict = (colr < row).astype(F32)
    incl = (colr <= row).astype(F32)
    eye = (colr == row).astype(F32)
    eye_l = (lax.broadcasted_iota(I32, (L, L), 0) == lax.broadcasted_iota(I32, (L, L), 1)).astype(BF16)
    br = lax.broadcasted_iota(I32, (G, G), 0) >> 6
    bc = lax.broadcasted_iota(I32, (G, G), 1) >> 6
    block = (br == bc).astype(F32)
    block_bf = block.astype(BF16)
    rows1 = lax.broadcasted_iota(I32, (L, 1), 0)

    def stack4(x):
        xb = x.astype(BF16)
        return jnp.concatenate([xb, xb, xb, xb], axis=0) * block_bf

    def cat(a, b):
        return jnp.concatenate([a, b], axis=0).astype(BF16)

    def precompute(it, carry):
        chains = [(cl, g) for cl in range(SCAN_BATCH) for g in range(n_groups)]
        each = lambda f, *lists: [f(*args) for args in zip(*lists)]

        def load(ref):
            out = []
            for cl, g in chains:
                base = pl.multiple_of((it * SCAN_BATCH + cl) * L, L)
                out.append(ref[0, pl.ds(base, L), g * G:(g + 1) * G].astype(F32))
            return out

        r, lw, k, v, kk, b = load(r_ref), load(lw_ref), load(k_ref), load(v_ref), load(kk_ref), load(b_ref)

        def cumsum(x):
            sh = 1
            while sh < L:
                x = x + jnp.where(rows1 >= sh, pltpu.roll(x, sh, 0), 0.0)
                sh *= 2
            return x

        cs = each(cumsum, lw)
        tot = each(lambda c: c[L - 1:L, :], cs)
        a_hat = each(lambda kk_, c, l: -kk_ * jnp.exp(c - l), kk, cs, lw)
        r_hat = each(lambda r_, c: r_ * jnp.exp(c), r, cs)
        w_inv = each(lambda c: jnp.exp(-c), cs)
        w_end = each(lambda t_, c: jnp.exp(t_ - c), tot, cs)
        lhs = each(cat, a_hat, r_hat)
        ab = each(lambda l_, b_, wi: _nt(l_, stack4(b_ * wi)), lhs, b, w_inv)
        ak = each(lambda l_, k_, wi: _nt(l_, stack4(k_ * wi)), lhs, k, w_inv)
        a_ab = each(lambda x: x[:L] * strict, ab)
        a_rb = each(lambda x: (x[L:] * incl).astype(BF16), ab)
        a_k = each(lambda x: cat(x[:L] * strict, x[L:] * incl), ak)
        t_mat = each(lambda a: eye + a, a_ab)
        p_mat = each(lambda a: _dot(a.astype(BF16), stack4(a)), a_ab)
        for _ in range(4):
            tp = each(lambda t_, p_: _dot(cat(t_, p_), stack4(p_)), t_mat, p_mat)
            t_mat = each(lambda t_, x: t_ + x[:L], t_mat, tp)
            p_mat = each(lambda x: x[L:], tp)
        t_bf = each(lambda t_, p_: (t_ + _dot(t_.astype(BF16), stack4(p_))).astype(BF16), t_mat, p_mat)
        av = each(lambda a, v_: _dot(a, stack4(v_)), a_k, v)
        w_til = each(lambda t_, a: _dot(t_, stack4(a)), t_bf, a_hat)
        u_til = each(lambda t_, x: _dot(t_, stack4(x[:L])), t_bf, av)
        u_til_t = each(lambda u_: _tn(u_.astype(BF16), eye_l), u_til)
        vk = each(lambda v_, k_, we: _tn(v_.astype(BF16), (k_ * we).astype(BF16)) * block, v, k, w_end)
        for n, (cl, g) in enumerate(chains):
            slot = (it * SCAN_BATCH + cl) * n_groups + g
            wr_s[slot] = cat(w_til[n], r_hat[n])
            ut_s[slot] = u_til[n]
            utt_s[slot] = u_til_t[n]
            arb_s[slot] = a_rb[n]
            pv_s[slot] = av[n][L:]
            bh_s[slot] = (b[n] * w_end[n]).astype(BF16)
            vk_s[slot] = vk[n]
            wt_s[slot] = jnp.broadcast_to(jnp.exp(tot[n]), (SUBLANES, G))
        return carry

    lax.fori_loop(0, n_chunks // SCAN_BATCH, precompute, 0)

    def recur(c, carry):
        gs = range(n_groups)
        slots = [c * n_groups + g for g in gs]
        s0 = [state_ref[g] for g in gs]
        s0b = [s.astype(BF16) for s in s0]
        wr = [wr_s[sl] for sl in slots]
        u_t = [_nt(s0b[g], wr[g][:L]) + utt_s[slots[g]] for g in gs]
        ub = [_dot(u_t[g].astype(BF16), bh_s[slots[g]]) for g in gs]
        for g in gs:
            state_ref[g] = s0[g] * wt_s[slots[g]][0:1, :] + ub[g] * block + vk_s[slots[g]]
        uy = [_nt(wr[g], s0b[g]) for g in gs]
        u = [uy[g][:L] + ut_s[slots[g]] for g in gs]
        base = pl.multiple_of(c * L, L)
        for g in gs:
            y = uy[g][L:] + _dot(arb_s[slots[g]], stack4(u[g])) + pv_s[slots[g]]
            y_ref[0, pl.ds(base, L), g * G:(g + 1) * G] = y
        return carry

    lax.fori_loop(0, n_chunks, recur, 0)


def _scan(r3, lw3, k3, v3, kk3, b3):
    bsz, s, w = r3.shape
    ts = min(SCAN_TILE, s)
    n_chunks = ts // CHUNK
    slots = n_chunks * (w // GROUP)
    L, G = CHUNK, GROUP
    spec = pl.BlockSpec((1, ts, w), lambda bi, i: (bi, i, 0))
    return pl.pallas_call(
        functools.partial(_scan_kernel, n_chunks=n_chunks),
        grid=(bsz, s // ts),
        in_specs=[spec] * 6,
        out_specs=spec,
        out_shape=jax.ShapeDtypeStruct((bsz, s, w), F32),
        scratch_shapes=[pltpu.VMEM((w // GROUP, G, G), F32),
                        pltpu.VMEM((slots, 2 * L, G), BF16),
                        pltpu.VMEM((slots, L, G), F32),
                        pltpu.VMEM((slots, G, L), F32),
                        pltpu.VMEM((slots, L, G), BF16),
                        pltpu.VMEM((slots, L, G), F32),
                        pltpu.VMEM((slots, L, G), BF16),
                        pltpu.VMEM((slots, G, G), F32),
                        pltpu.VMEM((slots, SUBLANES, G), F32)],
        compiler_params=_cparams(("parallel", "arbitrary")),
        name="rwkv_scan",
    )(r3, lw3, k3, v3, kk3, b3)


def _mix_kernel(od_ref, y_ref, g_ref, bonus_ref, x_ref, lnw_ref, lnb_ref, ones_ref, wo_ref,
                fw_ref, rw_ref, rb_ref,
                h1_ref, xn_ref, sel_ref, idx_ref, gate_ref, cnt_ref):
    ones = ones_ref[...]
    y = y_ref[...]
    inv_n = 1.0 / RWKV_HEAD
    mean = _split_dot(y, ones) * inv_n
    d = y - mean
    var = _split_dot(d * d, ones) * inv_n
    yn = d * lax.rsqrt(var + RWKV_GN_EPS) * lnw_ref[...] + lnb_ref[...]
    orw = ((yn + bonus_ref[...]) * g_ref[...]).astype(BF16)
    h1 = (x_ref[...] + _dot(od_ref[...], wo_ref[0:DIFF_WIDTH, :])
          + _dot(orw, wo_ref[DIFF_WIDTH:DIFF_WIDTH + RWKV_WIDTH, :]))
    h1_ref[...] = h1
    ms = jnp.mean(h1 * h1, axis=-1, keepdims=True)
    xn = h1 * lax.rsqrt(ms + NORM_EPS) * fw_ref[...]
    xn_ref[...] = xn.astype(xn_ref.dtype)
    logits = _split3_dot(xn, rw_ref[0], rw_ref[1]) + rb_ref[...]
    tm = logits.shape[0]
    lane = lax.broadcasted_iota(I32, (tm, LANES), 1).astype(F32)
    work = logits
    sel = jnp.zeros((tm, LANES), F32)
    idx_l = jnp.zeros((tm, LANES), F32)
    val_l = jnp.zeros((tm, LANES), F32)
    top = None
    for kslot in range(TOP_K):
        m = jnp.max(work, axis=-1, keepdims=True)
        pick = jnp.min(jnp.where(work == m, lane, float(LANES)), axis=-1, keepdims=True)
        hit = lane == pick
        sel = jnp.where(hit, 1.0, sel)
        idx_l = jnp.where(lane == kslot, pick, idx_l)
        if top is None:
            top = m
        val_l = jnp.where(lane == kslot, jnp.exp(m - top), val_l)
        work = jnp.where(hit, -jnp.inf, work)
    sel_ref[...] = sel
    idx_ref[...] = idx_l.astype(I32)
    gate_ref[...] = val_l / jnp.sum(val_l, axis=-1, keepdims=True)
    for part in range(tm // MOE_TILE):
        count = jnp.sum(sel[part * MOE_TILE:(part + 1) * MOE_TILE], axis=0, keepdims=True)
        cnt_ref[part * SUBLANES:(part + 1) * SUBLANES, :] = jnp.broadcast_to(count, (SUBLANES, LANES)).astype(I32)


def _mix(od, y, g, bonus, x2, lnw, lnb, ones, wo, fw, rw, rb):
    t, d = x2.shape
    tm = min(ROW_TILE, t)
    w = RWKV_WIDTH
    row = lambda i: (i, 0)
    fixed = lambda i: (0, 0)
    rs = lambda n: pl.BlockSpec((tm, n), row)
    return pl.pallas_call(
        _mix_kernel,
        grid=(t // tm,),
        in_specs=[rs(DIFF_WIDTH), rs(w), rs(w), rs(w), rs(d),
                  pl.BlockSpec((1, w), fixed), pl.BlockSpec((1, w), fixed),
                  pl.BlockSpec((w, w), fixed), pl.BlockSpec(wo.shape, fixed),
                  pl.BlockSpec((1, d), fixed), pl.BlockSpec(rw.shape, lambda i: (0, 0, 0)),
                  pl.BlockSpec((1, LANES), fixed)],
        out_specs=[rs(d), rs(d), rs(LANES), rs(LANES), rs(LANES),
                   pl.BlockSpec((tm // MOE_TILE * SUBLANES, LANES), row)],
        out_shape=[jax.ShapeDtypeStruct((t, d), F32), jax.ShapeDtypeStruct((t, d), BF16),
                   jax.ShapeDtypeStruct((t, LANES), F32), jax.ShapeDtypeStruct((t, LANES), I32),
                   jax.ShapeDtypeStruct((t, LANES), F32),
                   jax.ShapeDtypeStruct((t // MOE_TILE * SUBLANES, LANES), I32)],
        compiler_params=_cparams(("parallel",)),
        name="mix_router",
    )(od, y, g, bonus, x2, lnw, lnb, ones, wo, fw, rw, rb)


def _slot_positions(sel, idx_l, lstart):
    tt = sel.shape[0]
    r = lax.broadcasted_iota(I32, (tt, tt), 0)
    c = lax.broadcasted_iota(I32, (tt, tt), 1)
    lower = (c < r).astype(BF16)
    where_to = _dot(lower, sel.astype(BF16)) + lstart
    lane = lax.broadcasted_iota(I32, (tt, LANES), 1).astype(F32)
    idx = idx_l.astype(F32)
    pos = jnp.full((tt, LANES), -1.0, F32)
    for kslot in range(TOP_K):
        e = jnp.sum(jnp.where(lane == kslot, idx, 0.0), axis=-1, keepdims=True)
        p = jnp.sum(jnp.where(lane == e, where_to, 0.0), axis=-1, keepdims=True)
        pos = jnp.where(lane == kslot, p, pos)
    return pos


def _piece(ref, q):
    return ref.at[pl.ds(pl.multiple_of(q * SUBLANES, SUBLANES), SUBLANES), :]


def _drain(count, src_ref, dst_ref, sem, max_pieces):
    b = 0
    while (1 << b) <= max_pieces:
        rows = SUBLANES << b

        @pl.when((count >> b) & 1 == 1)
        def _():
            pltpu.make_async_copy(src_ref.at[pl.ds(0, rows), :], dst_ref.at[pl.ds(0, rows), :], sem).wait()

        b += 1


def _dispatch_kernel(np_ref, fill_ref, nfill_ref, gdst_ref, sel_ref, idx_ref, lstart_ref, xn_ref,
                     pos_ref, buf_ref, xs_ref, zero_ref, sem, fill_sem, *, n_slots):
    i = pl.program_id(0)
    last = pl.num_programs(0) - 1
    slot = i % 2

    def copy(q, sl):
        return pltpu.make_async_copy(_piece(xs_ref.at[sl], q), _piece(buf_ref, gdst_ref[0, 0, q]), sem.at[sl])

    def drain(count, sl):
        _drain(count, xs_ref.at[sl], buf_ref, sem.at[sl], n_slots // SUBLANES)

    @pl.when(i >= 2)
    def _():
        drain(np_ref[jnp.maximum(i - 2, 0)], slot)

    pos = _slot_positions(sel_ref[...], idx_ref[...], lstart_ref[0:1, :].astype(F32))
    pos_ref[...] = pos
    tt = pos.shape[0]
    pos_t = pos.T.astype(I32)
    s_iota = lax.broadcasted_iota(I32, (n_slots, tt), 0)
    perm = jnp.zeros((n_slots, tt), F32)
    for kslot in range(TOP_K):
        perm = perm + jnp.where(s_iota == pos_t[kslot:kslot + 1, :], 1.0, 0.0)
    xs_ref[slot] = _pack_halves(_dot(perm.astype(BF16), xn_ref[...]))
    lax.fori_loop(0, np_ref[i], lambda q, c: (copy(q, slot).start(), c)[1], 0)

    @pl.when(i == 0)
    def _():
        zero_ref[...] = jnp.zeros_like(zero_ref)

        def fill(j):
            return pltpu.make_async_copy(zero_ref, _piece(buf_ref, fill_ref[j]), fill_sem)

        lax.fori_loop(0, nfill_ref[0], lambda j, c: (fill(j).start(), c)[1], 0)
        lax.fori_loop(0, nfill_ref[0], lambda j, c: (fill(j).wait(), c)[1], 0)

    @pl.when(i == last)
    def _():
        @pl.when(i >= 1)
        def _():
            drain(np_ref[jnp.maximum(i - 1, 0)], 1 - slot)

        drain(np_ref[i], slot)


def _dispatch(npieces, fill, nfill, gdst3, sel, idx_l, lstart8, xn, n_rows):
    t, d = xn.shape
    n_tiles, _, n_pieces = gdst3.shape
    tt = t // n_tiles
    n_slots = n_pieces * SUBLANES
    row = lambda i, *_: (i, 0)
    grid_spec = pltpu.PrefetchScalarGridSpec(
        num_scalar_prefetch=3,
        grid=(n_tiles,),
        in_specs=[pl.BlockSpec((1, 1, n_pieces), lambda i, *_: (i, 0, 0), memory_space=pltpu.SMEM),
                  pl.BlockSpec((tt, LANES), row),
                  pl.BlockSpec((tt, LANES), row),
                  pl.BlockSpec((SUBLANES, LANES), row),
                  pl.BlockSpec((tt, d), row)],
        out_specs=[pl.BlockSpec((tt, LANES), row), pl.BlockSpec(memory_space=pl.ANY)],
        scratch_shapes=[pltpu.VMEM((2, n_slots, d // 2), U32), pltpu.VMEM((SUBLANES, d // 2), U32),
                        pltpu.SemaphoreType.DMA((2,)), pltpu.SemaphoreType.DMA(())],
    )
    return pl.pallas_call(
        functools.partial(_dispatch_kernel, n_slots=n_slots),
        grid_spec=grid_spec,
        out_shape=[jax.ShapeDtypeStruct((t, LANES), F32), jax.ShapeDtypeStruct((n_rows, d // 2), U32)],
        compiler_params=_cparams(("arbitrary",)),
        name="moe_dispatch",
    )(npieces, fill, nfill, gdst3, sel, idx_l, lstart8, xn)


def _expert_kernel(be_ref, first_ref, ord_ref, next_ref, nused_ref, x_ref, w1_hbm, b1_ref, w2_hbm, b2_ref,
                   y_ref, w1f_ref, w2f_ref, w1b_ref, w2i_ref, w2b_ref, sem):
    i = pl.program_id(0)
    ff = w2f_ref.shape[1]
    used = i < nused_ref[0]
    new_expert = jnp.logical_and(used, first_ref[i] == 1)

    def fetch(expert, slot):
        return (pltpu.make_async_copy(w1_hbm.at[expert], w1f_ref.at[slot], sem.at[0, slot]),
                pltpu.make_async_copy(w2_hbm.at[expert], w2f_ref.at[slot], sem.at[1, slot]))

    @pl.when(i == 0)
    def _():
        for cp in fetch(be_ref[0], 0):
            cp.start()

    @pl.when(jnp.logical_and(new_expert, next_ref[i] >= 0))
    def _():
        for cp in fetch(next_ref[i], 1 - (ord_ref[i] & 1)):
            cp.start()

    @pl.when(new_expert)
    def _():
        slot = ord_ref[i] & 1
        for cp in fetch(be_ref[i], slot):
            cp.wait()
        rows = w1f_ref.shape[1] // CAST_CHUNKS

        def cast1(c, carry):
            r0 = pl.multiple_of(c * rows, rows)
            w1b_ref[pl.ds(r0, rows), :] = w1f_ref[slot, pl.ds(r0, rows), :].astype(BF16)
            return carry

        lax.fori_loop(0, CAST_CHUNKS, cast1, 0)
        for g in range(w2f_ref.shape[2] // LANES):
            cols = slice(g * LANES, (g + 1) * LANES)
            w2i_ref[pl.ds(0, ff // 2, stride=2), :] = w2f_ref[slot, 0:ff // 2, cols]
            w2i_ref[pl.ds(1, ff // 2, stride=2), :] = w2f_ref[slot, ff // 2:ff, cols]
            w2b_ref[:, cols] = w2i_ref[...].astype(BF16)

    @pl.when(used)
    def _():
        x = _unpack_halves(x_ref[...])
        hid = _dot(x, w1b_ref[...]) + b1_ref[0]
        even = (lax.broadcasted_iota(I32, (1, LANES), 1) & 1) == 0

        def act_even(g):
            hg = hid[:, g * LANES:(g + 1) * LANES]
            glu = jnp.minimum(hg, SWIGLU_LIMIT)
            lin = jnp.clip(hg, -SWIGLU_LIMIT, SWIGLU_LIMIT) + 1.0
            return glu * jax.nn.sigmoid(SWIGLU_ALPHA * glu) * pltpu.roll(lin, LANES - 1, 1)

        half = ff // LANES
        act = jnp.concatenate(
            [jnp.where(even, act_even(g), pltpu.roll(act_even(g + half), 1, 1)) for g in range(half)], axis=1)
        y = _dot(act.astype(BF16), w2b_ref[...]) + b2_ref[0]
        y_ref[...] = _pack_halves(y.astype(BF16).astype(F32))

    @pl.when(jnp.logical_not(used))
    def _():
        y_ref[...] = jnp.zeros_like(y_ref)


def _experts(block_e, rows_per_expert, n_used, xbuf, w1, b1, w2, b2):
    n_rows = xbuf.shape[0]
    bm = EXPERT_BLOCK
    n_blocks = n_rows // bm
    e, d, ff2 = w1.shape
    ff = ff2 // 2
    has = rows_per_expert > 0
    ids = jnp.arange(e, dtype=I32)
    later = (ids[None, :] > ids[:, None]) & has[None, :]
    next_used = jnp.where(jnp.any(later, axis=1), jnp.argmax(later, axis=1), -1).astype(I32)
    ordinal = (jnp.cumsum(has.astype(I32)) - 1).astype(I32)
    first = jnp.concatenate([jnp.ones((1,), I32), (block_e[1:] != block_e[:-1]).astype(I32)])
    of_block = block_e[:, None] == ids[None, :]
    order = jnp.sum(jnp.where(of_block, ordinal[None, :], 0), axis=1).astype(I32)
    next_e = jnp.sum(jnp.where(of_block, next_used[None, :], 0), axis=1).astype(I32)
    blk =lambda i, be, fi, od, ne, nu: (jnp.maximum(jnp.minimum(i, nu[0] - 1), 0), 0)
    ex3 = lambda i, be, fi, od, ne, nu: (be[i], 0, 0)
    grid_spec = pltpu.PrefetchScalarGridSpec(
        num_scalar_prefetch=5,
        grid=(n_blocks,),
        in_specs=[pl.BlockSpec((bm, d // 2), blk),
                  pl.BlockSpec(memory_space=pl.ANY),
                  pl.BlockSpec((1, 1, ff2), ex3),
                  pl.BlockSpec(memory_space=pl.ANY),
                  pl.BlockSpec((1, 1, d), ex3)],
        out_specs=pl.BlockSpec((bm, d // 2), lambda i, be, fi, od, ne, nu: (i, 0)),
        scratch_shapes=[pltpu.VMEM((2, d, ff2), F32), pltpu.VMEM((2, ff, d), F32),
                        pltpu.VMEM((d, ff2), BF16), pltpu.VMEM((ff, LANES), F32), pltpu.VMEM((ff, d), BF16),
                        pltpu.SemaphoreType.DMA((2, 2))],
    )
    return pl.pallas_call(
        _expert_kernel,
        grid_spec=grid_spec,
        out_shape=jax.ShapeDtypeStruct((n_rows, d // 2), U32),
        compiler_params=_cparams(("arbitrary",)),
        name="moe_experts",
    )(block_e, first, order, next_e, n_used, xbuf, w1, b1, w2, b2)


def _combine_kernel(np_ref, gdst_ref, gnext_ref, pos_ref, gate_ref, h1_ref, fw_ref, ybuf_ref, o_ref, ys_ref, sem,
                    *, n_slots):
    i = pl.program_id(0)
    last = pl.num_programs(0) - 1
    slot = i % 2

    def fetch(table_ref, tile, sl):
        count = np_ref[tile]

        def copy(q):
            return pltpu.make_async_copy(_piece(ybuf_ref, table_ref[0, 0, q]), _piece(ys_ref.at[sl], q), sem.at[sl])

        lax.fori_loop(0, count, lambda q, c: (copy(q).start(), c)[1], 0)

        def zero(q, carry):
            _piece(ys_ref.at[sl], q)[...] = jnp.zeros((SUBLANES, ys_ref.shape[2]), ys_ref.dtype)
            return carry

        lax.fori_loop(count, n_slots // SUBLANES, zero, 0)

    @pl.when(i == 0)
    def _():
        fetch(gdst_ref, i, slot)

    @pl.when(i < last)
    def _():
        fetch(gnext_ref, jnp.minimum(i + 1, last), 1 - slot)

    n = np_ref[i]
    pos = pos_ref[...].astype(I32)
    gate = gate_ref[...]
    tt = pos.shape[0]
    s_iota = lax.broadcasted_iota(I32, (tt, n_slots), 1)
    weight = jnp.zeros((tt, n_slots), F32)
    for kslot in range(TOP_K):
        weight = weight + jnp.where(s_iota == pos[:, kslot:kslot + 1], gate[:, kslot:kslot + 1], 0.0)
    _drain(n, ybuf_ref, ys_ref.at[slot], sem.at[slot], n_slots // SUBLANES)
    h = h1_ref[...] + _dot(weight.astype(BF16), _unpack_halves(ys_ref[slot]))
    ms = jnp.mean(h * h, axis=-1, keepdims=True)
    o_ref[...] = h * lax.rsqrt(ms + NORM_EPS) * fw_ref[...]


def _combine(npieces, gdst3, pos_l, gate_l, h1, fw, ybuf):
    t, d = h1.shape
    n_tiles, _, n_pieces = gdst3.shape
    tt = t // n_tiles
    n_slots = n_pieces * SUBLANES
    row = lambda i, *_: (i, 0)
    grid_spec = pltpu.PrefetchScalarGridSpec(
        num_scalar_prefetch=1,
        grid=(n_tiles,),
        in_specs=[pl.BlockSpec((1, 1, n_pieces), lambda i, *_: (i, 0, 0), memory_space=pltpu.SMEM),
                  pl.BlockSpec((1, 1, n_pieces), lambda i, *_: (jnp.minimum(i + 1, n_tiles - 1), 0, 0),
                               memory_space=pltpu.SMEM),
                  pl.BlockSpec((tt, LANES), row),
                  pl.BlockSpec((tt, LANES), row),
                  pl.BlockSpec((tt, d), row),
                  pl.BlockSpec((1, d), lambda i, *_: (0, 0)),
                  pl.BlockSpec(memory_space=pl.ANY)],
        out_specs=pl.BlockSpec((tt, d), row),
        scratch_shapes=[pltpu.VMEM((2, n_slots, d // 2), U32), pltpu.SemaphoreType.DMA((2,))],
    )
    return pl.pallas_call(
        functools.partial(_combine_kernel, n_slots=n_slots),
        grid_spec=grid_spec,
        out_shape=jax.ShapeDtypeStruct((t, d), F32),
        compiler_params=_cparams(("arbitrary",)),
        name="moe_combine",
    )(npieces, gdst3, gdst3, pos_l, gate_l, h1, fw, ybuf)


def _rotary_tables(positions):
    half = ROT_DIM // 2
    inv_freq = ROPE_THETA ** (-jnp.arange(0, ROT_DIM, 2, dtype=F32) / ROT_DIM)
    ang = positions.astype(F32).reshape(-1, 1) * inv_freq
    cos = jnp.tile(jnp.cos(ang), (1, LANES // half))
    sin = jnp.tile(jnp.sin(ang), (1, LANES // half))
    dim = np.arange(LANES)[None, :] % DIFF_HEAD_DIM
    ctab = jnp.where(dim < ROT_DIM, cos, 1.0)
    satab = jnp.where(dim < half, -sin, 0.0)
    sbtab = jnp.where((dim >= half) & (dim < ROT_DIM), sin, 0.0)
    return ctab, satab, sbtab


def _pad_rows(a, rows):
    return jnp.concatenate([a, jnp.zeros((rows - a.shape[0],) + a.shape[1:], a.dtype)], axis=0)


def _layer(h, l, tabs, attn_norm_w, w_in, diff_lambda_q1, diff_lambda_k1, diff_lambda_q2, diff_lambda_k2,
           diff_subln_w, rwkv_mu, rwkv_w0, rwkv_w_up, rwkv_a0, rwkv_a_up, rwkv_g_up, rwkv_k_k, rwkv_k_a,
           rwkv_r_k, rwkv_ln_w, rwkv_ln_b, w_out, ffn_norm_w, router_w, router_b, exp_w1, exp_b1,
           exp_w2, exp_b2, final_w):
    bsz, seq, d = h.shape
    t = bsz * seq
    w = RWKV_WIDTH
    lambda_init = 0.8 - 0.6 * math.exp(-0.3 * l)
    x2 = h.reshape(t, d)
    row1 = lambda a: a.reshape(1, -1).astype(F32)

    wi = w_in[l]
    qkv_cols = 3 * DIFF_WIDTH
    wq = wi[:, :qkv_cols].astype(BF16)
    o = qkv_cols + 3 * w
    zcol = lambda n: jnp.zeros((d, n), wi.dtype)
    wr = jnp.concatenate([
        wi[:, qkv_cols:o],
        wi[:, o:o + DECAY_LORA], zcol(LORA_PAD - DECAY_LORA),
        wi[:, o + DECAY_LORA:o + DECAY_LORA + AAA_LORA], zcol(LORA_PAD - AAA_LORA),
        wi[:, o + DECAY_LORA + AAA_LORA:], zcol(LORA_PAD - GATE_LORA)], axis=1).astype(BF16)
    mu = rwkv_mu[l]
    zv = lambda n: jnp.zeros((n,), mu.dtype)
    mu_p = jnp.concatenate([
        mu[:3 * w],
        mu[3 * w:3 * w + DECAY_LORA], zv(LORA_PAD - DECAY_LORA),
        mu[3 * w + DECAY_LORA:3 * w + DECAY_LORA + AAA_LORA], zv(LORA_PAD - AAA_LORA),
        mu[3 * w + DECAY_LORA + AAA_LORA:], zv(LORA_PAD - GATE_LORA)]).reshape(1, -1)

    head = np.arange(w) // RWKV_HEAD
    ones = jnp.asarray(head[:, None] == head[None, :], BF16)
    q, k, v, r, lw, kmod, vv, kk, bb, g, bonus = _inproj(
        x2, row1(attn_norm_w[l]), wq, wr, *tabs,
        mu_p, row1(rwkv_w0[l]), row1(rwkv_a0[l]), row1(rwkv_k_k[l]), row1(rwkv_k_a[l]),
        row1(rwkv_r_k[l]), _pad_rows(rwkv_w_up[l].astype(BF16), LORA_PAD),
        _pad_rows(rwkv_a_up[l].astype(BF16), LORA_PAD), _pad_rows(rwkv_g_up[l].astype(BF16), LORA_PAD),
        ones, seq)

    od = _attn(q.reshape(bsz, seq, -1), k.reshape(bsz, seq, -1), v.reshape(bsz, seq, -1),
               row1(diff_lambda_q1[l]), row1(diff_lambda_k1[l]), row1(diff_lambda_q2[l]),
               row1(diff_lambda_k2[l]), row1(diff_subln_w[l]), lambda_init)

    s3 = lambda a: a.reshape(bsz, seq, w)
    y = _scan(s3(r), s3(lw), s3(kmod), s3(vv), s3(kk), s3(bb)).reshape(t, w)

    n_e = router_w.shape[-1]
    rw = jnp.concatenate([router_w[l].astype(F32), jnp.zeros((d, LANES - n_e), F32)], axis=1)
    rw_hi = rw.astype(BF16)
    rw = jnp.stack([rw_hi, (rw - rw_hi.astype(F32)).astype(BF16)])
    rb = jnp.concatenate([router_b[l].astype(F32), jnp.full((LANES - n_e,), -1e30, F32)]).reshape(1, -1)
    h1, xn, sel, idx_l, gate_l, cnt8 = _mix(
        od.reshape(t, -1), y, g, bonus, x2, row1(rwkv_ln_w[l]), row1(rwkv_ln_b[l]), ones,
        w_out[l].astype(BF16), row1(ffn_norm_w[l]), rw, rb)

    bm = EXPERT_BLOCK
    pc = SUBLANES
    tm = MOE_TILE
    n_tiles = t // tm
    cnt = cnt8.reshape(n_tiles, SUBLANES, LANES)[:, 0, :n_e]
    seg = (cnt + pc - 1) // pc * pc
    lend = jnp.cumsum(seg, axis=1)
    lstart = lend - seg
    rows_e = jnp.sum(seg, axis=0)
    padded = (rows_e + bm - 1) // bm * bm
    pad_ends = jnp.cumsum(padded)
    gstart = (pad_ends - padded)[None, :] + jnp.cumsum(seg, axis=0) - seg
    n_slots = tm * TOP_K + n_e * pc
    n_pieces = n_slots // pc
    n_blocks = -(-(t * TOP_K + n_tiles * n_e * (pc - 1) + n_e * (bm - pc)) // bm)
    n_rows = n_blocks * bm
    piece_row = jnp.arange(n_pieces, dtype=I32) * pc
    piece_e = jnp.minimum(jnp.sum(lend[:, None, :] <= piece_row[None, :, None], axis=-1), n_e - 1)
    pick = piece_e[:, :, None] == jnp.arange(n_e, dtype=I32)[None, None, :]
    take = lambda a: jnp.sum(jnp.where(pick, a[:, None, :], 0), axis=-1)
    gdst = ((take(gstart) + piece_row[None, :] - take(lstart)) // pc).astype(I32)
    gdst3 = jnp.clip(gdst, 0, n_rows // pc - 1).reshape(n_tiles, 1, n_pieces)
    npieces = (lend[:, -1] // pc).astype(I32)
    lstart8 = jnp.zeros((n_tiles, SUBLANES, LANES), I32).at[:, :, :n_e].set(lstart[:, None, :])
    lstart8 = lstart8.reshape(n_tiles * SUBLANES, LANES)
    gap_start = jnp.concatenate([pad_ends - padded + rows_e, pad_ends[-1:]]) // pc
    gap_len = jnp.concatenate([padded - rows_e, n_rows - pad_ends[-1:]]) // pc
    gap_end = jnp.cumsum(gap_len)
    max_fill = n_e * (bm // pc - 1) + (n_rows - t * TOP_K) // pc
    j = jnp.arange(max_fill, dtype=I32)
    gap = jnp.minimum(jnp.sum(gap_end[None, :] <= j[:, None], axis=1), n_e)
    in_gap = gap[:, None] == jnp.arange(n_e + 1, dtype=I32)[None, :]
    shift = jnp.sum(jnp.where(in_gap, (gap_start - gap_end + gap_len)[None, :], 0), axis=1)
    fill = jnp.clip(j + shift, 0, n_rows // pc - 1).astype(I32)
    nfill = gap_end[-1:].astype(I32)
    first_row = jnp.arange(n_blocks, dtype=I32) * bm
    block_e = jnp.minimum(jnp.sum(pad_ends[None, :] <= first_row[:, None], axis=1), n_e - 1).astype(I32)
    n_used = (pad_ends[-1] // bm).astype(I32).reshape(1)

    pos_l, xbuf = _dispatch(npieces, fill, nfill, gdst3, sel, idx_l, lstart8, xn, n_rows)

    ff2 = exp_w1.shape[-1]
    b1 = exp_b1[l].astype(F32).reshape(n_e, 1, ff2)
    b2 = exp_b2[l].astype(F32).reshape(n_e, 1, d)
    ybuf = _experts(block_e, padded, n_used, xbuf, exp_w1[l].astype(F32), b1, exp_w2[l].astype(F32), b2)

    out = _combine(npieces, gdst3, pos_l, gate_l, h1, row1(final_w), ybuf)
    return out.reshape(bsz, seq, d)


def kernel(x, positions, attn_norm_w, w_in, diff_lambda_q1, diff_lambda_k1, diff_lambda_q2, diff_lambda_k2, diff_subln_w, rwkv_mu, rwkv_w0, rwkv_w_up, rwkv_a0, rwkv_a_up, rwkv_g_up, rwkv_k_k, rwkv_k_a, rwkv_r_k, rwkv_ln_w, rwkv_ln_b, w_out, ffn_norm_w, router_w, router_b, exp_w1, exp_b1, exp_w2, exp_b2, final_norm_w):
    depth = w_in.shape[0]
    assert depth == 1, "the final norm is fused into the last (only) layer's combine kernel"
    tabs = _rotary_tables(positions)
    return _layer(x, 0, tabs, attn_norm_w, w_in, diff_lambda_q1, diff_lambda_k1, diff_lambda_q2,
                  diff_lambda_k2, diff_subln_w, rwkv_mu, rwkv_w0, rwkv_w_up, rwkv_a0, rwkv_a_up, rwkv_g_up,
                  rwkv_k_k, rwkv_k_a, rwkv_r_k, rwkv_ln_w, rwkv_ln_b, w_out, ffn_norm_w, router_w, router_b,
                  exp_w1, exp_b1, exp_w2, exp_b2, final_norm_w)
```

```python
import functools
import math

import jax
import jax.numpy as jnp
import numpy as np
from jax import lax
from jax.experimental import pallas as pl
from jax.experimental.pallas import tpu as pltpu

F32 = jnp.float32
BF16 = jnp.bfloat16
I32 = jnp.int32
U32 = jnp.uint32

DIFF_HEAD_DIM = 64
DIFF_V_DIM = 128
DIFF_HEADS = 4
DIFF_WIDTH = DIFF_HEADS * DIFF_V_DIM
ROT_DIM = 16
ROPE_THETA = 500000.0
RWKV_HEAD = 64
RWKV_HEADS = 8
RWKV_WIDTH = RWKV_HEAD * RWKV_HEADS
DECAY_LORA = 32
AAA_LORA = 32
GATE_LORA = 96
N_EXPERTS = 32
TOP_K = 4
SWIGLU_LIMIT = 7.0
SWIGLU_ALPHA = 1.702
NORM_EPS = 1e-5
RWKV_GN_EPS = 64e-5

LANES = 128
SUBLANES = 8
VMEM_LIMIT = 56 * 1024 * 1024

ROW_TILE = 512
SUB_TILES = 4
ATTN_TILE = 512
CHUNK = 64
SCAN_TILE = 512
SCAN_SEQS = 2
SCAN_BATCH = 4
GROUP = 4 * RWKV_HEAD
EXPERT_BLOCK = 512
MOE_TILE = 256
CAST_CHUNKS = 8
LORA_PAD = LANES
ZR_COLS = 3 * RWKV_WIDTH + 3 * LORA_PAD


def _cparams(sem, flags=None):
    return pltpu.CompilerParams(dimension_semantics=sem, vmem_limit_bytes=VMEM_LIMIT, flags=flags)


def _nt(a, b):
    return lax.dot_general(a, b, (((1,), (1,)), ((), ())), preferred_element_type=F32)


def _tn(a, b):
    return lax.dot_general(a, b, (((0,), (0,)), ((), ())), preferred_element_type=F32)


def _dot(a, b):
    return jnp.dot(a, b, preferred_element_type=F32)


def _pack_halves(x):
    n = x.shape[1] // 2
    bits = lax.bitcast_convert_type(x, U32)
    return (bits[:, :n] & jnp.uint32(0xFFFF0000)) | (bits[:, n:] >> 16)


def _unpack_halves(p):
    hi = lax.bitcast_convert_type(p & jnp.uint32(0xFFFF0000), F32)
    lo = lax.bitcast_convert_type(p << 16, F32)
    return jnp.concatenate([hi, lo], axis=1).astype(BF16)


def _split(x):
    hi = x.astype(BF16)
    return hi, (x - hi.astype(F32)).astype(BF16)


def _split_dot(x, w_bf16):
    hi, lo = _split(x)
    return _dot(hi, w_bf16) + _dot(lo, w_bf16)


def _split3_dot(x, w_hi, w_lo):
    hi, lo = _split(x)
    return _dot(hi, w_hi) + (_dot(hi, w_lo) + _dot(lo, w_hi))


def _inproj_kernel(x_ref, nw_ref, wq_ref, wr_ref, c_ref, sa_ref, sb_ref,
                   mu_ref, w0_ref, a0_ref, kk_ref, ka_ref, rk_ref, wup_ref, aup_ref, gup_ref, ones_ref,
                   q_ref, k_ref, v_ref, r_ref, lw_ref, rk_out_ref, rv_ref, kkn_ref, b_ref, g_ref, bonus_ref,
                   prev_ref, *, tiles_per_seq):
    i = pl.program_id(0)

    @pl.when(i % tiles_per_seq == 0)
    def _():
        prev_ref[...] = jnp.zeros_like(prev_ref)

    tm = x_ref.shape[0]
    sub = tm // SUB_TILES
    scale = DIFF_HEAD_DIM ** -0.5
    w = RWKV_WIDTH
    ones = ones_ref[...]
    rows = lax.broadcasted_iota(I32, (sub, 1), 0)
    prev = prev_ref[SUBLANES - 1:SUBLANES, :]
    for part in range(SUB_TILES):
        rs = slice(part * sub, (part + 1) * sub)
        x = x_ref[rs, :]
        ms = jnp.mean(x * x, axis=-1, keepdims=True)
        u = (x * lax.rsqrt(ms + NORM_EPS) * nw_ref[...]).astype(BF16)
        zq = _dot(u, wq_ref[...])
        c = c_ref[rs, :]
        sa = sa_ref[rs, :]
        sb = sb_ref[rs, :]
        for g in range(2 * DIFF_HEADS):
            zg = zq[:, g * LANES:(g + 1) * LANES]
            rot = zg * c + pltpu.roll(zg, LANES - ROT_DIM // 2, 1) * sa + pltpu.roll(zg, ROT_DIM // 2, 1) * sb
            if g < DIFF_HEADS:
                q_ref[rs, g * LANES:(g + 1) * LANES] = (rot * scale).astype(BF16)
            else:
                h = g - DIFF_HEADS
                k_ref[rs, h * LANES:(h + 1) * LANES] = rot.astype(BF16)
        v_ref[rs, :] = zq[:, 2 * DIFF_WIDTH:3 * DIFF_WIDTH].astype(BF16)

        z = _dot(u, wr_ref[...])
        shifted = jnp.where(rows == 0, prev, pltpu.roll(z, 1, 0))
        prev = z[sub - 1:sub, :]
        if part == SUB_TILES - 1:
            prev_ref[...] = z[sub - SUBLANES:sub, :]
        zf = z + mu_ref[...] * (shifted - z)
        r = zf[:, 0:w]
        k = zf[:, w:2 * w]
        v = zf[:, 2 * w:3 * w]
        wd = zf[:, 3 * w:3 * w + LORA_PAD]
        ad = zf[:, 3 * w + LORA_PAD:3 * w + 2 * LORA_PAD]
        gd = zf[:, 3 * w + 2 * LORA_PAD:3 * w + 3 * LORA_PAD]
        pre = w0_ref[...] + _split_dot(jnp.tanh(wd), wup_ref[...])
        neg = -pre
        softplus = jnp.maximum(neg, 0.0) + jnp.log(1.0 + jnp.exp(-jnp.abs(neg)))
        wlog = -softplus - 0.5
        lw_ref[rs, :] = -jnp.exp(wlog)
        a = jax.nn.sigmoid(a0_ref[...] + _split_dot(ad, aup_ref[...]))
        g_ref[rs, :] = _split_dot(jax.nn.sigmoid(gd), gup_ref[...]).astype(g_ref.dtype)
        kk = k * kk_ref[...]
        norm = jnp.sqrt(_split_dot(kk * kk, ones))
        kk = kk / jnp.maximum(norm, 1e-12)
        k = k * (1.0 + (a - 1.0) * ka_ref[...])
        r_ref[rs, :] = r.astype(r_ref.dtype)
        rk_out_ref[rs, :] = k.astype(rk_out_ref.dtype)
        rv_ref[rs, :] = v.astype(rv_ref.dtype)
        kkn_ref[rs, :] = kk.astype(kkn_ref.dtype)
        b_ref[rs, :] = (kk * a).astype(b_ref.dtype)
        bonus_ref[rs, :] = (_split_dot(r * k * rk_ref[...], ones) * v).astype(bonus_ref.dtype)


def _inproj(x2, nw, wq, wr, ctab, satab, sbtab, mu_p, w0, a0, k_k, k_a, rk, wup, aup, gup, ones, seq):
    t, d = x2.shape
    tm = min(ROW_TILE, seq)
    w = RWKV_WIDTH
    row = lambda i: (i, 0)
    fixed = lambda i: (0, 0)
    vecw = pl.BlockSpec((1, w), fixed)
    lora = pl.BlockSpec((LORA_PAD, w), fixed)
    attn_out = pl.BlockSpec((tm, DIFF_WIDTH), row)
    feat_out = pl.BlockSpec((tm, w), row)
    feat = lambda dt: jax.ShapeDtypeStruct((t, w), dt)
    return pl.pallas_call(
        functools.partial(_inproj_kernel, tiles_per_seq=seq // tm),
        grid=(t // tm,),
        in_specs=[
            pl.BlockSpec((tm, d), row),
            pl.BlockSpec((1, d), fixed),
            pl.BlockSpec(wq.shape, fixed),
            pl.BlockSpec(wr.shape, fixed),
            pl.BlockSpec((tm, LANES), row),
            pl.BlockSpec((tm, LANES), row),
            pl.BlockSpec((tm, LANES), row),
            pl.BlockSpec((1, ZR_COLS), fixed),
            vecw, vecw, vecw, vecw, vecw, lora, lora, lora,
            pl.BlockSpec((w, w), fixed),
        ],
        out_specs=[attn_out] * 3 + [feat_out] * 8,
        out_shape=[jax.ShapeDtypeStruct((t, DIFF_WIDTH), BF16)] * 3
        + [feat(BF16), feat(F32), feat(BF16), feat(BF16), feat(BF16), feat(BF16), feat(BF16), feat(BF16)],
        scratch_shapes=[pltpu.VMEM((SUBLANES, ZR_COLS), F32)],
        compiler_params=_cparams(("arbitrary",)),
        name="inproj",
    )(x2, nw, wq, wr, ctab, satab, sbtab, mu_p, w0, a0, k_k, k_a, rk, wup, aup, gup, ones)


def _attn_kernel(q_ref, k_ref, v_ref, lq1_ref, lk1_ref, lq2_ref, lk2_ref, sw_ref, o_ref,
                 *, tile, lambda_init):
    lane = lax.broadcasted_iota(I32, (1, LANES), 1)
    first = lane < DIFF_HEAD_DIM
    neg = -1e30
    lam = (jnp.exp(jnp.sum(lq1_ref[...] * lk1_ref[...], axis=-1, keepdims=True))
           - jnp.exp(jnp.sum(lq2_ref[...] * lk2_ref[...], axis=-1, keepdims=True)) + lambda_init)
    r = lax.broadcasted_iota(I32, (tile, tile), 0)
    c = lax.broadcasted_iota(I32, (tile, tile), 1)
    keep = c <= r
    for i in range(q_ref.shape[1] // tile):
        q = q_ref[0, i * tile:(i + 1) * tile, :]
        zero = jnp.zeros_like(q)
        n = (i + 1) * tile
        kb = k_ref[0, 0:n, :]
        vb = v_ref[0, 0:n, :]
        outs = []
        for qm in (jnp.where(first, q, zero), jnp.where(first, zero, q)):
            s = _nt(qm, kb)
            diag = jnp.where(keep, s[:, i * tile:], neg)
            s = diag if i == 0 else jnp.concatenate([s[:, :i * tile], diag], axis=1)
            p = jnp.exp(s - jnp.max(s, axis=-1, keepdims=True))
            outs.append(_dot(p.astype(BF16), vb) / jnp.sum(p, axis=-1, keepdims=True))
        o = outs[0] - lam * outs[1]
        ms = jnp.mean(o * o, axis=-1, keepdims=True)
        o = o * lax.rsqrt(ms + NORM_EPS) * sw_ref[...] * (1.0 - lambda_init)
        o_ref[0, i * tile:(i + 1) * tile, :] = o.astype(o_ref.dtype)


def _attn(q3, k3, v3, lq1, lk1, lq2, lk2, sw, lambda_init):
    b, s, _ = q3.shape
    tile = min(ATTN_TILE, s)
    spec = pl.BlockSpec((1, s, LANES), lambda bi, h: (bi, 0, h))
    vec = lambda n: pl.BlockSpec((1, n), lambda bi, h: (0, 0))
    return pl.pallas_call(
        functools.partial(_attn_kernel, tile=tile, lambda_init=lambda_init),
        grid=(b, DIFF_HEADS),
        in_specs=[spec, spec, spec, vec(DIFF_HEAD_DIM), vec(DIFF_HEAD_DIM), vec(DIFF_HEAD_DIM),
                  vec(DIFF_HEAD_DIM), vec(DIFF_V_DIM)],
        out_specs=spec,
        out_shape=jax.ShapeDtypeStruct((b, s, DIFF_WIDTH), BF16),
        compiler_params=_cparams(("parallel", "parallel")),
        name="attn",
    )(q3, k3, v3, lq1, lk1, lq2, lk2, sw)


def _scan_kernel(r_ref, lw_ref, k_ref, v_ref, kk_ref, b_ref, y_ref, state_ref,
                 wr_s, ut_s, utt_s, arb_s, pv_s, bh_s, vk_s, wt_s, *, n_chunks):
    L = CHUNK
    G = GROUP
    n_groups = RWKV_WIDTH // G
    n_seqs = r_ref.shape[0]

    @pl.when(pl.program_id(1) == 0)
    def _():
        state_ref[...] = jnp.zeros_like(state_ref)

    row = lax.broadcasted_iota(I32, (L, G), 0)
    colr = lax.broadcasted_iota(I32, (L, G), 1) & (L - 1)
    strict = (colr < row).astype(F32)
    incl = (colr <= row).astype(F32)
    eye = (colr == row).astype(F32)
    eye_l = (lax.broadcasted_iota(I32, (L, L), 0) == lax.broadcasted_iota(I32, (L, L), 1)).astype(BF16)
    br = lax.broadcasted_iota(I32, (G, G), 0) >> 6
    bc = lax.broadcasted_iota(I32, (G, G), 1) >> 6
    block = (br == bc).astype(F32)
    block_bf = block.astype(BF16)
    rows1 = lax.broadcasted_iota(I32, (L, 1), 0)

    def stack4(x):
        xb = x.astype(BF16)
        return jnp.concatenate([xb, xb, xb, xb], axis=0) * block_bf

    def cat(a, b):
        return jnp.concatenate([a, b], axis=0).astype(BF16)

    def precompute(it, carry):
        chains = [(bb, cl, g) for bb in range(n_seqs) for cl in range(SCAN_BATCH) for g in range(n_groups)]
        each = lambda f, *lists: [f(*args) for args in zip(*lists)]

        def load(ref):
            out = []
            for bb, cl, g in chains:
                base = pl.multiple_of((it * SCAN_BATCH + cl) * L, L)
                out.append(ref[bb, pl.ds(base, L), g * G:(g + 1) * G].astype(F32))
            return out

        r, lw, k, v, kk, b = load(r_ref), load(lw_ref), load(k_ref), load(v_ref), load(kk_ref), load(b_ref)

        def cumsum(x):
            sh = 1
            while sh < L:
                x = x + jnp.where(rows1 >= sh, pltpu.roll(x, sh, 0), 0.0)
                sh *= 2
            return x

        cs = each(cumsum, lw)
        tot = each(lambda c: c[L - 1:L, :], cs)
        a_hat = each(lambda kk_, c, l: -kk_ * jnp.exp(c - l), kk, cs, lw)
        r_hat = each(lambda r_, c: r_ * jnp.exp(c), r, cs)
        w_inv = each(lambda c: jnp.exp(-c), cs)
        w_end = each(lambda t_, c: jnp.exp(t_ - c), tot, cs)
        lhs = each(cat, a_hat, r_hat)
        ab = each(lambda l_, b_, wi: _nt(l_, stack4(b_ * wi)), lhs, b, w_inv)
        ak = each(lambda l_, k_, wi: _nt(l_, stack4(k_ * wi)), lhs, k, w_inv)
        a_ab = each(lambda x: x[:L] * strict, ab)
        a_rb = each(lambda x: (x[L:] * incl).astype(BF16), ab)
        a_k = each(lambda x: cat(x[:L] * strict, x[L:] * incl), ak)
        t_mat = each(lambda a: eye + a, a_ab)
        p_mat = each(lambda a: _dot(a.astype(BF16), stack4(a)), a_ab)
        for _ in range(4):
            tp = each(lambda t_, p_: _dot(cat(t_, p_), stack4(p_)), t_mat, p_mat)
            t_mat = each(lambda t_, x: t_ + x[:L], t_mat, tp)
            p_mat = each(lambda x: x[L:], tp)
        t_bf = each(lambda t_, p_: (t_ + _dot(t_.astype(BF16), stack4(p_))).astype(BF16), t_mat, p_mat)
        av = each(lambda a, v_: _dot(a, stack4(v_)), a_k, v)
        w_til = each(lambda t_, a: _dot(t_, stack4(a)), t_bf, a_hat)
        u_til = each(lambda t_, x: _dot(t_, stack4(x[:L])), t_bf, av)
        u_til_t = each(lambda u_: _tn(u_.astype(BF16), eye_l), u_til)
        vk = each(lambda v_, k_, we: _tn(v_.astype(BF16), (k_ * we).astype(BF16)) * block, v, k, w_end)
        for n, (bb, cl, g) in enumerate(chains):
            slot = ((it * SCAN_BATCH + cl) * n_seqs + bb) * n_groups + g
            wr_s[slot] = cat(w_til[n], r_hat[n])
            ut_s[slot] = u_til[n]
            utt_s[slot] = u_til_t[n]
            arb_s[slot] = a_rb[n]
            pv_s[slot] = av[n][L:]
            bh_s[slot] = (b[n] * w_end[n]).astype(BF16)
            vk_s[slot] = vk[n]
            wt_s[slot] = jnp.broadcast_to(jnp.exp(tot[n]), (SUBLANES, G))
        return carry

    lax.fori_loop(0, n_chunks // SCAN_BATCH, precompute, 0)

    def recur(c, carry):
        gs = range(n_seqs * n_groups)
        slots = [c * n_seqs * n_groups + g for g in gs]
        s0 = [state_ref[g] for g in gs]
        s0b = [s.astype(BF16) for s in s0]
        wr = [wr_s[sl] for sl in slots]
        u_t = [_nt(s0b[g], wr[g][:L]) + utt_s[slots[g]] for g in gs]
        ub = [_dot(u_t[g].astype(BF16), bh_s[slots[g]]) for g in gs]
        for g in gs:
            state_ref[g] = s0[g] * wt_s[slots[g]][0:1, :] + ub[g] * block + vk_s[slots[g]]
        uy = [_nt(wr[g], s0b[g]) for g in gs]
        u = [uy[g][:L] + ut_s[slots[g]] for g in gs]
        base = pl.multiple_of(c * L, L)
        for g in gs:
            y = uy[g][L:] + _dot(arb_s[slots[g]], stack4(u[g])) + pv_s[slots[g]]
            lanes = slice((g % n_groups) * G, (g % n_groups + 1) * G)
            y_ref[g // n_groups, pl.ds(base, L), lanes] = y
        return carry

    lax.fori_loop(0, n_chunks, recur, 0)


def _scan(r3, lw3, k3, v3, kk3, b3):
    bsz, s, w = r3.shape
    ts = min(SCAN_TILE, s)
    n_chunks = ts // CHUNK
    nb = SCAN_SEQS if bsz % SCAN_SEQS == 0 else 1
    slots = nb * n_chunks * (w // GROUP)
    L, G = CHUNK, GROUP
    spec = pl.BlockSpec((nb, ts, w), lambda bi, i: (bi, i, 0))
    return pl.pallas_call(
        functools.partial(_scan_kernel, n_chunks=n_chunks),
        grid=(bsz // nb, s // ts),
        in_specs=[spec] * 6,
        out_specs=spec,
        out_shape=jax.ShapeDtypeStruct((bsz, s, w), F32),
        scratch_shapes=[pltpu.VMEM((nb * (w // GROUP), G, G), F32),
                        pltpu.VMEM((slots, 2 * L, G), BF16),
                        pltpu.VMEM((slots, L, G), F32),
                        pltpu.VMEM((slots, G, L), F32),
                        pltpu.VMEM((slots, L, G), BF16),
                        pltpu.VMEM((slots, L, G), F32),
                        pltpu.VMEM((slots, L, G), BF16),
                        pltpu.VMEM((slots, G, G), F32),
                        pltpu.VMEM((slots, SUBLANES, G), F32)],
        compiler_params=_cparams(("parallel", "arbitrary")),
        name="rwkv_scan",
    )(r3, lw3, k3, v3, kk3, b3)


def _mix_kernel(od_ref, y_ref, g_ref, bonus_ref, x_ref, lnw_ref, lnb_ref, ones_ref, wo_ref,
                fw_ref, rw_ref, rb_ref,
                h1_ref, xn_ref, sel_ref, idx_ref, gate_ref, cnt_ref):
    ones = ones_ref[...]
    y = y_ref[...]
    inv_n = 1.0 / RWKV_HEAD
    mean = _split_dot(y, ones) * inv_n
    d = y - mean
    var = _split_dot(d * d, ones) * inv_n
    yn = d * lax.rsqrt(var + RWKV_GN_EPS) * lnw_ref[...] + lnb_ref[...]
    orw = ((yn + bonus_ref[...]) * g_ref[...]).astype(BF16)
    h1 = (x_ref[...] + _dot(od_ref[...], wo_ref[0:DIFF_WIDTH, :])
          + _dot(orw, wo_ref[DIFF_WIDTH:DIFF_WIDTH + RWKV_WIDTH, :]))
    h1_ref[...] = h1
    ms = jnp.mean(h1 * h1, axis=-1, keepdims=True)
    xn = h1 * lax.rsqrt(ms + NORM_EPS) * fw_ref[...]
    xn_ref[...] = xn.astype(xn_ref.dtype)
    logits = _split3_dot(xn, rw_ref[0], rw_ref[1]) + rb_ref[...]
    tm = logits.shape[0]
    lane = lax.broadcasted_iota(I32, (tm, LANES), 1).astype(F32)
    work = logits
    sel = jnp.zeros((tm, LANES), F32)
    idx_l = jnp.zeros((tm, LANES), F32)
    val_l = jnp.zeros((tm, LANES), F32)
    top = None
    for kslot in range(TOP_K):
        m = jnp.max(work, axis=-1, keepdims=True)
        pick = jnp.min(jnp.where(work == m, lane, float(LANES)), axis=-1, keepdims=True)
        hit = lane == pick
        sel = jnp.where(hit, 1.0, sel)
        idx_l = jnp.where(lane == kslot, pick, idx_l)
        if top is None:
            top = m
        val_l = jnp.where(lane == kslot, jnp.exp(m - top), val_l)
        work = jnp.where(hit, -jnp.inf, work)
    sel_ref[...] = sel
    idx_ref[...] = idx_l.astype(I32)
    gate_ref[...] = val_l / jnp.sum(val_l, axis=-1, keepdims=True)
    for part in range(tm // MOE_TILE):
        count = jnp.sum(sel[part * MOE_TILE:(part + 1) * MOE_TILE], axis=0, keepdims=True)
        cnt_ref[part * SUBLANES:(part + 1) * SUBLANES, :] = jnp.broadcast_to(count, (SUBLANES, LANES)).astype(I32)


def _mix(od, y, g, bonus, x2, lnw, lnb, ones, wo, fw, rw, rb):
    t, d = x2.shape
    tm = min(ROW_TILE, t)
    w = RWKV_WIDTH
    row = lambda i: (i, 0)
    fixed = lambda i: (0, 0)
    rs = lambda n: pl.BlockSpec((tm, n), row)
    return pl.pallas_call(
        _mix_kernel,
        grid=(t // tm,),
        in_specs=[rs(DIFF_WIDTH), rs(w), rs(w), rs(w), rs(d),
                  pl.BlockSpec((1, w), fixed), pl.BlockSpec((1, w), fixed),
                  pl.BlockSpec((w, w), fixed), pl.BlockSpec(wo.shape, fixed),
                  pl.BlockSpec((1, d), fixed), pl.BlockSpec(rw.shape, lambda i: (0, 0, 0)),
                  pl.BlockSpec((1, LANES), fixed)],
        out_specs=[rs(d), rs(d), rs(LANES), rs(LANES), rs(LANES),
                   pl.BlockSpec((tm // MOE_TILE * SUBLANES, LANES), row)],
        out_shape=[jax.ShapeDtypeStruct((t, d), F32), jax.ShapeDtypeStruct((t, d), BF16),
                   jax.ShapeDtypeStruct((t, LANES), F32), jax.ShapeDtypeStruct((t, LANES), I32),
                   jax.ShapeDtypeStruct((t, LANES), F32),
                   jax.ShapeDtypeStruct((t // MOE_TILE * SUBLANES, LANES), I32)],
        compiler_params=_cparams(("parallel",)),
        name="mix_router",
    )(od, y, g, bonus, x2, lnw, lnb, ones, wo, fw, rw, rb)


def _slot_positions(sel, idx_l, lstart):
    tt = sel.shape[0]
    r = lax.broadcasted_iota(I32, (tt, tt), 0)
    c = lax.broadcasted_iota(I32, (tt, tt), 1)
    lower = (c < r).astype(BF16)
    where_to = _dot(lower, sel.astype(BF16)) + lstart
    lane = lax.broadcasted_iota(I32, (tt, LANES), 1).astype(F32)
    idx = idx_l.astype(F32)
    pos = jnp.full((tt, LANES), -1.0, F32)
    for kslot in range(TOP_K):
        e = jnp.sum(jnp.where(lane == kslot, idx, 0.0), axis=-1, keepdims=True)
        p = jnp.sum(jnp.where(lane == e, where_to, 0.0), axis=-1, keepdims=True)
        pos = jnp.where(lane == kslot, p, pos)
    return pos


def _piece(ref, q):
    return ref.at[pl.ds(pl.multiple_of(q * SUBLANES, SUBLANES), SUBLANES), :]


def _drain(count, src_ref, dst_ref, sem, max_pieces):
    b = 0
    while (1 << b) <= max_pieces:
        rows = SUBLANES << b

        @pl.when((count >> b) & 1 == 1)
        def _():
            pltpu.make_async_copy(src_ref.at[pl.ds(0, rows), :], dst_ref.at[pl.ds(0, rows), :], sem).wait()

        b += 1


def _dispatch_kernel(np_ref, fill_ref, nfill_ref, gdst_ref, sel_ref, idx_ref, lstart_ref, xn_ref,
                     pos_ref, buf_ref, xs_ref, zero_ref, sem, fill_sem, *, n_slots):
    i = pl.program_id(0)
    last = pl.num_programs(0) - 1
    slot = i % 2

    def copy(q, sl):
        return pltpu.make_async_copy(_piece(xs_ref.at[sl], q), _piece(buf_ref, gdst_ref[0, 0, q]), sem.at[sl])

    def drain(count, sl):
        _drain(count, xs_ref.at[sl], buf_ref, sem.at[sl], n_slots // SUBLANES)

    @pl.when(i >= 2)
    def _():
        drain(np_ref[jnp.maximum(i - 2, 0)], slot)

    pos = _slot_positions(sel_ref[...], idx_ref[...], lstart_ref[0:1, :].astype(F32))
    pos_ref[...] = pos
    tt = pos.shape[0]
    pos_t = pos.T.astype(I32)
    s_iota = lax.broadcasted_iota(I32, (n_slots, tt), 0)
    perm = jnp.zeros((n_slots, tt), F32)
    for kslot in range(TOP_K):
        perm = perm + jnp.where(s_iota == pos_t[kslot:kslot + 1, :], 1.0, 0.0)
    xs_ref[slot] = _pack_halves(_dot(perm.astype(BF16), xn_ref[...]))
    lax.fori_loop(0, np_ref[i], lambda q, c: (copy(q, slot).start(), c)[1], 0)

    @pl.when(i == 0)
    def _():
        zero_ref[...] = jnp.zeros_like(zero_ref)

        def fill(j):
            return pltpu.make_async_copy(zero_ref, _piece(buf_ref, fill_ref[j]), fill_sem)

        lax.fori_loop(0, nfill_ref[0], lambda j, c: (fill(j).start(), c)[1], 0)
        lax.fori_loop(0, nfill_ref[0], lambda j, c: (fill(j).wait(), c)[1], 0)

    @pl.when(i == last)
    def _():
        @pl.when(i >= 1)
        def _():
            drain(np_ref[jnp.maximum(i - 1, 0)], 1 - slot)

        drain(np_ref[i], slot)


def _dispatch(npieces, fill, nfill, gdst3, sel, idx_l, lstart8, xn, n_rows):
    t, d = xn.shape
    n_tiles, _, n_pieces = gdst3.shape
    tt = t // n_tiles
    n_slots = n_pieces * SUBLANES
    row = lambda i, *_: (i, 0)
    grid_spec = pltpu.PrefetchScalarGridSpec(
        num_scalar_prefetch=3,
        grid=(n_tiles,),
        in_specs=[pl.BlockSpec((1, 1, n_pieces), lambda i, *_: (i, 0, 0), memory_space=pltpu.SMEM),
                  pl.BlockSpec((tt, LANES), row),
                  pl.BlockSpec((tt, LANES), row),
                  pl.BlockSpec((SUBLANES, LANES), row),
                  pl.BlockSpec((tt, d), row)],
        out_specs=[pl.BlockSpec((tt, LANES), row), pl.BlockSpec(memory_space=pl.ANY)],
        scratch_shapes=[pltpu.VMEM((2, n_slots, d // 2), U32), pltpu.VMEM((SUBLANES, d // 2), U32),
                        pltpu.SemaphoreType.DMA((2,)), pltpu.SemaphoreType.DMA(())],
    )
    return pl.pallas_call(
        functools.partial(_dispatch_kernel, n_slots=n_slots),
        grid_spec=grid_spec,
        out_shape=[jax.ShapeDtypeStruct((t, LANES), F32), jax.ShapeDtypeStruct((n_rows, d // 2), U32)],
        compiler_params=_cparams(("arbitrary",)),
        name="moe_dispatch",
    )(npieces, fill, nfill, gdst3, sel, idx_l, lstart8, xn)


def _expert_kernel(be_ref, first_ref, ord_ref, next_ref, nused_ref, x_ref, w1_hbm, b1_ref, w2_hbm, b2_ref,
                   y_ref, w1f_ref, w2f_ref, w1b_ref, w2i_ref, w2b_ref, sem):
    i = pl.program_id(0)
    ff = w2f_ref.shape[1]
    used = i < nused_ref[0]
    new_expert = jnp.logical_and(used, first_ref[i] == 1)

    def fetch(expert, slot):
        return (pltpu.make_async_copy(w1_hbm.at[expert], w1f_ref.at[slot], sem.at[0, slot]),
                pltpu.make_async_copy(w2_hbm.at[expert], w2f_ref.at[slot], sem.at[1, slot]))

    @pl.when(i == 0)
    def _():
        for cp in fetch(be_ref[0], 0):
            cp.start()

    @pl.when(jnp.logical_and(new_expert, next_ref[i] >= 0))
    def _():
        for cp in fetch(next_ref[i], 1 - (ord_ref[i] & 1)):
            cp.start()

    @pl.when(new_expert)
    def _():
        slot = ord_ref[i] & 1
        for cp in fetch(be_ref[i], slot):
            cp.wait()
        rows = w1f_ref.shape[1] // CAST_CHUNKS

        def cast1(c, carry):
            r0 = pl.multiple_of(c * rows, rows)
            w1b_ref[pl.ds(r0, rows), :] = w1f_ref[slot, pl.ds(r0, rows), :].astype(BF16)
            return carry

        lax.fori_loop(0, CAST_CHUNKS, cast1, 0)
        for g in range(w2f_ref.shape[2] // LANES):
            cols = slice(g * LANES, (g + 1) * LANES)
            w2i_ref[pl.ds(0, ff // 2, stride=2), :] = w2f_ref[slot, 0:ff // 2, cols]
            w2i_ref[pl.ds(1, ff // 2, stride=2), :] = w2f_ref[slot, ff // 2:ff, cols]
            w2b_ref[:, cols] = w2i_ref[...].astype(BF16)

    @pl.when(used)
    def _():
        x = _unpack_halves(x_ref[...])
        hid = _dot(x, w1b_ref[...]) + b1_ref[0]
        even = (lax.broadcasted_iota(I32, (1, LANES), 1) & 1) == 0

        def act_even(g):
            hg = hid[:, g * LANES:(g + 1) * LANES]
            glu = jnp.minimum(hg, SWIGLU_LIMIT)
            lin = jnp.clip(hg, -SWIGLU_LIMIT, SWIGLU_LIMIT) + 1.0
            return glu * jax.nn.sigmoid(SWIGLU_ALPHA * glu) * pltpu.roll(lin, LANES - 1, 1)

        half = ff // LANES
        act = jnp.concatenate(
            [jnp.where(even, act_even(g), pltpu.roll(act_even(g + half), 1, 1)) for g in range(half)], axis=1)
        y = _dot(act.astype(BF16), w2b_ref[...]) + b2_ref[0]
        y_ref[...] = _pack_halves(y.astype(BF16).astype(F32))

    @pl.when(jnp.logical_not(used))
    def _():
        y_ref[...] = jnp.zeros_like(y_ref)


def _experts(block_e, rows_per_expert, n_used, xbuf, w1, b1, w2, b2):
    n_rows = xbuf.shape[0]
    bm = EXPERT_BLOCK
    n_blocks = n_rows // bm
    e, d, ff2 = w1.shape
    ff = ff2 // 2
    has = rows_per_expert > 0
    ids = jnp.arange(e, dtype=I32)
    later = (ids[None, :] > ids[:, None]) & has[None, :]
    next_used = jnp.where(jnp.any(later, axis=1), jnp.argmax(later, axis=1), -1).astype(I32)
    ordinal = (jnp.cumsum(has.astype(I32)) - 1).astype(I32)
    first = jnp.concatenate([jnp.ones((1,), I32), (block_e[1:] != block_e[:-1]).astype(I32)])
    of_block = block_e[:, None] == ids[None, :]
    order = jnp.sum(jnp.where(of_block, ordinal[None, :], 0), axis=1).astype(I32)
    next_e = jnp.sum(jnp.where(of_block, next_used[None, :], 0), axis=1).astype(I32)
    blk =lambda i, be, fi, od, ne, nu: (jnp.maximum(jnp.minimum(i, nu[0] - 1), 0), 0)
    ex3 = lambda i, be, fi, od, ne, nu: (be[i], 0, 0)
    grid_spec = pltpu.PrefetchScalarGridSpec(
        num_scalar_prefetch=5,
        grid=(n_blocks,),
        in_specs=[pl.BlockSpec((bm, d // 2), blk),
                  pl.BlockSpec(memory_space=pl.ANY),
                  pl.BlockSpec((1, 1, ff2), ex3),
                  pl.BlockSpec(memory_space=pl.ANY),
                  pl.BlockSpec((1, 1, d), ex3)],
        out_specs=pl.BlockSpec((bm, d // 2), lambda i, be, fi, od, ne, nu: (i, 0)),
        scratch_shapes=[pltpu.VMEM((2, d, ff2), F32), pltpu.VMEM((2, ff, d), F32),
                        pltpu.VMEM((d, ff2), BF16), pltpu.VMEM((ff, LANES), F32), pltpu.VMEM((ff, d), BF16),
                        pltpu.SemaphoreType.DMA((2, 2))],
    )
    return pl.pallas_call(
        _expert_kernel,
        grid_spec=grid_spec,
        out_shape=jax.ShapeDtypeStruct((n_rows, d // 2), U32),
        compiler_params=_cparams(("arbitrary",)),
        name="moe_experts",
    )(block_e, first, order, next_e, n_used, xbuf, w1, b1, w2, b2)


def _combine_kernel(np_ref, gdst_ref, gnext_ref, pos_ref, gate_ref, h1_ref, fw_ref, ybuf_ref, o_ref, ys_ref, sem,
                    *, n_slots):
    i = pl.program_id(0)
    last = pl.num_programs(0) - 1
    slot = i % 2

    def fetch(table_ref, tile, sl):
        count = np_ref[tile]

        def copy(q):
            return pltpu.make_async_copy(_piece(ybuf_ref, table_ref[0, 0, q]), _piece(ys_ref.at[sl], q), sem.at[sl])

        lax.fori_loop(0, count, lambda q, c: (copy(q).start(), c)[1], 0)

        def zero(q, carry):
            _piece(ys_ref.at[sl], q)[...] = jnp.zeros((SUBLANES, ys_ref.shape[2]), ys_ref.dtype)
            return carry

        lax.fori_loop(count, n_slots // SUBLANES, zero, 0)

    @pl.when(i == 0)
    def _():
        fetch(gdst_ref, i, slot)

    @pl.when(i < last)
    def _():
        fetch(gnext_ref, jnp.minimum(i + 1, last), 1 - slot)

    n = np_ref[i]
    pos = pos_ref[...].astype(I32)
    gate = gate_ref[...]
    tt = pos.shape[0]
    s_iota = lax.broadcasted_iota(I32, (tt, n_slots), 1)
    weight = jnp.zeros((tt, n_slots), F32)
    for kslot in range(TOP_K):
        weight = weight + jnp.where(s_iota == pos[:, kslot:kslot + 1], gate[:, kslot:kslot + 1], 0.0)
    _drain(n, ybuf_ref, ys_ref.at[slot], sem.at[slot], n_slots // SUBLANES)
    h = h1_ref[...] + _dot(weight.astype(BF16), _unpack_halves(ys_ref[slot]))
    ms = jnp.mean(h * h, axis=-1, keepdims=True)
    o_ref[...] = h * lax.rsqrt(ms + NORM_EPS) * fw_ref[...]


def _combine(npieces, gdst3, pos_l, gate_l, h1, fw, ybuf):
    t, d = h1.shape
    n_tiles, _, n_pieces = gdst3.shape
    tt = t // n_tiles
    n_slots = n_pieces * SUBLANES
    row = lambda i, *_: (i, 0)
    grid_spec = pltpu.PrefetchScalarGridSpec(
        num_scalar_prefetch=1,
        grid=(n_tiles,),
        in_specs=[pl.BlockSpec((1, 1, n_pieces), lambda i, *_: (i, 0, 0), memory_space=pltpu.SMEM),
                  pl.BlockSpec((1, 1, n_pieces), lambda i, *_: (jnp.minimum(i + 1, n_tiles - 1), 0, 0),
                               memory_space=pltpu.SMEM),
                  pl.BlockSpec((tt, LANES), row),
                  pl.BlockSpec((tt, LANES), row),
                  pl.BlockSpec((tt, d), row),
                  pl.BlockSpec((1, d), lambda i, *_: (0, 0)),
                  pl.BlockSpec(memory_space=pl.ANY)],
        out_specs=pl.BlockSpec((tt, d), row),
        scratch_shapes=[pltpu.VMEM((2, n_slots, d // 2), U32), pltpu.SemaphoreType.DMA((2,))],
    )
    return pl.pallas_call(
        functools.partial(_combine_kernel, n_slots=n_slots),
        grid_spec=grid_spec,
        out_shape=jax.ShapeDtypeStruct((t, d), F32),
        compiler_params=_cparams(("arbitrary",)),
        name="moe_combine",
    )(npieces, gdst3, gdst3, pos_l, gate_l, h1, fw, ybuf)


def _rotary_tables(positions):
    half = ROT_DIM // 2
    inv_freq = ROPE_THETA ** (-jnp.arange(0, ROT_DIM, 2, dtype=F32) / ROT_DIM)
    ang = positions.astype(F32).reshape(-1, 1) * inv_freq
    cos = jnp.tile(jnp.cos(ang), (1, LANES // half))
    sin = jnp.tile(jnp.sin(ang), (1, LANES // half))
    dim = np.arange(LANES)[None, :] % DIFF_HEAD_DIM
    ctab = jnp.where(dim < ROT_DIM, cos, 1.0)
    satab = jnp.where(dim < half, -sin, 0.0)
    sbtab = jnp.where((dim >= half) & (dim < ROT_DIM), sin, 0.0)
    return ctab, satab, sbtab


def _pad_rows(a, rows):
    return jnp.concatenate([a, jnp.zeros((rows - a.shape[0],) + a.shape[1:], a.dtype)], axis=0)


def _layer(h, l, tabs, attn_norm_w, w_in, diff_lambda_q1, diff_lambda_k1, diff_lambda_q2, diff_lambda_k2,
           diff_subln_w, rwkv_mu, rwkv_w0, rwkv_w_up, rwkv_a0, rwkv_a_up, rwkv_g_up, rwkv_k_k, rwkv_k_a,
           rwkv_r_k, rwkv_ln_w, rwkv_ln_b, w_out, ffn_norm_w, router_w, router_b, exp_w1, exp_b1,
           exp_w2, exp_b2, final_w):
    bsz, seq, d = h.shape
    t = bsz * seq
    w = RWKV_WIDTH
    lambda_init = 0.8 - 0.6 * math.exp(-0.3 * l)
    x2 = h.reshape(t, d)
    row1 = lambda a: a.reshape(1, -1).astype(F32)

    wi = w_in[l]
    qkv_cols = 3 * DIFF_WIDTH
    wq = wi[:, :qkv_cols].astype(BF16)
    o = qkv_cols + 3 * w
    zcol = lambda n: jnp.zeros((d, n), wi.dtype)
    wr = jnp.concatenate([
        wi[:, qkv_cols:o],
        wi[:, o:o + DECAY_LORA], zcol(LORA_PAD - DECAY_LORA),
        wi[:, o + DECAY_LORA:o + DECAY_LORA + AAA_LORA], zcol(LORA_PAD - AAA_LORA),
        wi[:, o + DECAY_LORA + AAA_LORA:], zcol(LORA_PAD - GATE_LORA)], axis=1).astype(BF16)
    mu = rwkv_mu[l]
    zv = lambda n: jnp.zeros((n,), mu.dtype)
    mu_p = jnp.concatenate([
        mu[:3 * w],
        mu[3 * w:3 * w + DECAY_LORA], zv(LORA_PAD - DECAY_LORA),
        mu[3 * w + DECAY_LORA:3 * w + DECAY_LORA + AAA_LORA], zv(LORA_PAD - AAA_LORA),
        mu[3 * w + DECAY_LORA + AAA_LORA:], zv(LORA_PAD - GATE_LORA)]).reshape(1, -1)

    head = np.arange(w) // RWKV_HEAD
    ones = jnp.asarray(head[:, None] == head[None, :], BF16)
    q, k, v, r, lw, kmod, vv, kk, bb, g, bonus = _inproj(
        x2, row1(attn_norm_w[l]), wq, wr, *tabs,
        mu_p, row1(rwkv_w0[l]), row1(rwkv_a0[l]), row1(rwkv_k_k[l]), row1(rwkv_k_a[l]),
        row1(rwkv_r_k[l]), _pad_rows(rwkv_w_up[l].astype(BF16), LORA_PAD),
        _pad_rows(rwkv_a_up[l].astype(BF16), LORA_PAD), _pad_rows(rwkv_g_up[l].astype(BF16), LORA_PAD),
        ones, seq)

    od = _attn(q.reshape(bsz, seq, -1), k.reshape(bsz, seq, -1), v.reshape(bsz, seq, -1),
               row1(diff_lambda_q1[l]), row1(diff_lambda_k1[l]), row1(diff_lambda_q2[l]),
               row1(diff_lambda_k2[l]), row1(diff_subln_w[l]), lambda_init)

    s3 = lambda a: a.reshape(bsz, seq, w)
    y = _scan(s3(r), s3(lw), s3(kmod), s3(vv), s3(kk), s3(bb)).reshape(t, w)

    n_e = router_w.shape[-1]
    rw = jnp.concatenate([router_w[l].astype(F32), jnp.zeros((d, LANES - n_e), F32)], axis=1)
    rw_hi = rw.astype(BF16)
    rw = jnp.stack([rw_hi, (rw - rw_hi.astype(F32)).astype(BF16)])
    rb = jnp.concatenate([router_b[l].astype(F32), jnp.full((LANES - n_e,), -1e30, F32)]).reshape(1, -1)
    h1, xn, sel, idx_l, gate_l, cnt8 = _mix(
        od.reshape(t, -1), y, g, bonus, x2, row1(rwkv_ln_w[l]), row1(rwkv_ln_b[l]), ones,
        w_out[l].astype(BF16), row1(ffn_norm_w[l]), rw, rb)

    bm = EXPERT_BLOCK
    pc = SUBLANES
    tm = MOE_TILE
    n_tiles = t // tm
    cnt = cnt8.reshape(n_tiles, SUBLANES, LANES)[:, 0, :n_e]
    seg = (cnt + pc - 1) // pc * pc
    lend = jnp.cumsum(seg, axis=1)
    lstart = lend - seg
    rows_e = jnp.sum(seg, axis=0)
    padded = (rows_e + bm - 1) // bm * bm
    pad_ends = jnp.cumsum(padded)
    gstart = (pad_ends - padded)[None, :] + jnp.cumsum(seg, axis=0) - seg
    n_slots = tm * TOP_K + n_e * pc
    n_pieces = n_slots // pc
    n_blocks = -(-(t * TOP_K + n_tiles * n_e * (pc - 1) + n_e * (bm - pc)) // bm)
    n_rows = n_blocks * bm
    piece_row = jnp.arange(n_pieces, dtype=I32) * pc
    piece_e = jnp.minimum(jnp.sum(lend[:, None, :] <= piece_row[None, :, None], axis=-1), n_e - 1)
    pick = piece_e[:, :, None] == jnp.arange(n_e, dtype=I32)[None, None, :]
    take = lambda a: jnp.sum(jnp.where(pick, a[:, None, :], 0), axis=-1)
    gdst = ((take(gstart) + piece_row[None, :] - take(lstart)) // pc).astype(I32)
    gdst3 = jnp.clip(gdst, 0, n_rows // pc - 1).reshape(n_tiles, 1, n_pieces)
    npieces = (lend[:, -1] // pc).astype(I32)
    lstart8 = jnp.zeros((n_tiles, SUBLANES, LANES), I32).at[:, :, :n_e].set(lstart[:, None, :])
    lstart8 = lstart8.reshape(n_tiles * SUBLANES, LANES)
    gap_start = jnp.concatenate([pad_ends - padded + rows_e, pad_ends[-1:]]) // pc
    gap_len = jnp.concatenate([padded - rows_e, n_rows - pad_ends[-1:]]) // pc
    gap_end = jnp.cumsum(gap_len)
    max_fill = n_e * (bm // pc - 1) + (n_rows - t * TOP_K) // pc
    j = jnp.arange(max_fill, dtype=I32)
    gap = jnp.minimum(jnp.sum(gap_end[None, :] <= j[:, None], axis=1), n_e)
    in_gap = gap[:, None] == jnp.arange(n_e + 1, dtype=I32)[None, :]
    shift = jnp.sum(jnp.where(in_gap, (gap_start - gap_end + gap_len)[None, :], 0), axis=1)
    fill = jnp.clip(j + shift, 0, n_rows // pc - 1).astype(I32)
    nfill = gap_end[-1:].astype(I32)
    first_row = jnp.arange(n_blocks, dtype=I32) * bm
    block_e = jnp.minimum(jnp.sum(pad_ends[None, :] <= first_row[:, None], axis=1), n_e - 1).astype(I32)
    n_used = (pad_ends[-1] // bm).astype(I32).reshape(1)

    pos_l, xbuf = _dispatch(npieces, fill, nfill, gdst3, sel, idx_l, lstart8, xn, n_rows)

    ff2 = exp_w1.shape[-1]
    b1 = exp_b1[l].astype(F32).reshape(n_e, 1, ff2)
    b2 = exp_b2[l].astype(F32).reshape(n_e, 1, d)
    ybuf = _experts(block_e, padded, n_used, xbuf, exp_w1[l].astype(F32), b1, exp_w2[l].astype(F32), b2)

    out = _combine(npieces, gdst3, pos_l, gate_l, h1, row1(final_w), ybuf)
    return out.reshape(bsz, seq, d)


def kernel(x, positions, attn_norm_w, w_in, diff_lambda_q1, diff_lambda_k1, diff_lambda_q2, diff_lambda_k2, diff_subln_w, rwkv_mu, rwkv_w0, rwkv_w_up, rwkv_a0, rwkv_a_up, rwkv_g_up, rwkv_k_k, rwkv_k_a, rwkv_r_k, rwkv_ln_w, rwkv_ln_b, w_out, ffn_norm_w, router_w, router_b, exp_w1, exp_b1, exp_w2, exp_b2, final_norm_w):
    depth = w_in.shape[0]
    assert depth == 1, "the final norm is fused into the last (only) layer's combine kernel"
    tabs = _rotary_tables(positions)
    return _layer(x, 0, tabs, attn_norm_w, w_in, diff_lambda_q1, diff_lambda_k1, diff_lambda_q2,
                  diff_lambda_k2, diff_subln_w, rwkv_mu, rwkv_w0, rwkv_w_up, rwkv_a0, rwkv_a_up, rwkv_g_up,
                  rwkv_k_k, rwkv_k_a, rwkv_r_k, rwkv_ln_w, rwkv_ln_b, w_out, ffn_norm_w, router_w, router_b,
                  exp_w1, exp_b1, exp_w2, exp_b2, final_norm_w)
```

```python
import functools
import math

import jax
import jax.numpy as jnp
import numpy as np
from jax import lax
from jax.experimental import pallas as pl
from jax.experimental.pallas import tpu as pltpu

F32 = jnp.float32
BF16 = jnp.bfloat16
I32 = jnp.int32
U32 = jnp.uint32

DIFF_HEAD_DIM = 64
DIFF_V_DIM = 128
DIFF_HEADS = 4
DIFF_WIDTH = DIFF_HEADS * DIFF_V_DIM
ROT_DIM = 16
ROPE_THETA = 500000.0
RWKV_HEAD = 64
RWKV_HEADS = 8
RWKV_WIDTH = RWKV_HEAD * RWKV_HEADS
DECAY_LORA = 32
AAA_LORA = 32
GATE_LORA = 96
N_EXPERTS = 32
TOP_K = 4
SWIGLU_LIMIT = 7.0
SWIGLU_ALPHA = 1.702
NORM_EPS = 1e-5
RWKV_GN_EPS = 64e-5

LANES = 128
SUBLANES = 8
VMEM_LIMIT = 56 * 1024 * 1024

ROW_TILE = 512
SUB_TILES = 2
ATTN_TILE = 512
CHUNK = 64
SCAN_TILE = 512
SCAN_SEQS = 2
SCAN_BATCH = 4
GROUP = 4 * RWKV_HEAD
EXPERT_BLOCK = 512
MOE_TILE = 256
CAST_CHUNKS = 8
LORA_PAD = LANES
ZR_COLS = 3 * RWKV_WIDTH + 3 * LORA_PAD


def _cparams(sem, flags=None):
    return pltpu.CompilerParams(dimension_semantics=sem, vmem_limit_bytes=VMEM_LIMIT, flags=flags)


def _nt(a, b):
    return lax.dot_general(a, b, (((1,), (1,)), ((), ())), preferred_element_type=F32)


def _tn(a, b):
    return lax.dot_general(a, b, (((0,), (0,)), ((), ())), preferred_element_type=F32)


def _dot(a, b):
    return jnp.dot(a, b, preferred_element_type=F32)


def _pack_halves(x):
    n = x.shape[1] // 2
    bits = lax.bitcast_convert_type(x, U32)
    return (bits[:, :n] & jnp.uint32(0xFFFF0000)) | (bits[:, n:] >> 16)


def _unpack_halves(p):
    hi = lax.bitcast_convert_type(p & jnp.uint32(0xFFFF0000), F32)
    lo = lax.bitcast_convert_type(p << 16, F32)
    return jnp.concatenate([hi, lo], axis=1).astype(BF16)


def _split(x):
    hi = x.astype(BF16)
    return hi, (x - hi.astype(F32)).astype(BF16)


def _split_dot(x, w_bf16):
    hi, lo = _split(x)
    return _dot(hi, w_bf16) + _dot(lo, w_bf16)


def _split3_dot(x, w_parts):
    n = w_parts.shape[1] // 2
    hi, lo = _split(x)
    both = _dot(hi, w_parts)
    return both[:, :n] + (both[:, n:] + _dot(lo, w_parts[:, :n]))


def _inproj_kernel(x_ref, nw_ref, wq_ref, wr_ref, c_ref, sa_ref, sb_ref,
                   mu_ref, w0_ref, a0_ref, kk_ref, ka_ref, rk_ref, wup_ref, aup_ref, gup_ref, ones_ref,
                   q_ref, k_ref, v_ref, r_ref, lw_ref, rk_out_ref, rv_ref, kkn_ref, b_ref, g_ref, bonus_ref,
                   prev_ref, *, tiles_per_seq):
    i = pl.program_id(0)

    @pl.when(i % tiles_per_seq == 0)
    def _():
        prev_ref[...] = jnp.zeros_like(prev_ref)

    tm = x_ref.shape[0]
    sub = tm // SUB_TILES
    scale = DIFF_HEAD_DIM ** -0.5
    w = RWKV_WIDTH
    ones = ones_ref[...]
    rows = lax.broadcasted_iota(I32, (sub, 1), 0)
    prev = prev_ref[SUBLANES - 1:SUBLANES, :]
    for part in range(SUB_TILES):
        rs = slice(part * sub, (part + 1) * sub)
        x = x_ref[rs, :]
        ms = jnp.mean(x * x, axis=-1, keepdims=True)
        u = (x * lax.rsqrt(ms + NORM_EPS) * nw_ref[...]).astype(BF16)
        zq = _dot(u, wq_ref[...])
        c = c_ref[rs, :]
        sa = sa_ref[rs, :]
        sb = sb_ref[rs, :]
        for g in range(2 * DIFF_HEADS):
            zg = zq[:, g * LANES:(g + 1) * LANES]
            rot = zg * c + pltpu.roll(zg, LANES - ROT_DIM // 2, 1) * sa + pltpu.roll(zg, ROT_DIM // 2, 1) * sb
            if g < DIFF_HEADS:
                q_ref[rs, g * LANES:(g + 1) * LANES] = (rot * scale).astype(BF16)
            else:
                h = g - DIFF_HEADS
                k_ref[rs, h * LANES:(h + 1) * LANES] = rot.astype(BF16)
        v_ref[rs, :] = zq[:, 2 * DIFF_WIDTH:3 * DIFF_WIDTH].astype(BF16)

        z = _dot(u, wr_ref[...])
        shifted = jnp.where(rows == 0, prev, pltpu.roll(z, 1, 0))
        prev = z[sub - 1:sub, :]
        if part == SUB_TILES - 1:
            prev_ref[...] = z[sub - SUBLANES:sub, :]
        zf = z + mu_ref[...] * (shifted - z)
        r = zf[:, 0:w]
        k = zf[:, w:2 * w]
        v = zf[:, 2 * w:3 * w]
        wd = zf[:, 3 * w:3 * w + LORA_PAD]
        ad = zf[:, 3 * w + LORA_PAD:3 * w + 2 * LORA_PAD]
        gd = zf[:, 3 * w + 2 * LORA_PAD:3 * w + 3 * LORA_PAD]
        pre = w0_ref[...] + _split_dot(jnp.tanh(wd), wup_ref[...])
        neg = -pre
        softplus = jnp.maximum(neg, 0.0) + jnp.log(1.0 + jnp.exp(-jnp.abs(neg)))
        wlog = -softplus - 0.5
        lw_ref[rs, :] = -jnp.exp(wlog)
        a = jax.nn.sigmoid(a0_ref[...] + _split_dot(ad, aup_ref[...]))
        g_ref[rs, :] = _split_dot(jax.nn.sigmoid(gd), gup_ref[...]).astype(g_ref.dtype)
        kk = k * kk_ref[...]
        norm = jnp.sqrt(_dot((kk * kk).astype(BF16), ones))
        kk = kk / jnp.maximum(norm, 1e-12)
        k = k * (1.0 + (a - 1.0) * ka_ref[...])
        r_ref[rs, :] = r.astype(r_ref.dtype)
        rk_out_ref[rs, :] = k.astype(rk_out_ref.dtype)
        rv_ref[rs, :] = v.astype(rv_ref.dtype)
        kkn_ref[rs, :] = kk.astype(kkn_ref.dtype)
        b_ref[rs, :] = (kk * a).astype(b_ref.dtype)
        bonus_ref[rs, :] = (_dot((r * k * rk_ref[...]).astype(BF16), ones) * v).astype(bonus_ref.dtype)


def _inproj(x2, nw, wq, wr, ctab, satab, sbtab, mu_p, w0, a0, k_k, k_a, rk, wup, aup, gup, ones, seq):
    t, d = x2.shape
    tm = min(ROW_TILE, seq)
    w = RWKV_WIDTH
    row = lambda i: (i, 0)
    fixed = lambda i: (0, 0)
    vecw = pl.BlockSpec((1, w), fixed)
    lora = pl.BlockSpec((LORA_PAD, w), fixed)
    attn_out = pl.BlockSpec((tm, DIFF_WIDTH), row)
    feat_out = pl.BlockSpec((tm, w), row)
    feat = lambda dt: jax.ShapeDtypeStruct((t, w), dt)
    return pl.pallas_call(
        functools.partial(_inproj_kernel, tiles_per_seq=seq // tm),
        grid=(t // tm,),
        in_specs=[
            pl.BlockSpec((tm, d), row),
            pl.BlockSpec((1, d), fixed),
            pl.BlockSpec(wq.shape, fixed),
            pl.BlockSpec(wr.shape, fixed),
            pl.BlockSpec((tm, LANES), row),
            pl.BlockSpec((tm, LANES), row),
            pl.BlockSpec((tm, LANES), row),
            pl.BlockSpec((1, ZR_COLS), fixed),
            vecw, vecw, vecw, vecw, vecw, lora, lora, lora,
            pl.BlockSpec((w, w), fixed),
        ],
        out_specs=[attn_out] * 3 + [feat_out] * 8,
        out_shape=[jax.ShapeDtypeStruct((t, DIFF_WIDTH), BF16)] * 3
        + [feat(BF16), feat(F32), feat(BF16), feat(BF16), feat(BF16), feat(BF16), feat(BF16), feat(BF16)],
        scratch_shapes=[pltpu.VMEM((SUBLANES, ZR_COLS), F32)],
        compiler_params=_cparams(("arbitrary",)),
        name="inproj",
    )(x2, nw, wq, wr, ctab, satab, sbtab, mu_p, w0, a0, k_k, k_a, rk, wup, aup, gup, ones)


def _attn_kernel(q_ref, k_ref, v_ref, lq1_ref, lk1_ref, lq2_ref, lk2_ref, sw_ref, o_ref,
                 *, tile, lambda_init):
    lane = lax.broadcasted_iota(I32, (1, LANES), 1)
    first = lane < DIFF_HEAD_DIM
    neg = -1e30
    lam = (jnp.exp(jnp.sum(lq1_ref[...] * lk1_ref[...], axis=-1, keepdims=True))
           - jnp.exp(jnp.sum(lq2_ref[...] * lk2_ref[...], axis=-1, keepdims=True)) + lambda_init)
    r = lax.broadcasted_iota(I32, (tile, tile), 0)
    c = lax.broadcasted_iota(I32, (tile, tile), 1)
    keep = c <= r
    for i in range(q_ref.shape[1] // tile):
        q = q_ref[0, i * tile:(i + 1) * tile, :]
        zero = jnp.zeros_like(q)
        n = (i + 1) * tile
        kb = k_ref[0, 0:n, :]
        vb = v_ref[0, 0:n, :]
        outs = []
        for qm in (jnp.where(first, q, zero), jnp.where(first, zero, q)):
            s = _nt(qm, kb)
            diag = jnp.where(keep, s[:, i * tile:], neg)
            s = diag if i == 0 else jnp.concatenate([s[:, :i * tile], diag], axis=1)
            p = jnp.exp(s - jnp.max(s, axis=-1, keepdims=True))
            outs.append(_dot(p.astype(BF16), vb) / jnp.sum(p, axis=-1, keepdims=True))
        o = outs[0] - lam * outs[1]
        ms = jnp.mean(o * o, axis=-1, keepdims=True)
        o = o * lax.rsqrt(ms + NORM_EPS) * sw_ref[...] * (1.0 - lambda_init)
        o_ref[0, i * tile:(i + 1) * tile, :] = o.astype(o_ref.dtype)


def _attn(q3, k3, v3, lq1, lk1, lq2, lk2, sw, lambda_init):
    b, s, _ = q3.shape
    tile = min(ATTN_TILE, s)
    spec = pl.BlockSpec((1, s, LANES), lambda bi, h: (bi, 0, h))
    vec = lambda n: pl.BlockSpec((1, n), lambda bi, h: (0, 0))
    return pl.pallas_call(
        functools.partial(_attn_kernel, tile=tile, lambda_init=lambda_init),
        grid=(b, DIFF_HEADS),
        in_specs=[spec, spec, spec, vec(DIFF_HEAD_DIM), vec(DIFF_HEAD_DIM), vec(DIFF_HEAD_DIM),
                  vec(DIFF_HEAD_DIM), vec(DIFF_V_DIM)],
        out_specs=spec,
        out_shape=jax.ShapeDtypeStruct((b, s, DIFF_WIDTH), BF16),
        compiler_params=_cparams(("parallel", "parallel")),
        name="attn",
    )(q3, k3, v3, lq1, lk1, lq2, lk2, sw)


def _scan_kernel(r_ref, lw_ref, k_ref, v_ref, kk_ref, b_ref, y_ref, state_ref,
                 wr_s, ut_s, utt_s, arb_s, pv_s, bh_s, vk_s, wt_s, *, n_chunks):
    L = CHUNK
    G = GROUP
    n_groups = RWKV_WIDTH // G
    n_seqs = r_ref.shape[0]

    @pl.when(pl.program_id(1) == 0)
    def _():
        state_ref[...] = jnp.zeros_like(state_ref)

    row = lax.broadcasted_iota(I32, (L, G), 0)
    colr = lax.broadcasted_iota(I32, (L, G), 1) & (L - 1)
    strict = (colr < row).astype(F32)
    incl = (colr <= row).astype(F32)
    eye = (colr == row).astype(F32)
    eye_l = (lax.broadcasted_iota(I32, (L, L), 0) == lax.broadcasted_iota(I32, (L, L), 1)).astype(BF16)
    br = lax.broadcasted_iota(I32, (G, G), 0) >> 6
    bc = lax.broadcasted_iota(I32, (G, G), 1) >> 6
    block = (br == bc).astype(F32)
    block_bf = block.astype(BF16)
    rows1 = lax.broadcasted_iota(I32, (L, 1), 0)

    def stack4(x):
        xb = x.astype(BF16)
        return jnp.concatenate([xb, xb, xb, xb], axis=0) * block_bf

    def cat(a, b):
        return jnp.concatenate([a, b], axis=0).astype(BF16)

    def precompute(it, carry):
        chains = [(bb, cl, g) for bb in range(n_seqs) for cl in range(SCAN_BATCH) for g in range(n_groups)]
        each = lambda f, *lists: [f(*args) for args in zip(*lists)]

        def load(ref):
            out = []
            for bb, cl, g in chains:
                base = pl.multiple_of((it * SCAN_BATCH + cl) * L, L)
                out.append(ref[bb, pl.ds(base, L), g * G:(g + 1) * G].astype(F32))
            return out

        r, lw, k, v, kk, b = load(r_ref), load(lw_ref), load(k_ref), load(v_ref), load(kk_ref), load(b_ref)

        def cumsum(x):
            sh = 1
            while sh < L:
                x = x + jnp.where(rows1 >= sh, pltpu.roll(x, sh, 0), 0.0)
                sh *= 2
            return x

        cs = each(cumsum, lw)
        tot = each(lambda c: c[L - 1:L, :], cs)
        a_hat = each(lambda kk_, c, l: -kk_ * jnp.exp(c - l), kk, cs, lw)
        r_hat = each(lambda r_, c: r_ * jnp.exp(c), r, cs)
        w_inv = each(lambda c: jnp.exp(-c), cs)
        w_end = each(lambda t_, c: jnp.exp(t_ - c), tot, cs)
        lhs = each(cat, a_hat, r_hat)
        ab = each(lambda l_, b_, wi: _nt(l_, stack4(b_ * wi)), lhs, b, w_inv)
        ak = each(lambda l_, k_, wi: _nt(l_, stack4(k_ * wi)), lhs, k, w_inv)
        a_ab = each(lambda x: x[:L] * strict, ab)
        a_rb = each(lambda x: (x[L:] * incl).astype(BF16), ab)
        a_k = each(lambda x: cat(x[:L] * strict, x[L:] * incl), ak)
        t_mat = each(lambda a: eye + a, a_ab)
        p_mat = each(lambda a: _dot(a.astype(BF16), stack4(a)), a_ab)
        for _ in range(4):
            tp = each(lambda t_, p_: _dot(cat(t_, p_), stack4(p_)), t_mat, p_mat)
            t_mat = each(lambda t_, x: t_ + x[:L], t_mat, tp)
            p_mat = each(lambda x: x[L:], tp)
        t_bf = each(lambda t_, p_: (t_ + _dot(t_.astype(BF16), stack4(p_))).astype(BF16), t_mat, p_mat)
        av = each(lambda a, v_: _dot(a, stack4(v_)), a_k, v)
        w_til = each(lambda t_, a: _dot(t_, stack4(a)), t_bf, a_hat)
        u_til = each(lambda t_, x: _dot(t_, stack4(x[:L])), t_bf, av)
        u_til_t = each(lambda u_: _tn(u_.astype(BF16), eye_l), u_til)
        vk = each(lambda v_, k_, we: _tn(v_.astype(BF16), (k_ * we).astype(BF16)) * block, v, k, w_end)
        for n, (bb, cl, g) in enumerate(chains):
            slot = ((it * SCAN_BATCH + cl) * n_seqs + bb) * n_groups + g
            wr_s[slot] = cat(w_til[n], r_hat[n])
            ut_s[slot] = u_til[n]
            utt_s[slot] = u_til_t[n]
            arb_s[slot] = a_rb[n]
            pv_s[slot] = av[n][L:]
            bh_s[slot] = (b[n] * w_end[n]).astype(BF16)
            vk_s[slot] = vk[n]
            wt_s[slot] = jnp.broadcast_to(jnp.exp(tot[n]), (SUBLANES, G))
        return carry

    lax.fori_loop(0, n_chunks // SCAN_BATCH, precompute, 0)

    def recur(c, carry):
        gs = range(n_seqs * n_groups)
        slots = [c * n_seqs * n_groups + g for g in gs]
        s0 = [state_ref[g] for g in gs]
        s0b = [s.astype(BF16) for s in s0]
        wr = [wr_s[sl] for sl in slots]
        u_t = [_nt(s0b[g], wr[g][:L]) + utt_s[slots[g]] for g in gs]
        ub = [_dot(u_t[g].astype(BF16), bh_s[slots[g]]) for g in gs]
        for g in gs:
            state_ref[g] = s0[g] * wt_s[slots[g]][0:1, :] + ub[g] * block + vk_s[slots[g]]
        uy = [_nt(wr[g], s0b[g]) for g in gs]
        u = [uy[g][:L] + ut_s[slots[g]] for g in gs]
        base = pl.multiple_of(c * L, L)
        for g in gs:
            y = uy[g][L:] + _dot(arb_s[slots[g]], stack4(u[g])) + pv_s[slots[g]]
            lanes = slice((g % n_groups) * G, (g % n_groups + 1) * G)
            y_ref[g // n_groups, pl.ds(base, L), lanes] = y
        return carry

    lax.fori_loop(0, n_chunks, recur, 0)


def _scan(r3, lw3, k3, v3, kk3, b3):
    bsz, s, w = r3.shape
    ts = min(SCAN_TILE, s)
    n_chunks = ts // CHUNK
    nb = SCAN_SEQS if bsz % SCAN_SEQS == 0 else 1
    slots = nb * n_chunks * (w // GROUP)
    L, G = CHUNK, GROUP
    spec = pl.BlockSpec((nb, ts, w), lambda bi, i: (bi, i, 0))
    return pl.pallas_call(
        functools.partial(_scan_kernel, n_chunks=n_chunks),
        grid=(bsz // nb, s // ts),
        in_specs=[spec] * 6,
        out_specs=spec,
        out_shape=jax.ShapeDtypeStruct((bsz, s, w), F32),
        scratch_shapes=[pltpu.VMEM((nb * (w // GROUP), G, G), F32),
                        pltpu.VMEM((slots, 2 * L, G), BF16),
                        pltpu.VMEM((slots, L, G), F32),
                        pltpu.VMEM((slots, G, L), F32),
                        pltpu.VMEM((slots, L, G), BF16),
                        pltpu.VMEM((slots, L, G), F32),
                        pltpu.VMEM((slots, L, G), BF16),
                        pltpu.VMEM((slots, G, G), F32),
                        pltpu.VMEM((slots, SUBLANES, G), F32)],
        compiler_params=_cparams(("parallel", "arbitrary")),
        name="rwkv_scan",
    )(r3, lw3, k3, v3, kk3, b3)


def _mix_kernel(od_ref, y_ref, g_ref, bonus_ref, x_ref, lnw_ref, lnb_ref, ones_ref, wo_ref,
                fw_ref, rw_ref, rb_ref,
                h1_ref, xn_ref, sel_ref, idx_ref, gate_ref, cnt_ref):
    ones = ones_ref[...]
    y = y_ref[...]
    inv_n = 1.0 / RWKV_HEAD
    mean = _split_dot(y, ones) * inv_n
    d = y - mean
    var = _dot((d * d).astype(BF16), ones) * inv_n
    yn = d * lax.rsqrt(var + RWKV_GN_EPS) * lnw_ref[...] + lnb_ref[...]
    orw = ((yn + bonus_ref[...]) * g_ref[...]).astype(BF16)
    h1 = (x_ref[...] + _dot(od_ref[...], wo_ref[0:DIFF_WIDTH, :])
          + _dot(orw, wo_ref[DIFF_WIDTH:DIFF_WIDTH + RWKV_WIDTH, :]))
    h1_ref[...] = h1
    ms = jnp.mean(h1 * h1, axis=-1, keepdims=True)
    xn = h1 * lax.rsqrt(ms + NORM_EPS) * fw_ref[...]
    xn_ref[...] = xn.astype(xn_ref.dtype)
    logits = _split3_dot(xn, rw_ref[...]) + rb_ref[...]
    tm = logits.shape[0]
    lane = lax.broadcasted_iota(I32, (tm, LANES), 1).astype(F32)
    work = logits
    sel = jnp.zeros((tm, LANES), F32)
    idx_l = jnp.zeros((tm, LANES), F32)
    val_l = jnp.zeros((tm, LANES), F32)
    top = None
    for kslot in range(TOP_K):
        m = jnp.max(work, axis=-1, keepdims=True)
        pick = jnp.min(jnp.where(work == m, lane, float(LANES)), axis=-1, keepdims=True)
        hit = lane == pick
        sel = jnp.where(hit, 1.0, sel)
        idx_l = jnp.where(lane == kslot, pick, idx_l)
        if top is None:
            top = m
        val_l = jnp.where(lane == kslot, jnp.exp(m - top), val_l)
        work = jnp.where(hit, -jnp.inf, work)
    sel_ref[...] = sel
    idx_ref[...] = idx_l.astype(I32)
    gate_ref[...] = val_l / jnp.sum(val_l, axis=-1, keepdims=True)
    for part in range(tm // MOE_TILE):
        count = jnp.sum(sel[part * MOE_TILE:(part + 1) * MOE_TILE], axis=0, keepdims=True)
        cnt_ref[part * SUBLANES:(part + 1) * SUBLANES, :] = jnp.broadcast_to(count, (SUBLANES, LANES)).astype(I32)


def _mix(od, y, g, bonus, x2, lnw, lnb, ones, wo, fw, rw, rb):
    t, d = x2.shape
    tm = min(ROW_TILE, t)
    w = RWKV_WIDTH
    row = lambda i: (i, 0)
    fixed = lambda i: (0, 0)
    rs = lambda n: pl.BlockSpec((tm, n), row)
    return pl.pallas_call(
        _mix_kernel,
        grid=(t // tm,),
        in_specs=[rs(DIFF_WIDTH), rs(w), rs(w), rs(w), rs(d),
                  pl.BlockSpec((1, w), fixed), pl.BlockSpec((1, w), fixed),
                  pl.BlockSpec((w, w), fixed), pl.BlockSpec(wo.shape, fixed),
                  pl.BlockSpec((1, d), fixed), pl.BlockSpec(rw.shape, fixed),
                  pl.BlockSpec((1, LANES), fixed)],
        out_specs=[rs(d), rs(d), rs(LANES), rs(LANES), rs(LANES),
                   pl.BlockSpec((tm // MOE_TILE * SUBLANES, LANES), row)],
        out_shape=[jax.ShapeDtypeStruct((t, d), F32), jax.ShapeDtypeStruct((t, d), BF16),
                   jax.ShapeDtypeStruct((t, LANES), F32), jax.ShapeDtypeStruct((t, LANES), I32),
                   jax.ShapeDtypeStruct((t, LANES), F32),
                   jax.ShapeDtypeStruct((t // MOE_TILE * SUBLANES, LANES), I32)],
        compiler_params=_cparams(("parallel",)),
        name="mix_router",
    )(od, y, g, bonus, x2, lnw, lnb, ones, wo, fw, rw, rb)


def _slot_positions(sel, idx_l, lstart):
    tt = sel.shape[0]
    r = lax.broadcasted_iota(I32, (tt, tt), 0)
    c = lax.broadcasted_iota(I32, (tt, tt), 1)
    lower = (c < r).astype(BF16)
    where_to = _dot(lower, sel.astype(BF16)) + lstart
    lane = lax.broadcasted_iota(I32, (tt, LANES), 1).astype(F32)
    idx = idx_l.astype(F32)
    pos = jnp.full((tt, LANES), -1.0, F32)
    for kslot in range(TOP_K):
        e = jnp.sum(jnp.where(lane == kslot, idx, 0.0), axis=-1, keepdims=True)
        p = jnp.sum(jnp.where(lane == e, where_to, 0.0), axis=-1, keepdims=True)
        pos = jnp.where(lane == kslot, p, pos)
    return pos


def _piece(ref, q):
    return ref.at[pl.ds(pl.multiple_of(q * SUBLANES, SUBLANES), SUBLANES), :]


def _drain(count, src_ref, dst_ref, sem, max_pieces):
    b = 0
    while (1 << b) <= max_pieces:
        rows = SUBLANES << b

        @pl.when((count >> b) & 1 == 1)
        def _():
            pltpu.make_async_copy(src_ref.at[pl.ds(0, rows), :], dst_ref.at[pl.ds(0, rows), :], sem).wait()

        b += 1


def _dispatch_kernel(np_ref, fill_ref, nfill_ref, gdst_ref, sel_ref, idx_ref, lstart_ref, xn_ref,
                     pos_ref, buf_ref, xs_ref, zero_ref, sem, fill_sem, *, n_slots):
    i = pl.program_id(0)
    last = pl.num_programs(0) - 1
    slot = i % 2

    def copy(q, sl):
        return pltpu.make_async_copy(_piece(xs_ref.at[sl], q), _piece(buf_ref, gdst_ref[0, 0, q]), sem.at[sl])

    def drain(count, sl):
        _drain(count, xs_ref.at[sl], buf_ref, sem.at[sl], n_slots // SUBLANES)

    @pl.when(i >= 2)
    def _():
        drain(np_ref[jnp.maximum(i - 2, 0)], slot)

    pos = _slot_positions(sel_ref[...], idx_ref[...], lstart_ref[0:1, :].astype(F32))
    pos_ref[...] = pos
    tt = pos.shape[0]
    pos_t = pos.T.astype(I32)
    s_iota = lax.broadcasted_iota(I32, (n_slots, tt), 0)
    perm = jnp.zeros((n_slots, tt), F32)
    for kslot in range(TOP_K):
        perm = perm + jnp.where(s_iota == pos_t[kslot:kslot + 1, :], 1.0, 0.0)
    xs_ref[slot] = _pack_halves(_dot(perm.astype(BF16), xn_ref[...]))
    lax.fori_loop(0, np_ref[i], lambda q, c: (copy(q, slot).start(), c)[1], 0)

    @pl.when(i == 0)
    def _():
        zero_ref[...] = jnp.zeros_like(zero_ref)

        def fill(j):
            return pltpu.make_async_copy(zero_ref, _piece(buf_ref, fill_ref[j]), fill_sem)

        lax.fori_loop(0, nfill_ref[0], lambda j, c: (fill(j).start(), c)[1], 0)
        lax.fori_loop(0, nfill_ref[0], lambda j, c: (fill(j).wait(), c)[1], 0)

    @pl.when(i == last)
    def _():
        @pl.when(i >= 1)
        def _():
            drain(np_ref[jnp.maximum(i - 1, 0)], 1 - slot)

        drain(np_ref[i], slot)


def _dispatch(npieces, fill, nfill, gdst3, sel, idx_l, lstart8, xn, n_rows):
    t, d = xn.shape
    n_tiles, _, n_pieces = gdst3.shape
    tt = t // n_tiles
    n_slots = n_pieces * SUBLANES
    row = lambda i, *_: (i, 0)
    grid_spec = pltpu.PrefetchScalarGridSpec(
        num_scalar_prefetch=3,
        grid=(n_tiles,),
        in_specs=[pl.BlockSpec((1, 1, n_pieces), lambda i, *_: (i, 0, 0), memory_space=pltpu.SMEM),
                  pl.BlockSpec((tt, LANES), row),
                  pl.BlockSpec((tt, LANES), row),
                  pl.BlockSpec((SUBLANES, LANES), row),
                  pl.BlockSpec((tt, d), row)],
        out_specs=[pl.BlockSpec((tt, LANES), row), pl.BlockSpec(memory_space=pl.ANY)],
        scratch_shapes=[pltpu.VMEM((2, n_slots, d // 2), U32), pltpu.VMEM((SUBLANES, d // 2), U32),
                        pltpu.SemaphoreType.DMA((2,)), pltpu.SemaphoreType.DMA(())],
    )
    return pl.pallas_call(
        functools.partial(_dispatch_kernel, n_slots=n_slots),
        grid_spec=grid_spec,
        out_shape=[jax.ShapeDtypeStruct((t, LANES), F32), jax.ShapeDtypeStruct((n_rows, d // 2), U32)],
        compiler_params=_cparams(("arbitrary",)),
        name="moe_dispatch",
    )(npieces, fill, nfill, gdst3, sel, idx_l, lstart8, xn)


def _expert_kernel(be_ref, first_ref, ord_ref, next_ref, nused_ref, x_ref, w1_hbm, b1_ref, w2_hbm, b2_ref,
                   y_ref, w1f_ref, w2f_ref, w1b_ref, w2i_ref, w2b_ref, sem):
    i = pl.program_id(0)
    ff = w2f_ref.shape[1]
    used = i < nused_ref[0]
    new_expert = jnp.logical_and(used, first_ref[i] == 1)

    def fetch(expert, slot):
        return (pltpu.make_async_copy(w1_hbm.at[expert], w1f_ref.at[slot], sem.at[0, slot]),
                pltpu.make_async_copy(w2_hbm.at[expert], w2f_ref.at[slot], sem.at[1, slot]))

    @pl.when(i == 0)
    def _():
        for cp in fetch(be_ref[0], 0):
            cp.start()

    @pl.when(jnp.logical_and(new_expert, next_ref[i] >= 0))
    def _():
        for cp in fetch(next_ref[i], 1 - (ord_ref[i] & 1)):
            cp.start()

    @pl.when(new_expert)
    def _():
        slot = ord_ref[i] & 1
        for cp in fetch(be_ref[i], slot):
            cp.wait()
        rows = w1f_ref.shape[1] // CAST_CHUNKS

        def cast1(c, carry):
            r0 = pl.multiple_of(c * rows, rows)
            w1b_ref[pl.ds(r0, rows), :] = w1f_ref[slot, pl.ds(r0, rows), :].astype(BF16)
            return carry

        lax.fori_loop(0, CAST_CHUNKS, cast1, 0)
        for g in range(w2f_ref.shape[2] // LANES):
            cols = slice(g * LANES, (g + 1) * LANES)
            w2i_ref[pl.ds(0, ff // 2, stride=2), :] = w2f_ref[slot, 0:ff // 2, cols]
            w2i_ref[pl.ds(1, ff // 2, stride=2), :] = w2f_ref[slot, ff // 2:ff, cols]
            w2b_ref[:, cols] = w2i_ref[...].astype(BF16)

    @pl.when(used)
    def _():
        x = _unpack_halves(x_ref[...])
        hid = _dot(x, w1b_ref[...]) + b1_ref[0]
        even = (lax.broadcasted_iota(I32, (1, LANES), 1) & 1) == 0

        def act_even(g):
            hg = hid[:, g * LANES:(g + 1) * LANES]
            glu = jnp.minimum(hg, SWIGLU_LIMIT)
            lin = jnp.clip(hg, -SWIGLU_LIMIT, SWIGLU_LIMIT) + 1.0
            return glu * jax.nn.sigmoid(SWIGLU_ALPHA * glu) * pltpu.roll(lin, LANES - 1, 1)

        half = ff // LANES
        act = jnp.concatenate(
            [jnp.where(even, act_even(g), pltpu.roll(act_even(g + half), 1, 1)) for g in range(half)], axis=1)
        y = _dot(act.astype(BF16), w2b_ref[...]) + b2_ref[0]
        y_ref[...] = _pack_halves(y.astype(BF16).astype(F32))

    @pl.when(jnp.logical_not(used))
    def _():
        y_ref[...] = jnp.zeros_like(y_ref)


def _experts(block_e, rows_per_expert, n_used, xbuf, w1, b1, w2, b2):
    n_rows = xbuf.shape[0]
    bm = EXPERT_BLOCK
    n_blocks = n_rows // bm
    e, d, ff2 = w1.shape
    ff = ff2 // 2
    has = rows_per_expert > 0
    ids = jnp.arange(e, dtype=I32)
    later = (ids[None, :] > ids[:, None]) & has[None, :]
    next_used = jnp.where(jnp.any(later, axis=1), jnp.argmax(later, axis=1), -1).astype(I32)
    ordinal = (jnp.cumsum(has.astype(I32)) - 1).astype(I32)
    first = jnp.concatenate([jnp.ones((1,), I32), (block_e[1:] != block_e[:-1]).astype(I32)])
    of_block = block_e[:, None] == ids[None, :]
    order = jnp.sum(jnp.where(of_block, ordinal[None, :], 0), axis=1).astype(I32)
    next_e = jnp.sum(jnp.where(of_block, next_used[None, :], 0), axis=1).astype(I32)
    blk =lambda i, be, fi, od, ne, nu: (jnp.maximum(jnp.minimum(i, nu[0] - 1), 0), 0)
    ex3 = lambda i, be, fi, od, ne, nu: (be[i], 0, 0)
    grid_spec = pltpu.PrefetchScalarGridSpec(
        num_scalar_prefetch=5,
        grid=(n_blocks,),
        in_specs=[pl.BlockSpec((bm, d // 2), blk),
                  pl.BlockSpec(memory_space=pl.ANY),
                  pl.BlockSpec((1, 1, ff2), ex3),
                  pl.BlockSpec(memory_space=pl.ANY),
                  pl.BlockSpec((1, 1, d), ex3)],
        out_specs=pl.BlockSpec((bm, d // 2), lambda i, be, fi, od, ne, nu: (i, 0)),
        scratch_shapes=[pltpu.VMEM((2, d, ff2), F32), pltpu.VMEM((2, ff, d), F32),
                        pltpu.VMEM((d, ff2), BF16), pltpu.VMEM((ff, LANES), F32), pltpu.VMEM((ff, d), BF16),
                        pltpu.SemaphoreType.DMA((2, 2))],
    )
    return pl.pallas_call(
        _expert_kernel,
        grid_spec=grid_spec,
        out_shape=jax.ShapeDtypeStruct((n_rows, d // 2), U32),
        compiler_params=_cparams(("arbitrary",)),
        name="moe_experts",
    )(block_e, first, order, next_e, n_used, xbuf, w1, b1, w2, b2)


def _combine_kernel(np_ref, gdst_ref, gnext_ref, pos_ref, gate_ref, h1_ref, fw_ref, ybuf_ref, o_ref, ys_ref, sem,
                    *, n_slots):
    i = pl.program_id(0)
    last = pl.num_programs(0) - 1
    slot = i % 2

    def fetch(table_ref, tile, sl):
        count = np_ref[tile]

        def copy(q):
            return pltpu.make_async_copy(_piece(ybuf_ref, table_ref[0, 0, q]), _piece(ys_ref.at[sl], q), sem.at[sl])

        lax.fori_loop(0, count, lambda q, c: (copy(q).start(), c)[1], 0)

        def zero(q, carry):
            _piece(ys_ref.at[sl], q)[...] = jnp.zeros((SUBLANES, ys_ref.shape[2]), ys_ref.dtype)
            return carry

        lax.fori_loop(count, n_slots // SUBLANES, zero, 0)

    @pl.when(i == 0)
    def _():
        fetch(gdst_ref, i, slot)

    @pl.when(i < last)
    def _():
        fetch(gnext_ref, jnp.minimum(i + 1, last), 1 - slot)

    n = np_ref[i]
    pos = pos_ref[...].astype(I32)
    gate = gate_ref[...]
    tt = pos.shape[0]
    s_iota = lax.broadcasted_iota(I32, (tt, n_slots), 1)
    weight = jnp.zeros((tt, n_slots), F32)
    for kslot in range(TOP_K):
        weight = weight + jnp.where(s_iota == pos[:, kslot:kslot + 1], gate[:, kslot:kslot + 1], 0.0)
    _drain(n, ybuf_ref, ys_ref.at[slot], sem.at[slot], n_slots // SUBLANES)
    h = h1_ref[...] + _dot(weight.astype(BF16), _unpack_halves(ys_ref[slot]))
    ms = jnp.mean(h * h, axis=-1, keepdims=True)
    o_ref[...] = h * lax.rsqrt(ms + NORM_EPS) * fw_ref[...]


def _combine(npieces, gdst3, pos_l, gate_l, h1, fw, ybuf):
    t, d = h1.shape
    n_tiles, _, n_pieces = gdst3.shape
    tt = t // n_tiles
    n_slots = n_pieces * SUBLANES
    row = lambda i, *_: (i, 0)
    grid_spec = pltpu.PrefetchScalarGridSpec(
        num_scalar_prefetch=1,
        grid=(n_tiles,),
        in_specs=[pl.BlockSpec((1, 1, n_pieces), lambda i, *_: (i, 0, 0), memory_space=pltpu.SMEM),
                  pl.BlockSpec((1, 1, n_pieces), lambda i, *_: (jnp.minimum(i + 1, n_tiles - 1), 0, 0),
                               memory_space=pltpu.SMEM),
                  pl.BlockSpec((tt, LANES), row),
                  pl.BlockSpec((tt, LANES), row),
                  pl.BlockSpec((tt, d), row),
                  pl.BlockSpec((1, d), lambda i, *_: (0, 0)),
                  pl.BlockSpec(memory_space=pl.ANY)],
        out_specs=pl.BlockSpec((tt, d), row),
        scratch_shapes=[pltpu.VMEM((2, n_slots, d // 2), U32), pltpu.SemaphoreType.DMA((2,))],
    )
    return pl.pallas_call(
        functools.partial(_combine_kernel, n_slots=n_slots),
        grid_spec=grid_spec,
        out_shape=jax.ShapeDtypeStruct((t, d), F32),
        compiler_params=_cparams(("arbitrary",)),
        name="moe_combine",
    )(npieces, gdst3, gdst3, pos_l, gate_l, h1, fw, ybuf)


def _rotary_tables(positions):
    half = ROT_DIM // 2
    inv_freq = ROPE_THETA ** (-jnp.arange(0, ROT_DIM, 2, dtype=F32) / ROT_DIM)
    ang = positions.astype(F32).reshape(-1, 1) * inv_freq
    cos = jnp.tile(jnp.cos(ang), (1, LANES // half))
    sin = jnp.tile(jnp.sin(ang), (1, LANES // half))
    dim = np.arange(LANES)[None, :] % DIFF_HEAD_DIM
    ctab = jnp.where(dim < ROT_DIM, cos, 1.0)
    satab = jnp.where(dim < half, -sin, 0.0)
    sbtab = jnp.where((dim >= half) & (dim < ROT_DIM), sin, 0.0)
    return ctab, satab, sbtab


def _pad_rows(a, rows):
    return jnp.concatenate([a, jnp.zeros((rows - a.shape[0],) + a.shape[1:], a.dtype)], axis=0)


def _layer(h, l, tabs, attn_norm_w, w_in, diff_lambda_q1, diff_lambda_k1, diff_lambda_q2, diff_lambda_k2,
           diff_subln_w, rwkv_mu, rwkv_w0, rwkv_w_up, rwkv_a0, rwkv_a_up, rwkv_g_up, rwkv_k_k, rwkv_k_a,
           rwkv_r_k, rwkv_ln_w, rwkv_ln_b, w_out, ffn_norm_w, router_w, router_b, exp_w1, exp_b1,
           exp_w2, exp_b2, final_w):
    bsz, seq, d = h.shape
    t = bsz * seq
    w = RWKV_WIDTH
    lambda_init = 0.8 - 0.6 * math.exp(-0.3 * l)
    x2 = h.reshape(t, d)
    row1 = lambda a: a.reshape(1, -1).astype(F32)

    wi = w_in[l]
    qkv_cols = 3 * DIFF_WIDTH
    wq = wi[:, :qkv_cols].astype(BF16)
    o = qkv_cols + 3 * w
    zcol = lambda n: jnp.zeros((d, n), wi.dtype)
    wr = jnp.concatenate([
        wi[:, qkv_cols:o],
        wi[:, o:o + DECAY_LORA], zcol(LORA_PAD - DECAY_LORA),
        wi[:, o + DECAY_LORA:o + DECAY_LORA + AAA_LORA], zcol(LORA_PAD - AAA_LORA),
        wi[:, o + DECAY_LORA + AAA_LORA:], zcol(LORA_PAD - GATE_LORA)], axis=1).astype(BF16)
    mu = rwkv_mu[l]
    zv = lambda n: jnp.zeros((n,), mu.dtype)
    mu_p = jnp.concatenate([
        mu[:3 * w],
        mu[3 * w:3 * w + DECAY_LORA], zv(LORA_PAD - DECAY_LORA),
        mu[3 * w + DECAY_LORA:3 * w + DECAY_LORA + AAA_LORA], zv(LORA_PAD - AAA_LORA),
        mu[3 * w + DECAY_LORA + AAA_LORA:], zv(LORA_PAD - GATE_LORA)]).reshape(1, -1)

    head = np.arange(w) // RWKV_HEAD
    ones = jnp.asarray(head[:, None] == head[None, :], BF16)
    q, k, v, r, lw, kmod, vv, kk, bb, g, bonus = _inproj(
        x2, row1(attn_norm_w[l]), wq, wr, *tabs,
        mu_p, row1(rwkv_w0[l]), row1(rwkv_a0[l]), row1(rwkv_k_k[l]), row1(rwkv_k_a[l]),
        row1(rwkv_r_k[l]), _pad_rows(rwkv_w_up[l].astype(BF16), LORA_PAD),
        _pad_rows(rwkv_a_up[l].astype(BF16), LORA_PAD), _pad_rows(rwkv_g_up[l].astype(BF16), LORA_PAD),
        ones, seq)

    od = _attn(q.reshape(bsz, seq, -1), k.reshape(bsz, seq, -1), v.reshape(bsz, seq, -1),
               row1(diff_lambda_q1[l]), row1(diff_lambda_k1[l]), row1(diff_lambda_q2[l]),
               row1(diff_lambda_k2[l]), row1(diff_subln_w[l]), lambda_init)

    s3 = lambda a: a.reshape(bsz, seq, w)
    y = _scan(s3(r), s3(lw), s3(kmod), s3(vv), s3(kk), s3(bb)).reshape(t, w)

    n_e = router_w.shape[-1]
    rw = jnp.concatenate([router_w[l].astype(F32), jnp.zeros((d, LANES - n_e), F32)], axis=1)
    rw_hi = rw.astype(BF16)
    rw = jnp.concatenate([rw_hi, (rw - rw_hi.astype(F32)).astype(BF16)], axis=1)
    rb = jnp.concatenate([router_b[l].astype(F32), jnp.full((LANES - n_e,), -1e30, F32)]).reshape(1, -1)
    h1, xn, sel, idx_l, gate_l, cnt8 = _mix(
        od.reshape(t, -1), y, g, bonus, x2, row1(rwkv_ln_w[l]), row1(rwkv_ln_b[l]), ones,
        w_out[l].astype(BF16), row1(ffn_norm_w[l]), rw, rb)

    bm = EXPERT_BLOCK
    pc = SUBLANES
    tm = MOE_TILE
    n_tiles = t // tm
    cnt = cnt8.reshape(n_tiles, SUBLANES, LANES)[:, 0, :n_e]
    seg = (cnt + pc - 1) // pc * pc
    lend = jnp.cumsum(seg, axis=1)
    lstart = lend - seg
    rows_e = jnp.sum(seg, axis=0)
    padded = (rows_e + bm - 1) // bm * bm
    pad_ends = jnp.cumsum(padded)
    gstart = (pad_ends - padded)[None, :] + jnp.cumsum(seg, axis=0) - seg
    n_slots = tm * TOP_K + n_e * pc
    n_pieces = n_slots // pc
    n_blocks = -(-(t * TOP_K + n_tiles * n_e * (pc - 1) + n_e * (bm - pc)) // bm)
    n_rows = n_blocks * bm
    piece_row = jnp.arange(n_pieces, dtype=I32) * pc
    piece_e = jnp.minimum(jnp.sum(lend[:, None, :] <= piece_row[None, :, None], axis=-1), n_e - 1)
    pick = piece_e[:, :, None] == jnp.arange(n_e, dtype=I32)[None, None, :]
    take = lambda a: jnp.sum(jnp.where(pick, a[:, None, :], 0), axis=-1)
    gdst = ((take(gstart) + piece_row[None, :] - take(lstart)) // pc).astype(I32)
    gdst3 = jnp.clip(gdst, 0, n_rows // pc - 1).reshape(n_tiles, 1, n_pieces)
    npieces = (lend[:, -1] // pc).astype(I32)
    lstart8 = jnp.zeros((n_tiles, SUBLANES, LANES), I32).at[:, :, :n_e].set(lstart[:, None, :])
    lstart8 = lstart8.reshape(n_tiles * SUBLANES, LANES)
    gap_start = jnp.concatenate([pad_ends - padded + rows_e, pad_ends[-1:]]) // pc
    gap_len = jnp.concatenate([padded - rows_e, n_rows - pad_ends[-1:]]) // pc
    gap_end = jnp.cumsum(gap_len)
    max_fill = n_e * (bm // pc - 1) + (n_rows - t * TOP_K) // pc
    j = jnp.arange(max_fill, dtype=I32)
    gap = jnp.minimum(jnp.sum(gap_end[None, :] <= j[:, None], axis=1), n_e)
    in_gap = gap[:, None] == jnp.arange(n_e + 1, dtype=I32)[None, :]
    shift = jnp.sum(jnp.where(in_gap, (gap_start - gap_end + gap_len)[None, :], 0), axis=1)
    fill = jnp.clip(j + shift, 0, n_rows // pc - 1).astype(I32)
    nfill = gap_end[-1:].astype(I32)
    first_row = jnp.arange(n_blocks, dtype=I32) * bm
    block_e = jnp.minimum(jnp.sum(pad_ends[None, :] <= first_row[:, None], axis=1), n_e - 1).astype(I32)
    n_used = (pad_ends[-1] // bm).astype(I32).reshape(1)

    pos_l, xbuf = _dispatch(npieces, fill, nfill, gdst3, sel, idx_l, lstart8, xn, n_rows)

    ff2 = exp_w1.shape[-1]
    b1 = exp_b1[l].astype(F32).reshape(n_e, 1, ff2)
    b2 = exp_b2[l].astype(F32).reshape(n_e, 1, d)
    ybuf = _experts(block_e, padded, n_used, xbuf, exp_w1[l].astype(F32), b1, exp_w2[l].astype(F32), b2)

    out = _combine(npieces, gdst3, pos_l, gate_l, h1, row1(final_w), ybuf)
    return out.reshape(bsz, seq, d)


def kernel(x, positions, attn_norm_w, w_in, diff_lambda_q1, diff_lambda_k1, diff_lambda_q2, diff_lambda_k2, diff_subln_w, rwkv_mu, rwkv_w0, rwkv_w_up, rwkv_a0, rwkv_a_up, rwkv_g_up, rwkv_k_k, rwkv_k_a, rwkv_r_k, rwkv_ln_w, rwkv_ln_b, w_out, ffn_norm_w, router_w, router_b, exp_w1, exp_b1, exp_w2, exp_b2, final_norm_w):
    depth = w_in.shape[0]
    assert depth == 1, "the final norm is fused into the last (only) layer's combine kernel"
    tabs = _rotary_tables(positions)
    return _layer(x, 0, tabs, attn_norm_w, w_in, diff_lambda_q1, diff_lambda_k1, diff_lambda_q2,
                  diff_lambda_k2, diff_subln_w, rwkv_mu, rwkv_w0, rwkv_w_up, rwkv_a0, rwkv_a_up, rwkv_g_up,
                  rwkv_k_k, rwkv_k_a, rwkv_r_k, rwkv_ln_w, rwkv_ln_b, w_out, ffn_norm_w, router_w, router_b,
                  exp_w1, exp_b1, exp_w2, exp_b2, final_norm_w)
```

```python
import functools
import math

import jax
import jax.numpy as jnp
import numpy as np
from jax import lax
from jax.experimental import pallas as pl
from jax.experimental.pallas import tpu as pltpu

F32 = jnp.float32
BF16 = jnp.bfloat16
I32 = jnp.int32
U32 = jnp.uint32

DIFF_HEAD_DIM = 64
DIFF_V_DIM = 128
DIFF_HEADS = 4
DIFF_WIDTH = DIFF_HEADS * DIFF_V_DIM
ROT_DIM = 16
ROPE_THETA = 500000.0
RWKV_HEAD = 64
RWKV_HEADS = 8
RWKV_WIDTH = RWKV_HEAD * RWKV_HEADS
DECAY_LORA = 32
AAA_LORA = 32
GATE_LORA = 96
N_EXPERTS = 32
TOP_K = 4
SWIGLU_LIMIT = 7.0
SWIGLU_ALPHA = 1.702
NORM_EPS = 1e-5
RWKV_GN_EPS = 64e-5

LANES = 128
SUBLANES = 8
VMEM_LIMIT = 56 * 1024 * 1024

ROW_TILE = 512
SUB_TILES = 2
ATTN_TILE = 512
CHUNK = 64
SCAN_TILE = 512
SCAN_SEQS = 2
SCAN_BATCH = 4
GROUP = 4 * RWKV_HEAD
EXPERT_BLOCK = 512
MOE_TILE = 256
CAST_CHUNKS = 8
LORA_PAD = LANES
ZR_COLS = 3 * RWKV_WIDTH + 3 * LORA_PAD


def _cparams(sem, flags=None):
    return pltpu.CompilerParams(dimension_semantics=sem, vmem_limit_bytes=VMEM_LIMIT, flags=flags)


def _nt(a, b):
    return lax.dot_general(a, b, (((1,), (1,)), ((), ())), preferred_element_type=F32)


def _tn(a, b):
    return lax.dot_general(a, b, (((0,), (0,)), ((), ())), preferred_element_type=F32)


def _dot(a, b):
    return jnp.dot(a, b, preferred_element_type=F32)


def _pack_halves(x):
    n = x.shape[1] // 2
    bits = lax.bitcast_convert_type(x, U32)
    return (bits[:, :n] & jnp.uint32(0xFFFF0000)) | (bits[:, n:] >> 16)


def _unpack_halves(p):
    hi = lax.bitcast_convert_type(p & jnp.uint32(0xFFFF0000), F32)
    lo = lax.bitcast_convert_type(p << 16, F32)
    return jnp.concatenate([hi, lo], axis=1).astype(BF16)


def _split(x):
    hi = x.astype(BF16)
    return hi, (x - hi.astype(F32)).astype(BF16)


def _split_dot(x, w_bf16):
    hi, lo = _split(x)
    return _dot(hi, w_bf16) + _dot(lo, w_bf16)


def _split3_dot(x, w_parts):
    n = w_parts.shape[1] // 2
    hi, lo = _split(x)
    both = _dot(hi, w_parts)
    return both[:, :n] + (both[:, n:] + _dot(lo, w_parts[:, :n]))


def _inproj_kernel(x_ref, nw_ref, wq_ref, wr_ref, c_ref, sa_ref, sb_ref,
                   mu_ref, w0_ref, a0_ref, kk_ref, ka_ref, rk_ref, wup_ref, aup_ref, gup_ref, ones_ref,
                   q_ref, k_ref, v_ref, r_ref, lw_ref, rk_out_ref, rv_ref, kkn_ref, b_ref, g_ref, bonus_ref,
                   prev_ref, *, tiles_per_seq):
    i = pl.program_id(0)

    @pl.when(i % tiles_per_seq == 0)
    def _():
        prev_ref[...] = jnp.zeros_like(prev_ref)

    tm = x_ref.shape[0]
    sub = tm // SUB_TILES
    scale = DIFF_HEAD_DIM ** -0.5
    w = RWKV_WIDTH
    ones = ones_ref[...]
    rows = lax.broadcasted_iota(I32, (sub, 1), 0)
    prev = prev_ref[SUBLANES - 1:SUBLANES, :]
    for part in range(SUB_TILES):
        rs = slice(part * sub, (part + 1) * sub)
        x = x_ref[rs, :]
        ms = jnp.mean(x * x, axis=-1, keepdims=True)
        u = (x * lax.rsqrt(ms + NORM_EPS) * nw_ref[...]).astype(BF16)
        zq = _dot(u, wq_ref[...])
        c = c_ref[rs, :]
        sa = sa_ref[rs, :]
        sb = sb_ref[rs, :]
        for g in range(2 * DIFF_HEADS):
            zg = zq[:, g * LANES:(g + 1) * LANES]
            rot = zg * c + pltpu.roll(zg, LANES - ROT_DIM // 2, 1) * sa + pltpu.roll(zg, ROT_DIM // 2, 1) * sb
            if g < DIFF_HEADS:
                q_ref[rs, g * LANES:(g + 1) * LANES] = (rot * scale).astype(BF16)
            else:
                h = g - DIFF_HEADS
                k_ref[rs, h * LANES:(h + 1) * LANES] = rot.astype(BF16)
        v_ref[rs, :] = zq[:, 2 * DIFF_WIDTH:3 * DIFF_WIDTH].astype(BF16)

        z = _dot(u, wr_ref[...])
        shifted = jnp.where(rows == 0, prev, pltpu.roll(z, 1, 0))
        prev = z[sub - 1:sub, :]
        if part == SUB_TILES - 1:
            prev_ref[...] = z[sub - SUBLANES:sub, :]
        zf = z + mu_ref[...] * (shifted - z)
        r = zf[:, 0:w]
        k = zf[:, w:2 * w]
        v = zf[:, 2 * w:3 * w]
        wd = zf[:, 3 * w:3 * w + LORA_PAD]
        ad = zf[:, 3 * w + LORA_PAD:3 * w + 2 * LORA_PAD]
        gd = zf[:, 3 * w + 2 * LORA_PAD:3 * w + 3 * LORA_PAD]
        pre = w0_ref[...] + _split_dot(jnp.tanh(wd), wup_ref[...])
        neg = -pre
        softplus = jnp.maximum(neg, 0.0) + jnp.log(1.0 + jnp.exp(-jnp.abs(neg)))
        wlog = -softplus - 0.5
        lw_ref[rs, :] = -jnp.exp(wlog)
        a = jax.nn.sigmoid(a0_ref[...] + _split_dot(ad, aup_ref[...]))
        g_ref[rs, :] = _split_dot(jax.nn.sigmoid(gd), gup_ref[...]).astype(g_ref.dtype)
        kk = k * kk_ref[...]
        norm = jnp.sqrt(_dot((kk * kk).astype(BF16), ones))
        kk = kk / jnp.maximum(norm, 1e-12)
        k = k * (1.0 + (a - 1.0) * ka_ref[...])
        r_ref[rs, :] = r.astype(r_ref.dtype)
        rk_out_ref[rs, :] = k.astype(rk_out_ref.dtype)
        rv_ref[rs, :] = v.astype(rv_ref.dtype)
        kkn_ref[rs, :] = kk.astype(kkn_ref.dtype)
        b_ref[rs, :] = (kk * a).astype(b_ref.dtype)
        bonus_ref[rs, :] = (_dot((r * k * rk_ref[...]).astype(BF16), ones) * v).astype(bonus_ref.dtype)


def _inproj(x2, nw, wq, wr, ctab, satab, sbtab, mu_p, w0, a0, k_k, k_a, rk, wup, aup, gup, ones, seq):
    t, d = x2.shape
    tm = min(ROW_TILE, seq)
    w = RWKV_WIDTH
    row = lambda i: (i, 0)
    fixed = lambda i: (0, 0)
    vecw = pl.BlockSpec((1, w), fixed)
    lora = pl.BlockSpec((LORA_PAD, w), fixed)
    attn_out = pl.BlockSpec((tm, DIFF_WIDTH), row)
    feat_out = pl.BlockSpec((tm, w), row)
    feat = lambda dt: jax.ShapeDtypeStruct((t, w), dt)
    return pl.pallas_call(
        functools.partial(_inproj_kernel, tiles_per_seq=seq // tm),
        grid=(t // tm,),
        in_specs=[
            pl.BlockSpec((tm, d), row),
            pl.BlockSpec((1, d), fixed),
            pl.BlockSpec(wq.shape, fixed),
            pl.BlockSpec(wr.shape, fixed),
            pl.BlockSpec((tm, LANES), row),
            pl.BlockSpec((tm, LANES), row),
            pl.BlockSpec((tm, LANES), row),
            pl.BlockSpec((1, ZR_COLS), fixed),
            vecw, vecw, vecw, vecw, vecw, lora, lora, lora,
            pl.BlockSpec((w, w), fixed),
        ],
        out_specs=[attn_out] * 3 + [feat_out] * 8,
        out_shape=[jax.ShapeDtypeStruct((t, DIFF_WIDTH), BF16)] * 3
        + [feat(BF16), feat(F32), feat(BF16), feat(BF16), feat(BF16), feat(BF16), feat(BF16), feat(BF16)],
        scratch_shapes=[pltpu.VMEM((SUBLANES, ZR_COLS), F32)],
        compiler_params=_cparams(("arbitrary",)),
        name="inproj",
    )(x2, nw, wq, wr, ctab, satab, sbtab, mu_p, w0, a0, k_k, k_a, rk, wup, aup, gup, ones)


def _attn_kernel(q_ref, k_ref, v_ref, lq1_ref, lk1_ref, lq2_ref, lk2_ref, sw_ref, o_ref,
                 *, tile, lambda_init):
    lane = lax.broadcasted_iota(I32, (1, LANES), 1)
    first = lane < DIFF_HEAD_DIM
    neg = -1e30
    lam = (jnp.exp(jnp.sum(lq1_ref[...] * lk1_ref[...], axis=-1, keepdims=True))
           - jnp.exp(jnp.sum(lq2_ref[...] * lk2_ref[...], axis=-1, keepdims=True)) + lambda_init)
    r = lax.broadcasted_iota(I32, (tile, tile), 0)
    c = lax.broadcasted_iota(I32, (tile, tile), 1)
    keep = c <= r
    for i in range(q_ref.shape[1] // tile):
        q = q_ref[0, i * tile:(i + 1) * tile, :]
        zero = jnp.zeros_like(q)
        n = (i + 1) * tile
        kb = k_ref[0, 0:n, :]
        vb = v_ref[0, 0:n, :]
        outs = []
        for qm in (jnp.where(first, q, zero), jnp.where(first, zero, q)):
            s = _nt(qm, kb)
            diag = jnp.where(keep, s[:, i * tile:], neg)
            s = diag if i == 0 else jnp.concatenate([s[:, :i * tile], diag], axis=1)
            p = jnp.exp(s - jnp.max(s, axis=-1, keepdims=True))
            outs.append(_dot(p.astype(BF16), vb) / jnp.sum(p, axis=-1, keepdims=True))
        o = outs[0] - lam * outs[1]
        ms = jnp.mean(o * o, axis=-1, keepdims=True)
        o = o * lax.rsqrt(ms + NORM_EPS) * sw_ref[...] * (1.0 - lambda_init)
        o_ref[0, i * tile:(i + 1) * tile, :] = o.astype(o_ref.dtype)


def _attn(q3, k3, v3, lq1, lk1, lq2, lk2, sw, lambda_init):
    b, s, _ = q3.shape
    tile = min(ATTN_TILE, s)
    spec = pl.BlockSpec((1, s, LANES), lambda bi, h: (bi, 0, h))
    vec = lambda n: pl.BlockSpec((1, n), lambda bi, h: (0, 0))
    return pl.pallas_call(
        functools.partial(_attn_kernel, tile=tile, lambda_init=lambda_init),
        grid=(b, DIFF_HEADS),
        in_specs=[spec, spec, spec, vec(DIFF_HEAD_DIM), vec(DIFF_HEAD_DIM), vec(DIFF_HEAD_DIM),
                  vec(DIFF_HEAD_DIM), vec(DIFF_V_DIM)],
        out_specs=spec,
        out_shape=jax.ShapeDtypeStruct((b, s, DIFF_WIDTH), BF16),
        compiler_params=_cparams(("parallel", "parallel")),
        name="attn",
    )(q3, k3, v3, lq1, lk1, lq2, lk2, sw)


def _scan_kernel(r_ref, lw_ref, k_ref, v_ref, kk_ref, b_ref, y_ref, state_ref,
                 wr_s, ut_s, utt_s, arb_s, pv_s, bh_s, vk_s, wt_s, *, n_chunks):
    L = CHUNK
    G = GROUP
    n_groups = RWKV_WIDTH // G
    n_seqs = r_ref.shape[0]

    @pl.when(pl.program_id(1) == 0)
    def _():
        state_ref[...] = jnp.zeros_like(state_ref)

    row = lax.broadcasted_iota(I32, (L, G), 0)
    colr = lax.broadcasted_iota(I32, (L, G), 1) & (L - 1)
    strict = (colr < row).astype(F32)
    incl = (colr <= row).astype(F32)
    eye = (colr == row).astype(F32)
    eye_l = (lax.broadcasted_iota(I32, (L, L), 0) == lax.broadcasted_iota(I32, (L, L), 1)).astype(BF16)
    br = lax.broadcasted_iota(I32, (G, G), 0) >> 6
    bc = lax.broadcasted_iota(I32, (G, G), 1) >> 6
    block = (br == bc).astype(F32)
    block_bf = block.astype(BF16)
    rows1 = lax.broadcasted_iota(I32, (L, 1), 0)

    def stack4(x):
        xb = x.astype(BF16)
        return jnp.concatenate([xb, xb, xb, xb], axis=0) * block_bf

    def cat(a, b):
        return jnp.concatenate([a, b], axis=0).astype(BF16)

    def precompute(it, carry):
        chains = [(bb, cl, g) for bb in range(n_seqs) for cl in range(SCAN_BATCH) for g in range(n_groups)]
        each = lambda f, *lists: [f(*args) for args in zip(*lists)]

        def load(ref):
            out = []
            for bb, cl, g in chains:
                base = pl.multiple_of((it * SCAN_BATCH + cl) * L, L)
                out.append(ref[bb, pl.ds(base, L), g * G:(g + 1) * G].astype(F32))
            return out

        r, lw, k, v, kk, b = load(r_ref), load(lw_ref), load(k_ref), load(v_ref), load(kk_ref), load(b_ref)

        def cumsum(x):
            sh = 1
            while sh < L:
                x = x + jnp.where(rows1 >= sh, pltpu.roll(x, sh, 0), 0.0)
                sh *= 2
            return x

        cs = each(cumsum, lw)
        tot = each(lambda c: c[L - 1:L, :], cs)
        a_hat = each(lambda kk_, c, l: -kk_ * jnp.exp(c - l), kk, cs, lw)
        r_hat = each(lambda r_, c: r_ * jnp.exp(c), r, cs)
        w_inv = each(lambda c: jnp.exp(-c), cs)
        w_end = each(lambda t_, c: jnp.exp(t_ - c), tot, cs)
        lhs = each(cat, a_hat, r_hat)
        ab = each(lambda l_, b_, wi: _nt(l_, stack4(b_ * wi)), lhs, b, w_inv)
        ak = each(lambda l_, k_, wi: _nt(l_, stack4(k_ * wi)), lhs, k, w_inv)
        a_ab = each(lambda x: x[:L] * strict, ab)
        a_rb = each(lambda x: (x[L:] * incl).astype(BF16), ab)
        a_k = each(lambda x: cat(x[:L] * strict, x[L:] * incl), ak)
        t_mat = each(lambda a: eye + a, a_ab)
        p_mat = each(lambda a: _dot(a.astype(BF16), stack4(a)), a_ab)
        for _ in range(4):
            tp = each(lambda t_, p_: _dot(cat(t_, p_), stack4(p_)), t_mat, p_mat)
            t_mat = each(lambda t_, x: t_ + x[:L], t_mat, tp)
            p_mat = each(lambda x: x[L:], tp)
        t_bf = each(lambda t_, p_: (t_ + _dot(t_.astype(BF16), stack4(p_))).astype(BF16), t_mat, p_mat)
        av = each(lambda a, v_: _dot(a, stack4(v_)), a_k, v)
        w_til = each(lambda t_, a: _dot(t_, stack4(a)), t_bf, a_hat)
        u_til = each(lambda t_, x: _dot(t_, stack4(x[:L])), t_bf, av)
        u_til_t = each(lambda u_: _tn(u_.astype(BF16), eye_l), u_til)
        vk = each(lambda v_, k_, we: _tn(v_.astype(BF16), (k_ * we).astype(BF16)) * block, v, k, w_end)
        for n, (bb, cl, g) in enumerate(chains):
            slot = ((it * SCAN_BATCH + cl) * n_seqs + bb) * n_groups + g
            wr_s[slot] = cat(w_til[n], r_hat[n])
            ut_s[slot] = u_til[n]
            utt_s[slot] = u_til_t[n]
            arb_s[slot] = a_rb[n]
            pv_s[slot] = av[n][L:]
            bh_s[slot] = (b[n] * w_end[n]).astype(BF16)
            vk_s[slot] = vk[n]
            wt_s[slot] = jnp.broadcast_to(jnp.exp(tot[n]), (SUBLANES, G))
        return carry

    lax.fori_loop(0, n_chunks // SCAN_BATCH, precompute, 0)

    def recur(c, carry):
        gs = range(n_seqs * n_groups)
        slots = [c * n_seqs * n_groups + g for g in gs]
        s0 = [state_ref[g] for g in gs]
        s0b = [s.astype(BF16) for s in s0]
        wr = [wr_s[sl] for sl in slots]
        u_t = [_nt(s0b[g], wr[g][:L]) + utt_s[slots[g]] for g in gs]
        ub = [_dot(u_t[g].astype(BF16), bh_s[slots[g]]) for g in gs]
        for g in gs:
            state_ref[g] = s0[g] * wt_s[slots[g]][0:1, :] + ub[g] * block + vk_s[slots[g]]
        uy = [_nt(wr[g], s0b[g]) for g in gs]
        u = [uy[g][:L] + ut_s[slots[g]] for g in gs]
        base = pl.multiple_of(c * L, L)
        for g in gs:
            y = uy[g][L:] + _dot(arb_s[slots[g]], stack4(u[g])) + pv_s[slots[g]]
            lanes = slice((g % n_groups) * G, (g % n_groups + 1) * G)
            y_ref[g // n_groups, pl.ds(base, L), lanes] = y
        return carry

    lax.fori_loop(0, n_chunks, recur, 0)


def _scan(r3, lw3, k3, v3, kk3, b3):
    bsz, s, w = r3.shape
    ts = min(SCAN_TILE, s)
    n_chunks = ts // CHUNK
    nb = SCAN_SEQS if bsz % SCAN_SEQS == 0 else 1
    slots = nb * n_chunks * (w // GROUP)
    L, G = CHUNK, GROUP
    spec = pl.BlockSpec((nb, ts, w), lambda bi, i: (bi, i, 0))
    return pl.pallas_call(
        functools.partial(_scan_kernel, n_chunks=n_chunks),
        grid=(bsz // nb, s // ts),
        in_specs=[spec] * 6,
        out_specs=spec,
        out_shape=jax.ShapeDtypeStruct((bsz, s, w), F32),
        scratch_shapes=[pltpu.VMEM((nb * (w // GROUP), G, G), F32),
                        pltpu.VMEM((slots, 2 * L, G), BF16),
                        pltpu.VMEM((slots, L, G), F32),
                        pltpu.VMEM((slots, G, L), F32),
                        pltpu.VMEM((slots, L, G), BF16),
                        pltpu.VMEM((slots, L, G), F32),
                        pltpu.VMEM((slots, L, G), BF16),
                        pltpu.VMEM((slots, G, G), F32),
                        pltpu.VMEM((slots, SUBLANES, G), F32)],
        compiler_params=_cparams(("parallel", "arbitrary")),
        name="rwkv_scan",
    )(r3, lw3, k3, v3, kk3, b3)


def _mix_kernel(od_ref, y_ref, g_ref, bonus_ref, x_ref, lnw_ref, lnb_ref, ones_ref, wo_ref,
                fw_ref, rw_ref, rb_ref,
                h1_ref, xn_ref, sel_ref, idx_ref, gate_ref, cnt_ref):
    ones = ones_ref[...]
    y = y_ref[...]
    inv_n = 1.0 / RWKV_HEAD
    mean = _split_dot(y, ones) * inv_n
    d = y - mean
    var = _dot((d * d).astype(BF16), ones) * inv_n
    yn = d * lax.rsqrt(var + RWKV_GN_EPS) * lnw_ref[...] + lnb_ref[...]
    orw = ((yn + bonus_ref[...]) * g_ref[...]).astype(BF16)
    h1 = (x_ref[...] + _dot(od_ref[...], wo_ref[0:DIFF_WIDTH, :])
          + _dot(orw, wo_ref[DIFF_WIDTH:DIFF_WIDTH + RWKV_WIDTH, :]))
    h1_ref[...] = h1
    ms = jnp.mean(h1 * h1, axis=-1, keepdims=True)
    xn = h1 * lax.rsqrt(ms + NORM_EPS) * fw_ref[...]
    xn_ref[...] = xn.astype(xn_ref.dtype)
    logits = _split3_dot(xn, rw_ref[...]) + rb_ref[...]
    tm = logits.shape[0]
    lane = lax.broadcasted_iota(I32, (tm, LANES), 1).astype(F32)
    work = logits
    sel = jnp.zeros((tm, LANES), F32)
    idx_l = jnp.zeros((tm, LANES), F32)
    val_l = jnp.zeros((tm, LANES), F32)
    top = None
    for kslot in range(TOP_K):
        m = jnp.max(work, axis=-1, keepdims=True)
        pick = jnp.min(jnp.where(work == m, lane, float(LANES)), axis=-1, keepdims=True)
        hit = lane == pick
        sel = jnp.where(hit, 1.0, sel)
        idx_l = jnp.where(lane == kslot, pick, idx_l)
        if top is None:
            top = m
        val_l = jnp.where(lane == kslot, jnp.exp(m - top), val_l)
        work = jnp.where(hit, -jnp.inf, work)
    sel_ref[...] = sel
    idx_ref[...] = idx_l.astype(I32)
    gate_ref[...] = val_l / jnp.sum(val_l, axis=-1, keepdims=True)
    for part in range(tm // MOE_TILE):
        count = jnp.sum(sel[part * MOE_TILE:(part + 1) * MOE_TILE], axis=0, keepdims=True)
        cnt_ref[part * SUBLANES:(part + 1) * SUBLANES, :] = jnp.broadcast_to(count, (SUBLANES, LANES)).astype(I32)


def _mix(od, y, g, bonus, x2, lnw, lnb, ones, wo, fw, rw, rb):
    t, d = x2.shape
    tm = min(ROW_TILE, t)
    w = RWKV_WIDTH
    row = lambda i: (i, 0)
    fixed = lambda i: (0, 0)
    rs = lambda n: pl.BlockSpec((tm, n), row)
    return pl.pallas_call(
        _mix_kernel,
        grid=(t // tm,),
        in_specs=[rs(DIFF_WIDTH), rs(w), rs(w), rs(w), rs(d),
                  pl.BlockSpec((1, w), fixed), pl.BlockSpec((1, w), fixed),
                  pl.BlockSpec((w, w), fixed), pl.BlockSpec(wo.shape, fixed),
                  pl.BlockSpec((1, d), fixed), pl.BlockSpec(rw.shape, fixed),
                  pl.BlockSpec((1, LANES), fixed)],
        out_specs=[rs(d), rs(d), rs(LANES), rs(LANES), rs(LANES),
                   pl.BlockSpec((tm // MOE_TILE * SUBLANES, LANES), row)],
        out_shape=[jax.ShapeDtypeStruct((t, d), F32), jax.ShapeDtypeStruct((t, d), BF16),
                   jax.ShapeDtypeStruct((t, LANES), F32), jax.ShapeDtypeStruct((t, LANES), I32),
                   jax.ShapeDtypeStruct((t, LANES), F32),
                   jax.ShapeDtypeStruct((t // MOE_TILE * SUBLANES, LANES), I32)],
        compiler_params=_cparams(("parallel",)),
        name="mix_router",
    )(od, y, g, bonus, x2, lnw, lnb, ones, wo, fw, rw, rb)


def _slot_positions(sel, idx_l, lstart):
    tt = sel.shape[0]
    r = lax.broadcasted_iota(I32, (tt, tt), 0)
    c = lax.broadcasted_iota(I32, (tt, tt), 1)
    lower = (c < r).astype(BF16)
    where_to = _dot(lower, sel.astype(BF16)) + lstart
    lane = lax.broadcasted_iota(I32, (tt, LANES), 1).astype(F32)
    idx = idx_l.astype(F32)
    pos = jnp.full((tt, LANES), -1.0, F32)
    for kslot in range(TOP_K):
        e = jnp.sum(jnp.where(lane == kslot, idx, 0.0), axis=-1, keepdims=True)
        p = jnp.sum(jnp.where(lane == e, where_to, 0.0), axis=-1, keepdims=True)
        pos = jnp.where(lane == kslot, p, pos)
    return pos


def _piece(ref, q):
    return ref.at[pl.ds(pl.multiple_of(q * SUBLANES, SUBLANES), SUBLANES), :]


def _drain(count, src_ref, dst_ref, sem, max_pieces):
    b = 0
    while (1 << b) <= max_pieces:
        rows = SUBLANES << b

        @pl.when((count >> b) & 1 == 1)
        def _():
            pltpu.make_async_copy(src_ref.at[pl.ds(0, rows), :], dst_ref.at[pl.ds(0, rows), :], sem).wait()

        b += 1


def _dispatch_kernel(np_ref, fill_ref, nfill_ref, gdst_ref, sel_ref, idx_ref, lstart_ref, xn_ref,
                     pos_ref, buf_ref, xs_ref, zero_ref, sem, fill_sem, *, n_slots):
    i = pl.program_id(0)
    last = pl.num_programs(0) - 1
    slot = i % 2

    def copy(q, sl):
        return pltpu.make_async_copy(_piece(xs_ref.at[sl], q), _piece(buf_ref, gdst_ref[0, 0, q]), sem.at[sl])

    def drain(count, sl):
        _drain(count, xs_ref.at[sl], buf_ref, sem.at[sl], n_slots // SUBLANES)

    @pl.when(i >= 2)
    def _():
        drain(np_ref[jnp.maximum(i - 2, 0)], slot)

    pos = _slot_positions(sel_ref[...], idx_ref[...], lstart_ref[0:1, :].astype(F32))
    pos_ref[...] = pos
    tt = pos.shape[0]
    pos_t = pos.T.astype(I32)
    s_iota = lax.broadcasted_iota(I32, (n_slots, tt), 0)
    perm = jnp.zeros((n_slots, tt), F32)
    for kslot in range(TOP_K):
        perm = perm + jnp.where(s_iota == pos_t[kslot:kslot + 1, :], 1.0, 0.0)
    xs_ref[slot] = _pack_halves(_dot(perm.astype(BF16), xn_ref[...]))
    lax.fori_loop(0, np_ref[i], lambda q, c: (copy(q, slot).start(), c)[1], 0)

    @pl.when(i == 0)
    def _():
        zero_ref[...] = jnp.zeros_like(zero_ref)

        def fill(j):
            return pltpu.make_async_copy(zero_ref, _piece(buf_ref, fill_ref[j]), fill_sem)

        lax.fori_loop(0, nfill_ref[0], lambda j, c: (fill(j).start(), c)[1], 0)
        lax.fori_loop(0, nfill_ref[0], lambda j, c: (fill(j).wait(), c)[1], 0)

    @pl.when(i == last)
    def _():
        @pl.when(i >= 1)
        def _():
            drain(np_ref[jnp.maximum(i - 1, 0)], 1 - slot)

        drain(np_ref[i], slot)


def _dispatch(npieces, fill, nfill, gdst3, sel, idx_l, lstart8, xn, n_rows):
    t, d = xn.shape
    n_tiles, _, n_pieces = gdst3.shape
    tt = t // n_tiles
    n_slots = n_pieces * SUBLANES
    row = lambda i, *_: (i, 0)
    grid_spec = pltpu.PrefetchScalarGridSpec(
        num_scalar_prefetch=3,
        grid=(n_tiles,),
        in_specs=[pl.BlockSpec((1, 1, n_pieces), lambda i, *_: (i, 0, 0), memory_space=pltpu.SMEM),
                  pl.BlockSpec((tt, LANES), row),
                  pl.BlockSpec((tt, LANES), row),
                  pl.BlockSpec((SUBLANES, LANES), row),
                  pl.BlockSpec((tt, d), row)],
        out_specs=[pl.BlockSpec((tt, LANES), row), pl.BlockSpec(memory_space=pl.ANY)],
        scratch_shapes=[pltpu.VMEM((2, n_slots, d // 2), U32), pltpu.VMEM((SUBLANES, d // 2), U32),
                        pltpu.SemaphoreType.DMA((2,)), pltpu.SemaphoreType.DMA(())],
    )
    return pl.pallas_call(
        functools.partial(_dispatch_kernel, n_slots=n_slots),
        grid_spec=grid_spec,
        out_shape=[jax.ShapeDtypeStruct((t, LANES), F32), jax.ShapeDtypeStruct((n_rows, d // 2), U32)],
        compiler_params=_cparams(("arbitrary",)),
        name="moe_dispatch",
    )(npieces, fill, nfill, gdst3, sel, idx_l, lstart8, xn)


def _expert_kernel(be_ref, first_ref, ord_ref, next_ref, valid_ref, nused_ref, x_ref, w1_hbm, b1_ref, w2_hbm, b2_ref,
                   y_ref, w1f_ref, w2f_ref, w1b_ref, w2i_ref, w2b_ref, sem):
    i = pl.program_id(0)
    ff = w2f_ref.shape[1]
    used = i < nused_ref[0]
    new_expert = jnp.logical_and(used, first_ref[i] == 1)

    def fetch(expert, slot):
        return (pltpu.make_async_copy(w1_hbm.at[expert], w1f_ref.at[slot], sem.at[0, slot]),
                pltpu.make_async_copy(w2_hbm.at[expert], w2f_ref.at[slot], sem.at[1, slot]))

    @pl.when(i == 0)
    def _():
        for cp in fetch(be_ref[0], 0):
            cp.start()

    @pl.when(jnp.logical_and(new_expert, next_ref[i] >= 0))
    def _():
        for cp in fetch(next_ref[i], 1 - (ord_ref[i] & 1)):
            cp.start()

    @pl.when(new_expert)
    def _():
        slot = ord_ref[i] & 1
        for cp in fetch(be_ref[i], slot):
            cp.wait()
        rows = w1f_ref.shape[1] // CAST_CHUNKS

        def cast1(c, carry):
            r0 = pl.multiple_of(c * rows, rows)
            w1b_ref[pl.ds(r0, rows), :] = w1f_ref[slot, pl.ds(r0, rows), :].astype(BF16)
            return carry

        lax.fori_loop(0, CAST_CHUNKS, cast1, 0)
        for g in range(w2f_ref.shape[2] // LANES):
            cols = slice(g * LANES, (g + 1) * LANES)
            w2i_ref[pl.ds(0, ff // 2, stride=2), :] = w2f_ref[slot, 0:ff // 2, cols]
            w2i_ref[pl.ds(1, ff // 2, stride=2), :] = w2f_ref[slot, ff // 2:ff, cols]
            w2b_ref[:, cols] = w2i_ref[...].astype(BF16)

    def mlp(rs):
        even = (lax.broadcasted_iota(I32, (1, LANES), 1) & 1) == 0
        x = _unpack_halves(x_ref[rs, :])
        hid = _dot(x, w1b_ref[...]) + b1_ref[0]

        def act_even(g):
            hg = hid[:, g * LANES:(g + 1) * LANES]
            glu = jnp.minimum(hg, SWIGLU_LIMIT)
            lin = jnp.clip(hg, -SWIGLU_LIMIT, SWIGLU_LIMIT) + 1.0
            return glu * jax.nn.sigmoid(SWIGLU_ALPHA * glu) * pltpu.roll(lin, LANES - 1, 1)

        half = ff // LANES
        act = jnp.concatenate(
            [jnp.where(even, act_even(g), pltpu.roll(act_even(g + half), 1, 1)) for g in range(half)], axis=1)
        y = _dot(act.astype(BF16), w2b_ref[...]) + b2_ref[0]
        y_ref[rs, :] = _pack_halves(y.astype(BF16).astype(F32))

    bm = x_ref.shape[0]
    half_rows = bm // 2
    wide = jnp.logical_and(used, valid_ref[i] > half_rows)
    narrow = jnp.logical_and(used, valid_ref[i] <= half_rows)

    @pl.when(wide)
    def _():
        mlp(slice(0, bm))

    @pl.when(narrow)
    def _():
        mlp(slice(0, half_rows))
        y_ref[half_rows:bm, :] = jnp.zeros((bm - half_rows, y_ref.shape[1]), y_ref.dtype)

    @pl.when(jnp.logical_not(used))
    def _():
        y_ref[...] = jnp.zeros_like(y_ref)


def _experts(block_e, rows_per_expert, real_rows, n_used, xbuf, w1, b1, w2, b2):
    n_rows = xbuf.shape[0]
    bm = EXPERT_BLOCK
    n_blocks = n_rows // bm
    e, d, ff2 = w1.shape
    ff = ff2 // 2
    has = rows_per_expert > 0
    ids = jnp.arange(e, dtype=I32)
    later = (ids[None, :] > ids[:, None]) & has[None, :]
    next_used = jnp.where(jnp.any(later, axis=1), jnp.argmax(later, axis=1), -1).astype(I32)
    ordinal = (jnp.cumsum(has.astype(I32)) - 1).astype(I32)
    first = jnp.concatenate([jnp.ones((1,), I32), (block_e[1:] != block_e[:-1]).astype(I32)])
    of_block = block_e[:, None] == ids[None, :]
    order = jnp.sum(jnp.where(of_block, ordinal[None, :], 0), axis=1).astype(I32)
    next_e = jnp.sum(jnp.where(of_block, next_used[None, :], 0), axis=1).astype(I32)
    data_end = jnp.cumsum(rows_per_expert) - rows_per_expert + real_rows
    block_end = jnp.sum(jnp.where(of_block, data_end[None, :], 0), axis=1)
    valid = jnp.clip(block_end - jnp.arange(n_blocks, dtype=I32) * bm, 0, bm).astype(I32)
    blk = lambda i, be, fi, od, ne, va, nu: (jnp.maximum(jnp.minimum(i, nu[0] - 1), 0), 0)
    ex3 = lambda i, be, fi, od, ne, va, nu: (be[i], 0, 0)
    grid_spec = pltpu.PrefetchScalarGridSpec(
        num_scalar_prefetch=6,
        grid=(n_blocks,),
        in_specs=[pl.BlockSpec((bm, d // 2), blk),
                  pl.BlockSpec(memory_space=pl.ANY),
                  pl.BlockSpec((1, 1, ff2), ex3),
                  pl.BlockSpec(memory_space=pl.ANY),
                  pl.BlockSpec((1, 1, d), ex3)],
        out_specs=pl.BlockSpec((bm, d // 2), lambda i, be, fi, od, ne, va, nu: (i, 0)),
        scratch_shapes=[pltpu.VMEM((2, d, ff2), F32), pltpu.VMEM((2, ff, d), F32),
                        pltpu.VMEM((d, ff2), BF16), pltpu.VMEM((ff, LANES), F32), pltpu.VMEM((ff, d), BF16),
                        pltpu.SemaphoreType.DMA((2, 2))],
    )
    return pl.pallas_call(
        _expert_kernel,
        grid_spec=grid_spec,
        out_shape=jax.ShapeDtypeStruct((n_rows, d // 2), U32),
        compiler_params=_cparams(("arbitrary",)),
        name="moe_experts",
    )(block_e, first, order, next_e, valid, n_used, xbuf, w1, b1, w2, b2)


def _combine_kernel(np_ref, gdst_ref, gnext_ref, pos_ref, gate_ref, h1_ref, fw_ref, ybuf_ref, o_ref, ys_ref, sem,
                    *, n_slots):
    i = pl.program_id(0)
    last = pl.num_programs(0) - 1
    slot = i % 2

    def fetch(table_ref, tile, sl):
        count = np_ref[tile]

        def copy(q):
            return pltpu.make_async_copy(_piece(ybuf_ref, table_ref[0, 0, q]), _piece(ys_ref.at[sl], q), sem.at[sl])

        lax.fori_loop(0, count, lambda q, c: (copy(q).start(), c)[1], 0)

        def zero(q, carry):
            _piece(ys_ref.at[sl], q)[...] = jnp.zeros((SUBLANES, ys_ref.shape[2]), ys_ref.dtype)
            return carry

        lax.fori_loop(count, n_slots // SUBLANES, zero, 0)

    @pl.when(i == 0)
    def _():
        fetch(gdst_ref, i, slot)

    @pl.when(i < last)
    def _():
        fetch(gnext_ref, jnp.minimum(i + 1, last), 1 - slot)

    n = np_ref[i]
    pos = pos_ref[...].astype(I32)
    gate = gate_ref[...]
    tt = pos.shape[0]
    s_iota = lax.broadcasted_iota(I32, (tt, n_slots), 1)
    weight = jnp.zeros((tt, n_slots), F32)
    for kslot in range(TOP_K):
        weight = weight + jnp.where(s_iota == pos[:, kslot:kslot + 1], gate[:, kslot:kslot + 1], 0.0)
    _drain(n, ybuf_ref, ys_ref.at[slot], sem.at[slot], n_slots // SUBLANES)
    h = h1_ref[...] + _dot(weight.astype(BF16), _unpack_halves(ys_ref[slot]))
    ms = jnp.mean(h * h, axis=-1, keepdims=True)
    o_ref[...] = h * lax.rsqrt(ms + NORM_EPS) * fw_ref[...]


def _combine(npieces, gdst3, pos_l, gate_l, h1, fw, ybuf):
    t, d = h1.shape
    n_tiles, _, n_pieces = gdst3.shape
    tt = t // n_tiles
    n_slots = n_pieces * SUBLANES
    row = lambda i, *_: (i, 0)
    grid_spec = pltpu.PrefetchScalarGridSpec(
        num_scalar_prefetch=1,
        grid=(n_tiles,),
        in_specs=[pl.BlockSpec((1, 1, n_pieces), lambda i, *_: (i, 0, 0), memory_space=pltpu.SMEM),
                  pl.BlockSpec((1, 1, n_pieces), lambda i, *_: (jnp.minimum(i + 1, n_tiles - 1), 0, 0),
                               memory_space=pltpu.SMEM),
                  pl.BlockSpec((tt, LANES), row),
                  pl.BlockSpec((tt, LANES), row),
                  pl.BlockSpec((tt, d), row),
                  pl.BlockSpec((1, d), lambda i, *_: (0, 0)),
                  pl.BlockSpec(memory_space=pl.ANY)],
        out_specs=pl.BlockSpec((tt, d), row),
        scratch_shapes=[pltpu.VMEM((2, n_slots, d // 2), U32), pltpu.SemaphoreType.DMA((2,))],
    )
    return pl.pallas_call(
        functools.partial(_combine_kernel, n_slots=n_slots),
        grid_spec=grid_spec,
        out_shape=jax.ShapeDtypeStruct((t, d), F32),
        compiler_params=_cparams(("arbitrary",)),
        name="moe_combine",
    )(npieces, gdst3, gdst3, pos_l, gate_l, h1, fw, ybuf)


def _rotary_tables(positions):
    half = ROT_DIM // 2
    inv_freq = ROPE_THETA ** (-jnp.arange(0, ROT_DIM, 2, dtype=F32) / ROT_DIM)
    ang = positions.astype(F32).reshape(-1, 1) * inv_freq
    cos = jnp.tile(jnp.cos(ang), (1, LANES // half))
    sin = jnp.tile(jnp.sin(ang), (1, LANES // half))
    dim = np.arange(LANES)[None, :] % DIFF_HEAD_DIM
    ctab = jnp.where(dim < ROT_DIM, cos, 1.0)
    satab = jnp.where(dim < half, -sin, 0.0)
    sbtab = jnp.where((dim >= half) & (dim < ROT_DIM), sin, 0.0)
    return ctab, satab, sbtab


def _pad_rows(a, rows):
    return jnp.concatenate([a, jnp.zeros((rows - a.shape[0],) + a.shape[1:], a.dtype)], axis=0)


def _layer(h, l, tabs, attn_norm_w, w_in, diff_lambda_q1, diff_lambda_k1, diff_lambda_q2, diff_lambda_k2,
           diff_subln_w, rwkv_mu, rwkv_w0, rwkv_w_up, rwkv_a0, rwkv_a_up, rwkv_g_up, rwkv_k_k, rwkv_k_a,
           rwkv_r_k, rwkv_ln_w, rwkv_ln_b, w_out, ffn_norm_w, router_w, router_b, exp_w1, exp_b1,
           exp_w2, exp_b2, final_w):
    bsz, seq, d = h.shape
    t = bsz * seq
    w = RWKV_WIDTH
    lambda_init = 0.8 - 0.6 * math.exp(-0.3 * l)
    x2 = h.reshape(t, d)
    row1 = lambda a: a.reshape(1, -1).astype(F32)

    wi = w_in[l]
    qkv_cols = 3 * DIFF_WIDTH
    wq = wi[:, :qkv_cols].astype(BF16)
    o = qkv_cols + 3 * w
    zcol = lambda n: jnp.zeros((d, n), wi.dtype)
    wr = jnp.concatenate([
        wi[:, qkv_cols:o],
        wi[:, o:o + DECAY_LORA], zcol(LORA_PAD - DECAY_LORA),
        wi[:, o + DECAY_LORA:o + DECAY_LORA + AAA_LORA], zcol(LORA_PAD - AAA_LORA),
        wi[:, o + DECAY_LORA + AAA_LORA:], zcol(LORA_PAD - GATE_LORA)], axis=1).astype(BF16)
    mu = rwkv_mu[l]
    zv = lambda n: jnp.zeros((n,), mu.dtype)
    mu_p = jnp.concatenate([
        mu[:3 * w],
        mu[3 * w:3 * w + DECAY_LORA], zv(LORA_PAD - DECAY_LORA),
        mu[3 * w + DECAY_LORA:3 * w + DECAY_LORA + AAA_LORA], zv(LORA_PAD - AAA_LORA),
        mu[3 * w + DECAY_LORA + AAA_LORA:], zv(LORA_PAD - GATE_LORA)]).reshape(1, -1)

    head = np.arange(w) // RWKV_HEAD
    ones = jnp.asarray(head[:, None] == head[None, :], BF16)
    q, k, v, r, lw, kmod, vv, kk, bb, g, bonus = _inproj(
        x2, row1(attn_norm_w[l]), wq, wr, *tabs,
        mu_p, row1(rwkv_w0[l]), row1(rwkv_a0[l]), row1(rwkv_k_k[l]), row1(rwkv_k_a[l]),
        row1(rwkv_r_k[l]), _pad_rows(rwkv_w_up[l].astype(BF16), LORA_PAD),
        _pad_rows(rwkv_a_up[l].astype(BF16), LORA_PAD), _pad_rows(rwkv_g_up[l].astype(BF16), LORA_PAD),
        ones, seq)

    od = _attn(q.reshape(bsz, seq, -1), k.reshape(bsz, seq, -1), v.reshape(bsz, seq, -1),
               row1(diff_lambda_q1[l]), row1(diff_lambda_k1[l]), row1(diff_lambda_q2[l]),
               row1(diff_lambda_k2[l]), row1(diff_subln_w[l]), lambda_init)

    s3 = lambda a: a.reshape(bsz, seq, w)
    y = _scan(s3(r), s3(lw), s3(kmod), s3(vv), s3(kk), s3(bb)).reshape(t, w)

    n_e = router_w.shape[-1]
    rw = jnp.concatenate([router_w[l].astype(F32), jnp.zeros((d, LANES - n_e), F32)], axis=1)
    rw_hi = rw.astype(BF16)
    rw = jnp.concatenate([rw_hi, (rw - rw_hi.astype(F32)).astype(BF16)], axis=1)
    rb = jnp.concatenate([router_b[l].astype(F32), jnp.full((LANES - n_e,), -1e30, F32)]).reshape(1, -1)
    h1, xn, sel, idx_l, gate_l, cnt8 = _mix(
        od.reshape(t, -1), y, g, bonus, x2, row1(rwkv_ln_w[l]), row1(rwkv_ln_b[l]), ones,
        w_out[l].astype(BF16), row1(ffn_norm_w[l]), rw, rb)

    bm = EXPERT_BLOCK
    pc = SUBLANES
    tm = MOE_TILE
    n_tiles = t // tm
    cnt = cnt8.reshape(n_tiles, SUBLANES, LANES)[:, 0, :n_e]
    seg = (cnt + pc - 1) // pc * pc
    lend = jnp.cumsum(seg, axis=1)
    lstart = lend - seg
    rows_e = jnp.sum(seg, axis=0)
    padded = (rows_e + bm - 1) // bm * bm
    pad_ends = jnp.cumsum(padded)
    gstart = (pad_ends - padded)[None, :] + jnp.cumsum(seg, axis=0) - seg
    n_slots = tm * TOP_K + n_e * pc
    n_pieces = n_slots // pc
    n_blocks = -(-(t * TOP_K + n_tiles * n_e * (pc - 1) + n_e * (bm - pc)) // bm)
    n_rows = n_blocks * bm
    piece_row = jnp.arange(n_pieces, dtype=I32) * pc
    piece_e = jnp.minimum(jnp.sum(lend[:, None, :] <= piece_row[None, :, None], axis=-1), n_e - 1)
    pick = piece_e[:, :, None] == jnp.arange(n_e, dtype=I32)[None, None, :]
    take = lambda a: jnp.sum(jnp.where(pick, a[:, None, :], 0), axis=-1)
    gdst = ((take(gstart) + piece_row[None, :] - take(lstart)) // pc).astype(I32)
    gdst3 = jnp.clip(gdst, 0, n_rows // pc - 1).reshape(n_tiles, 1, n_pieces)
    npieces = (lend[:, -1] // pc).astype(I32)
    lstart8 = jnp.zeros((n_tiles, SUBLANES, LANES), I32).at[:, :, :n_e].set(lstart[:, None, :])
    lstart8 = lstart8.reshape(n_tiles * SUBLANES, LANES)
    gap_start = jnp.concatenate([pad_ends - padded + rows_e, pad_ends[-1:]]) // pc
    gap_len = jnp.concatenate([padded - rows_e, n_rows - pad_ends[-1:]]) // pc
    gap_end = jnp.cumsum(gap_len)
    max_fill = n_e * (bm // pc - 1) + (n_rows - t * TOP_K) // pc
    j = jnp.arange(max_fill, dtype=I32)
    gap = jnp.minimum(jnp.sum(gap_end[None, :] <= j[:, None], axis=1), n_e)
    in_gap = gap[:, None] == jnp.arange(n_e + 1, dtype=I32)[None, :]
    shift = jnp.sum(jnp.where(in_gap, (gap_start - gap_end + gap_len)[None, :], 0), axis=1)
    fill = jnp.clip(j + shift, 0, n_rows // pc - 1).astype(I32)
    nfill = gap_end[-1:].astype(I32)
    first_row = jnp.arange(n_blocks, dtype=I32) * bm
    block_e = jnp.minimum(jnp.sum(pad_ends[None, :] <= first_row[:, None], axis=1), n_e - 1).astype(I32)
    n_used = (pad_ends[-1] // bm).astype(I32).reshape(1)

    pos_l, xbuf = _dispatch(npieces, fill, nfill, gdst3, sel, idx_l, lstart8, xn, n_rows)

    ff2 = exp_w1.shape[-1]
    b1 = exp_b1[l].astype(F32).reshape(n_e, 1, ff2)
    b2 = exp_b2[l].astype(F32).reshape(n_e, 1, d)
    ybuf = _experts(block_e, padded, rows_e, n_used, xbuf, exp_w1[l].astype(F32), b1, exp_w2[l].astype(F32), b2)

    out = _combine(npieces, gdst3, pos_l, gate_l, h1, row1(final_w), ybuf)
    return out.reshape(bsz, seq, d)


def kernel(x, positions, attn_norm_w, w_in, diff_lambda_q1, diff_lambda_k1, diff_lambda_q2, diff_lambda_k2, diff_subln_w, rwkv_mu, rwkv_w0, rwkv_w_up, rwkv_a0, rwkv_a_up, rwkv_g_up, rwkv_k_k, rwkv_k_a, rwkv_r_k, rwkv_ln_w, rwkv_ln_b, w_out, ffn_norm_w, router_w, router_b, exp_w1, exp_b1, exp_w2, exp_b2, final_norm_w):
    depth = w_in.shape[0]
    assert depth == 1, "the final norm is fused into the last (only) layer's combine kernel"
    tabs = _rotary_tables(positions)
    return _layer(x, 0, tabs, attn_norm_w, w_in, diff_lambda_q1, diff_lambda_k1, diff_lambda_q2,
                  diff_lambda_k2, diff_subln_w, rwkv_mu, rwkv_w0, rwkv_w_up, rwkv_a0, rwkv_a_up, rwkv_g_up,
                  rwkv_k_k, rwkv_k_a, rwkv_r_k, rwkv_ln_w, rwkv_ln_b, w_out, ffn_norm_w, router_w, router_b,
                  exp_w1, exp_b1, exp_w2, exp_b2, final_norm_w)
```

```python
import functools
import math

import jax
import jax.numpy as jnp
import numpy as np
from jax import lax
from jax.experimental import pallas as pl
from jax.experimental.pallas import tpu as pltpu

F32 = jnp.float32
BF16 = jnp.bfloat16
I32 = jnp.int32
U32 = jnp.uint32

DIFF_HEAD_DIM = 64
DIFF_V_DIM = 128
DIFF_HEADS = 4
DIFF_WIDTH = DIFF_HEADS * DIFF_V_DIM
ROT_DIM = 16
ROPE_THETA = 500000.0
RWKV_HEAD = 64
RWKV_HEADS = 8
RWKV_WIDTH = RWKV_HEAD * RWKV_HEADS
DECAY_LORA = 32
AAA_LORA = 32
GATE_LORA = 96
TOP_K = 4
SWIGLU_LIMIT = 7.0
SWIGLU_ALPHA = 1.702
NORM_EPS = 1e-5
RWKV_GN_EPS = 64e-5

LANES = 128
SUBLANES = 8
VMEM_LIMIT = 56 * 1024 * 1024

ROW_TILE = 512
SUB_TILES = 2
ATTN_TILE = 512
CHUNK = 64
SCAN_TILE = 512
SCAN_SEQS = 2
SCAN_BATCH = 4
GROUP = 4 * RWKV_HEAD
EXPERT_BLOCK = 512
MOE_TILE = 256
CAST_CHUNKS = 8
LORA_PAD = LANES
ZR_COLS = 3 * RWKV_WIDTH + 3 * LORA_PAD


def _cparams(sem):
    return pltpu.CompilerParams(dimension_semantics=sem, vmem_limit_bytes=VMEM_LIMIT)


def _nt(a, b):
    return lax.dot_general(a, b, (((1,), (1,)), ((), ())), preferred_element_type=F32)


def _tn(a, b):
    return lax.dot_general(a, b, (((0,), (0,)), ((), ())), preferred_element_type=F32)


def _dot(a, b):
    return jnp.dot(a, b, preferred_element_type=F32)


def _pack_halves(x):
    n = x.shape[1] // 2
    bits = lax.bitcast_convert_type(x, U32)
    return (bits[:, :n] & jnp.uint32(0xFFFF0000)) | (bits[:, n:] >> 16)


def _unpack_halves(p):
    hi = lax.bitcast_convert_type(p & jnp.uint32(0xFFFF0000), F32)
    lo = lax.bitcast_convert_type(p << 16, F32)
    return jnp.concatenate([hi, lo], axis=1).astype(BF16)


def _split(x):
    hi = x.astype(BF16)
    return hi, (x - hi.astype(F32)).astype(BF16)


def _split_dot(x, w_bf16):
    hi, lo = _split(x)
    return _dot(hi, w_bf16) + _dot(lo, w_bf16)


def _split3_dot(x, w_parts):
    n = w_parts.shape[1] // 2
    hi, lo = _split(x)
    both = _dot(hi, w_parts)
    return both[:, :n] + (both[:, n:] + _dot(lo, w_parts[:, :n]))


def _inproj_kernel(x_ref, nw_ref, wq_ref, wr_ref, c_ref, sa_ref, sb_ref,
                   mu_ref, w0_ref, a0_ref, kk_ref, ka_ref, rk_ref, wup_ref, aup_ref, gup_ref, ones_ref,
                   q_ref, k_ref, v_ref, r_ref, lw_ref, rk_out_ref, rv_ref, kkn_ref, b_ref, g_ref, bonus_ref,
                   prev_ref, *, tiles_per_seq):
    i = pl.program_id(0)

    @pl.when(i % tiles_per_seq == 0)
    def _():
        prev_ref[...] = jnp.zeros_like(prev_ref)

    tm = x_ref.shape[0]
    sub = tm // SUB_TILES
    scale = DIFF_HEAD_DIM ** -0.5
    w = RWKV_WIDTH
    ones = ones_ref[...]
    rows = lax.broadcasted_iota(I32, (sub, 1), 0)
    prev = prev_ref[SUBLANES - 1:SUBLANES, :]
    for part in range(SUB_TILES):
        rs = slice(part * sub, (part + 1) * sub)
        x = x_ref[rs, :]
        ms = jnp.mean(x * x, axis=-1, keepdims=True)
        u = (x * lax.rsqrt(ms + NORM_EPS) * nw_ref[...]).astype(BF16)
        zq = _dot(u, wq_ref[...])
        c = c_ref[rs, :]
        sa = sa_ref[rs, :]
        sb = sb_ref[rs, :]
        for g in range(2 * DIFF_HEADS):
            zg = zq[:, g * LANES:(g + 1) * LANES]
            rot = zg * c + pltpu.roll(zg, LANES - ROT_DIM // 2, 1) * sa + pltpu.roll(zg, ROT_DIM // 2, 1) * sb
            if g < DIFF_HEADS:
                q_ref[rs, g * LANES:(g + 1) * LANES] = (rot * scale).astype(BF16)
            else:
                h = g - DIFF_HEADS
                k_ref[rs, h * LANES:(h + 1) * LANES] = rot.astype(BF16)
        v_ref[rs, :] = zq[:, 2 * DIFF_WIDTH:3 * DIFF_WIDTH].astype(BF16)

        z = _dot(u, wr_ref[...])
        shifted = jnp.where(rows == 0, prev, pltpu.roll(z, 1, 0))
        prev = z[sub - 1:sub, :]
        if part == SUB_TILES - 1:
            prev_ref[...] = z[sub - SUBLANES:sub, :]
        zf = z + mu_ref[...] * (shifted - z)
        r = zf[:, 0:w]
        k = zf[:, w:2 * w]
        v = zf[:, 2 * w:3 * w]
        wd = zf[:, 3 * w:3 * w + LORA_PAD]
        ad = zf[:, 3 * w + LORA_PAD:3 * w + 2 * LORA_PAD]
        gd = zf[:, 3 * w + 2 * LORA_PAD:3 * w + 3 * LORA_PAD]
        pre = w0_ref[...] + _split_dot(jnp.tanh(wd), wup_ref[...])
        neg = -pre
        softplus = jnp.maximum(neg, 0.0) + jnp.log(1.0 + jnp.exp(-jnp.abs(neg)))
        wlog = -softplus - 0.5
        lw_ref[rs, :] = -jnp.exp(wlog)
        a = jax.nn.sigmoid(a0_ref[...] + _split_dot(ad, aup_ref[...]))
        g_ref[rs, :] = _split_dot(jax.nn.sigmoid(gd), gup_ref[...]).astype(g_ref.dtype)
        kk = k * kk_ref[...]
        norm = jnp.sqrt(_dot((kk * kk).astype(BF16), ones))
        kk = kk / jnp.maximum(norm, 1e-12)
        k = k * (1.0 + (a - 1.0) * ka_ref[...])
        r_ref[rs, :] = r.astype(r_ref.dtype)
        rk_out_ref[rs, :] = k.astype(rk_out_ref.dtype)
        rv_ref[rs, :] = v.astype(rv_ref.dtype)
        kkn_ref[rs, :] = kk.astype(kkn_ref.dtype)
        b_ref[rs, :] = (kk * a).astype(b_ref.dtype)
        bonus_ref[rs, :] = (_dot((r * k * rk_ref[...]).astype(BF16), ones) * v).astype(bonus_ref.dtype)


def _inproj(x2, nw, wq, wr, ctab, satab, sbtab, mu_p, w0, a0, k_k, k_a, rk, wup, aup, gup, ones, seq):
    t, d = x2.shape
    tm = min(ROW_TILE, seq)
    w = RWKV_WIDTH
    row = lambda i: (i, 0)
    fixed = lambda i: (0, 0)
    vecw = pl.BlockSpec((1, w), fixed)
    lora = pl.BlockSpec((LORA_PAD, w), fixed)
    attn_out = pl.BlockSpec((tm, DIFF_WIDTH), row)
    feat_out = pl.BlockSpec((tm, w), row)
    feat = lambda dt: jax.ShapeDtypeStruct((t, w), dt)
    return pl.pallas_call(
        functools.partial(_inproj_kernel, tiles_per_seq=seq // tm),
        grid=(t // tm,),
        in_specs=[
            pl.BlockSpec((tm, d), row),
            pl.BlockSpec((1, d), fixed),
            pl.BlockSpec(wq.shape, fixed),
            pl.BlockSpec(wr.shape, fixed),
            pl.BlockSpec((tm, LANES), row),
            pl.BlockSpec((tm, LANES), row),
            pl.BlockSpec((tm, LANES), row),
            pl.BlockSpec((1, ZR_COLS), fixed),
            vecw, vecw, vecw, vecw, vecw, lora, lora, lora,
            pl.BlockSpec((w, w), fixed),
        ],
        out_specs=[attn_out] * 3 + [feat_out] * 8,
        out_shape=[jax.ShapeDtypeStruct((t, DIFF_WIDTH), BF16)] * 3
        + [feat(BF16), feat(F32), feat(BF16), feat(BF16), feat(BF16), feat(BF16), feat(BF16), feat(BF16)],
        scratch_shapes=[pltpu.VMEM((SUBLANES, ZR_COLS), F32)],
        compiler_params=_cparams(("arbitrary",)),
        name="inproj",
    )(x2, nw, wq, wr, ctab, satab, sbtab, mu_p, w0, a0, k_k, k_a, rk, wup, aup, gup, ones)


def _attn_kernel(q_ref, k_ref, v_ref, lq1_ref, lk1_ref, lq2_ref, lk2_ref, sw_ref, o_ref,
                 *, tile, lambda_init):
    lane = lax.broadcasted_iota(I32, (1, LANES), 1)
    first = lane < DIFF_HEAD_DIM
    neg = -1e30
    lam = (jnp.exp(jnp.sum(lq1_ref[...] * lk1_ref[...], axis=-1, keepdims=True))
           - jnp.exp(jnp.sum(lq2_ref[...] * lk2_ref[...], axis=-1, keepdims=True)) + lambda_init)
    r = lax.broadcasted_iota(I32, (tile, tile), 0)
    c = lax.broadcasted_iota(I32, (tile, tile), 1)
    keep = c <= r
    for i in range(q_ref.shape[1] // tile):
        q = q_ref[0, i * tile:(i + 1) * tile, :]
        zero = jnp.zeros_like(q)
        n = (i + 1) * tile
        kb = k_ref[0, 0:n, :]
        vb = v_ref[0, 0:n, :]
        outs = []
        for qm in (jnp.where(first, q, zero), jnp.where(first, zero, q)):
            s = _nt(qm, kb)
            diag = jnp.where(keep, s[:, i * tile:], neg)
            s = diag if i == 0 else jnp.concatenate([s[:, :i * tile], diag], axis=1)
            p = jnp.exp(s - jnp.max(s, axis=-1, keepdims=True))
            outs.append(_dot(p.astype(BF16), vb) / jnp.sum(p, axis=-1, keepdims=True))
        o = outs[0] - lam * outs[1]
        ms = jnp.mean(o * o, axis=-1, keepdims=True)
        o = o * lax.rsqrt(ms + NORM_EPS) * sw_ref[...] * (1.0 - lambda_init)
        o_ref[0, i * tile:(i + 1) * tile, :] = o.astype(o_ref.dtype)


def _attn(q3, k3, v3, lq1, lk1, lq2, lk2, sw, lambda_init):
    b, s, _ = q3.shape
    tile = min(ATTN_TILE, s)
    spec = pl.BlockSpec((1, s, LANES), lambda bi, h: (bi, 0, h))
    vec = lambda n: pl.BlockSpec((1, n), lambda bi, h: (0, 0))
    return pl.pallas_call(
        functools.partial(_attn_kernel, tile=tile, lambda_init=lambda_init),
        grid=(b, DIFF_HEADS),
        in_specs=[spec, spec, spec, vec(DIFF_HEAD_DIM), vec(DIFF_HEAD_DIM), vec(DIFF_HEAD_DIM),
                  vec(DIFF_HEAD_DIM), vec(DIFF_V_DIM)],
        out_specs=spec,
        out_shape=jax.ShapeDtypeStruct((b, s, DIFF_WIDTH), BF16),
        compiler_params=_cparams(("parallel", "parallel")),
        name="attn",
    )(q3, k3, v3, lq1, lk1, lq2, lk2, sw)


def _scan_kernel(r_ref, lw_ref, k_ref, v_ref, kk_ref, b_ref, y_ref, state_ref,
                 wr_s, ut_s, utt_s, arb_s, pv_s, bh_s, vk_s, wt_s, *, n_chunks):
    L = CHUNK
    G = GROUP
    n_groups = RWKV_WIDTH // G
    n_seqs = r_ref.shape[0]

    @pl.when(pl.program_id(1) == 0)
    def _():
        state_ref[...] = jnp.zeros_like(state_ref)

    row = lax.broadcasted_iota(I32, (L, G), 0)
    colr = lax.broadcasted_iota(I32, (L, G), 1) & (L - 1)
    strict = (colr < row).astype(F32)
    incl = (colr <= row).astype(F32)
    eye = (colr == row).astype(F32)
    eye_l = (lax.broadcasted_iota(I32, (L, L), 0) == lax.broadcasted_iota(I32, (L, L), 1)).astype(BF16)
    br = lax.broadcasted_iota(I32, (G, G), 0) >> 6
    bc = lax.broadcasted_iota(I32, (G, G), 1) >> 6
    block = (br == bc).astype(F32)
    block_bf = block.astype(BF16)
    rows1 = lax.broadcasted_iota(I32, (L, 1), 0)

    def stack4(x):
        xb = x.astype(BF16)
        return jnp.concatenate([xb, xb, xb, xb], axis=0) * block_bf

    def cat(a, b):
        return jnp.concatenate([a, b], axis=0).astype(BF16)

    def precompute(it, carry):
        chains = [(bb, cl, g) for bb in range(n_seqs) for cl in range(SCAN_BATCH) for g in range(n_groups)]
        each = lambda f, *lists: [f(*args) for args in zip(*lists)]

        def load(ref):
            out = []
            for bb, cl, g in chains:
                base = pl.multiple_of((it * SCAN_BATCH + cl) * L, L)
                out.append(ref[bb, pl.ds(base, L), g * G:(g + 1) * G].astype(F32))
            return out

        r, lw, k, v, kk, b = load(r_ref), load(lw_ref), load(k_ref), load(v_ref), load(kk_ref), load(b_ref)

        def cumsum(x):
            sh = 1
            while sh < L:
                x = x + jnp.where(rows1 >= sh, pltpu.roll(x, sh, 0), 0.0)
                sh *= 2
            return x

        cs = each(cumsum, lw)
        tot = each(lambda c: c[L - 1:L, :], cs)
        a_hat = each(lambda kk_, c, l: -kk_ * jnp.exp(c - l), kk, cs, lw)
        r_hat = each(lambda r_, c: r_ * jnp.exp(c), r, cs)
        w_inv = each(lambda c: jnp.exp(-c), cs)
        w_end = each(lambda t_, c: jnp.exp(t_ - c), tot, cs)
        lhs = each(cat, a_hat, r_hat)
        ab = each(lambda l_, b_, wi: _nt(l_, stack4(b_ * wi)), lhs, b, w_inv)
        ak = each(lambda l_, k_, wi: _nt(l_, stack4(k_ * wi)), lhs, k, w_inv)
        a_ab = each(lambda x: x[:L] * strict, ab)
        a_rb = each(lambda x: (x[L:] * incl).astype(BF16), ab)
        a_k = each(lambda x: cat(x[:L] * strict, x[L:] * incl), ak)
        t_mat = each(lambda a: eye + a, a_ab)
        p_mat = each(lambda a: _dot(a.astype(BF16), stack4(a)), a_ab)
        for _ in range(4):
            tp = each(lambda t_, p_: _dot(cat(t_, p_), stack4(p_)), t_mat, p_mat)
            t_mat = each(lambda t_, x: t_ + x[:L], t_mat, tp)
            p_mat = each(lambda x: x[L:], tp)
        t_bf = each(lambda t_, p_: (t_ + _dot(t_.astype(BF16), stack4(p_))).astype(BF16), t_mat, p_mat)
        av = each(lambda a, v_: _dot(a, stack4(v_)), a_k, v)
        w_til = each(lambda t_, a: _dot(t_, stack4(a)), t_bf, a_hat)
        u_til = each(lambda t_, x: _dot(t_, stack4(x[:L])), t_bf, av)
        u_til_t = each(lambda u_: _tn(u_.astype(BF16), eye_l), u_til)
        vk = each(lambda v_, k_, we: _tn(v_.astype(BF16), (k_ * we).astype(BF16)) * block, v, k, w_end)
        for n, (bb, cl, g) in enumerate(chains):
            slot = ((it * SCAN_BATCH + cl) * n_seqs + bb) * n_groups + g
            wr_s[slot] = cat(w_til[n], r_hat[n])
            ut_s[slot] = u_til[n]
            utt_s[slot] = u_til_t[n]
            arb_s[slot] = a_rb[n]
            pv_s[slot] = av[n][L:]
            bh_s[slot] = (b[n] * w_end[n]).astype(BF16)
            vk_s[slot] = vk[n]
            wt_s[slot] = jnp.broadcast_to(jnp.exp(tot[n]), (SUBLANES, G))
        return carry

    lax.fori_loop(0, n_chunks // SCAN_BATCH, precompute, 0)

    def recur(c, carry):
        gs = range(n_seqs * n_groups)
        slots = [c * n_seqs * n_groups + g for g in gs]
        s0 = [state_ref[g] for g in gs]
        s0b = [s.astype(BF16) for s in s0]
        wr = [wr_s[sl] for sl in slots]
        u_t = [_nt(s0b[g], wr[g][:L]) + utt_s[slots[g]] for g in gs]
        ub = [_dot(u_t[g].astype(BF16), bh_s[slots[g]]) for g in gs]
        for g in gs:
            state_ref[g] = s0[g] * wt_s[slots[g]][0:1, :] + ub[g] * block + vk_s[slots[g]]
        uy = [_nt(wr[g], s0b[g]) for g in gs]
        u = [uy[g][:L] + ut_s[slots[g]] for g in gs]
        base = pl.multiple_of(c * L, L)
        for g in gs:
            y = uy[g][L:] + _dot(arb_s[slots[g]], stack4(u[g])) + pv_s[slots[g]]
            lanes = slice((g % n_groups) * G, (g % n_groups + 1) * G)
            y_ref[g // n_groups, pl.ds(base, L), lanes] = y
        return carry

    lax.fori_loop(0, n_chunks, recur, 0)


def _scan(r3, lw3, k3, v3, kk3, b3):
    bsz, s, w = r3.shape
    ts = min(SCAN_TILE, s)
    n_chunks = ts // CHUNK
    nb = SCAN_SEQS if bsz % SCAN_SEQS == 0 else 1
    slots = nb * n_chunks * (w // GROUP)
    L, G = CHUNK, GROUP
    spec = pl.BlockSpec((nb, ts, w), lambda bi, i: (bi, i, 0))
    return pl.pallas_call(
        functools.partial(_scan_kernel, n_chunks=n_chunks),
        grid=(bsz // nb, s // ts),
        in_specs=[spec] * 6,
        out_specs=spec,
        out_shape=jax.ShapeDtypeStruct((bsz, s, w), F32),
        scratch_shapes=[pltpu.VMEM((nb * (w // GROUP), G, G), F32),
                        pltpu.VMEM((slots, 2 * L, G), BF16),
                        pltpu.VMEM((slots, L, G), F32),
                        pltpu.VMEM((slots, G, L), F32),
                        pltpu.VMEM((slots, L, G), BF16),
                        pltpu.VMEM((slots, L, G), F32),
                        pltpu.VMEM((slots, L, G), BF16),
                        pltpu.VMEM((slots, G, G), F32),
                        pltpu.VMEM((slots, SUBLANES, G), F32)],
        compiler_params=_cparams(("parallel", "arbitrary")),
        name="rwkv_scan",
    )(r3, lw3, k3, v3, kk3, b3)


def _mix_kernel(od_ref, y_ref, g_ref, bonus_ref, x_ref, lnw_ref, lnb_ref, ones_ref, wo_ref,
                fw_ref, rw_ref, rb_ref,
                h1_ref, xn_ref, sel_ref, idx_ref, gate_ref, cnt_ref):
    ones = ones_ref[...]
    y = y_ref[...]
    inv_n = 1.0 / RWKV_HEAD
    mean = _split_dot(y, ones) * inv_n
    d = y - mean
    var = _dot((d * d).astype(BF16), ones) * inv_n
    yn = d * lax.rsqrt(var + RWKV_GN_EPS) * lnw_ref[...] + lnb_ref[...]
    orw = ((yn + bonus_ref[...]) * g_ref[...]).astype(BF16)
    h1 = (x_ref[...] + _dot(od_ref[...], wo_ref[0:DIFF_WIDTH, :])
          + _dot(orw, wo_ref[DIFF_WIDTH:DIFF_WIDTH + RWKV_WIDTH, :]))
    h1_ref[...] = h1
    ms = jnp.mean(h1 * h1, axis=-1, keepdims=True)
    xn = h1 * lax.rsqrt(ms + NORM_EPS) * fw_ref[...]
    xn_ref[...] = xn.astype(xn_ref.dtype)
    logits = _split3_dot(xn, rw_ref[...]) + rb_ref[...]
    tm = logits.shape[0]
    lane = lax.broadcasted_iota(I32, (tm, LANES), 1).astype(F32)
    work = logits
    sel = jnp.zeros((tm, LANES), F32)
    idx_l = jnp.zeros((tm, LANES), F32)
    val_l = jnp.zeros((tm, LANES), F32)
    top = None
    for kslot in range(TOP_K):
        m = jnp.max(work, axis=-1, keepdims=True)
        pick = jnp.min(jnp.where(work == m, lane, float(LANES)), axis=-1, keepdims=True)
        hit = lane == pick
        sel = jnp.where(hit, 1.0, sel)
        idx_l = jnp.where(lane == kslot, pick, idx_l)
        if top is None:
            top = m
        val_l = jnp.where(lane == kslot, jnp.exp(m - top), val_l)
        work = jnp.where(hit, -jnp.inf, work)
    sel_ref[...] = sel
    idx_ref[...] = idx_l.astype(I32)
    gate_ref[...] = val_l / jnp.sum(val_l, axis=-1, keepdims=True)
    for part in range(tm // MOE_TILE):
        count = jnp.sum(sel[part * MOE_TILE:(part + 1) * MOE_TILE], axis=0, keepdims=True)
        cnt_ref[part * SUBLANES:(part + 1) * SUBLANES, :] = jnp.broadcast_to(count, (SUBLANES, LANES)).astype(I32)


def _mix(od, y, g, bonus, x2, lnw, lnb, ones, wo, fw, rw, rb):
    t, d = x2.shape
    tm = min(ROW_TILE, t)
    w = RWKV_WIDTH
    row = lambda i: (i, 0)
    fixed = lambda i: (0, 0)
    rs = lambda n: pl.BlockSpec((tm, n), row)
    return pl.pallas_call(
        _mix_kernel,
        grid=(t // tm,),
        in_specs=[rs(DIFF_WIDTH), rs(w), rs(w), rs(w), rs(d),
                  pl.BlockSpec((1, w), fixed), pl.BlockSpec((1, w), fixed),
                  pl.BlockSpec((w, w), fixed), pl.BlockSpec(wo.shape, fixed),
                  pl.BlockSpec((1, d), fixed), pl.BlockSpec(rw.shape, fixed),
                  pl.BlockSpec((1, LANES), fixed)],
        out_specs=[rs(d), rs(d), rs(LANES), rs(LANES), rs(LANES),
                   pl.BlockSpec((tm // MOE_TILE * SUBLANES, LANES), row)],
        out_shape=[jax.ShapeDtypeStruct((t, d), F32), jax.ShapeDtypeStruct((t, d), BF16),
                   jax.ShapeDtypeStruct((t, LANES), F32), jax.ShapeDtypeStruct((t, LANES), I32),
                   jax.ShapeDtypeStruct((t, LANES), F32),
                   jax.ShapeDtypeStruct((t // MOE_TILE * SUBLANES, LANES), I32)],
        compiler_params=_cparams(("parallel",)),
        name="mix_router",
    )(od, y, g, bonus, x2, lnw, lnb, ones, wo, fw, rw, rb)


def _slot_positions(sel, idx_l, lstart):
    tt = sel.shape[0]
    r = lax.broadcasted_iota(I32, (tt, tt), 0)
    c = lax.broadcasted_iota(I32, (tt, tt), 1)
    lower = (c < r).astype(BF16)
    where_to = _dot(lower, sel.astype(BF16)) + lstart
    lane = lax.broadcasted_iota(I32, (tt, LANES), 1).astype(F32)
    idx = idx_l.astype(F32)
    pos = jnp.full((tt, LANES), -1.0, F32)
    for kslot in range(TOP_K):
        e = jnp.sum(jnp.where(lane == kslot, idx, 0.0), axis=-1, keepdims=True)
        p = jnp.sum(jnp.where(lane == e, where_to, 0.0), axis=-1, keepdims=True)
        pos = jnp.where(lane == kslot, p, pos)
    return pos


def _piece(ref, q):
    return ref.at[pl.ds(pl.multiple_of(q * SUBLANES, SUBLANES), SUBLANES), :]


def _drain(count, src_ref, dst_ref, sem, max_pieces):
    b = 0
    while (1 << b) <= max_pieces:
        rows = SUBLANES << b

        @pl.when((count >> b) & 1 == 1)
        def _():
            pltpu.make_async_copy(src_ref.at[pl.ds(0, rows), :], dst_ref.at[pl.ds(0, rows), :], sem).wait()

        b += 1


def _dispatch_kernel(np_ref, fill_ref, nfill_ref, gdst_ref, sel_ref, idx_ref, lstart_ref, xn_ref,
                     pos_ref, buf_ref, xs_ref, zero_ref, sem, fill_sem, *, n_slots):
    i = pl.program_id(0)
    last = pl.num_programs(0) - 1
    slot = i % 2

    def copy(q, sl):
        return pltpu.make_async_copy(_piece(xs_ref.at[sl], q), _piece(buf_ref, gdst_ref[0, 0, q]), sem.at[sl])

    def drain(count, sl):
        _drain(count, xs_ref.at[sl], buf_ref, sem.at[sl], n_slots // SUBLANES)

    @pl.when(i >= 2)
    def _():
        drain(np_ref[jnp.maximum(i - 2, 0)], slot)

    pos = _slot_positions(sel_ref[...], idx_ref[...], lstart_ref[0:1, :].astype(F32))
    pos_ref[...] = pos
    tt = pos.shape[0]
    pos_t = pos.T.astype(I32)
    s_iota = lax.broadcasted_iota(I32, (n_slots, tt), 0)
    perm = jnp.zeros((n_slots, tt), F32)
    for kslot in range(TOP_K):
        perm = perm + jnp.where(s_iota == pos_t[kslot:kslot + 1, :], 1.0, 0.0)
    xs_ref[slot] = _pack_halves(_dot(perm.astype(BF16), xn_ref[...]))
    lax.fori_loop(0, np_ref[i], lambda q, c: (copy(q, slot).start(), c)[1], 0)

    @pl.when(i == 0)
    def _():
        zero_ref[...] = jnp.zeros_like(zero_ref)

        def fill(j):
            return pltpu.make_async_copy(zero_ref, _piece(buf_ref, fill_ref[j]), fill_sem)

        lax.fori_loop(0, nfill_ref[0], lambda j, c: (fill(j).start(), c)[1], 0)
        lax.fori_loop(0, nfill_ref[0], lambda j, c: (fill(j).wait(), c)[1], 0)

    @pl.when(i == last)
    def _():
        @pl.when(i >= 1)
        def _():
            drain(np_ref[jnp.maximum(i - 1, 0)], 1 - slot)

        drain(np_ref[i], slot)


def _dispatch(npieces, fill, nfill, gdst3, sel, idx_l, lstart8, xn, n_rows):
    t, d = xn.shape
    n_tiles, _, n_pieces = gdst3.shape
    tt = t // n_tiles
    n_slots = n_pieces * SUBLANES
    row = lambda i, *_: (i, 0)
    grid_spec = pltpu.PrefetchScalarGridSpec(
        num_scalar_prefetch=3,
        grid=(n_tiles,),
        in_specs=[pl.BlockSpec((1, 1, n_pieces), lambda i, *_: (i, 0, 0), memory_space=pltpu.SMEM),
                  pl.BlockSpec((tt, LANES), row),
                  pl.BlockSpec((tt, LANES), row),
                  pl.BlockSpec((SUBLANES, LANES), row),
                  pl.BlockSpec((tt, d), row)],
        out_specs=[pl.BlockSpec((tt, LANES), row), pl.BlockSpec(memory_space=pl.ANY)],
        scratch_shapes=[pltpu.VMEM((2, n_slots, d // 2), U32), pltpu.VMEM((SUBLANES, d // 2), U32),
                        pltpu.SemaphoreType.DMA((2,)), pltpu.SemaphoreType.DMA(())],
    )
    return pl.pallas_call(
        functools.partial(_dispatch_kernel, n_slots=n_slots),
        grid_spec=grid_spec,
        out_shape=[jax.ShapeDtypeStruct((t, LANES), F32), jax.ShapeDtypeStruct((n_rows, d // 2), U32)],
        compiler_params=_cparams(("arbitrary",)),
        name="moe_dispatch",
    )(npieces, fill, nfill, gdst3, sel, idx_l, lstart8, xn)


def _expert_kernel(be_ref, first_ref, ord_ref, next_ref, valid_ref, nused_ref, x_ref, w1_hbm, b1_ref, w2_hbm, b2_ref,
                   y_ref, w1f_ref, w2f_ref, w1b_ref, w2i_ref, w2b_ref, sem):
    i = pl.program_id(0)
    ff = w2f_ref.shape[1]
    used = i < nused_ref[0]
    new_expert = jnp.logical_and(used, first_ref[i] == 1)

    def fetch(expert, slot):
        return (pltpu.make_async_copy(w1_hbm.at[expert], w1f_ref.at[slot], sem.at[0, slot]),
                pltpu.make_async_copy(w2_hbm.at[expert], w2f_ref.at[slot], sem.at[1, slot]))

    @pl.when(i == 0)
    def _():
        for cp in fetch(be_ref[0], 0):
            cp.start()

    @pl.when(jnp.logical_and(new_expert, next_ref[i] >= 0))
    def _():
        for cp in fetch(next_ref[i], 1 - (ord_ref[i] & 1)):
            cp.start()

    @pl.when(new_expert)
    def _():
        slot = ord_ref[i] & 1
        for cp in fetch(be_ref[i], slot):
            cp.wait()
        rows = w1f_ref.shape[1] // CAST_CHUNKS

        def cast1(c, carry):
            r0 = pl.multiple_of(c * rows, rows)
            w1b_ref[pl.ds(r0, rows), :] = w1f_ref[slot, pl.ds(r0, rows), :].astype(BF16)
            return carry

        lax.fori_loop(0, CAST_CHUNKS, cast1, 0)
        for g in range(w2f_ref.shape[2] // LANES):
            cols = slice(g * LANES, (g + 1) * LANES)
            w2i_ref[pl.ds(0, ff // 2, stride=2), :] = w2f_ref[slot, 0:ff // 2, cols]
            w2i_ref[pl.ds(1, ff // 2, stride=2), :] = w2f_ref[slot, ff // 2:ff, cols]
            w2b_ref[:, cols] = w2i_ref[...].astype(BF16)

    def mlp(rs):
        even = (lax.broadcasted_iota(I32, (1, LANES), 1) & 1) == 0
        x = _unpack_halves(x_ref[rs, :])
        hid = _dot(x, w1b_ref[...]) + b1_ref[0]

        def act_even(g):
            hg = hid[:, g * LANES:(g + 1) * LANES]
            glu = jnp.minimum(hg, SWIGLU_LIMIT)
            lin = jnp.clip(hg, -SWIGLU_LIMIT, SWIGLU_LIMIT) + 1.0
            return glu * jax.nn.sigmoid(SWIGLU_ALPHA * glu) * pltpu.roll(lin, LANES - 1, 1)

        half = ff // LANES
        act = jnp.concatenate(
            [jnp.where(even, act_even(g), pltpu.roll(act_even(g + half), 1, 1)) for g in range(half)], axis=1)
        y = _dot(act.astype(BF16), w2b_ref[...]) + b2_ref[0]
        y_ref[rs, :] = _pack_halves(y.astype(BF16).astype(F32))

    bm = x_ref.shape[0]
    half_rows = bm // 2
    wide = jnp.logical_and(used, valid_ref[i] > half_rows)
    narrow = jnp.logical_and(used, valid_ref[i] <= half_rows)

    @pl.when(wide)
    def _():
        mlp(slice(0, bm))

    @pl.when(narrow)
    def _():
        mlp(slice(0, half_rows))
        y_ref[half_rows:bm, :] = jnp.zeros((bm - half_rows, y_ref.shape[1]), y_ref.dtype)

    @pl.when(jnp.logical_not(used))
    def _():
        y_ref[...] = jnp.zeros_like(y_ref)


def _experts(block_e, rows_per_expert, real_rows, n_used, xbuf, w1, b1, w2, b2):
    n_rows = xbuf.shape[0]
    bm = EXPERT_BLOCK
    n_blocks = n_rows // bm
    e, d, ff2 = w1.shape
    ff = ff2 // 2
    has = rows_per_expert > 0
    ids = jnp.arange(e, dtype=I32)
    later = (ids[None, :] > ids[:, None]) & has[None, :]
    next_used = jnp.where(jnp.any(later, axis=1), jnp.argmax(later, axis=1), -1).astype(I32)
    ordinal = (jnp.cumsum(has.astype(I32)) - 1).astype(I32)
    first = jnp.concatenate([jnp.ones((1,), I32), (block_e[1:] != block_e[:-1]).astype(I32)])
    of_block = block_e[:, None] == ids[None, :]
    order = jnp.sum(jnp.where(of_block, ordinal[None, :], 0), axis=1).astype(I32)
    next_e = jnp.sum(jnp.where(of_block, next_used[None, :], 0), axis=1).astype(I32)
    data_end = jnp.cumsum(rows_per_expert) - rows_per_expert + real_rows
    block_end = jnp.sum(jnp.where(of_block, data_end[None, :], 0), axis=1)
    valid = jnp.clip(block_end - jnp.arange(n_blocks, dtype=I32) * bm, 0, bm).astype(I32)
    blk = lambda i, be, fi, od, ne, va, nu: (jnp.maximum(jnp.minimum(i, nu[0] - 1), 0), 0)
    ex3 = lambda i, be, fi, od, ne, va, nu: (be[i], 0, 0)
    grid_spec = pltpu.PrefetchScalarGridSpec(
        num_scalar_prefetch=6,
        grid=(n_blocks,),
        in_specs=[pl.BlockSpec((bm, d // 2), blk),
                  pl.BlockSpec(memory_space=pl.ANY),
                  pl.BlockSpec((1, 1, ff2), ex3),
                  pl.BlockSpec(memory_space=pl.ANY),
                  pl.BlockSpec((1, 1, d), ex3)],
        out_specs=pl.BlockSpec((bm, d // 2), lambda i, be, fi, od, ne, va, nu: (i, 0)),
        scratch_shapes=[pltpu.VMEM((2, d, ff2), F32), pltpu.VMEM((2, ff, d), F32),
                        pltpu.VMEM((d, ff2), BF16), pltpu.VMEM((ff, LANES), F32), pltpu.VMEM((ff, d), BF16),
                        pltpu.SemaphoreType.DMA((2, 2))],
    )
    return pl.pallas_call(
        _expert_kernel,
        grid_spec=grid_spec,
        out_shape=jax.ShapeDtypeStruct((n_rows, d // 2), U32),
        compiler_params=_cparams(("arbitrary",)),
        name="moe_experts",
    )(block_e, first, order, next_e, valid, n_used, xbuf, w1, b1, w2, b2)


def _combine_kernel(np_ref, gdst_ref, gnext_ref, pos_ref, gate_ref, h1_ref, fw_ref, ybuf_ref, o_ref, ys_ref, sem,
                    *, n_slots):
    i = pl.program_id(0)
    last = pl.num_programs(0) - 1
    slot = i % 2

    def fetch(table_ref, tile, sl):
        count = np_ref[tile]

        def copy(q):
            return pltpu.make_async_copy(_piece(ybuf_ref, table_ref[0, 0, q]), _piece(ys_ref.at[sl], q), sem.at[sl])

        lax.fori_loop(0, count, lambda q, c: (copy(q).start(), c)[1], 0)

        def zero(q, carry):
            _piece(ys_ref.at[sl], q)[...] = jnp.zeros((SUBLANES, ys_ref.shape[2]), ys_ref.dtype)
            return carry

        lax.fori_loop(count, n_slots // SUBLANES, zero, 0)

    @pl.when(i == 0)
    def _():
        fetch(gdst_ref, i, slot)

    @pl.when(i < last)
    def _():
        fetch(gnext_ref, jnp.minimum(i + 1, last), 1 - slot)

    n = np_ref[i]
    pos = pos_ref[...].astype(I32)
    gate = gate_ref[...]
    tt = pos.shape[0]
    s_iota = lax.broadcasted_iota(I32, (tt, n_slots), 1)
    weight = jnp.zeros((tt, n_slots), F32)
    for kslot in range(TOP_K):
        weight = weight + jnp.where(s_iota == pos[:, kslot:kslot + 1], gate[:, kslot:kslot + 1], 0.0)
    _drain(n, ybuf_ref, ys_ref.at[slot], sem.at[slot], n_slots // SUBLANES)
    h = h1_ref[...] + _dot(weight.astype(BF16), _unpack_halves(ys_ref[slot]))
    ms = jnp.mean(h * h, axis=-1, keepdims=True)
    o_ref[...] = h * lax.rsqrt(ms + NORM_EPS) * fw_ref[...]


def _combine(npieces, gdst3, pos_l, gate_l, h1, fw, ybuf):
    t, d = h1.shape
    n_tiles, _, n_pieces = gdst3.shape
    tt = t // n_tiles
    n_slots = n_pieces * SUBLANES
    row = lambda i, *_: (i, 0)
    grid_spec = pltpu.PrefetchScalarGridSpec(
        num_scalar_prefetch=1,
        grid=(n_tiles,),
        in_specs=[pl.BlockSpec((1, 1, n_pieces), lambda i, *_: (i, 0, 0), memory_space=pltpu.SMEM),
                  pl.BlockSpec((1, 1, n_pieces), lambda i, *_: (jnp.minimum(i + 1, n_tiles - 1), 0, 0),
                               memory_space=pltpu.SMEM),
                  pl.BlockSpec((tt, LANES), row),
                  pl.BlockSpec((tt, LANES), row),
                  pl.BlockSpec((tt, d), row),
                  pl.BlockSpec((1, d), lambda i, *_: (0, 0)),
                  pl.BlockSpec(memory_space=pl.ANY)],
        out_specs=pl.BlockSpec((tt, d), row),
        scratch_shapes=[pltpu.VMEM((2, n_slots, d // 2), U32), pltpu.SemaphoreType.DMA((2,))],
    )
    return pl.pallas_call(
        functools.partial(_combine_kernel, n_slots=n_slots),
        grid_spec=grid_spec,
        out_shape=jax.ShapeDtypeStruct((t, d), F32),
        compiler_params=_cparams(("arbitrary",)),
        name="moe_combine",
    )(npieces, gdst3, gdst3, pos_l, gate_l, h1, fw, ybuf)


def _rotary_tables(positions):
    half = ROT_DIM // 2
    inv_freq = ROPE_THETA ** (-jnp.arange(0, ROT_DIM, 2, dtype=F32) / ROT_DIM)
    ang = positions.astype(F32).reshape(-1, 1) * inv_freq
    cos = jnp.tile(jnp.cos(ang), (1, LANES // half))
    sin = jnp.tile(jnp.sin(ang), (1, LANES // half))
    dim = np.arange(LANES)[None, :] % DIFF_HEAD_DIM
    ctab = jnp.where(dim < ROT_DIM, cos, 1.0)
    satab = jnp.where(dim < half, -sin, 0.0)
    sbtab = jnp.where((dim >= half) & (dim < ROT_DIM), sin, 0.0)
    return ctab, satab, sbtab


def _pad_rows(a, rows):
    return jnp.concatenate([a, jnp.zeros((rows - a.shape[0],) + a.shape[1:], a.dtype)], axis=0)


def _layer(h, l, tabs, attn_norm_w, w_in, diff_lambda_q1, diff_lambda_k1, diff_lambda_q2, diff_lambda_k2,
           diff_subln_w, rwkv_mu, rwkv_w0, rwkv_w_up, rwkv_a0, rwkv_a_up, rwkv_g_up, rwkv_k_k, rwkv_k_a,
           rwkv_r_k, rwkv_ln_w, rwkv_ln_b, w_out, ffn_norm_w, router_w, router_b, exp_w1, exp_b1,
           exp_w2, exp_b2, final_w):
    bsz, seq, d = h.shape
    t = bsz * seq
    w = RWKV_WIDTH
    lambda_init = 0.8 - 0.6 * math.exp(-0.3 * l)
    x2 = h.reshape(t, d)
    row1 = lambda a: a.reshape(1, -1).astype(F32)

    wi = w_in[l]
    qkv_cols = 3 * DIFF_WIDTH
    wq = wi[:, :qkv_cols].astype(BF16)
    o = qkv_cols + 3 * w
    zcol = lambda n: jnp.zeros((d, n), wi.dtype)
    wr = jnp.concatenate([
        wi[:, qkv_cols:o],
        wi[:, o:o + DECAY_LORA], zcol(LORA_PAD - DECAY_LORA),
        wi[:, o + DECAY_LORA:o + DECAY_LORA + AAA_LORA], zcol(LORA_PAD - AAA_LORA),
        wi[:, o + DECAY_LORA + AAA_LORA:], zcol(LORA_PAD - GATE_LORA)], axis=1).astype(BF16)
    mu = rwkv_mu[l]
    zv = lambda n: jnp.zeros((n,), mu.dtype)
    mu_p = jnp.concatenate([
        mu[:3 * w],
        mu[3 * w:3 * w + DECAY_LORA], zv(LORA_PAD - DECAY_LORA),
        mu[3 * w + DECAY_LORA:3 * w + DECAY_LORA + AAA_LORA], zv(LORA_PAD - AAA_LORA),
        mu[3 * w + DECAY_LORA + AAA_LORA:], zv(LORA_PAD - GATE_LORA)]).reshape(1, -1)

    head = np.arange(w) // RWKV_HEAD
    ones = jnp.asarray(head[:, None] == head[None, :], BF16)
    q, k, v, r, lw, kmod, vv, kk, bb, g, bonus = _inproj(
        x2, row1(attn_norm_w[l]), wq, wr, *tabs,
        mu_p, row1(rwkv_w0[l]), row1(rwkv_a0[l]), row1(rwkv_k_k[l]), row1(rwkv_k_a[l]),
        row1(rwkv_r_k[l]), _pad_rows(rwkv_w_up[l].astype(BF16), LORA_PAD),
        _pad_rows(rwkv_a_up[l].astype(BF16), LORA_PAD), _pad_rows(rwkv_g_up[l].astype(BF16), LORA_PAD),
        ones, seq)

    od = _attn(q.reshape(bsz, seq, -1), k.reshape(bsz, seq, -1), v.reshape(bsz, seq, -1),
               row1(diff_lambda_q1[l]), row1(diff_lambda_k1[l]), row1(diff_lambda_q2[l]),
               row1(diff_lambda_k2[l]), row1(diff_subln_w[l]), lambda_init)

    s3 = lambda a: a.reshape(bsz, seq, w)
    y = _scan(s3(r), s3(lw), s3(kmod), s3(vv), s3(kk), s3(bb)).reshape(t, w)

    n_e = router_w.shape[-1]
    rw = jnp.concatenate([router_w[l].astype(F32), jnp.zeros((d, LANES - n_e), F32)], axis=1)
    rw_hi = rw.astype(BF16)
    rw = jnp.concatenate([rw_hi, (rw - rw_hi.astype(F32)).astype(BF16)], axis=1)
    rb = jnp.concatenate([router_b[l].astype(F32), jnp.full((LANES - n_e,), -1e30, F32)]).reshape(1, -1)
    h1, xn, sel, idx_l, gate_l, cnt8 = _mix(
        od.reshape(t, -1), y, g, bonus, x2, row1(rwkv_ln_w[l]), row1(rwkv_ln_b[l]), ones,
        w_out[l].astype(BF16), row1(ffn_norm_w[l]), rw, rb)

    bm = EXPERT_BLOCK
    pc = SUBLANES
    tm = MOE_TILE
    n_tiles = t // tm
    cnt = cnt8.reshape(n_tiles, SUBLANES, LANES)[:, 0, :n_e]
    seg = (cnt + pc - 1) // pc * pc
    lend = jnp.cumsum(seg, axis=1)
    lstart = lend - seg
    rows_e = jnp.sum(seg, axis=0)
    padded = (rows_e + bm - 1) // bm * bm
    pad_ends = jnp.cumsum(padded)
    gstart = (pad_ends - padded)[None, :] + jnp.cumsum(seg, axis=0) - seg
    n_slots = tm * TOP_K + n_e * pc
    n_pieces = n_slots // pc
    n_blocks = -(-(t * TOP_K + n_tiles * n_e * (pc - 1) + n_e * (bm - pc)) // bm)
    n_rows = n_blocks * bm
    piece_row = jnp.arange(n_pieces, dtype=I32) * pc
    piece_e = jnp.minimum(jnp.sum(lend[:, None, :] <= piece_row[None, :, None], axis=-1), n_e - 1)
    pick = piece_e[:, :, None] == jnp.arange(n_e, dtype=I32)[None, None, :]
    take = lambda a: jnp.sum(jnp.where(pick, a[:, None, :], 0), axis=-1)
    gdst = ((take(gstart) + piece_row[None, :] - take(lstart)) // pc).astype(I32)
    gdst3 = jnp.clip(gdst, 0, n_rows // pc - 1).reshape(n_tiles, 1, n_pieces)
    npieces = (lend[:, -1] // pc).astype(I32)
    lstart8 = jnp.zeros((n_tiles, SUBLANES, LANES), I32).at[:, :, :n_e].set(lstart[:, None, :])
    lstart8 = lstart8.reshape(n_tiles * SUBLANES, LANES)
    gap_start = jnp.concatenate([pad_ends - padded + rows_e, pad_ends[-1:]]) // pc
    gap_len = jnp.concatenate([padded - rows_e, n_rows - pad_ends[-1:]]) // pc
    gap_end = jnp.cumsum(gap_len)
    max_fill = n_e * (bm // pc - 1) + (n_rows - t * TOP_K) // pc
    j = jnp.arange(max_fill, dtype=I32)
    gap = jnp.minimum(jnp.sum(gap_end[None, :] <= j[:, None], axis=1), n_e)
    in_gap = gap[:, None] == jnp.arange(n_e + 1, dtype=I32)[None, :]
    shift = jnp.sum(jnp.where(in_gap, (gap_start - gap_end + gap_len)[None, :], 0), axis=1)
    fill = jnp.clip(j + shift, 0, n_rows // pc - 1).astype(I32)
    nfill = gap_end[-1:].astype(I32)
    first_row = jnp.arange(n_blocks, dtype=I32) * bm
    block_e = jnp.minimum(jnp.sum(pad_ends[None, :] <= first_row[:, None], axis=1), n_e - 1).astype(I32)
    n_used = (pad_ends[-1] // bm).astype(I32).reshape(1)

    pos_l, xbuf = _dispatch(npieces, fill, nfill, gdst3, sel, idx_l, lstart8, xn, n_rows)

    ff2 = exp_w1.shape[-1]
    b1 = exp_b1[l].astype(F32).reshape(n_e, 1, ff2)
    b2 = exp_b2[l].astype(F32).reshape(n_e, 1, d)
    ybuf = _experts(block_e, padded, rows_e, n_used, xbuf, exp_w1[l].astype(F32), b1, exp_w2[l].astype(F32), b2)

    out = _combine(npieces, gdst3, pos_l, gate_l, h1, row1(final_w), ybuf)
    return out.reshape(bsz, seq, d)


def kernel(x, positions, attn_norm_w, w_in, diff_lambda_q1, diff_lambda_k1, diff_lambda_q2, diff_lambda_k2, diff_subln_w, rwkv_mu, rwkv_w0, rwkv_w_up, rwkv_a0, rwkv_a_up, rwkv_g_up, rwkv_k_k, rwkv_k_a, rwkv_r_k, rwkv_ln_w, rwkv_ln_b, w_out, ffn_norm_w, router_w, router_b, exp_w1, exp_b1, exp_w2, exp_b2, final_norm_w):
    depth = w_in.shape[0]
    assert depth == 1, "the final norm is fused into the last (only) layer's combine kernel"
    tabs = _rotary_tables(positions)
    return _layer(x, 0, tabs, attn_norm_w, w_in, diff_lambda_q1, diff_lambda_k1, diff_lambda_q2,
                  diff_lambda_k2, diff_subln_w, rwkv_mu, rwkv_w0, rwkv_w_up, rwkv_a0, rwkv_a_up, rwkv_g_up,
                  rwkv_k_k, rwkv_k_a, rwkv_r_k, rwkv_ln_w, rwkv_ln_b, w_out, ffn_norm_w, router_w, router_b,
                  exp_w1, exp_b1, exp_w2, exp_b2, final_norm_w)
```

```python
import functools
import math

import jax
import jax.numpy as jnp
import numpy as np
from jax import lax
from jax.experimental import pallas as pl
from jax.experimental.pallas import tpu as pltpu

F32 = jnp.float32
BF16 = jnp.bfloat16
I32 = jnp.int32
U32 = jnp.uint32

DIFF_HEAD_DIM = 64
DIFF_V_DIM = 128
DIFF_HEADS = 4
DIFF_WIDTH = DIFF_HEADS * DIFF_V_DIM
ROT_DIM = 16
ROPE_THETA = 500000.0
RWKV_HEAD = 64
RWKV_HEADS = 8
RWKV_WIDTH = RWKV_HEAD * RWKV_HEADS
DECAY_LORA = 32
AAA_LORA = 32
GATE_LORA = 96
TOP_K = 4
SWIGLU_LIMIT = 7.0
SWIGLU_ALPHA = 1.702
NORM_EPS = 1e-5
RWKV_GN_EPS = 64e-5

LANES = 128
SUBLANES = 8
VMEM_LIMIT = 56 * 1024 * 1024

ROW_TILE = 512
SUB_TILES = 2
ATTN_TILE = 512
CHUNK = 64
SCAN_TILE = 512
SCAN_SEQS = 2
SCAN_BATCH = 4
GROUP = 4 * RWKV_HEAD
EXPERT_BLOCK = 512
MOE_TILE = 512
CAST_CHUNKS = 8
LORA_PAD = LANES
ZR_COLS = 3 * RWKV_WIDTH + 3 * LORA_PAD


def _cparams(sem):
    return pltpu.CompilerParams(dimension_semantics=sem, vmem_limit_bytes=VMEM_LIMIT)


def _nt(a, b):
    return lax.dot_general(a, b, (((1,), (1,)), ((), ())), preferred_element_type=F32)


def _tn(a, b):
    return lax.dot_general(a, b, (((0,), (0,)), ((), ())), preferred_element_type=F32)


def _dot(a, b):
    return jnp.dot(a, b, preferred_element_type=F32)


def _pack_halves(x):
    n = x.shape[1] // 2
    bits = lax.bitcast_convert_type(x, U32)
    return (bits[:, :n] & jnp.uint32(0xFFFF0000)) | (bits[:, n:] >> 16)


def _unpack_halves(p):
    hi = lax.bitcast_convert_type(p & jnp.uint32(0xFFFF0000), F32)
    lo = lax.bitcast_convert_type(p << 16, F32)
    return jnp.concatenate([hi, lo], axis=1).astype(BF16)


def _split(x):
    hi = x.astype(BF16)
    return hi, (x - hi.astype(F32)).astype(BF16)


def _split_dot(x, w_bf16):
    hi, lo = _split(x)
    return _dot(hi, w_bf16) + _dot(lo, w_bf16)


def _split3_dot(x, w_parts):
    n = w_parts.shape[1] // 2
    hi, lo = _split(x)
    both = _dot(hi, w_parts)
    return both[:, :n] + (both[:, n:] + _dot(lo, w_parts[:, :n]))


def _inproj_kernel(x_ref, nw_ref, wq_ref, wr_ref, c_ref, sa_ref, sb_ref,
                   mu_ref, w0_ref, a0_ref, kk_ref, ka_ref, rk_ref, wup_ref, aup_ref, gup_ref, ones_ref,
                   q_ref, k_ref, v_ref, r_ref, lw_ref, rk_out_ref, rv_ref, kkn_ref, b_ref, g_ref, bonus_ref,
                   prev_ref, *, tiles_per_seq):
    i = pl.program_id(0)

    @pl.when(i % tiles_per_seq == 0)
    def _():
        prev_ref[...] = jnp.zeros_like(prev_ref)

    tm = x_ref.shape[0]
    sub = tm // SUB_TILES
    scale = DIFF_HEAD_DIM ** -0.5
    w = RWKV_WIDTH
    ones = ones_ref[...]
    rows = lax.broadcasted_iota(I32, (sub, 1), 0)
    prev = prev_ref[SUBLANES - 1:SUBLANES, :]
    for part in range(SUB_TILES):
        rs = slice(part * sub, (part + 1) * sub)
        x = x_ref[rs, :]
        ms = jnp.mean(x * x, axis=-1, keepdims=True)
        u = (x * lax.rsqrt(ms + NORM_EPS) * nw_ref[...]).astype(BF16)
        zq = _dot(u, wq_ref[...])
        c = c_ref[rs, :]
        sa = sa_ref[rs, :]
        sb = sb_ref[rs, :]
        for g in range(2 * DIFF_HEADS):
            zg = zq[:, g * LANES:(g + 1) * LANES]
            rot = zg * c + pltpu.roll(zg, LANES - ROT_DIM // 2, 1) * sa + pltpu.roll(zg, ROT_DIM // 2, 1) * sb
            if g < DIFF_HEADS:
                q_ref[rs, g * LANES:(g + 1) * LANES] = (rot * scale).astype(BF16)
            else:
                h = g - DIFF_HEADS
                k_ref[rs, h * LANES:(h + 1) * LANES] = rot.astype(BF16)
        v_ref[rs, :] = zq[:, 2 * DIFF_WIDTH:3 * DIFF_WIDTH].astype(BF16)

        z = _dot(u, wr_ref[...])
        shifted = jnp.where(rows == 0, prev, pltpu.roll(z, 1, 0))
        prev = z[sub - 1:sub, :]
        if part == SUB_TILES - 1:
            prev_ref[...] = z[sub - SUBLANES:sub, :]
        zf = z + mu_ref[...] * (shifted - z)
        r = zf[:, 0:w]
        k = zf[:, w:2 * w]
        v = zf[:, 2 * w:3 * w]
        wd = zf[:, 3 * w:3 * w + LORA_PAD]
        ad = zf[:, 3 * w + LORA_PAD:3 * w + 2 * LORA_PAD]
        gd = zf[:, 3 * w + 2 * LORA_PAD:3 * w + 3 * LORA_PAD]
        pre = w0_ref[...] + _split_dot(jnp.tanh(wd), wup_ref[...])
        neg = -pre
        softplus = jnp.maximum(neg, 0.0) + jnp.log(1.0 + jnp.exp(-jnp.abs(neg)))
        wlog = -softplus - 0.5
        lw_ref[rs, :] = -jnp.exp(wlog)
        a = jax.nn.sigmoid(a0_ref[...] + _split_dot(ad, aup_ref[...]))
        g_ref[rs, :] = _split_dot(jax.nn.sigmoid(gd), gup_ref[...]).astype(g_ref.dtype)
        kk = k * kk_ref[...]
        norm = jnp.sqrt(_dot((kk * kk).astype(BF16), ones))
        kk = kk / jnp.maximum(norm, 1e-12)
        k = k * (1.0 + (a - 1.0) * ka_ref[...])
        r_ref[rs, :] = r.astype(r_ref.dtype)
        rk_out_ref[rs, :] = k.astype(rk_out_ref.dtype)
        rv_ref[rs, :] = v.astype(rv_ref.dtype)
        kkn_ref[rs, :] = kk.astype(kkn_ref.dtype)
        b_ref[rs, :] = (kk * a).astype(b_ref.dtype)
        bonus_ref[rs, :] = (_dot((r * k * rk_ref[...]).astype(BF16), ones) * v).astype(bonus_ref.dtype)


def _inproj(x2, nw, wq, wr, ctab, satab, sbtab, mu_p, w0, a0, k_k, k_a, rk, wup, aup, gup, ones, seq):
    t, d = x2.shape
    tm = min(ROW_TILE, seq)
    w = RWKV_WIDTH
    row = lambda i: (i, 0)
    fixed = lambda i: (0, 0)
    vecw = pl.BlockSpec((1, w), fixed)
    lora = pl.BlockSpec((LORA_PAD, w), fixed)
    attn_out = pl.BlockSpec((tm, DIFF_WIDTH), row)
    feat_out = pl.BlockSpec((tm, w), row)
    feat = lambda dt: jax.ShapeDtypeStruct((t, w), dt)
    return pl.pallas_call(
        functools.partial(_inproj_kernel, tiles_per_seq=seq // tm),
        grid=(t // tm,),
        in_specs=[
            pl.BlockSpec((tm, d), row),
            pl.BlockSpec((1, d), fixed),
            pl.BlockSpec(wq.shape, fixed),
            pl.BlockSpec(wr.shape, fixed),
            pl.BlockSpec((tm, LANES), row),
            pl.BlockSpec((tm, LANES), row),
            pl.BlockSpec((tm, LANES), row),
            pl.BlockSpec((1, ZR_COLS), fixed),
            vecw, vecw, vecw, vecw, vecw, lora, lora, lora,
            pl.BlockSpec((w, w), fixed),
        ],
        out_specs=[attn_out] * 3 + [feat_out] * 8,
        out_shape=[jax.ShapeDtypeStruct((t, DIFF_WIDTH), BF16)] * 3
        + [feat(BF16), feat(F32), feat(BF16), feat(BF16), feat(BF16), feat(BF16), feat(BF16), feat(BF16)],
        scratch_shapes=[pltpu.VMEM((SUBLANES, ZR_COLS), F32)],
        compiler_params=_cparams(("arbitrary",)),
        name="inproj",
    )(x2, nw, wq, wr, ctab, satab, sbtab, mu_p, w0, a0, k_k, k_a, rk, wup, aup, gup, ones)


def _attn_kernel(q_ref, k_ref, v_ref, lq1_ref, lk1_ref, lq2_ref, lk2_ref, sw_ref, o_ref,
                 *, tile, lambda_init):
    lane = lax.broadcasted_iota(I32, (1, LANES), 1)
    first = lane < DIFF_HEAD_DIM
    neg = -1e30
    lam = (jnp.exp(jnp.sum(lq1_ref[...] * lk1_ref[...], axis=-1, keepdims=True))
           - jnp.exp(jnp.sum(lq2_ref[...] * lk2_ref[...], axis=-1, keepdims=True)) + lambda_init)
    r = lax.broadcasted_iota(I32, (tile, tile), 0)
    c = lax.broadcasted_iota(I32, (tile, tile), 1)
    keep = c <= r
    for i in range(q_ref.shape[1] // tile):
        q = q_ref[0, i * tile:(i + 1) * tile, :]
        zero = jnp.zeros_like(q)
        n = (i + 1) * tile
        kb = k_ref[0, 0:n, :]
        vb = v_ref[0, 0:n, :]
        outs = []
        for qm in (jnp.where(first, q, zero), jnp.where(first, zero, q)):
            s = _nt(qm, kb)
            diag = jnp.where(keep, s[:, i * tile:], neg)
            s = diag if i == 0 else jnp.concatenate([s[:, :i * tile], diag], axis=1)
            p = jnp.exp(s - jnp.max(s, axis=-1, keepdims=True))
            outs.append(_dot(p.astype(BF16), vb) / jnp.sum(p, axis=-1, keepdims=True))
        o = outs[0] - lam * outs[1]
        ms = jnp.mean(o * o, axis=-1, keepdims=True)
        o = o * lax.rsqrt(ms + NORM_EPS) * sw_ref[...] * (1.0 - lambda_init)
        o_ref[0, i * tile:(i + 1) * tile, :] = o.astype(o_ref.dtype)


def _attn(q3, k3, v3, lq1, lk1, lq2, lk2, sw, lambda_init):
    b, s, _ = q3.shape
    tile = min(ATTN_TILE, s)
    spec = pl.BlockSpec((1, s, LANES), lambda bi, h: (bi, 0, h))
    vec = lambda n: pl.BlockSpec((1, n), lambda bi, h: (0, 0))
    return pl.pallas_call(
        functools.partial(_attn_kernel, tile=tile, lambda_init=lambda_init),
        grid=(b, DIFF_HEADS),
        in_specs=[spec, spec, spec, vec(DIFF_HEAD_DIM), vec(DIFF_HEAD_DIM), vec(DIFF_HEAD_DIM),
                  vec(DIFF_HEAD_DIM), vec(DIFF_V_DIM)],
        out_specs=spec,
        out_shape=jax.ShapeDtypeStruct((b, s, DIFF_WIDTH), BF16),
        compiler_params=_cparams(("parallel", "parallel")),
        name="attn",
    )(q3, k3, v3, lq1, lk1, lq2, lk2, sw)


def _scan_kernel(r_ref, lw_ref, k_ref, v_ref, kk_ref, b_ref, y_ref, state_ref,
                 wr_s, ut_s, utt_s, arb_s, pv_s, bh_s, vk_s, wt_s, *, n_chunks):
    L = CHUNK
    G = GROUP
    n_groups = RWKV_WIDTH // G
    n_seqs = r_ref.shape[0]

    @pl.when(pl.program_id(1) == 0)
    def _():
        state_ref[...] = jnp.zeros_like(state_ref)

    row = lax.broadcasted_iota(I32, (L, G), 0)
    colr = lax.broadcasted_iota(I32, (L, G), 1) & (L - 1)
    strict = (colr < row).astype(F32)
    incl = (colr <= row).astype(F32)
    eye = (colr == row).astype(F32)
    eye_l = (lax.broadcasted_iota(I32, (L, L), 0) == lax.broadcasted_iota(I32, (L, L), 1)).astype(BF16)
    br = lax.broadcasted_iota(I32, (G, G), 0) >> 6
    bc = lax.broadcasted_iota(I32, (G, G), 1) >> 6
    block = (br == bc).astype(F32)
    block_bf = block.astype(BF16)
    rows1 = lax.broadcasted_iota(I32, (L, 1), 0)

    def stack4(x):
        xb = x.astype(BF16)
        return jnp.concatenate([xb, xb, xb, xb], axis=0) * block_bf

    def cat(a, b):
        return jnp.concatenate([a, b], axis=0).astype(BF16)

    def precompute(it, carry):
        chains = [(bb, cl, g) for bb in range(n_seqs) for cl in range(SCAN_BATCH) for g in range(n_groups)]
        each = lambda f, *lists: [f(*args) for args in zip(*lists)]

        def load(ref):
            out = []
            for bb, cl, g in chains:
                base = pl.multiple_of((it * SCAN_BATCH + cl) * L, L)
                out.append(ref[bb, pl.ds(base, L), g * G:(g + 1) * G].astype(F32))
            return out

        r, lw, k, v, kk, b = load(r_ref), load(lw_ref), load(k_ref), load(v_ref), load(kk_ref), load(b_ref)

        def cumsum(x):
            sh = 1
            while sh < L:
                x = x + jnp.where(rows1 >= sh, pltpu.roll(x, sh, 0), 0.0)
                sh *= 2
            return x

        cs = each(cumsum, lw)
        tot = each(lambda c: c[L - 1:L, :], cs)
        a_hat = each(lambda kk_, c, l: -kk_ * jnp.exp(c - l), kk, cs, lw)
        r_hat = each(lambda r_, c: r_ * jnp.exp(c), r, cs)
        w_inv = each(lambda c: jnp.exp(-c), cs)
        w_end = each(lambda t_, c: jnp.exp(t_ - c), tot, cs)
        lhs = each(cat, a_hat, r_hat)
        ab = each(lambda l_, b_, wi: _nt(l_, stack4(b_ * wi)), lhs, b, w_inv)
        ak = each(lambda l_, k_, wi: _nt(l_, stack4(k_ * wi)), lhs, k, w_inv)
        a_ab = each(lambda x: x[:L] * strict, ab)
        a_rb = each(lambda x: (x[L:] * incl).astype(BF16), ab)
        a_k = each(lambda x: cat(x[:L] * strict, x[L:] * incl), ak)
        t_mat = each(lambda a: eye + a, a_ab)
        p_mat = each(lambda a: _dot(a.astype(BF16), stack4(a)), a_ab)
        for _ in range(4):
            tp = each(lambda t_, p_: _dot(cat(t_, p_), stack4(p_)), t_mat, p_mat)
            t_mat = each(lambda t_, x: t_ + x[:L], t_mat, tp)
            p_mat = each(lambda x: x[L:], tp)
        t_bf = each(lambda t_, p_: (t_ + _dot(t_.astype(BF16), stack4(p_))).astype(BF16), t_mat, p_mat)
        av = each(lambda a, v_: _dot(a, stack4(v_)), a_k, v)
        w_til = each(lambda t_, a: _dot(t_, stack4(a)), t_bf, a_hat)
        u_til = each(lambda t_, x: _dot(t_, stack4(x[:L])), t_bf, av)
        u_til_t = each(lambda u_: _tn(u_.astype(BF16), eye_l), u_til)
        vk = each(lambda v_, k_, we: _tn(v_.astype(BF16), (k_ * we).astype(BF16)) * block, v, k, w_end)
        for n, (bb, cl, g) in enumerate(chains):
            slot = ((it * SCAN_BATCH + cl) * n_seqs + bb) * n_groups + g
            wr_s[slot] = cat(w_til[n], r_hat[n])
            ut_s[slot] = u_til[n]
            utt_s[slot] = u_til_t[n]
            arb_s[slot] = a_rb[n]
            pv_s[slot] = av[n][L:]
            bh_s[slot] = (b[n] * w_end[n]).astype(BF16)
            vk_s[slot] = vk[n]
            wt_s[slot] = jnp.broadcast_to(jnp.exp(tot[n]), (SUBLANES, G))
        return carry

    lax.fori_loop(0, n_chunks // SCAN_BATCH, precompute, 0)

    def recur(c, carry):
        gs = range(n_seqs * n_groups)
        slots = [c * n_seqs * n_groups + g for g in gs]
        s0 = [state_ref[g] for g in gs]
        s0b = [s.astype(BF16) for s in s0]
        wr = [wr_s[sl] for sl in slots]
        u_t = [_nt(s0b[g], wr[g][:L]) + utt_s[slots[g]] for g in gs]
        ub = [_dot(u_t[g].astype(BF16), bh_s[slots[g]]) for g in gs]
        for g in gs:
            state_ref[g] = s0[g] * wt_s[slots[g]][0:1, :] + ub[g] * block + vk_s[slots[g]]
        uy = [_nt(wr[g], s0b[g]) for g in gs]
        u = [uy[g][:L] + ut_s[slots[g]] for g in gs]
        base = pl.multiple_of(c * L, L)
        for g in gs:
            y = uy[g][L:] + _dot(arb_s[slots[g]], stack4(u[g])) + pv_s[slots[g]]
            lanes = slice((g % n_groups) * G, (g % n_groups + 1) * G)
            y_ref[g // n_groups, pl.ds(base, L), lanes] = y
        return carry

    lax.fori_loop(0, n_chunks, recur, 0)


def _scan(r3, lw3, k3, v3, kk3, b3):
    bsz, s, w = r3.shape
    ts = min(SCAN_TILE, s)
    n_chunks = ts // CHUNK
    nb = SCAN_SEQS if bsz % SCAN_SEQS == 0 else 1
    slots = nb * n_chunks * (w // GROUP)
    L, G = CHUNK, GROUP
    spec = pl.BlockSpec((nb, ts, w), lambda bi, i: (bi, i, 0))
    return pl.pallas_call(
        functools.partial(_scan_kernel, n_chunks=n_chunks),
        grid=(bsz // nb, s // ts),
        in_specs=[spec] * 6,
        out_specs=spec,
        out_shape=jax.ShapeDtypeStruct((bsz, s, w), F32),
        scratch_shapes=[pltpu.VMEM((nb * (w // GROUP), G, G), F32),
                        pltpu.VMEM((slots, 2 * L, G), BF16),
                        pltpu.VMEM((slots, L, G), F32),
                        pltpu.VMEM((slots, G, L), F32),
                        pltpu.VMEM((slots, L, G), BF16),
                        pltpu.VMEM((slots, L, G), F32),
                        pltpu.VMEM((slots, L, G), BF16),
                        pltpu.VMEM((slots, G, G), F32),
                        pltpu.VMEM((slots, SUBLANES, G), F32)],
        compiler_params=_cparams(("parallel", "arbitrary")),
        name="rwkv_scan",
    )(r3, lw3, k3, v3, kk3, b3)


def _mix_kernel(od_ref, y_ref, g_ref, bonus_ref, x_ref, lnw_ref, lnb_ref, ones_ref, wo_ref,
                fw_ref, rw_ref, rb_ref,
                h1_ref, xn_ref, sel_ref, idx_ref, gate_ref, cnt_ref):
    ones = ones_ref[...]
    y = y_ref[...]
    inv_n = 1.0 / RWKV_HEAD
    mean = _split_dot(y, ones) * inv_n
    d = y - mean
    var = _dot((d * d).astype(BF16), ones) * inv_n
    yn = d * lax.rsqrt(var + RWKV_GN_EPS) * lnw_ref[...] + lnb_ref[...]
    orw = ((yn + bonus_ref[...]) * g_ref[...]).astype(BF16)
    h1 = (x_ref[...] + _dot(od_ref[...], wo_ref[0:DIFF_WIDTH, :])
          + _dot(orw, wo_ref[DIFF_WIDTH:DIFF_WIDTH + RWKV_WIDTH, :]))
    h1_ref[...] = h1
    ms = jnp.mean(h1 * h1, axis=-1, keepdims=True)
    xn = h1 * lax.rsqrt(ms + NORM_EPS) * fw_ref[...]
    xn_ref[...] = xn.astype(xn_ref.dtype)
    logits = _split3_dot(xn, rw_ref[...]) + rb_ref[...]
    tm = logits.shape[0]
    lane = lax.broadcasted_iota(I32, (tm, LANES), 1).astype(F32)
    work = logits
    sel = jnp.zeros((tm, LANES), F32)
    idx_l = jnp.zeros((tm, LANES), F32)
    val_l = jnp.zeros((tm, LANES), F32)
    top = None
    for kslot in range(TOP_K):
        m = jnp.max(work, axis=-1, keepdims=True)
        pick = jnp.min(jnp.where(work == m, lane, float(LANES)), axis=-1, keepdims=True)
        hit = lane == pick
        sel = jnp.where(hit, 1.0, sel)
        idx_l = jnp.where(lane == kslot, pick, idx_l)
        if top is None:
            top = m
        val_l = jnp.where(lane == kslot, jnp.exp(m - top), val_l)
        work = jnp.where(hit, -jnp.inf, work)
    sel_ref[...] = sel
    idx_ref[...] = idx_l.astype(I32)
    gate_ref[...] = val_l / jnp.sum(val_l, axis=-1, keepdims=True)
    for part in range(tm // MOE_TILE):
        count = jnp.sum(sel[part * MOE_TILE:(part + 1) * MOE_TILE], axis=0, keepdims=True)
        cnt_ref[part * SUBLANES:(part + 1) * SUBLANES, :] = jnp.broadcast_to(count, (SUBLANES, LANES)).astype(I32)


def _mix(od, y, g, bonus, x2, lnw, lnb, ones, wo, fw, rw, rb):
    t, d = x2.shape
    tm = min(ROW_TILE, t)
    w = RWKV_WIDTH
    row = lambda i: (i, 0)
    fixed = lambda i: (0, 0)
    rs = lambda n: pl.BlockSpec((tm, n), row)
    return pl.pallas_call(
        _mix_kernel,
        grid=(t // tm,),
        in_specs=[rs(DIFF_WIDTH), rs(w), rs(w), rs(w), rs(d),
                  pl.BlockSpec((1, w), fixed), pl.BlockSpec((1, w), fixed),
                  pl.BlockSpec((w, w), fixed), pl.BlockSpec(wo.shape, fixed),
                  pl.BlockSpec((1, d), fixed), pl.BlockSpec(rw.shape, fixed),
                  pl.BlockSpec((1, LANES), fixed)],
        out_specs=[rs(d), rs(d), rs(LANES), rs(LANES), rs(LANES),
                   pl.BlockSpec((tm // MOE_TILE * SUBLANES, LANES), row)],
        out_shape=[jax.ShapeDtypeStruct((t, d), F32), jax.ShapeDtypeStruct((t, d), BF16),
                   jax.ShapeDtypeStruct((t, LANES), F32), jax.ShapeDtypeStruct((t, LANES), I32),
                   jax.ShapeDtypeStruct((t, LANES), F32),
                   jax.ShapeDtypeStruct((t // MOE_TILE * SUBLANES, LANES), I32)],
        compiler_params=_cparams(("parallel",)),
        name="mix_router",
    )(od, y, g, bonus, x2, lnw, lnb, ones, wo, fw, rw, rb)


def _slot_positions(sel, idx_l, lstart):
    tt = sel.shape[0]
    r = lax.broadcasted_iota(I32, (tt, tt), 0)
    c = lax.broadcasted_iota(I32, (tt, tt), 1)
    lower = (c < r).astype(BF16)
    where_to = _dot(lower, sel.astype(BF16)) + lstart
    lane = lax.broadcasted_iota(I32, (tt, LANES), 1).astype(F32)
    idx = idx_l.astype(F32)
    pos = jnp.full((tt, LANES), -1.0, F32)
    for kslot in range(TOP_K):
        e = jnp.sum(jnp.where(lane == kslot, idx, 0.0), axis=-1, keepdims=True)
        p = jnp.sum(jnp.where(lane == e, where_to, 0.0), axis=-1, keepdims=True)
        pos = jnp.where(lane == kslot, p, pos)
    return pos


def _piece(ref, q):
    return ref.at[pl.ds(pl.multiple_of(q * SUBLANES, SUBLANES), SUBLANES), :]


def _drain(count, src_ref, dst_ref, sem, max_pieces):
    b = 0
    while (1 << b) <= max_pieces:
        rows = SUBLANES << b

        @pl.when((count >> b) & 1 == 1)
        def _():
            pltpu.make_async_copy(src_ref.at[pl.ds(0, rows), :], dst_ref.at[pl.ds(0, rows), :], sem).wait()

        b += 1


def _dispatch_kernel(np_ref, fill_ref, nfill_ref, gdst_ref, sel_ref, idx_ref, lstart_ref, xn_ref,
                     pos_ref, buf_ref, xs_ref, zero_ref, sem, fill_sem, *, n_slots):
    i = pl.program_id(0)
    last = pl.num_programs(0) - 1
    slot = i % 2

    def copy(q, sl):
        return pltpu.make_async_copy(_piece(xs_ref.at[sl], q), _piece(buf_ref, gdst_ref[0, 0, q]), sem.at[sl])

    def drain(count, sl):
        _drain(count, xs_ref.at[sl], buf_ref, sem.at[sl], n_slots // SUBLANES)

    @pl.when(i >= 2)
    def _():
        drain(np_ref[jnp.maximum(i - 2, 0)], slot)

    pos = _slot_positions(sel_ref[...], idx_ref[...], lstart_ref[0:1, :].astype(F32))
    pos_ref[...] = pos
    tt = pos.shape[0]
    pos_t = pos.T.astype(I32)
    s_iota = lax.broadcasted_iota(I32, (n_slots, tt), 0)
    perm = jnp.zeros((n_slots, tt), F32)
    for kslot in range(TOP_K):
        perm = perm + jnp.where(s_iota == pos_t[kslot:kslot + 1, :], 1.0, 0.0)
    xs_ref[slot] = _pack_halves(_dot(perm.astype(BF16), xn_ref[...]))
    lax.fori_loop(0, np_ref[i], lambda q, c: (copy(q, slot).start(), c)[1], 0)

    @pl.when(i == 0)
    def _():
        zero_ref[...] = jnp.zeros_like(zero_ref)

        def fill(j):
            return pltpu.make_async_copy(zero_ref, _piece(buf_ref, fill_ref[j]), fill_sem)

        lax.fori_loop(0, nfill_ref[0], lambda j, c: (fill(j).start(), c)[1], 0)
        lax.fori_loop(0, nfill_ref[0], lambda j, c: (fill(j).wait(), c)[1], 0)

    @pl.when(i == last)
    def _():
        @pl.when(i >= 1)
        def _():
            drain(np_ref[jnp.maximum(i - 1, 0)], 1 - slot)

        drain(np_ref[i], slot)


def _dispatch(npieces, fill, nfill, gdst3, sel, idx_l, lstart8, xn, n_rows):
    t, d = xn.shape
    n_tiles, _, n_pieces = gdst3.shape
    tt = t // n_tiles
    n_slots = n_pieces * SUBLANES
    row = lambda i, *_: (i, 0)
    grid_spec = pltpu.PrefetchScalarGridSpec(
        num_scalar_prefetch=3,
        grid=(n_tiles,),
        in_specs=[pl.BlockSpec((1, 1, n_pieces), lambda i, *_: (i, 0, 0), memory_space=pltpu.SMEM),
                  pl.BlockSpec((tt, LANES), row),
                  pl.BlockSpec((tt, LANES), row),
                  pl.BlockSpec((SUBLANES, LANES), row),
                  pl.BlockSpec((tt, d), row)],
        out_specs=[pl.BlockSpec((tt, LANES), row), pl.BlockSpec(memory_space=pl.ANY)],
        scratch_shapes=[pltpu.VMEM((2, n_slots, d // 2), U32), pltpu.VMEM((SUBLANES, d // 2), U32),
                        pltpu.SemaphoreType.DMA((2,)), pltpu.SemaphoreType.DMA(())],
    )
    return pl.pallas_call(
        functools.partial(_dispatch_kernel, n_slots=n_slots),
        grid_spec=grid_spec,
        out_shape=[jax.ShapeDtypeStruct((t, LANES), F32), jax.ShapeDtypeStruct((n_rows, d // 2), U32)],
        compiler_params=_cparams(("arbitrary",)),
        name="moe_dispatch",
    )(npieces, fill, nfill, gdst3, sel, idx_l, lstart8, xn)


def _expert_kernel(be_ref, first_ref, ord_ref, next_ref, valid_ref, nused_ref, x_ref, w1_hbm, b1_ref, w2_hbm, b2_ref,
                   y_ref, w1f_ref, w2f_ref, w1b_ref, w2i_ref, w2b_ref, sem):
    i = pl.program_id(0)
    ff = w2f_ref.shape[1]
    used = i < nused_ref[0]
    new_expert = jnp.logical_and(used, first_ref[i] == 1)

    def fetch(expert, slot):
        return (pltpu.make_async_copy(w1_hbm.at[expert], w1f_ref.at[slot], sem.at[0, slot]),
                pltpu.make_async_copy(w2_hbm.at[expert], w2f_ref.at[slot], sem.at[1, slot]))

    @pl.when(i == 0)
    def _():
        for cp in fetch(be_ref[0], 0):
            cp.start()

    @pl.when(jnp.logical_and(new_expert, next_ref[i] >= 0))
    def _():
        for cp in fetch(next_ref[i], 1 - (ord_ref[i] & 1)):
            cp.start()

    @pl.when(new_expert)
    def _():
        slot = ord_ref[i] & 1
        for cp in fetch(be_ref[i], slot):
            cp.wait()
        rows = w1f_ref.shape[1] // CAST_CHUNKS

        def cast1(c, carry):
            r0 = pl.multiple_of(c * rows, rows)
            w1b_ref[pl.ds(r0, rows), :] = w1f_ref[slot, pl.ds(r0, rows), :].astype(BF16)
            return carry

        lax.fori_loop(0, CAST_CHUNKS, cast1, 0)
        for g in range(w2f_ref.shape[2] // LANES):
            cols = slice(g * LANES, (g + 1) * LANES)
            w2i_ref[pl.ds(0, ff // 2, stride=2), :] = w2f_ref[slot, 0:ff // 2, cols]
            w2i_ref[pl.ds(1, ff // 2, stride=2), :] = w2f_ref[slot, ff // 2:ff, cols]
            w2b_ref[:, cols] = w2i_ref[...].astype(BF16)

    def mlp(rs):
        even = (lax.broadcasted_iota(I32, (1, LANES), 1) & 1) == 0
        x = _unpack_halves(x_ref[rs, :])
        hid = _dot(x, w1b_ref[...]) + b1_ref[0]

        def act_even(g):
            hg = hid[:, g * LANES:(g + 1) * LANES]
            glu = jnp.minimum(hg, SWIGLU_LIMIT)
            lin = jnp.clip(hg, -SWIGLU_LIMIT, SWIGLU_LIMIT) + 1.0
            return glu * jax.nn.sigmoid(SWIGLU_ALPHA * glu) * pltpu.roll(lin, LANES - 1, 1)

        half = ff // LANES
        act = jnp.concatenate(
            [jnp.where(even, act_even(g), pltpu.roll(act_even(g + half), 1, 1)) for g in range(half)], axis=1)
        y = _dot(act.astype(BF16), w2b_ref[...]) + b2_ref[0]
        y_ref[rs, :] = _pack_halves(y.astype(BF16).astype(F32))

    bm = x_ref.shape[0]
    half_rows = bm // 2
    wide = jnp.logical_and(used, valid_ref[i] > half_rows)
    narrow = jnp.logical_and(used, valid_ref[i] <= half_rows)

    @pl.when(wide)
    def _():
        mlp(slice(0, bm))

    @pl.when(narrow)
    def _():
        mlp(slice(0, half_rows))
        y_ref[half_rows:bm, :] = jnp.zeros((bm - half_rows, y_ref.shape[1]), y_ref.dtype)

    @pl.when(jnp.logical_not(used))
    def _():
        y_ref[...] = jnp.zeros_like(y_ref)


def _experts(block_e, rows_per_expert, real_rows, n_used, xbuf, w1, b1, w2, b2):
    n_rows = xbuf.shape[0]
    bm = EXPERT_BLOCK
    n_blocks = n_rows // bm
    e, d, ff2 = w1.shape
    ff = ff2 // 2
    has = rows_per_expert > 0
    ids = jnp.arange(e, dtype=I32)
    later = (ids[None, :] > ids[:, None]) & has[None, :]
    next_used = jnp.where(jnp.any(later, axis=1), jnp.argmax(later, axis=1), -1).astype(I32)
    ordinal = (jnp.cumsum(has.astype(I32)) - 1).astype(I32)
    first = jnp.concatenate([jnp.ones((1,), I32), (block_e[1:] != block_e[:-1]).astype(I32)])
    of_block = block_e[:, None] == ids[None, :]
    order = jnp.sum(jnp.where(of_block, ordinal[None, :], 0), axis=1).astype(I32)
    next_e = jnp.sum(jnp.where(of_block, next_used[None, :], 0), axis=1).astype(I32)
    data_end = jnp.cumsum(rows_per_expert) - rows_per_expert + real_rows
    block_end = jnp.sum(jnp.where(of_block, data_end[None, :], 0), axis=1)
    valid = jnp.clip(block_end - jnp.arange(n_blocks, dtype=I32) * bm, 0, bm).astype(I32)
    blk = lambda i, be, fi, od, ne, va, nu: (jnp.maximum(jnp.minimum(i, nu[0] - 1), 0), 0)
    ex3 = lambda i, be, fi, od, ne, va, nu: (be[i], 0, 0)
    grid_spec = pltpu.PrefetchScalarGridSpec(
        num_scalar_prefetch=6,
        grid=(n_blocks,),
        in_specs=[pl.BlockSpec((bm, d // 2), blk),
                  pl.BlockSpec(memory_space=pl.ANY),
                  pl.BlockSpec((1, 1, ff2), ex3),
                  pl.BlockSpec(memory_space=pl.ANY),
                  pl.BlockSpec((1, 1, d), ex3)],
        out_specs=pl.BlockSpec((bm, d // 2), lambda i, be, fi, od, ne, va, nu: (i, 0)),
        scratch_shapes=[pltpu.VMEM((2, d, ff2), F32), pltpu.VMEM((2, ff, d), F32),
                        pltpu.VMEM((d, ff2), BF16), pltpu.VMEM((ff, LANES), F32), pltpu.VMEM((ff, d), BF16),
                        pltpu.SemaphoreType.DMA((2, 2))],
    )
    return pl.pallas_call(
        _expert_kernel,
        grid_spec=grid_spec,
        out_shape=jax.ShapeDtypeStruct((n_rows, d // 2), U32),
        compiler_params=_cparams(("arbitrary",)),
        name="moe_experts",
    )(block_e, first, order, next_e, valid, n_used, xbuf, w1, b1, w2, b2)


def _combine_kernel(np_ref, gdst_ref, gnext_ref, pos_ref, gate_ref, h1_ref, fw_ref, ybuf_ref, o_ref, ys_ref, sem,
                    *, n_slots):
    i = pl.program_id(0)
    last = pl.num_programs(0) - 1
    slot = i % 2

    def fetch(table_ref, tile, sl):
        count = np_ref[tile]

        def copy(q):
            return pltpu.make_async_copy(_piece(ybuf_ref, table_ref[0, 0, q]), _piece(ys_ref.at[sl], q), sem.at[sl])

        lax.fori_loop(0, count, lambda q, c: (copy(q).start(), c)[1], 0)

        def zero(q, carry):
            _piece(ys_ref.at[sl], q)[...] = jnp.zeros((SUBLANES, ys_ref.shape[2]), ys_ref.dtype)
            return carry

        lax.fori_loop(count, n_slots // SUBLANES, zero, 0)

    @pl.when(i == 0)
    def _():
        fetch(gdst_ref, i, slot)

    @pl.when(i < last)
    def _():
        fetch(gnext_ref, jnp.minimum(i + 1, last), 1 - slot)

    n = np_ref[i]
    pos = pos_ref[...].astype(I32)
    gate = gate_ref[...]
    tt = pos.shape[0]
    s_iota = lax.broadcasted_iota(I32, (tt, n_slots), 1)
    weight = jnp.zeros((tt, n_slots), F32)
    for kslot in range(TOP_K):
        weight = weight + jnp.where(s_iota == pos[:, kslot:kslot + 1], gate[:, kslot:kslot + 1], 0.0)
    _drain(n, ybuf_ref, ys_ref.at[slot], sem.at[slot], n_slots // SUBLANES)
    h = h1_ref[...] + _dot(weight.astype(BF16), _unpack_halves(ys_ref[slot]))
    ms = jnp.mean(h * h, axis=-1, keepdims=True)
    o_ref[...] = h * lax.rsqrt(ms + NORM_EPS) * fw_ref[...]


def _combine(npieces, gdst3, pos_l, gate_l, h1, fw, ybuf):
    t, d = h1.shape
    n_tiles, _, n_pieces = gdst3.shape
    tt = t // n_tiles
    n_slots = n_pieces * SUBLANES
    row = lambda i, *_: (i, 0)
    grid_spec = pltpu.PrefetchScalarGridSpec(
        num_scalar_prefetch=1,
        grid=(n_tiles,),
        in_specs=[pl.BlockSpec((1, 1, n_pieces), lambda i, *_: (i, 0, 0), memory_space=pltpu.SMEM),
                  pl.BlockSpec((1, 1, n_pieces), lambda i, *_: (jnp.minimum(i + 1, n_tiles - 1), 0, 0),
                               memory_space=pltpu.SMEM),
                  pl.BlockSpec((tt, LANES), row),
                  pl.BlockSpec((tt, LANES), row),
                  pl.BlockSpec((tt, d), row),
                  pl.BlockSpec((1, d), lambda i, *_: (0, 0)),
                  pl.BlockSpec(memory_space=pl.ANY)],
        out_specs=pl.BlockSpec((tt, d), row),
        scratch_shapes=[pltpu.VMEM((2, n_slots, d // 2), U32), pltpu.SemaphoreType.DMA((2,))],
    )
    return pl.pallas_call(
        functools.partial(_combine_kernel, n_slots=n_slots),
        grid_spec=grid_spec,
        out_shape=jax.ShapeDtypeStruct((t, d), F32),
        compiler_params=_cparams(("arbitrary",)),
        name="moe_combine",
    )(npieces, gdst3, gdst3, pos_l, gate_l, h1, fw, ybuf)


def _rotary_tables(positions):
    half = ROT_DIM // 2
    inv_freq = ROPE_THETA ** (-jnp.arange(0, ROT_DIM, 2, dtype=F32) / ROT_DIM)
    ang = positions.astype(F32).reshape(-1, 1) * inv_freq
    cos = jnp.tile(jnp.cos(ang), (1, LANES // half))
    sin = jnp.tile(jnp.sin(ang), (1, LANES // half))
    dim = np.arange(LANES)[None, :] % DIFF_HEAD_DIM
    ctab = jnp.where(dim < ROT_DIM, cos, 1.0)
    satab = jnp.where(dim < half, -sin, 0.0)
    sbtab = jnp.where((dim >= half) & (dim < ROT_DIM), sin, 0.0)
    return ctab, satab, sbtab


def _pad_rows(a, rows):
    return jnp.concatenate([a, jnp.zeros((rows - a.shape[0],) + a.shape[1:], a.dtype)], axis=0)


def _layer(h, l, tabs, attn_norm_w, w_in, diff_lambda_q1, diff_lambda_k1, diff_lambda_q2, diff_lambda_k2,
           diff_subln_w, rwkv_mu, rwkv_w0, rwkv_w_up, rwkv_a0, rwkv_a_up, rwkv_g_up, rwkv_k_k, rwkv_k_a,
           rwkv_r_k, rwkv_ln_w, rwkv_ln_b, w_out, ffn_norm_w, router_w, router_b, exp_w1, exp_b1,
           exp_w2, exp_b2, final_w):
    bsz, seq, d = h.shape
    t = bsz * seq
    w = RWKV_WIDTH
    lambda_init = 0.8 - 0.6 * math.exp(-0.3 * l)
    x2 = h.reshape(t, d)
    row1 = lambda a: a.reshape(1, -1).astype(F32)

    wi = w_in[l]
    qkv_cols = 3 * DIFF_WIDTH
    wq = wi[:, :qkv_cols].astype(BF16)
    o = qkv_cols + 3 * w
    zcol = lambda n: jnp.zeros((d, n), wi.dtype)
    wr = jnp.concatenate([
        wi[:, qkv_cols:o],
        wi[:, o:o + DECAY_LORA], zcol(LORA_PAD - DECAY_LORA),
        wi[:, o + DECAY_LORA:o + DECAY_LORA + AAA_LORA], zcol(LORA_PAD - AAA_LORA),
        wi[:, o + DECAY_LORA + AAA_LORA:], zcol(LORA_PAD - GATE_LORA)], axis=1).astype(BF16)
    mu = rwkv_mu[l]
    zv = lambda n: jnp.zeros((n,), mu.dtype)
    mu_p = jnp.concatenate([
        mu[:3 * w],
        mu[3 * w:3 * w + DECAY_LORA], zv(LORA_PAD - DECAY_LORA),
        mu[3 * w + DECAY_LORA:3 * w + DECAY_LORA + AAA_LORA], zv(LORA_PAD - AAA_LORA),
        mu[3 * w + DECAY_LORA + AAA_LORA:], zv(LORA_PAD - GATE_LORA)]).reshape(1, -1)

    head = np.arange(w) // RWKV_HEAD
    ones = jnp.asarray(head[:, None] == head[None, :], BF16)
    q, k, v, r, lw, kmod, vv, kk, bb, g, bonus = _inproj(
        x2, row1(attn_norm_w[l]), wq, wr, *tabs,
        mu_p, row1(rwkv_w0[l]), row1(rwkv_a0[l]), row1(rwkv_k_k[l]), row1(rwkv_k_a[l]),
        row1(rwkv_r_k[l]), _pad_rows(rwkv_w_up[l].astype(BF16), LORA_PAD),
        _pad_rows(rwkv_a_up[l].astype(BF16), LORA_PAD), _pad_rows(rwkv_g_up[l].astype(BF16), LORA_PAD),
        ones, seq)

    od = _attn(q.reshape(bsz, seq, -1), k.reshape(bsz, seq, -1), v.reshape(bsz, seq, -1),
               row1(diff_lambda_q1[l]), row1(diff_lambda_k1[l]), row1(diff_lambda_q2[l]),
               row1(diff_lambda_k2[l]), row1(diff_subln_w[l]), lambda_init)

    s3 = lambda a: a.reshape(bsz, seq, w)
    y = _scan(s3(r), s3(lw), s3(kmod), s3(vv), s3(kk), s3(bb)).reshape(t, w)

    n_e = router_w.shape[-1]
    rw = jnp.concatenate([router_w[l].astype(F32), jnp.zeros((d, LANES - n_e), F32)], axis=1)
    rw_hi = rw.astype(BF16)
    rw = jnp.concatenate([rw_hi, (rw - rw_hi.astype(F32)).astype(BF16)], axis=1)
    rb = jnp.concatenate([router_b[l].astype(F32), jnp.full((LANES - n_e,), -1e30, F32)]).reshape(1, -1)
    h1, xn, sel, idx_l, gate_l, cnt8 = _mix(
        od.reshape(t, -1), y, g, bonus, x2, row1(rwkv_ln_w[l]), row1(rwkv_ln_b[l]), ones,
        w_out[l].astype(BF16), row1(ffn_norm_w[l]), rw, rb)

    bm = EXPERT_BLOCK
    pc = SUBLANES
    tm = MOE_TILE
    n_tiles = t // tm
    cnt = cnt8.reshape(n_tiles, SUBLANES, LANES)[:, 0, :n_e]
    seg = (cnt + pc - 1) // pc * pc
    lend = jnp.cumsum(seg, axis=1)
    lstart = lend - seg
    rows_e = jnp.sum(seg, axis=0)
    padded = (rows_e + bm - 1) // bm * bm
    pad_ends = jnp.cumsum(padded)
    gstart = (pad_ends - padded)[None, :] + jnp.cumsum(seg, axis=0) - seg
    n_slots = tm * TOP_K + n_e * pc
    n_pieces = n_slots // pc
    n_blocks = -(-(t * TOP_K + n_tiles * n_e * (pc - 1) + n_e * (bm - pc)) // bm)
    n_rows = n_blocks * bm
    piece_row = jnp.arange(n_pieces, dtype=I32) * pc
    piece_e = jnp.minimum(jnp.sum(lend[:, None, :] <= piece_row[None, :, None], axis=-1), n_e - 1)
    pick = piece_e[:, :, None] == jnp.arange(n_e, dtype=I32)[None, None, :]
    take = lambda a: jnp.sum(jnp.where(pick, a[:, None, :], 0), axis=-1)
    gdst = ((take(gstart) + piece_row[None, :] - take(lstart)) // pc).astype(I32)
    gdst3 = jnp.clip(gdst, 0, n_rows // pc - 1).reshape(n_tiles, 1, n_pieces)
    npieces = (lend[:, -1] // pc).astype(I32)
    lstart8 = jnp.zeros((n_tiles, SUBLANES, LANES), I32).at[:, :, :n_e].set(lstart[:, None, :])
    lstart8 = lstart8.reshape(n_tiles * SUBLANES, LANES)
    gap_start = jnp.concatenate([pad_ends - padded + rows_e, pad_ends[-1:]]) // pc
    gap_len = jnp.concatenate([padded - rows_e, n_rows - pad_ends[-1:]]) // pc
    gap_end = jnp.cumsum(gap_len)
    max_fill = n_e * (bm // pc - 1) + (n_rows - t * TOP_K) // pc
    j = jnp.arange(max_fill, dtype=I32)
    gap = jnp.minimum(jnp.sum(gap_end[None, :] <= j[:, None], axis=1), n_e)
    in_gap = gap[:, None] == jnp.arange(n_e + 1, dtype=I32)[None, :]
    shift = jnp.sum(jnp.where(in_gap, (gap_start - gap_end + gap_len)[None, :], 0), axis=1)
    fill = jnp.clip(j + shift, 0, n_rows // pc - 1).astype(I32)
    nfill = gap_end[-1:].astype(I32)
    first_row = jnp.arange(n_blocks, dtype=I32) * bm
    block_e = jnp.minimum(jnp.sum(pad_ends[None, :] <= first_row[:, None], axis=1), n_e - 1).astype(I32)
    n_used = (pad_ends[-1] // bm).astype(I32).reshape(1)

    pos_l, xbuf = _dispatch(npieces, fill, nfill, gdst3, sel, idx_l, lstart8, xn, n_rows)

    ff2 = exp_w1.shape[-1]
    b1 = exp_b1[l].astype(F32).reshape(n_e, 1, ff2)
    b2 = exp_b2[l].astype(F32).reshape(n_e, 1, d)
    ybuf = _experts(block_e, padded, rows_e, n_used, xbuf, exp_w1[l].astype(F32), b1, exp_w2[l].astype(F32), b2)

    out = _combine(npieces, gdst3, pos_l, gate_l, h1, row1(final_w), ybuf)
    return out.reshape(bsz, seq, d)


def kernel(x, positions, attn_norm_w, w_in, diff_lambda_q1, diff_lambda_k1, diff_lambda_q2, diff_lambda_k2, diff_subln_w, rwkv_mu, rwkv_w0, rwkv_w_up, rwkv_a0, rwkv_a_up, rwkv_g_up, rwkv_k_k, rwkv_k_a, rwkv_r_k, rwkv_ln_w, rwkv_ln_b, w_out, ffn_norm_w, router_w, router_b, exp_w1, exp_b1, exp_w2, exp_b2, final_norm_w):
    depth = w_in.shape[0]
    assert depth == 1, "the final norm is fused into the last (only) layer's combine kernel"
    tabs = _rotary_tables(positions)
    return _layer(x, 0, tabs, attn_norm_w, w_in, diff_lambda_q1, diff_lambda_k1, diff_lambda_q2,
                  diff_lambda_k2, diff_subln_w, rwkv_mu, rwkv_w0, rwkv_w_up, rwkv_a0, rwkv_a_up, rwkv_g_up,
                  rwkv_k_k, rwkv_k_a, rwkv_r_k, rwkv_ln_w, rwkv_ln_b, w_out, ffn_norm_w, router_w, router_b,
                  exp_w1, exp_b1, exp_w2, exp_b2, final_norm_w)
```

```python
import functools
import math

import jax
import jax.numpy as jnp
import numpy as np
from jax import lax
from jax.experimental import pallas as pl
from jax.experimental.pallas import tpu as pltpu

F32 = jnp.float32
BF16 = jnp.bfloat16
I32 = jnp.int32
U32 = jnp.uint32

DIFF_HEAD_DIM = 64
DIFF_V_DIM = 128
DIFF_HEADS = 4
DIFF_WIDTH = DIFF_HEADS * DIFF_V_DIM
ROT_DIM = 16
ROPE_THETA = 500000.0
RWKV_HEAD = 64
RWKV_HEADS = 8
RWKV_WIDTH = RWKV_HEAD * RWKV_HEADS
DECAY_LORA = 32
AAA_LORA = 32
GATE_LORA = 96
TOP_K = 4
SWIGLU_LIMIT = 7.0
SWIGLU_ALPHA = 1.702
NORM_EPS = 1e-5
RWKV_GN_EPS = 64e-5

LANES = 128
SUBLANES = 8
VMEM_LIMIT = 56 * 1024 * 1024

ROW_TILE = 512
SUB_TILES = 2
ATTN_TILE = 512
CHUNK = 64
SCAN_TILE = 512
SCAN_SEQS = 2
SCAN_BATCH = 4
GROUP = 4 * RWKV_HEAD
EXPERT_BLOCK = 512
MOE_TILE = 256
CAST_CHUNKS = 8
LORA_PAD = LANES
ZR_COLS = 3 * RWKV_WIDTH + 3 * LORA_PAD


def _cparams(sem):
    return pltpu.CompilerParams(dimension_semantics=sem, vmem_limit_bytes=VMEM_LIMIT)


def _nt(a, b):
    return lax.dot_general(a, b, (((1,), (1,)), ((), ())), preferred_element_type=F32)


def _tn(a, b):
    return lax.dot_general(a, b, (((0,), (0,)), ((), ())), preferred_element_type=F32)


def _dot(a, b):
    return jnp.dot(a, b, preferred_element_type=F32)


def _pack_halves(x):
    n = x.shape[1] // 2
    bits = lax.bitcast_convert_type(x, U32)
    return (bits[:, :n] & jnp.uint32(0xFFFF0000)) | (bits[:, n:] >> 16)


def _unpack_halves(p):
    hi = lax.bitcast_convert_type(p & jnp.uint32(0xFFFF0000), F32)
    lo = lax.bitcast_convert_type(p << 16, F32)
    return jnp.concatenate([hi, lo], axis=1).astype(BF16)


def _split(x):
    hi = x.astype(BF16)
    return hi, (x - hi.astype(F32)).astype(BF16)


def _split_dot(x, w_bf16):
    hi, lo = _split(x)
    return _dot(hi, w_bf16) + _dot(lo, w_bf16)


def _split3_dot(x, w_parts):
    n = w_parts.shape[1] // 2
    hi, lo = _split(x)
    both = _dot(hi, w_parts)
    return both[:, :n] + (both[:, n:] + _dot(lo, w_parts[:, :n]))


def _inproj_kernel(x_ref, nw_ref, wq_ref, wr_ref, c_ref, sa_ref, sb_ref,
                   mu_ref, w0_ref, a0_ref, kk_ref, ka_ref, rk_ref, wup_ref, aup_ref, gup_ref, ones_ref,
                   q_ref, k_ref, v_ref, r_ref, lw_ref, rk_out_ref, rv_ref, kkn_ref, b_ref, g_ref, bonus_ref,
                   prev_ref, *, tiles_per_seq):
    i = pl.program_id(0)

    @pl.when(i % tiles_per_seq == 0)
    def _():
        prev_ref[...] = jnp.zeros_like(prev_ref)

    tm = x_ref.shape[0]
    sub = tm // SUB_TILES
    scale = DIFF_HEAD_DIM ** -0.5
    w = RWKV_WIDTH
    ones = ones_ref[...]
    rows = lax.broadcasted_iota(I32, (sub, 1), 0)
    prev = prev_ref[SUBLANES - 1:SUBLANES, :]
    for part in range(SUB_TILES):
        rs = slice(part * sub, (part + 1) * sub)
        x = x_ref[rs, :]
        ms = jnp.mean(x * x, axis=-1, keepdims=True)
        u = (x * lax.rsqrt(ms + NORM_EPS) * nw_ref[...]).astype(BF16)
        zq = _dot(u, wq_ref[...])
        c = c_ref[rs, :]
        sa = sa_ref[rs, :]
        sb = sb_ref[rs, :]
        for g in range(2 * DIFF_HEADS):
            zg = zq[:, g * LANES:(g + 1) * LANES]
            rot = zg * c + pltpu.roll(zg, LANES - ROT_DIM // 2, 1) * sa + pltpu.roll(zg, ROT_DIM // 2, 1) * sb
            if g < DIFF_HEADS:
                q_ref[rs, g * LANES:(g + 1) * LANES] = (rot * scale).astype(BF16)
            else:
                h = g - DIFF_HEADS
                k_ref[rs, h * LANES:(h + 1) * LANES] = rot.astype(BF16)
        v_ref[rs, :] = zq[:, 2 * DIFF_WIDTH:3 * DIFF_WIDTH].astype(BF16)

        z = _dot(u, wr_ref[...])
        shifted = jnp.where(rows == 0, prev, pltpu.roll(z, 1, 0))
        prev = z[sub - 1:sub, :]
        if part == SUB_TILES - 1:
            prev_ref[...] = z[sub - SUBLANES:sub, :]
        zf = z + mu_ref[...] * (shifted - z)
        r = zf[:, 0:w]
        k = zf[:, w:2 * w]
        v = zf[:, 2 * w:3 * w]
        wd = zf[:, 3 * w:3 * w + LORA_PAD]
        ad = zf[:, 3 * w + LORA_PAD:3 * w + 2 * LORA_PAD]
        gd = zf[:, 3 * w + 2 * LORA_PAD:3 * w + 3 * LORA_PAD]
        pre = w0_ref[...] + _split_dot(jnp.tanh(wd), wup_ref[...])
        neg = -pre
        softplus = jnp.maximum(neg, 0.0) + jnp.log(1.0 + jnp.exp(-jnp.abs(neg)))
        wlog = -softplus - 0.5
        lw_ref[rs, :] = -jnp.exp(wlog)
        a = jax.nn.sigmoid(a0_ref[...] + _split_dot(ad, aup_ref[...]))
        g_ref[rs, :] = _split_dot(jax.nn.sigmoid(gd), gup_ref[...]).astype(g_ref.dtype)
        kk = k * kk_ref[...]
        norm = jnp.sqrt(_dot((kk * kk).astype(BF16), ones))
        kk = kk / jnp.maximum(norm, 1e-12)
        k = k * (1.0 + (a - 1.0) * ka_ref[...])
        r_ref[rs, :] = r.astype(r_ref.dtype)
        rk_out_ref[rs, :] = k.astype(rk_out_ref.dtype)
        rv_ref[rs, :] = v.astype(rv_ref.dtype)
        kkn_ref[rs, :] = kk.astype(kkn_ref.dtype)
        b_ref[rs, :] = (kk * a).astype(b_ref.dtype)
        bonus_ref[rs, :] = (_dot((r * k * rk_ref[...]).astype(BF16), ones) * v).astype(bonus_ref.dtype)


def _inproj(x2, nw, wq, wr, ctab, satab, sbtab, mu_p, w0, a0, k_k, k_a, rk, wup, aup, gup, ones, seq):
    t, d = x2.shape
    tm = min(ROW_TILE, seq)
    w = RWKV_WIDTH
    row = lambda i: (i, 0)
    fixed = lambda i: (0, 0)
    vecw = pl.BlockSpec((1, w), fixed)
    lora = pl.BlockSpec((LORA_PAD, w), fixed)
    attn_out = pl.BlockSpec((tm, DIFF_WIDTH), row)
    feat_out = pl.BlockSpec((tm, w), row)
    feat = lambda dt: jax.ShapeDtypeStruct((t, w), dt)
    return pl.pallas_call(
        functools.partial(_inproj_kernel, tiles_per_seq=seq // tm),
        grid=(t // tm,),
        in_specs=[
            pl.BlockSpec((tm, d), row),
            pl.BlockSpec((1, d), fixed),
            pl.BlockSpec(wq.shape, fixed),
            pl.BlockSpec(wr.shape, fixed),
            pl.BlockSpec((tm, LANES), row),
            pl.BlockSpec((tm, LANES), row),
            pl.BlockSpec((tm, LANES), row),
            pl.BlockSpec((1, ZR_COLS), fixed),
            vecw, vecw, vecw, vecw, vecw, lora, lora, lora,
            pl.BlockSpec((w, w), fixed),
        ],
        out_specs=[attn_out] * 3 + [feat_out] * 8,
        out_shape=[jax.ShapeDtypeStruct((t, DIFF_WIDTH), BF16)] * 3
        + [feat(BF16), feat(F32), feat(BF16), feat(BF16), feat(BF16), feat(BF16), feat(BF16), feat(BF16)],
        scratch_shapes=[pltpu.VMEM((SUBLANES, ZR_COLS), F32)],
        compiler_params=_cparams(("arbitrary",)),
        name="inproj",
    )(x2, nw, wq, wr, ctab, satab, sbtab, mu_p, w0, a0, k_k, k_a, rk, wup, aup, gup, ones)


def _attn_kernel(q_ref, k_ref, v_ref, lq1_ref, lk1_ref, lq2_ref, lk2_ref, sw_ref, o_ref,
                 *, tile, lambda_init):
    lane = lax.broadcasted_iota(I32, (1, LANES), 1)
    first = lane < DIFF_HEAD_DIM
    neg = -1e30
    lam = (jnp.exp(jnp.sum(lq1_ref[...] * lk1_ref[...], axis=-1, keepdims=True))
           - jnp.exp(jnp.sum(lq2_ref[...] * lk2_ref[...], axis=-1, keepdims=True)) + lambda_init)
    r = lax.broadcasted_iota(I32, (tile, tile), 0)
    c = lax.broadcasted_iota(I32, (tile, tile), 1)
    keep = c <= r
    for i in range(q_ref.shape[1] // tile):
        q = q_ref[0, i * tile:(i + 1) * tile, :]
        zero = jnp.zeros_like(q)
        n = (i + 1) * tile
        kb = k_ref[0, 0:n, :]
        vb = v_ref[0, 0:n, :]
        outs = []
        for qm in (jnp.where(first, q, zero), jnp.where(first, zero, q)):
            s = _nt(qm, kb)
            diag = jnp.where(keep, s[:, i * tile:], neg)
            s = diag if i == 0 else jnp.concatenate([s[:, :i * tile], diag], axis=1)
            p = jnp.exp(s - jnp.max(s, axis=-1, keepdims=True))
            outs.append(_dot(p.astype(BF16), vb) / jnp.sum(p, axis=-1, keepdims=True))
        o = outs[0] - lam * outs[1]
        ms = jnp.mean(o * o, axis=-1, keepdims=True)
        o = o * lax.rsqrt(ms + NORM_EPS) * sw_ref[...] * (1.0 - lambda_init)
        o_ref[0, i * tile:(i + 1) * tile, :] = o.astype(o_ref.dtype)


def _attn(q3, k3, v3, lq1, lk1, lq2, lk2, sw, lambda_init):
    b, s, _ = q3.shape
    tile = min(ATTN_TILE, s)
    spec = pl.BlockSpec((1, s, LANES), lambda bi, h: (bi, 0, h))
    vec = lambda n: pl.BlockSpec((1, n), lambda bi, h: (0, 0))
    return pl.pallas_call(
        functools.partial(_attn_kernel, tile=tile, lambda_init=lambda_init),
        grid=(b, DIFF_HEADS),
        in_specs=[spec, spec, spec, vec(DIFF_HEAD_DIM), vec(DIFF_HEAD_DIM), vec(DIFF_HEAD_DIM),
                  vec(DIFF_HEAD_DIM), vec(DIFF_V_DIM)],
        out_specs=spec,
        out_shape=jax.ShapeDtypeStruct((b, s, DIFF_WIDTH), BF16),
        compiler_params=_cparams(("parallel", "parallel")),
        name="attn",
    )(q3, k3, v3, lq1, lk1, lq2, lk2, sw)


def _scan_kernel(r_ref, lw_ref, k_ref, v_ref, kk_ref, b_ref, y_ref, state_ref,
                 wr_s, ut_s, utt_s, arb_s, pv_s, bh_s, vk_s, wt_s, *, n_chunks):
    L = CHUNK
    G = GROUP
    n_groups = RWKV_WIDTH // G
    n_seqs = r_ref.shape[0]

    @pl.when(pl.program_id(1) == 0)
    def _():
        state_ref[...] = jnp.zeros_like(state_ref)

    row = lax.broadcasted_iota(I32, (L, G), 0)
    colr = lax.broadcasted_iota(I32, (L, G), 1) & (L - 1)
    strict = (colr < row).astype(F32)
    incl = (colr <= row).astype(F32)
    eye = (colr == row).astype(F32)
    eye_l = (lax.broadcasted_iota(I32, (L, L), 0) == lax.broadcasted_iota(I32, (L, L), 1)).astype(BF16)
    br = lax.broadcasted_iota(I32, (G, G), 0) >> 6
    bc = lax.broadcasted_iota(I32, (G, G), 1) >> 6
    block = (br == bc).astype(F32)
    block_bf = block.astype(BF16)
    rows1 = lax.broadcasted_iota(I32, (L, 1), 0)

    def stack4(x):
        xb = x.astype(BF16)
        return jnp.concatenate([xb, xb, xb, xb], axis=0) * block_bf

    def cat(a, b):
        return jnp.concatenate([a, b], axis=0).astype(BF16)

    def precompute(it, carry):
        chains = [(bb, cl, g) for bb in range(n_seqs) for cl in range(SCAN_BATCH) for g in range(n_groups)]
        each = lambda f, *lists: [f(*args) for args in zip(*lists)]

        def load(ref):
            out = []
            for bb, cl, g in chains:
                base = pl.multiple_of((it * SCAN_BATCH + cl) * L, L)
                out.append(ref[bb, pl.ds(base, L), g * G:(g + 1) * G].astype(F32))
            return out

        r, lw, k, v, kk, b = load(r_ref), load(lw_ref), load(k_ref), load(v_ref), load(kk_ref), load(b_ref)

        def cumsum(x):
            sh = 1
            while sh < L:
                x = x + jnp.where(rows1 >= sh, pltpu.roll(x, sh, 0), 0.0)
                sh *= 2
            return x

        cs = each(cumsum, lw)
        tot = each(lambda c: c[L - 1:L, :], cs)
        a_hat = each(lambda kk_, c, l: -kk_ * jnp.exp(c - l), kk, cs, lw)
        r_hat = each(lambda r_, c: r_ * jnp.exp(c), r, cs)
        w_inv = each(lambda c: jnp.exp(-c), cs)
        w_end = each(lambda t_, c: jnp.exp(t_ - c), tot, cs)
        lhs = each(cat, a_hat, r_hat)
        ab = each(lambda l_, b_, wi: _nt(l_, stack4(b_ * wi)), lhs, b, w_inv)
        ak = each(lambda l_, k_, wi: _nt(l_, stack4(k_ * wi)), lhs, k, w_inv)
        a_ab = each(lambda x: x[:L] * strict, ab)
        a_rb = each(lambda x: (x[L:] * incl).astype(BF16), ab)
        a_k = each(lambda x: cat(x[:L] * strict, x[L:] * incl), ak)
        t_mat = each(lambda a: eye + a, a_ab)
        p_mat = each(lambda a: _dot(a.astype(BF16), stack4(a)), a_ab)
        for _ in range(4):
            tp = each(lambda t_, p_: _dot(cat(t_, p_), stack4(p_)), t_mat, p_mat)
            t_mat = each(lambda t_, x: t_ + x[:L], t_mat, tp)
            p_mat = each(lambda x: x[L:], tp)
        t_bf = each(lambda t_, p_: (t_ + _dot(t_.astype(BF16), stack4(p_))).astype(BF16), t_mat, p_mat)
        av = each(lambda a, v_: _dot(a, stack4(v_)), a_k, v)
        w_til = each(lambda t_, a: _dot(t_, stack4(a)), t_bf, a_hat)
        u_til = each(lambda t_, x: _dot(t_, stack4(x[:L])), t_bf, av)
        u_til_t = each(lambda u_: _tn(u_.astype(BF16), eye_l), u_til)
        vk = each(lambda v_, k_, we: _tn(v_.astype(BF16), (k_ * we).astype(BF16)) * block, v, k, w_end)
        for n, (bb, cl, g) in enumerate(chains):
            slot = ((it * SCAN_BATCH + cl) * n_seqs + bb) * n_groups + g
            wr_s[slot] = cat(w_til[n], r_hat[n])
            ut_s[slot] = u_til[n]
            utt_s[slot] = u_til_t[n]
            arb_s[slot] = a_rb[n]
            pv_s[slot] = av[n][L:]
            bh_s[slot] = (b[n] * w_end[n]).astype(BF16)
            vk_s[slot] = vk[n]
            wt_s[slot] = jnp.broadcast_to(jnp.exp(tot[n]), (SUBLANES, G))
        return carry

    lax.fori_loop(0, n_chunks // SCAN_BATCH, precompute, 0)

    def recur(c, carry):
        gs = range(n_seqs * n_groups)
        slots = [c * n_seqs * n_groups + g for g in gs]
        s0 = [state_ref[g] for g in gs]
        s0b = [s.astype(BF16) for s in s0]
        wr = [wr_s[sl] for sl in slots]
        u_t = [_nt(s0b[g], wr[g][:L]) + utt_s[slots[g]] for g in gs]
        ub = [_dot(u_t[g].astype(BF16), bh_s[slots[g]]) for g in gs]
        for g in gs:
            state_ref[g] = s0[g] * wt_s[slots[g]][0:1, :] + ub[g] * block + vk_s[slots[g]]
        uy = [_nt(wr[g], s0b[g]) for g in gs]
        u = [uy[g][:L] + ut_s[slots[g]] for g in gs]
        base = pl.multiple_of(c * L, L)
        for g in gs:
            y = uy[g][L:] + _dot(arb_s[slots[g]], stack4(u[g])) + pv_s[slots[g]]
            lanes = slice((g % n_groups) * G, (g % n_groups + 1) * G)
            y_ref[g // n_groups, pl.ds(base, L), lanes] = y
        return carry

    lax.fori_loop(0, n_chunks, recur, 0, unroll=4)


def _scan(r3, lw3, k3, v3, kk3, b3):
    bsz, s, w = r3.shape
    ts = min(SCAN_TILE, s)
    n_chunks = ts // CHUNK
    nb = SCAN_SEQS if bsz % SCAN_SEQS == 0 else 1
    slots = nb * n_chunks * (w // GROUP)
    L, G = CHUNK, GROUP
    spec = pl.BlockSpec((nb, ts, w), lambda bi, i: (bi, i, 0))
    return pl.pallas_call(
        functools.partial(_scan_kernel, n_chunks=n_chunks),
        grid=(bsz // nb, s // ts),
        in_specs=[spec] * 6,
        out_specs=spec,
        out_shape=jax.ShapeDtypeStruct((bsz, s, w), F32),
        scratch_shapes=[pltpu.VMEM((nb * (w // GROUP), G, G), F32),
                        pltpu.VMEM((slots, 2 * L, G), BF16),
                        pltpu.VMEM((slots, L, G), F32),
                        pltpu.VMEM((slots, G, L), F32),
                        pltpu.VMEM((slots, L, G), BF16),
                        pltpu.VMEM((slots, L, G), F32),
                        pltpu.VMEM((slots, L, G), BF16),
                        pltpu.VMEM((slots, G, G), F32),
                        pltpu.VMEM((slots, SUBLANES, G), F32)],
        compiler_params=_cparams(("parallel", "arbitrary")),
        name="rwkv_scan",
    )(r3, lw3, k3, v3, kk3, b3)


def _mix_kernel(od_ref, y_ref, g_ref, bonus_ref, x_ref, lnw_ref, lnb_ref, ones_ref, wo_ref,
                fw_ref, rw_ref, rb_ref,
                h1_ref, xn_ref, sel_ref, idx_ref, gate_ref, cnt_ref):
    ones = ones_ref[...]
    y = y_ref[...]
    inv_n = 1.0 / RWKV_HEAD
    mean = _split_dot(y, ones) * inv_n
    d = y - mean
    var = _dot((d * d).astype(BF16), ones) * inv_n
    yn = d * lax.rsqrt(var + RWKV_GN_EPS) * lnw_ref[...] + lnb_ref[...]
    orw = ((yn + bonus_ref[...]) * g_ref[...]).astype(BF16)
    h1 = (x_ref[...] + _dot(od_ref[...], wo_ref[0:DIFF_WIDTH, :])
          + _dot(orw, wo_ref[DIFF_WIDTH:DIFF_WIDTH + RWKV_WIDTH, :]))
    h1_ref[...] = h1
    ms = jnp.mean(h1 * h1, axis=-1, keepdims=True)
    xn = h1 * lax.rsqrt(ms + NORM_EPS) * fw_ref[...]
    xn_ref[...] = xn.astype(xn_ref.dtype)
    logits = _split3_dot(xn, rw_ref[...]) + rb_ref[...]
    tm = logits.shape[0]
    lane = lax.broadcasted_iota(I32, (tm, LANES), 1).astype(F32)
    work = logits
    sel = jnp.zeros((tm, LANES), F32)
    idx_l = jnp.zeros((tm, LANES), F32)
    val_l = jnp.zeros((tm, LANES), F32)
    top = None
    for kslot in range(TOP_K):
        m = jnp.max(work, axis=-1, keepdims=True)
        pick = jnp.min(jnp.where(work == m, lane, float(LANES)), axis=-1, keepdims=True)
        hit = lane == pick
        sel = jnp.where(hit, 1.0, sel)
        idx_l = jnp.where(lane == kslot, pick, idx_l)
        if top is None:
            top = m
        val_l = jnp.where(lane == kslot, jnp.exp(m - top), val_l)
        work = jnp.where(hit, -jnp.inf, work)
    sel_ref[...] = sel
    idx_ref[...] = idx_l.astype(I32)
    gate_ref[...] = val_l / jnp.sum(val_l, axis=-1, keepdims=True)
    for part in range(tm // MOE_TILE):
        count = jnp.sum(sel[part * MOE_TILE:(part + 1) * MOE_TILE], axis=0, keepdims=True)
        cnt_ref[part * SUBLANES:(part + 1) * SUBLANES, :] = jnp.broadcast_to(count, (SUBLANES, LANES)).astype(I32)


def _mix(od, y, g, bonus, x2, lnw, lnb, ones, wo, fw, rw, rb):
    t, d = x2.shape
    tm = min(ROW_TILE, t)
    w = RWKV_WIDTH
    row = lambda i: (i, 0)
    fixed = lambda i: (0, 0)
    rs = lambda n: pl.BlockSpec((tm, n), row)
    return pl.pallas_call(
        _mix_kernel,
        grid=(t // tm,),
        in_specs=[rs(DIFF_WIDTH), rs(w), rs(w), rs(w), rs(d),
                  pl.BlockSpec((1, w), fixed), pl.BlockSpec((1, w), fixed),
                  pl.BlockSpec((w, w), fixed), pl.BlockSpec(wo.shape, fixed),
                  pl.BlockSpec((1, d), fixed), pl.BlockSpec(rw.shape, fixed),
                  pl.BlockSpec((1, LANES), fixed)],
        out_specs=[rs(d), rs(d), rs(LANES), rs(LANES), rs(LANES),
                   pl.BlockSpec((tm // MOE_TILE * SUBLANES, LANES), row)],
        out_shape=[jax.ShapeDtypeStruct((t, d), F32), jax.ShapeDtypeStruct((t, d), BF16),
                   jax.ShapeDtypeStruct((t, LANES), F32), jax.ShapeDtypeStruct((t, LANES), I32),
                   jax.ShapeDtypeStruct((t, LANES), F32),
                   jax.ShapeDtypeStruct((t // MOE_TILE * SUBLANES, LANES), I32)],
        compiler_params=_cparams(("parallel",)),
        name="mix_router",
    )(od, y, g, bonus, x2, lnw, lnb, ones, wo, fw, rw, rb)


def _slot_positions(sel, idx_l, lstart):
    tt = sel.shape[0]
    r = lax.broadcasted_iota(I32, (tt, tt), 0)
    c = lax.broadcasted_iota(I32, (tt, tt), 1)
    lower = (c < r).astype(BF16)
    where_to = _dot(lower, sel.astype(BF16)) + lstart
    lane = lax.broadcasted_iota(I32, (tt, LANES), 1).astype(F32)
    idx = idx_l.astype(F32)
    pos = jnp.full((tt, LANES), -1.0, F32)
    for kslot in range(TOP_K):
        e = jnp.sum(jnp.where(lane == kslot, idx, 0.0), axis=-1, keepdims=True)
        p = jnp.sum(jnp.where(lane == e, where_to, 0.0), axis=-1, keepdims=True)
        pos = jnp.where(lane == kslot, p, pos)
    return pos


def _piece(ref, q):
    return ref.at[pl.ds(pl.multiple_of(q * SUBLANES, SUBLANES), SUBLANES), :]


def _drain(count, src_ref, dst_ref, sem, max_pieces):
    b = 0
    while (1 << b) <= max_pieces:
        rows = SUBLANES << b

        @pl.when((count >> b) & 1 == 1)
        def _():
            pltpu.make_async_copy(src_ref.at[pl.ds(0, rows), :], dst_ref.at[pl.ds(0, rows), :], sem).wait()

        b += 1


def _dispatch_kernel(np_ref, fill_ref, nfill_ref, gdst_ref, sel_ref, idx_ref, lstart_ref, xn_ref,
                     pos_ref, buf_ref, xs_ref, zero_ref, sem, fill_sem, *, n_slots):
    i = pl.program_id(0)
    last = pl.num_programs(0) - 1
    slot = i % 2

    def copy(q, sl):
        return pltpu.make_async_copy(_piece(xs_ref.at[sl], q), _piece(buf_ref, gdst_ref[0, 0, q]), sem.at[sl])

    def drain(count, sl):
        _drain(count, xs_ref.at[sl], buf_ref, sem.at[sl], n_slots // SUBLANES)

    @pl.when(i >= 2)
    def _():
        drain(np_ref[jnp.maximum(i - 2, 0)], slot)

    pos = _slot_positions(sel_ref[...], idx_ref[...], lstart_ref[0:1, :].astype(F32))
    pos_ref[...] = pos
    tt = pos.shape[0]
    pos_t = pos.T.astype(I32)
    s_iota = lax.broadcasted_iota(I32, (n_slots, tt), 0)
    perm = jnp.zeros((n_slots, tt), F32)
    for kslot in range(TOP_K):
        perm = perm + jnp.where(s_iota == pos_t[kslot:kslot + 1, :], 1.0, 0.0)
    xs_ref[slot] = _pack_halves(_dot(perm.astype(BF16), xn_ref[...]))
    lax.fori_loop(0, np_ref[i], lambda q, c: (copy(q, slot).start(), c)[1], 0)

    @pl.when(i == 0)
    def _():
        zero_ref[...] = jnp.zeros_like(zero_ref)

        def fill(j):
            return pltpu.make_async_copy(zero_ref, _piece(buf_ref, fill_ref[j]), fill_sem)

        lax.fori_loop(0, nfill_ref[0], lambda j, c: (fill(j).start(), c)[1], 0)
        lax.fori_loop(0, nfill_ref[0], lambda j, c: (fill(j).wait(), c)[1], 0)

    @pl.when(i == last)
    def _():
        @pl.when(i >= 1)
        def _():
            drain(np_ref[jnp.maximum(i - 1, 0)], 1 - slot)

        drain(np_ref[i], slot)


def _dispatch(npieces, fill, nfill, gdst3, sel, idx_l, lstart8, xn, n_rows):
    t, d = xn.shape
    n_tiles, _, n_pieces = gdst3.shape
    tt = t // n_tiles
    n_slots = n_pieces * SUBLANES
    row = lambda i, *_: (i, 0)
    grid_spec = pltpu.PrefetchScalarGridSpec(
        num_scalar_prefetch=3,
        grid=(n_tiles,),
        in_specs=[pl.BlockSpec((1, 1, n_pieces), lambda i, *_: (i, 0, 0), memory_space=pltpu.SMEM),
                  pl.BlockSpec((tt, LANES), row),
                  pl.BlockSpec((tt, LANES), row),
                  pl.BlockSpec((SUBLANES, LANES), row),
                  pl.BlockSpec((tt, d), row)],
        out_specs=[pl.BlockSpec((tt, LANES), row), pl.BlockSpec(memory_space=pl.ANY)],
        scratch_shapes=[pltpu.VMEM((2, n_slots, d // 2), U32), pltpu.VMEM((SUBLANES, d // 2), U32),
                        pltpu.SemaphoreType.DMA((2,)), pltpu.SemaphoreType.DMA(())],
    )
    return pl.pallas_call(
        functools.partial(_dispatch_kernel, n_slots=n_slots),
        grid_spec=grid_spec,
        out_shape=[jax.ShapeDtypeStruct((t, LANES), F32), jax.ShapeDtypeStruct((n_rows, d // 2), U32)],
        compiler_params=_cparams(("arbitrary",)),
        name="moe_dispatch",
    )(npieces, fill, nfill, gdst3, sel, idx_l, lstart8, xn)


def _expert_kernel(be_ref, first_ref, ord_ref, next_ref, valid_ref, nused_ref, x_ref, w1_hbm, b1_ref, w2_hbm, b2_ref,
                   y_ref, w1f_ref, w2f_ref, w1b_ref, w2i_ref, w2b_ref, sem):
    i = pl.program_id(0)
    ff = w2f_ref.shape[1]
    used = i < nused_ref[0]
    new_expert = jnp.logical_and(used, first_ref[i] == 1)

    def fetch(expert, slot):
        return (pltpu.make_async_copy(w1_hbm.at[expert], w1f_ref.at[slot], sem.at[0, slot]),
                pltpu.make_async_copy(w2_hbm.at[expert], w2f_ref.at[slot], sem.at[1, slot]))

    @pl.when(i == 0)
    def _():
        for cp in fetch(be_ref[0], 0):
            cp.start()

    @pl.when(jnp.logical_and(new_expert, next_ref[i] >= 0))
    def _():
        for cp in fetch(next_ref[i], 1 - (ord_ref[i] & 1)):
            cp.start()

    @pl.when(new_expert)
    def _():
        slot = ord_ref[i] & 1
        for cp in fetch(be_ref[i], slot):
            cp.wait()
        rows = w1f_ref.shape[1] // CAST_CHUNKS

        def cast1(c, carry):
            r0 = pl.multiple_of(c * rows, rows)
            w1b_ref[pl.ds(r0, rows), :] = w1f_ref[slot, pl.ds(r0, rows), :].astype(BF16)
            return carry

        lax.fori_loop(0, CAST_CHUNKS, cast1, 0)
        for g in range(w2f_ref.shape[2] // LANES):
            cols = slice(g * LANES, (g + 1) * LANES)
            w2i_ref[pl.ds(0, ff // 2, stride=2), :] = w2f_ref[slot, 0:ff // 2, cols]
            w2i_ref[pl.ds(1, ff // 2, stride=2), :] = w2f_ref[slot, ff // 2:ff, cols]
            w2b_ref[:, cols] = w2i_ref[...].astype(BF16)

    def mlp(rs):
        even = (lax.broadcasted_iota(I32, (1, LANES), 1) & 1) == 0
        x = _unpack_halves(x_ref[rs, :])
        hid = _dot(x, w1b_ref[...]) + b1_ref[0]

        def act_even(g):
            hg = hid[:, g * LANES:(g + 1) * LANES]
            glu = jnp.minimum(hg, SWIGLU_LIMIT)
            lin = jnp.clip(hg, -SWIGLU_LIMIT, SWIGLU_LIMIT) + 1.0
            return glu * jax.nn.sigmoid(SWIGLU_ALPHA * glu) * pltpu.roll(lin, LANES - 1, 1)

        half = ff // LANES
        act = jnp.concatenate(
            [jnp.where(even, act_even(g), pltpu.roll(act_even(g + half), 1, 1)) for g in range(half)], axis=1)
        y = _dot(act.astype(BF16), w2b_ref[...]) + b2_ref[0]
        y_ref[rs, :] = _pack_halves(y.astype(BF16).astype(F32))

    bm = x_ref.shape[0]
    half_rows = bm // 2
    wide = jnp.logical_and(used, valid_ref[i] > half_rows)
    narrow = jnp.logical_and(used, valid_ref[i] <= half_rows)

    @pl.when(wide)
    def _():
        mlp(slice(0, bm))

    @pl.when(narrow)
    def _():
        mlp(slice(0, half_rows))
        y_ref[half_rows:bm, :] = jnp.zeros((bm - half_rows, y_ref.shape[1]), y_ref.dtype)

    @pl.when(jnp.logical_not(used))
    def _():
        y_ref[...] = jnp.zeros_like(y_ref)


def _experts(block_e, rows_per_expert, real_rows, n_used, xbuf, w1, b1, w2, b2):
    n_rows = xbuf.shape[0]
    bm = EXPERT_BLOCK
    n_blocks = n_rows // bm
    e, d, ff2 = w1.shape
    ff = ff2 // 2
    has = rows_per_expert > 0
    ids = jnp.arange(e, dtype=I32)
    later = (ids[None, :] > ids[:, None]) & has[None, :]
    next_used = jnp.where(jnp.any(later, axis=1), jnp.argmax(later, axis=1), -1).astype(I32)
    ordinal = (jnp.cumsum(has.astype(I32)) - 1).astype(I32)
    first = jnp.concatenate([jnp.ones((1,), I32), (block_e[1:] != block_e[:-1]).astype(I32)])
    of_block = block_e[:, None] == ids[None, :]
    order = jnp.sum(jnp.where(of_block, ordinal[None, :], 0), axis=1).astype(I32)
    next_e = jnp.sum(jnp.where(of_block, next_used[None, :], 0), axis=1).astype(I32)
    data_end = jnp.cumsum(rows_per_expert) - rows_per_expert + real_rows
    block_end = jnp.sum(jnp.where(of_block, data_end[None, :], 0), axis=1)
    valid = jnp.clip(block_end - jnp.arange(n_blocks, dtype=I32) * bm, 0, bm).astype(I32)
    blk = lambda i, be, fi, od, ne, va, nu: (jnp.maximum(jnp.minimum(i, nu[0] - 1), 0), 0)
    ex3 = lambda i, be, fi, od, ne, va, nu: (be[i], 0, 0)
    grid_spec = pltpu.PrefetchScalarGridSpec(
        num_scalar_prefetch=6,
        grid=(n_blocks,),
        in_specs=[pl.BlockSpec((bm, d // 2), blk),
                  pl.BlockSpec(memory_space=pl.ANY),
                  pl.BlockSpec((1, 1, ff2), ex3),
                  pl.BlockSpec(memory_space=pl.ANY),
                  pl.BlockSpec((1, 1, d), ex3)],
        out_specs=pl.BlockSpec((bm, d // 2), lambda i, be, fi, od, ne, va, nu: (i, 0)),
        scratch_shapes=[pltpu.VMEM((2, d, ff2), F32), pltpu.VMEM((2, ff, d), F32),
                        pltpu.VMEM((d, ff2), BF16), pltpu.VMEM((ff, LANES), F32), pltpu.VMEM((ff, d), BF16),
                        pltpu.SemaphoreType.DMA((2, 2))],
    )
    return pl.pallas_call(
        _expert_kernel,
        grid_spec=grid_spec,
        out_shape=jax.ShapeDtypeStruct((n_rows, d // 2), U32),
        compiler_params=_cparams(("arbitrary",)),
        name="moe_experts",
    )(block_e, first, order, next_e, valid, n_used, xbuf, w1, b1, w2, b2)


def _combine_kernel(np_ref, gdst_ref, gnext_ref, pos_ref, gate_ref, h1_ref, fw_ref, ybuf_ref, o_ref, ys_ref, sem,
                    *, n_slots):
    i = pl.program_id(0)
    last = pl.num_programs(0) - 1
    slot = i % 2

    def fetch(table_ref, tile, sl):
        count = np_ref[tile]

        def copy(q):
            return pltpu.make_async_copy(_piece(ybuf_ref, table_ref[0, 0, q]), _piece(ys_ref.at[sl], q), sem.at[sl])

        lax.fori_loop(0, count, lambda q, c: (copy(q).start(), c)[1], 0)

        def zero(q, carry):
            _piece(ys_ref.at[sl], q)[...] = jnp.zeros((SUBLANES, ys_ref.shape[2]), ys_ref.dtype)
            return carry

        lax.fori_loop(count, n_slots // SUBLANES, zero, 0)

    @pl.when(i == 0)
    def _():
        fetch(gdst_ref, i, slot)

    @pl.when(i < last)
    def _():
        fetch(gnext_ref, jnp.minimum(i + 1, last), 1 - slot)

    n = np_ref[i]
    pos = pos_ref[...].astype(I32)
    gate = gate_ref[...]
    tt = pos.shape[0]
    s_iota = lax.broadcasted_iota(I32, (tt, n_slots), 1)
    weight = jnp.zeros((tt, n_slots), F32)
    for kslot in range(TOP_K):
        weight = weight + jnp.where(s_iota == pos[:, kslot:kslot + 1], gate[:, kslot:kslot + 1], 0.0)
    _drain(n, ybuf_ref, ys_ref.at[slot], sem.at[slot], n_slots // SUBLANES)
    h = h1_ref[...] + _dot(weight.astype(BF16), _unpack_halves(ys_ref[slot]))
    ms = jnp.mean(h * h, axis=-1, keepdims=True)
    o_ref[...] = h * lax.rsqrt(ms + NORM_EPS) * fw_ref[...]


def _combine(npieces, gdst3, pos_l, gate_l, h1, fw, ybuf):
    t, d = h1.shape
    n_tiles, _, n_pieces = gdst3.shape
    tt = t // n_tiles
    n_slots = n_pieces * SUBLANES
    row = lambda i, *_: (i, 0)
    grid_spec = pltpu.PrefetchScalarGridSpec(
        num_scalar_prefetch=1,
        grid=(n_tiles,),
        in_specs=[pl.BlockSpec((1, 1, n_pieces), lambda i, *_: (i, 0, 0), memory_space=pltpu.SMEM),
                  pl.BlockSpec((1, 1, n_pieces), lambda i, *_: (jnp.minimum(i + 1, n_tiles - 1), 0, 0),
                               memory_space=pltpu.SMEM),
                  pl.BlockSpec((tt, LANES), row),
                  pl.BlockSpec((tt, LANES), row),
                  pl.BlockSpec((tt, d), row),
                  pl.BlockSpec((1, d), lambda i, *_: (0, 0)),
                  pl.BlockSpec(memory_space=pl.ANY)],
        out_specs=pl.BlockSpec((tt, d), row),
        scratch_shapes=[pltpu.VMEM((2, n_slots, d // 2), U32), pltpu.SemaphoreType.DMA((2,))],
    )
    return pl.pallas_call(
        functools.partial(_combine_kernel, n_slots=n_slots),
        grid_spec=grid_spec,
        out_shape=jax.ShapeDtypeStruct((t, d), F32),
        compiler_params=_cparams(("arbitrary",)),
        name="moe_combine",
    )(npieces, gdst3, gdst3, pos_l, gate_l, h1, fw, ybuf)


def _rotary_tables(positions):
    half = ROT_DIM // 2
    inv_freq = ROPE_THETA ** (-jnp.arange(0, ROT_DIM, 2, dtype=F32) / ROT_DIM)
    ang = positions.astype(F32).reshape(-1, 1) * inv_freq
    cos = jnp.tile(jnp.cos(ang), (1, LANES // half))
    sin = jnp.tile(jnp.sin(ang), (1, LANES // half))
    dim = np.arange(LANES)[None, :] % DIFF_HEAD_DIM
    ctab = jnp.where(dim < ROT_DIM, cos, 1.0)
    satab = jnp.where(dim < half, -sin, 0.0)
    sbtab = jnp.where((dim >= half) & (dim < ROT_DIM), sin, 0.0)
    return ctab, satab, sbtab


def _pad_rows(a, rows):
    return jnp.concatenate([a, jnp.zeros((rows - a.shape[0],) + a.shape[1:], a.dtype)], axis=0)


def _layer(h, l, tabs, attn_norm_w, w_in, diff_lambda_q1, diff_lambda_k1, diff_lambda_q2, diff_lambda_k2,
           diff_subln_w, rwkv_mu, rwkv_w0, rwkv_w_up, rwkv_a0, rwkv_a_up, rwkv_g_up, rwkv_k_k, rwkv_k_a,
           rwkv_r_k, rwkv_ln_w, rwkv_ln_b, w_out, ffn_norm_w, router_w, router_b, exp_w1, exp_b1,
           exp_w2, exp_b2, final_w):
    bsz, seq, d = h.shape
    t = bsz * seq
    w = RWKV_WIDTH
    lambda_init = 0.8 - 0.6 * math.exp(-0.3 * l)
    x2 = h.reshape(t, d)
    row1 = lambda a: a.reshape(1, -1).astype(F32)

    wi = w_in[l]
    qkv_cols = 3 * DIFF_WIDTH
    wq = wi[:, :qkv_cols].astype(BF16)
    o = qkv_cols + 3 * w
    zcol = lambda n: jnp.zeros((d, n), wi.dtype)
    wr = jnp.concatenate([
        wi[:, qkv_cols:o],
        wi[:, o:o + DECAY_LORA], zcol(LORA_PAD - DECAY_LORA),
        wi[:, o + DECAY_LORA:o + DECAY_LORA + AAA_LORA], zcol(LORA_PAD - AAA_LORA),
        wi[:, o + DECAY_LORA + AAA_LORA:], zcol(LORA_PAD - GATE_LORA)], axis=1).astype(BF16)
    mu = rwkv_mu[l]
    zv = lambda n: jnp.zeros((n,), mu.dtype)
    mu_p = jnp.concatenate([
        mu[:3 * w],
        mu[3 * w:3 * w + DECAY_LORA], zv(LORA_PAD - DECAY_LORA),
        mu[3 * w + DECAY_LORA:3 * w + DECAY_LORA + AAA_LORA], zv(LORA_PAD - AAA_LORA),
        mu[3 * w + DECAY_LORA + AAA_LORA:], zv(LORA_PAD - GATE_LORA)]).reshape(1, -1)

    head = np.arange(w) // RWKV_HEAD
    ones = jnp.asarray(head[:, None] == head[None, :], BF16)
    q, k, v, r, lw, kmod, vv, kk, bb, g, bonus = _inproj(
        x2, row1(attn_norm_w[l]), wq, wr, *tabs,
        mu_p, row1(rwkv_w0[l]), row1(rwkv_a0[l]), row1(rwkv_k_k[l]), row1(rwkv_k_a[l]),
        row1(rwkv_r_k[l]), _pad_rows(rwkv_w_up[l].astype(BF16), LORA_PAD),
        _pad_rows(rwkv_a_up[l].astype(BF16), LORA_PAD), _pad_rows(rwkv_g_up[l].astype(BF16), LORA_PAD),
        ones, seq)

    od = _attn(q.reshape(bsz, seq, -1), k.reshape(bsz, seq, -1), v.reshape(bsz, seq, -1),
               row1(diff_lambda_q1[l]), row1(diff_lambda_k1[l]), row1(diff_lambda_q2[l]),
               row1(diff_lambda_k2[l]), row1(diff_subln_w[l]), lambda_init)

    s3 = lambda a: a.reshape(bsz, seq, w)
    y = _scan(s3(r), s3(lw), s3(kmod), s3(vv), s3(kk), s3(bb)).reshape(t, w)

    n_e = router_w.shape[-1]
    rw = jnp.concatenate([router_w[l].astype(F32), jnp.zeros((d, LANES - n_e), F32)], axis=1)
    rw_hi = rw.astype(BF16)
    rw = jnp.concatenate([rw_hi, (rw - rw_hi.astype(F32)).astype(BF16)], axis=1)
    rb = jnp.concatenate([router_b[l].astype(F32), jnp.full((LANES - n_e,), -1e30, F32)]).reshape(1, -1)
    h1, xn, sel, idx_l, gate_l, cnt8 = _mix(
        od.reshape(t, -1), y, g, bonus, x2, row1(rwkv_ln_w[l]), row1(rwkv_ln_b[l]), ones,
        w_out[l].astype(BF16), row1(ffn_norm_w[l]), rw, rb)

    bm = EXPERT_BLOCK
    pc = SUBLANES
    tm = MOE_TILE
    n_tiles = t // tm
    cnt = cnt8.reshape(n_tiles, SUBLANES, LANES)[:, 0, :n_e]
    seg = (cnt + pc - 1) // pc * pc
    lend = jnp.cumsum(seg, axis=1)
    lstart = lend - seg
    rows_e = jnp.sum(seg, axis=0)
    padded = (rows_e + bm - 1) // bm * bm
    pad_ends = jnp.cumsum(padded)
    gstart = (pad_ends - padded)[None, :] + jnp.cumsum(seg, axis=0) - seg
    n_slots = tm * TOP_K + n_e * pc
    n_pieces = n_slots // pc
    n_blocks = -(-(t * TOP_K + n_tiles * n_e * (pc - 1) + n_e * (bm - pc)) // bm)
    n_rows = n_blocks * bm
    piece_row = jnp.arange(n_pieces, dtype=I32) * pc
    piece_e = jnp.minimum(jnp.sum(lend[:, None, :] <= piece_row[None, :, None], axis=-1), n_e - 1)
    pick = piece_e[:, :, None] == jnp.arange(n_e, dtype=I32)[None, None, :]
    take = lambda a: jnp.sum(jnp.where(pick, a[:, None, :], 0), axis=-1)
    gdst = ((take(gstart) + piece_row[None, :] - take(lstart)) // pc).astype(I32)
    gdst3 = jnp.clip(gdst, 0, n_rows // pc - 1).reshape(n_tiles, 1, n_pieces)
    npieces = (lend[:, -1] // pc).astype(I32)
    lstart8 = jnp.zeros((n_tiles, SUBLANES, LANES), I32).at[:, :, :n_e].set(lstart[:, None, :])
    lstart8 = lstart8.reshape(n_tiles * SUBLANES, LANES)
    gap_start = jnp.concatenate([pad_ends - padded + rows_e, pad_ends[-1:]]) // pc
    gap_len = jnp.concatenate([padded - rows_e, n_rows - pad_ends[-1:]]) // pc
    gap_end = jnp.cumsum(gap_len)
    max_fill = n_e * (bm // pc - 1) + (n_rows - t * TOP_K) // pc
    j = jnp.arange(max_fill, dtype=I32)
    gap = jnp.minimum(jnp.sum(gap_end[None, :] <= j[:, None], axis=1), n_e)
    in_gap = gap[:, None] == jnp.arange(n_e + 1, dtype=I32)[None, :]
    shift = jnp.sum(jnp.where(in_gap, (gap_start - gap_end + gap_len)[None, :], 0), axis=1)
    fill = jnp.clip(j + shift, 0, n_rows // pc - 1).astype(I32)
    nfill = gap_end[-1:].astype(I32)
    first_row = jnp.arange(n_blocks, dtype=I32) * bm
    block_e = jnp.minimum(jnp.sum(pad_ends[None, :] <= first_row[:, None], axis=1), n_e - 1).astype(I32)
    n_used = (pad_ends[-1] // bm).astype(I32).reshape(1)

    pos_l, xbuf = _dispatch(npieces, fill, nfill, gdst3, sel, idx_l, lstart8, xn, n_rows)

    ff2 = exp_w1.shape[-1]
    b1 = exp_b1[l].astype(F32).reshape(n_e, 1, ff2)
    b2 = exp_b2[l].astype(F32).reshape(n_e, 1, d)
    ybuf = _experts(block_e, padded, rows_e, n_used, xbuf, exp_w1[l].astype(F32), b1, exp_w2[l].astype(F32), b2)

    out = _combine(npieces, gdst3, pos_l, gate_l, h1, row1(final_w), ybuf)
    return out.reshape(bsz, seq, d)


def kernel(x, positions, attn_norm_w, w_in, diff_lambda_q1, diff_lambda_k1, diff_lambda_q2, diff_lambda_k2, diff_subln_w, rwkv_mu, rwkv_w0, rwkv_w_up, rwkv_a0, rwkv_a_up, rwkv_g_up, rwkv_k_k, rwkv_k_a, rwkv_r_k, rwkv_ln_w, rwkv_ln_b, w_out, ffn_norm_w, router_w, router_b, exp_w1, exp_b1, exp_w2, exp_b2, final_norm_w):
    depth = w_in.shape[0]
    assert depth == 1, "the final norm is fused into the last (only) layer's combine kernel"
    tabs = _rotary_tables(positions)
    return _layer(x, 0, tabs, attn_norm_w, w_in, diff_lambda_q1, diff_lambda_k1, diff_lambda_q2,
                  diff_lambda_k2, diff_subln_w, rwkv_mu, rwkv_w0, rwkv_w_up, rwkv_a0, rwkv_a_up, rwkv_g_up,
                  rwkv_k_k, rwkv_k_a, rwkv_r_k, rwkv_ln_w, rwkv_ln_b, w_out, ffn_norm_w, router_w, router_b,
                  exp_w1, exp_b1, exp_w2, exp_b2, final_norm_w)
```

```python
import functools
import math

import jax
import jax.numpy as jnp
import numpy as np
from jax import lax
from jax.experimental import pallas as pl
from jax.experimental.pallas import tpu as pltpu

F32 = jnp.float32
BF16 = jnp.bfloat16
I32 = jnp.int32
U32 = jnp.uint32

DIFF_HEAD_DIM = 64
DIFF_V_DIM = 128
DIFF_HEADS = 4
DIFF_WIDTH = DIFF_HEADS * DIFF_V_DIM
ROT_DIM = 16
ROPE_THETA = 500000.0
RWKV_HEAD = 64
RWKV_HEADS = 8
RWKV_WIDTH = RWKV_HEAD * RWKV_HEADS
DECAY_LORA = 32
AAA_LORA = 32
GATE_LORA = 96
TOP_K = 4
SWIGLU_LIMIT = 7.0
SWIGLU_ALPHA = 1.702
NORM_EPS = 1e-5
RWKV_GN_EPS = 64e-5

LANES = 128
SUBLANES = 8
VMEM_LIMIT = 56 * 1024 * 1024

ROW_TILE = 512
SUB_TILES = 2
ATTN_TILE = 512
CHUNK = 64
SCAN_TILE = 512
SCAN_SEQS = 2
SCAN_BATCH = 4
GROUP = 4 * RWKV_HEAD
EXPERT_BLOCK = 512
MOE_TILE = 256
CAST_CHUNKS = 8
LORA_PAD = LANES
ZR_COLS = 3 * RWKV_WIDTH + 3 * LORA_PAD


def _cparams(sem):
    return pltpu.CompilerParams(dimension_semantics=sem, vmem_limit_bytes=VMEM_LIMIT)


def _nt(a, b):
    return lax.dot_general(a, b, (((1,), (1,)), ((), ())), preferred_element_type=F32)


def _tn(a, b):
    return lax.dot_general(a, b, (((0,), (0,)), ((), ())), preferred_element_type=F32)


def _dot(a, b):
    return jnp.dot(a, b, preferred_element_type=F32)


def _pack_halves(x):
    n = x.shape[1] // 2
    bits = lax.bitcast_convert_type(x, U32)
    return (bits[:, :n] & jnp.uint32(0xFFFF0000)) | (bits[:, n:] >> 16)


def _unpack_halves(p):
    hi = lax.bitcast_convert_type(p & jnp.uint32(0xFFFF0000), F32)
    lo = lax.bitcast_convert_type(p << 16, F32)
    return jnp.concatenate([hi, lo], axis=1).astype(BF16)


def _split(x):
    hi = x.astype(BF16)
    return hi, (x - hi.astype(F32)).astype(BF16)


def _split_dot(x, w_bf16):
    hi, lo = _split(x)
    return _dot(hi, w_bf16) + _dot(lo, w_bf16)


def _split3_dot(x, w_parts):
    n = w_parts.shape[1] // 2
    hi, lo = _split(x)
    both = _dot(hi, w_parts)
    return both[:, :n] + (both[:, n:] + _dot(lo, w_parts[:, :n]))


def _inproj_kernel(x_ref, nw_ref, wq_ref, wr_ref, c_ref, sa_ref, sb_ref,
                   mu_ref, w0_ref, a0_ref, kk_ref, ka_ref, rk_ref, wup_ref, aup_ref, gup_ref, ones_ref,
                   q_ref, k_ref, v_ref, r_ref, lw_ref, rk_out_ref, rv_ref, kkn_ref, b_ref, g_ref, bonus_ref,
                   prev_ref, *, tiles_per_seq):
    i = pl.program_id(0)

    @pl.when(i % tiles_per_seq == 0)
    def _():
        prev_ref[...] = jnp.zeros_like(prev_ref)

    tm = x_ref.shape[0]
    sub = tm // SUB_TILES
    scale = DIFF_HEAD_DIM ** -0.5
    w = RWKV_WIDTH
    ones = ones_ref[...]
    rows = lax.broadcasted_iota(I32, (sub, 1), 0)
    prev = prev_ref[SUBLANES - 1:SUBLANES, :]
    for part in range(SUB_TILES):
        rs = slice(part * sub, (part + 1) * sub)
        x = x_ref[rs, :]
        ms = jnp.mean(x * x, axis=-1, keepdims=True)
        u = (x * lax.rsqrt(ms + NORM_EPS) * nw_ref[...]).astype(BF16)
        zq = _dot(u, wq_ref[...])
        c = c_ref[rs, :]
        sa = sa_ref[rs, :]
        sb = sb_ref[rs, :]
        for g in range(2 * DIFF_HEADS):
            zg = zq[:, g * LANES:(g + 1) * LANES]
            rot = zg * c + pltpu.roll(zg, LANES - ROT_DIM // 2, 1) * sa + pltpu.roll(zg, ROT_DIM // 2, 1) * sb
            if g < DIFF_HEADS:
                q_ref[rs, g * LANES:(g + 1) * LANES] = (rot * scale).astype(BF16)
            else:
                h = g - DIFF_HEADS
                k_ref[rs, h * LANES:(h + 1) * LANES] = rot.astype(BF16)
        v_ref[rs, :] = zq[:, 2 * DIFF_WIDTH:3 * DIFF_WIDTH].astype(BF16)

        z = _dot(u, wr_ref[...])
        shifted = jnp.where(rows == 0, prev, pltpu.roll(z, 1, 0))
        prev = z[sub - 1:sub, :]
        if part == SUB_TILES - 1:
            prev_ref[...] = z[sub - SUBLANES:sub, :]
        zf = z + mu_ref[...] * (shifted - z)
        r = zf[:, 0:w]
        k = zf[:, w:2 * w]
        v = zf[:, 2 * w:3 * w]
        wd = zf[:, 3 * w:3 * w + LORA_PAD]
        ad = zf[:, 3 * w + LORA_PAD:3 * w + 2 * LORA_PAD]
        gd = zf[:, 3 * w + 2 * LORA_PAD:3 * w + 3 * LORA_PAD]
        pre = w0_ref[...] + _split_dot(jnp.tanh(wd), wup_ref[...])
        neg = -pre
        softplus = jnp.maximum(neg, 0.0) + jnp.log(1.0 + jnp.exp(-jnp.abs(neg)))
        wlog = -softplus - 0.5
        lw_ref[rs, :] = -jnp.exp(wlog)
        a = jax.nn.sigmoid(a0_ref[...] + _split_dot(ad, aup_ref[...]))
        g_ref[rs, :] = _split_dot(jax.nn.sigmoid(gd), gup_ref[...]).astype(g_ref.dtype)
        kk = k * kk_ref[...]
        norm = jnp.sqrt(_dot((kk * kk).astype(BF16), ones))
        kk = kk / jnp.maximum(norm, 1e-12)
        k = k * (1.0 + (a - 1.0) * ka_ref[...])
        r_ref[rs, :] = r.astype(r_ref.dtype)
        rk_out_ref[rs, :] = k.astype(rk_out_ref.dtype)
        rv_ref[rs, :] = v.astype(rv_ref.dtype)
        kkn_ref[rs, :] = kk.astype(kkn_ref.dtype)
        b_ref[rs, :] = (kk * a).astype(b_ref.dtype)
        bonus_ref[rs, :] = (_dot((r * k * rk_ref[...]).astype(BF16), ones) * v).astype(bonus_ref.dtype)


def _inproj(x2, nw, wq, wr, ctab, satab, sbtab, mu_p, w0, a0, k_k, k_a, rk, wup, aup, gup, ones, seq):
    t, d = x2.shape
    tm = min(ROW_TILE, seq)
    w = RWKV_WIDTH
    row = lambda i: (i, 0)
    fixed = lambda i: (0, 0)
    vecw = pl.BlockSpec((1, w), fixed)
    lora = pl.BlockSpec((LORA_PAD, w), fixed)
    attn_out = pl.BlockSpec((tm, DIFF_WIDTH), row)
    feat_out = pl.BlockSpec((tm, w), row)
    feat = lambda dt: jax.ShapeDtypeStruct((t, w), dt)
    return pl.pallas_call(
        functools.partial(_inproj_kernel, tiles_per_seq=seq // tm),
        grid=(t // tm,),
        in_specs=[
            pl.BlockSpec((tm, d), row),
            pl.BlockSpec((1, d), fixed),
            pl.BlockSpec(wq.shape, fixed),
            pl.BlockSpec(wr.shape, fixed),
            pl.BlockSpec((tm, LANES), row),
            pl.BlockSpec((tm, LANES), row),
            pl.BlockSpec((tm, LANES), row),
            pl.BlockSpec((1, ZR_COLS), fixed),
            vecw, vecw, vecw, vecw, vecw, lora, lora, lora,
            pl.BlockSpec((w, w), fixed),
        ],
        out_specs=[attn_out] * 3 + [feat_out] * 8,
        out_shape=[jax.ShapeDtypeStruct((t, DIFF_WIDTH), BF16)] * 3
        + [feat(BF16), feat(F32), feat(BF16), feat(BF16), feat(BF16), feat(BF16), feat(BF16), feat(BF16)],
        scratch_shapes=[pltpu.VMEM((SUBLANES, ZR_COLS), F32)],
        compiler_params=_cparams(("arbitrary",)),
        name="inproj",
    )(x2, nw, wq, wr, ctab, satab, sbtab, mu_p, w0, a0, k_k, k_a, rk, wup, aup, gup, ones)


def _attn_kernel(q_ref, k_ref, v_ref, lq1_ref, lk1_ref, lq2_ref, lk2_ref, sw_ref, o_ref,
                 *, tile, lambda_init):
    lane = lax.broadcasted_iota(I32, (1, LANES), 1)
    first = lane < DIFF_HEAD_DIM
    neg = -1e30
    lam = (jnp.exp(jnp.sum(lq1_ref[...] * lk1_ref[...], axis=-1, keepdims=True))
           - jnp.exp(jnp.sum(lq2_ref[...] * lk2_ref[...], axis=-1, keepdims=True)) + lambda_init)
    r = lax.broadcasted_iota(I32, (tile, tile), 0)
    c = lax.broadcasted_iota(I32, (tile, tile), 1)
    keep = c <= r
    for i in range(q_ref.shape[1] // tile):
        q = q_ref[0, i * tile:(i + 1) * tile, :]
        zero = jnp.zeros_like(q)
        n = (i + 1) * tile
        kb = k_ref[0, 0:n, :]
        vb = v_ref[0, 0:n, :]
        outs = []
        for qm in (jnp.where(first, q, zero), jnp.where(first, zero, q)):
            s = _nt(qm, kb)
            diag = jnp.where(keep, s[:, i * tile:], neg)
            s = diag if i == 0 else jnp.concatenate([s[:, :i * tile], diag], axis=1)
            p = jnp.exp(s - jnp.max(s, axis=-1, keepdims=True))
            outs.append(_dot(p.astype(BF16), vb) / jnp.sum(p, axis=-1, keepdims=True))
        o = outs[0] - lam * outs[1]
        ms = jnp.mean(o * o, axis=-1, keepdims=True)
        o = o * lax.rsqrt(ms + NORM_EPS) * sw_ref[...] * (1.0 - lambda_init)
        o_ref[0, i * tile:(i + 1) * tile, :] = o.astype(o_ref.dtype)


def _attn(q3, k3, v3, lq1, lk1, lq2, lk2, sw, lambda_init):
    b, s, _ = q3.shape
    tile = min(ATTN_TILE, s)
    spec = pl.BlockSpec((1, s, LANES), lambda bi, h: (bi, 0, h))
    vec = lambda n: pl.BlockSpec((1, n), lambda bi, h: (0, 0))
    return pl.pallas_call(
        functools.partial(_attn_kernel, tile=tile, lambda_init=lambda_init),
        grid=(b, DIFF_HEADS),
        in_specs=[spec, spec, spec, vec(DIFF_HEAD_DIM), vec(DIFF_HEAD_DIM), vec(DIFF_HEAD_DIM),
                  vec(DIFF_HEAD_DIM), vec(DIFF_V_DIM)],
        out_specs=spec,
        out_shape=jax.ShapeDtypeStruct((b, s, DIFF_WIDTH), BF16),
        compiler_params=_cparams(("parallel", "parallel")),
        name="attn",
    )(q3, k3, v3, lq1, lk1, lq2, lk2, sw)


def _scan_kernel(r_ref, lw_ref, k_ref, v_ref, kk_ref, b_ref, y_ref, state_ref,
                 wr_s, ut_s, utt_s, arb_s, pv_s, bh_s, vk_s, wt_s, *, n_chunks):
    L = CHUNK
    G = GROUP
    n_groups = RWKV_WIDTH // G
    n_seqs = r_ref.shape[0]

    @pl.when(pl.program_id(1) == 0)
    def _():
        state_ref[...] = jnp.zeros_like(state_ref)

    row = lax.broadcasted_iota(I32, (L, G), 0)
    colr = lax.broadcasted_iota(I32, (L, G), 1) & (L - 1)
    strict = (colr < row).astype(F32)
    incl = (colr <= row).astype(F32)
    eye = (colr == row).astype(F32)
    eye_l = (lax.broadcasted_iota(I32, (L, L), 0) == lax.broadcasted_iota(I32, (L, L), 1)).astype(BF16)
    br = lax.broadcasted_iota(I32, (G, G), 0) >> 6
    bc = lax.broadcasted_iota(I32, (G, G), 1) >> 6
    block = (br == bc).astype(F32)
    block_bf = block.astype(BF16)
    rows1 = lax.broadcasted_iota(I32, (L, 1), 0)

    def stack4(x):
        xb = x.astype(BF16)
        return jnp.concatenate([xb, xb, xb, xb], axis=0) * block_bf

    def cat(a, b):
        return jnp.concatenate([a, b], axis=0).astype(BF16)

    def precompute(it, carry):
        chains = [(bb, cl, g) for bb in range(n_seqs) for cl in range(SCAN_BATCH) for g in range(n_groups)]
        each = lambda f, *lists: [f(*args) for args in zip(*lists)]

        def load(ref):
            out = []
            for bb, cl, g in chains:
                base = pl.multiple_of((it * SCAN_BATCH + cl) * L, L)
                out.append(ref[bb, pl.ds(base, L), g * G:(g + 1) * G].astype(F32))
            return out

        r, lw, k, v, kk, b = load(r_ref), load(lw_ref), load(k_ref), load(v_ref), load(kk_ref), load(b_ref)

        def cumsum(x):
            sh = 1
            while sh < L:
                x = x + jnp.where(rows1 >= sh, pltpu.roll(x, sh, 0), 0.0)
                sh *= 2
            return x

        cs = each(cumsum, lw)
        tot = each(lambda c: c[L - 1:L, :], cs)
        a_hat = each(lambda kk_, c, l: -kk_ * jnp.exp(c - l), kk, cs, lw)
        r_hat = each(lambda r_, c: r_ * jnp.exp(c), r, cs)
        w_inv = each(lambda c: jnp.exp(-c), cs)
        w_end = each(lambda t_, c: jnp.exp(t_ - c), tot, cs)
        lhs = each(cat, a_hat, r_hat)
        ab = each(lambda l_, b_, wi: _nt(l_, stack4(b_ * wi)), lhs, b, w_inv)
        ak = each(lambda l_, k_, wi: _nt(l_, stack4(k_ * wi)), lhs, k, w_inv)
        a_ab = each(lambda x: x[:L] * strict, ab)
        a_rb = each(lambda x: (x[L:] * incl).astype(BF16), ab)
        a_k = each(lambda x: cat(x[:L] * strict, x[L:] * incl), ak)
        t_mat = each(lambda a: eye + a, a_ab)
        p_mat = each(lambda a: _dot(a.astype(BF16), stack4(a)), a_ab)
        for _ in range(4):
            tp = each(lambda t_, p_: _dot(cat(t_, p_), stack4(p_)), t_mat, p_mat)
            t_mat = each(lambda t_, x: t_ + x[:L], t_mat, tp)
            p_mat = each(lambda x: x[L:], tp)
        t_bf = each(lambda t_, p_: (t_ + _dot(t_.astype(BF16), stack4(p_))).astype(BF16), t_mat, p_mat)
        av = each(lambda a, v_: _dot(a, stack4(v_)), a_k, v)
        w_til = each(lambda t_, a: _dot(t_, stack4(a)), t_bf, a_hat)
        u_til = each(lambda t_, x: _dot(t_, stack4(x[:L])), t_bf, av)
        u_til_t = each(lambda u_: _tn(u_.astype(BF16), eye_l), u_til)
        vk = each(lambda v_, k_, we: _tn(v_.astype(BF16), (k_ * we).astype(BF16)) * block, v, k, w_end)
        for n, (bb, cl, g) in enumerate(chains):
            slot = ((it * SCAN_BATCH + cl) * n_seqs + bb) * n_groups + g
            wr_s[slot] = cat(w_til[n], r_hat[n])
            ut_s[slot] = u_til[n]
            utt_s[slot] = u_til_t[n]
            arb_s[slot] = a_rb[n]
            pv_s[slot] = av[n][L:]
            bh_s[slot] = (b[n] * w_end[n]).astype(BF16)
            vk_s[slot] = vk[n]
            wt_s[slot] = jnp.broadcast_to(jnp.exp(tot[n]), (SUBLANES, G))
        return carry

    lax.fori_loop(0, n_chunks // SCAN_BATCH, precompute, 0)

    def recur(c, carry):
        gs = range(n_seqs * n_groups)
        slots = [c * n_seqs * n_groups + g for g in gs]
        s0 = [state_ref[g] for g in gs]
        s0b = [s.astype(BF16) for s in s0]
        wr = [wr_s[sl] for sl in slots]
        u_t = [_nt(s0b[g], wr[g][:L]) + utt_s[slots[g]] for g in gs]
        ub = [_dot(u_t[g].astype(BF16), bh_s[slots[g]]) for g in gs]
        for g in gs:
            state_ref[g] = s0[g] * wt_s[slots[g]][0:1, :] + ub[g] * block + vk_s[slots[g]]
        uy = [_nt(wr[g], s0b[g]) for g in gs]
        u = [uy[g][:L] + ut_s[slots[g]] for g in gs]
        base = pl.multiple_of(c * L, L)
        for g in gs:
            y = uy[g][L:] + _dot(arb_s[slots[g]], stack4(u[g])) + pv_s[slots[g]]
            lanes = slice((g % n_groups) * G, (g % n_groups + 1) * G)
            y_ref[g // n_groups, pl.ds(base, L), lanes] = y
        return carry

    lax.fori_loop(0, n_chunks, recur, 0, unroll=4)


def _scan(r3, lw3, k3, v3, kk3, b3):
    bsz, s, w = r3.shape
    ts = min(SCAN_TILE, s)
    n_chunks = ts // CHUNK
    nb = SCAN_SEQS if bsz % SCAN_SEQS == 0 else 1
    slots = nb * n_chunks * (w // GROUP)
    L, G = CHUNK, GROUP
    spec = pl.BlockSpec((nb, ts, w), lambda bi, i: (bi, i, 0))
    return pl.pallas_call(
        functools.partial(_scan_kernel, n_chunks=n_chunks),
        grid=(bsz // nb, s // ts),
        in_specs=[spec] * 6,
        out_specs=spec,
        out_shape=jax.ShapeDtypeStruct((bsz, s, w), F32),
        scratch_shapes=[pltpu.VMEM((nb * (w // GROUP), G, G), F32),
                        pltpu.VMEM((slots, 2 * L, G), BF16),
                        pltpu.VMEM((slots, L, G), F32),
                        pltpu.VMEM((slots, G, L), F32),
                        pltpu.VMEM((slots, L, G), BF16),
                        pltpu.VMEM((slots, L, G), F32),
                        pltpu.VMEM((slots, L, G), BF16),
                        pltpu.VMEM((slots, G, G), F32),
                        pltpu.VMEM((slots, SUBLANES, G), F32)],
        compiler_params=_cparams(("parallel", "arbitrary")),
        name="rwkv_scan",
    )(r3, lw3, k3, v3, kk3, b3)


def _mix_kernel(od_ref, y_ref, g_ref, bonus_ref, x_ref, lnw_ref, lnb_ref, ones_ref, wo_ref,
                fw_ref, rw_ref, rb_ref,
                h1_ref, xn_ref, sel_ref, idx_ref, gate_ref, cnt_ref):
    ones = ones_ref[...]
    y = y_ref[...]
    inv_n = 1.0 / RWKV_HEAD
    mean = _split_dot(y, ones) * inv_n
    d = y - mean
    var = _dot((d * d).astype(BF16), ones) * inv_n
    yn = d * lax.rsqrt(var + RWKV_GN_EPS) * lnw_ref[...] + lnb_ref[...]
    orw = ((yn + bonus_ref[...]) * g_ref[...]).astype(BF16)
    h1 = (x_ref[...] + _dot(od_ref[...], wo_ref[0:DIFF_WIDTH, :])
          + _dot(orw, wo_ref[DIFF_WIDTH:DIFF_WIDTH + RWKV_WIDTH, :]))
    h1_ref[...] = h1
    ms = jnp.mean(h1 * h1, axis=-1, keepdims=True)
    xn = h1 * lax.rsqrt(ms + NORM_EPS) * fw_ref[...]
    xn_ref[...] = xn.astype(xn_ref.dtype)
    logits = _split3_dot(xn, rw_ref[...]) + rb_ref[...]
    tm = logits.shape[0]
    lane = lax.broadcasted_iota(I32, (tm, LANES), 1).astype(F32)
    work = logits
    sel = jnp.zeros((tm, LANES), F32)
    idx_l = jnp.zeros((tm, LANES), F32)
    val_l = jnp.zeros((tm, LANES), F32)
    top = None
    for kslot in range(TOP_K):
        m = jnp.max(work, axis=-1, keepdims=True)
        pick = jnp.min(jnp.where(work == m, lane, float(LANES)), axis=-1, keepdims=True)
        hit = lane == pick
        sel = jnp.where(hit, 1.0, sel)
        idx_l = jnp.where(lane == kslot, pick, idx_l)
        if top is None:
            top = m
        val_l = jnp.where(lane == kslot, jnp.exp(m - top), val_l)
        work = jnp.where(hit, -jnp.inf, work)
    sel_ref[...] = sel
    idx_ref[...] = idx_l.astype(I32)
    gate_ref[...] = val_l / jnp.sum(val_l, axis=-1, keepdims=True)
    for part in range(tm // MOE_TILE):
        count = jnp.sum(sel[part * MOE_TILE:(part + 1) * MOE_TILE], axis=0, keepdims=True)
        cnt_ref[part * SUBLANES:(part + 1) * SUBLANES, :] = jnp.broadcast_to(count, (SUBLANES, LANES)).astype(I32)


def _mix(od, y, g, bonus, x2, lnw, lnb, ones, wo, fw, rw, rb):
    t, d = x2.shape
    tm = min(ROW_TILE, t)
    w = RWKV_WIDTH
    row = lambda i: (i, 0)
    fixed = lambda i: (0, 0)
    rs = lambda n: pl.BlockSpec((tm, n), row)
    return pl.pallas_call(
        _mix_kernel,
        grid=(t // tm,),
        in_specs=[rs(DIFF_WIDTH), rs(w), rs(w), rs(w), rs(d),
                  pl.BlockSpec((1, w), fixed), pl.BlockSpec((1, w), fixed),
                  pl.BlockSpec((w, w), fixed), pl.BlockSpec(wo.shape, fixed),
                  pl.BlockSpec((1, d), fixed), pl.BlockSpec(rw.shape, fixed),
                  pl.BlockSpec((1, LANES), fixed)],
        out_specs=[rs(d), rs(d), rs(LANES), rs(LANES), rs(LANES),
                   pl.BlockSpec((tm // MOE_TILE * SUBLANES, LANES), row)],
        out_shape=[jax.ShapeDtypeStruct((t, d), F32), jax.ShapeDtypeStruct((t, d), BF16),
                   jax.ShapeDtypeStruct((t, LANES), F32), jax.ShapeDtypeStruct((t, LANES), I32),
                   jax.ShapeDtypeStruct((t, LANES), F32),
                   jax.ShapeDtypeStruct((t // MOE_TILE * SUBLANES, LANES), I32)],
        compiler_params=_cparams(("parallel",)),
        name="mix_router",
    )(od, y, g, bonus, x2, lnw, lnb, ones, wo, fw, rw, rb)


def _slot_positions(sel, idx_l, lstart):
    tt = sel.shape[0]
    r = lax.broadcasted_iota(I32, (tt, tt), 0)
    c = lax.broadcasted_iota(I32, (tt, tt), 1)
    lower = (c < r).astype(BF16)
    where_to = _dot(lower, sel.astype(BF16)) + lstart
    lane = lax.broadcasted_iota(I32, (tt, LANES), 1).astype(F32)
    idx = idx_l.astype(F32)
    pos = jnp.full((tt, LANES), -1.0, F32)
    for kslot in range(TOP_K):
        e = jnp.sum(jnp.where(lane == kslot, idx, 0.0), axis=-1, keepdims=True)
        p = jnp.sum(jnp.where(lane == e, where_to, 0.0), axis=-1, keepdims=True)
        pos = jnp.where(lane == kslot, p, pos)
    return pos


def _piece(ref, q):
    return ref.at[pl.ds(pl.multiple_of(q * SUBLANES, SUBLANES), SUBLANES), :]


def _start_all(count, make_copy):
    def pair(j, carry):
        make_copy(2 * j).start(priority=0)
        make_copy(2 * j + 1).start(priority=1)
        return carry

    lax.fori_loop(0, count >> 1, pair, 0)

    @pl.when(count & 1 == 1)
    def _():
        make_copy(count - 1).start(priority=0)


def _drain(count, src_ref, dst_ref, sem, max_pieces):
    b = 0
    while (1 << b) <= max_pieces:
        rows = SUBLANES << b

        @pl.when((count >> b) & 1 == 1)
        def _():
            pltpu.make_async_copy(src_ref.at[pl.ds(0, rows), :], dst_ref.at[pl.ds(0, rows), :], sem).wait()

        b += 1


def _dispatch_kernel(np_ref, fill_ref, nfill_ref, gdst_ref, sel_ref, idx_ref, lstart_ref, xn_ref,
                     pos_ref, buf_ref, xs_ref, zero_ref, sem, fill_sem, *, n_slots):
    i = pl.program_id(0)
    last = pl.num_programs(0) - 1
    slot = i % 2

    def copy(q, sl):
        return pltpu.make_async_copy(_piece(xs_ref.at[sl], q), _piece(buf_ref, gdst_ref[0, 0, q]), sem.at[sl])

    def drain(count, sl):
        _drain(count, xs_ref.at[sl], buf_ref, sem.at[sl], n_slots // SUBLANES)

    @pl.when(i >= 2)
    def _():
        drain(np_ref[jnp.maximum(i - 2, 0)], slot)

    pos = _slot_positions(sel_ref[...], idx_ref[...], lstart_ref[0:1, :].astype(F32))
    pos_ref[...] = pos
    tt = pos.shape[0]
    pos_t = pos.T.astype(I32)
    s_iota = lax.broadcasted_iota(I32, (n_slots, tt), 0)
    perm = jnp.zeros((n_slots, tt), F32)
    for kslot in range(TOP_K):
        perm = perm + jnp.where(s_iota == pos_t[kslot:kslot + 1, :], 1.0, 0.0)
    xs_ref[slot] = _pack_halves(_dot(perm.astype(BF16), xn_ref[...]))
    _start_all(np_ref[i], lambda q: copy(q, slot))

    @pl.when(i == 0)
    def _():
        zero_ref[...] = jnp.zeros_like(zero_ref)

        def fill(j):
            return pltpu.make_async_copy(zero_ref, _piece(buf_ref, fill_ref[j]), fill_sem)

        lax.fori_loop(0, nfill_ref[0], lambda j, c: (fill(j).start(), c)[1], 0)
        lax.fori_loop(0, nfill_ref[0], lambda j, c: (fill(j).wait(), c)[1], 0)

    @pl.when(i == last)
    def _():
        @pl.when(i >= 1)
        def _():
            drain(np_ref[jnp.maximum(i - 1, 0)], 1 - slot)

        drain(np_ref[i], slot)


def _dispatch(npieces, fill, nfill, gdst3, sel, idx_l, lstart8, xn, n_rows):
    t, d = xn.shape
    n_tiles, _, n_pieces = gdst3.shape
    tt = t // n_tiles
    n_slots = n_pieces * SUBLANES
    row = lambda i, *_: (i, 0)
    grid_spec = pltpu.PrefetchScalarGridSpec(
        num_scalar_prefetch=3,
        grid=(n_tiles,),
        in_specs=[pl.BlockSpec((1, 1, n_pieces), lambda i, *_: (i, 0, 0), memory_space=pltpu.SMEM),
                  pl.BlockSpec((tt, LANES), row),
                  pl.BlockSpec((tt, LANES), row),
                  pl.BlockSpec((SUBLANES, LANES), row),
                  pl.BlockSpec((tt, d), row)],
        out_specs=[pl.BlockSpec((tt, LANES), row), pl.BlockSpec(memory_space=pl.ANY)],
        scratch_shapes=[pltpu.VMEM((2, n_slots, d // 2), U32), pltpu.VMEM((SUBLANES, d // 2), U32),
                        pltpu.SemaphoreType.DMA((2,)), pltpu.SemaphoreType.DMA(())],
    )
    return pl.pallas_call(
        functools.partial(_dispatch_kernel, n_slots=n_slots),
        grid_spec=grid_spec,
        out_shape=[jax.ShapeDtypeStruct((t, LANES), F32), jax.ShapeDtypeStruct((n_rows, d // 2), U32)],
        compiler_params=_cparams(("arbitrary",)),
        name="moe_dispatch",
    )(npieces, fill, nfill, gdst3, sel, idx_l, lstart8, xn)


def _expert_kernel(be_ref, first_ref, ord_ref, next_ref, valid_ref, nused_ref, x_ref, w1_hbm, b1_ref, w2_hbm, b2_ref,
                   y_ref, w1f_ref, w2f_ref, w1b_ref, w2i_ref, w2b_ref, sem):
    i = pl.program_id(0)
    ff = w2f_ref.shape[1]
    used = i < nused_ref[0]
    new_expert = jnp.logical_and(used, first_ref[i] == 1)

    def fetch(expert, slot):
        return (pltpu.make_async_copy(w1_hbm.at[expert], w1f_ref.at[slot], sem.at[0, slot]),
                pltpu.make_async_copy(w2_hbm.at[expert], w2f_ref.at[slot], sem.at[1, slot]))

    @pl.when(i == 0)
    def _():
        for cp in fetch(be_ref[0], 0):
            cp.start()

    @pl.when(jnp.logical_and(new_expert, next_ref[i] >= 0))
    def _():
        for cp in fetch(next_ref[i], 1 - (ord_ref[i] & 1)):
            cp.start()

    @pl.when(new_expert)
    def _():
        slot = ord_ref[i] & 1
        for cp in fetch(be_ref[i], slot):
            cp.wait()
        rows = w1f_ref.shape[1] // CAST_CHUNKS

        def cast1(c, carry):
            r0 = pl.multiple_of(c * rows, rows)
            w1b_ref[pl.ds(r0, rows), :] = w1f_ref[slot, pl.ds(r0, rows), :].astype(BF16)
            return carry

        lax.fori_loop(0, CAST_CHUNKS, cast1, 0)
        for g in range(w2f_ref.shape[2] // LANES):
            cols = slice(g * LANES, (g + 1) * LANES)
            w2i_ref[pl.ds(0, ff // 2, stride=2), :] = w2f_ref[slot, 0:ff // 2, cols]
            w2i_ref[pl.ds(1, ff // 2, stride=2), :] = w2f_ref[slot, ff // 2:ff, cols]
            w2b_ref[:, cols] = w2i_ref[...].astype(BF16)

    def mlp(rs):
        even = (lax.broadcasted_iota(I32, (1, LANES), 1) & 1) == 0
        x = _unpack_halves(x_ref[rs, :])
        hid = _dot(x, w1b_ref[...]) + b1_ref[0]

        def act_even(g):
            hg = hid[:, g * LANES:(g + 1) * LANES]
            glu = jnp.minimum(hg, SWIGLU_LIMIT)
            lin = jnp.clip(hg, -SWIGLU_LIMIT, SWIGLU_LIMIT) + 1.0
            return glu * jax.nn.sigmoid(SWIGLU_ALPHA * glu) * pltpu.roll(lin, LANES - 1, 1)

        half = ff // LANES
        act = jnp.concatenate(
            [jnp.where(even, act_even(g), pltpu.roll(act_even(g + half), 1, 1)) for g in range(half)], axis=1)
        y = _dot(act.astype(BF16), w2b_ref[...]) + b2_ref[0]
        y_ref[rs, :] = _pack_halves(y.astype(BF16).astype(F32))

    bm = x_ref.shape[0]
    half_rows = bm // 2
    wide = jnp.logical_and(used, valid_ref[i] > half_rows)
    narrow = jnp.logical_and(used, valid_ref[i] <= half_rows)

    @pl.when(wide)
    def _():
        mlp(slice(0, bm))

    @pl.when(narrow)
    def _():
        mlp(slice(0, half_rows))
        y_ref[half_rows:bm, :] = jnp.zeros((bm - half_rows, y_ref.shape[1]), y_ref.dtype)

    @pl.when(jnp.logical_not(used))
    def _():
        y_ref[...] = jnp.zeros_like(y_ref)


def _experts(block_e, rows_per_expert, real_rows, n_used, xbuf, w1, b1, w2, b2):
    n_rows = xbuf.shape[0]
    bm = EXPERT_BLOCK
    n_blocks = n_rows // bm
    e, d, ff2 = w1.shape
    ff = ff2 // 2
    has = rows_per_expert > 0
    ids = jnp.arange(e, dtype=I32)
    later = (ids[None, :] > ids[:, None]) & has[None, :]
    next_used = jnp.where(jnp.any(later, axis=1), jnp.argmax(later, axis=1), -1).astype(I32)
    ordinal = (jnp.cumsum(has.astype(I32)) - 1).astype(I32)
    first = jnp.concatenate([jnp.ones((1,), I32), (block_e[1:] != block_e[:-1]).astype(I32)])
    of_block = block_e[:, None] == ids[None, :]
    order = jnp.sum(jnp.where(of_block, ordinal[None, :], 0), axis=1).astype(I32)
    next_e = jnp.sum(jnp.where(of_block, next_used[None, :], 0), axis=1).astype(I32)
    data_end = jnp.cumsum(rows_per_expert) - rows_per_expert + real_rows
    block_end = jnp.sum(jnp.where(of_block, data_end[None, :], 0), axis=1)
    valid = jnp.clip(block_end - jnp.arange(n_blocks, dtype=I32) * bm, 0, bm).astype(I32)
    blk = lambda i, be, fi, od, ne, va, nu: (jnp.maximum(jnp.minimum(i, nu[0] - 1), 0), 0)
    ex3 = lambda i, be, fi, od, ne, va, nu: (be[i], 0, 0)
    grid_spec = pltpu.PrefetchScalarGridSpec(
        num_scalar_prefetch=6,
        grid=(n_blocks,),
        in_specs=[pl.BlockSpec((bm, d // 2), blk),
                  pl.BlockSpec(memory_space=pl.ANY),
                  pl.BlockSpec((1, 1, ff2), ex3),
                  pl.BlockSpec(memory_space=pl.ANY),
                  pl.BlockSpec((1, 1, d), ex3)],
        out_specs=pl.BlockSpec((bm, d // 2), lambda i, be, fi, od, ne, va, nu: (i, 0)),
        scratch_shapes=[pltpu.VMEM((2, d, ff2), F32), pltpu.VMEM((2, ff, d), F32),
                        pltpu.VMEM((d, ff2), BF16), pltpu.VMEM((ff, LANES), F32), pltpu.VMEM((ff, d), BF16),
                        pltpu.SemaphoreType.DMA((2, 2))],
    )
    return pl.pallas_call(
        _expert_kernel,
        grid_spec=grid_spec,
        out_shape=jax.ShapeDtypeStruct((n_rows, d // 2), U32),
        compiler_params=_cparams(("arbitrary",)),
        name="moe_experts",
    )(block_e, first, order, next_e, valid, n_used, xbuf, w1, b1, w2, b2)


def _combine_kernel(np_ref, gdst_ref, gnext_ref, pos_ref, gate_ref, h1_ref, fw_ref, ybuf_ref, o_ref, ys_ref, sem,
                    *, n_slots):
    i = pl.program_id(0)
    last = pl.num_programs(0) - 1
    slot = i % 2

    def fetch(table_ref, tile, sl):
        count = np_ref[tile]

        def copy(q):
            return pltpu.make_async_copy(_piece(ybuf_ref, table_ref[0, 0, q]), _piece(ys_ref.at[sl], q), sem.at[sl])

        _start_all(count, copy)

        def zero(q, carry):
            _piece(ys_ref.at[sl], q)[...] = jnp.zeros((SUBLANES, ys_ref.shape[2]), ys_ref.dtype)
            return carry

        lax.fori_loop(count, n_slots // SUBLANES, zero, 0)

    @pl.when(i == 0)
    def _():
        fetch(gdst_ref, i, slot)

    @pl.when(i < last)
    def _():
        fetch(gnext_ref, jnp.minimum(i + 1, last), 1 - slot)

    n = np_ref[i]
    pos = pos_ref[...].astype(I32)
    gate = gate_ref[...]
    tt = pos.shape[0]
    s_iota = lax.broadcasted_iota(I32, (tt, n_slots), 1)
    weight = jnp.zeros((tt, n_slots), F32)
    for kslot in range(TOP_K):
        weight = weight + jnp.where(s_iota == pos[:, kslot:kslot + 1], gate[:, kslot:kslot + 1], 0.0)
    _drain(n, ybuf_ref, ys_ref.at[slot], sem.at[slot], n_slots // SUBLANES)
    h = h1_ref[...] + _dot(weight.astype(BF16), _unpack_halves(ys_ref[slot]))
    ms = jnp.mean(h * h, axis=-1, keepdims=True)
    o_ref[...] = h * lax.rsqrt(ms + NORM_EPS) * fw_ref[...]


def _combine(npieces, gdst3, pos_l, gate_l, h1, fw, ybuf):
    t, d = h1.shape
    n_tiles, _, n_pieces = gdst3.shape
    tt = t // n_tiles
    n_slots = n_pieces * SUBLANES
    row = lambda i, *_: (i, 0)
    grid_spec = pltpu.PrefetchScalarGridSpec(
        num_scalar_prefetch=1,
        grid=(n_tiles,),
        in_specs=[pl.BlockSpec((1, 1, n_pieces), lambda i, *_: (i, 0, 0), memory_space=pltpu.SMEM),
                  pl.BlockSpec((1, 1, n_pieces), lambda i, *_: (jnp.minimum(i + 1, n_tiles - 1), 0, 0),
                               memory_space=pltpu.SMEM),
                  pl.BlockSpec((tt, LANES), row),
                  pl.BlockSpec((tt, LANES), row),
                  pl.BlockSpec((tt, d), row),
                  pl.BlockSpec((1, d), lambda i, *_: (0, 0)),
                  pl.BlockSpec(memory_space=pl.ANY)],
        out_specs=pl.BlockSpec((tt, d), row),
        scratch_shapes=[pltpu.VMEM((2, n_slots, d // 2), U32), pltpu.SemaphoreType.DMA((2,))],
    )
    return pl.pallas_call(
        functools.partial(_combine_kernel, n_slots=n_slots),
        grid_spec=grid_spec,
        out_shape=jax.ShapeDtypeStruct((t, d), F32),
        compiler_params=_cparams(("arbitrary",)),
        name="moe_combine",
    )(npieces, gdst3, gdst3, pos_l, gate_l, h1, fw, ybuf)


def _rotary_tables(positions):
    half = ROT_DIM // 2
    inv_freq = ROPE_THETA ** (-jnp.arange(0, ROT_DIM, 2, dtype=F32) / ROT_DIM)
    ang = positions.astype(F32).reshape(-1, 1) * inv_freq
    cos = jnp.tile(jnp.cos(ang), (1, LANES // half))
    sin = jnp.tile(jnp.sin(ang), (1, LANES // half))
    dim = np.arange(LANES)[None, :] % DIFF_HEAD_DIM
    ctab = jnp.where(dim < ROT_DIM, cos, 1.0)
    satab = jnp.where(dim < half, -sin, 0.0)
    sbtab = jnp.where((dim >= half) & (dim < ROT_DIM), sin, 0.0)
    return ctab, satab, sbtab


def _pad_rows(a, rows):
    return jnp.concatenate([a, jnp.zeros((rows - a.shape[0],) + a.shape[1:], a.dtype)], axis=0)


def _layer(h, l, tabs, attn_norm_w, w_in, diff_lambda_q1, diff_lambda_k1, diff_lambda_q2, diff_lambda_k2,
           diff_subln_w, rwkv_mu, rwkv_w0, rwkv_w_up, rwkv_a0, rwkv_a_up, rwkv_g_up, rwkv_k_k, rwkv_k_a,
           rwkv_r_k, rwkv_ln_w, rwkv_ln_b, w_out, ffn_norm_w, router_w, router_b, exp_w1, exp_b1,
           exp_w2, exp_b2, final_w):
    bsz, seq, d = h.shape
    t = bsz * seq
    w = RWKV_WIDTH
    lambda_init = 0.8 - 0.6 * math.exp(-0.3 * l)
    x2 = h.reshape(t, d)
    row1 = lambda a: a.reshape(1, -1).astype(F32)

    wi = w_in[l]
    qkv_cols = 3 * DIFF_WIDTH
    wq = wi[:, :qkv_cols].astype(BF16)
    o = qkv_cols + 3 * w
    zcol = lambda n: jnp.zeros((d, n), wi.dtype)
    wr = jnp.concatenate([
        wi[:, qkv_cols:o],
        wi[:, o:o + DECAY_LORA], zcol(LORA_PAD - DECAY_LORA),
        wi[:, o + DECAY_LORA:o + DECAY_LORA + AAA_LORA], zcol(LORA_PAD - AAA_LORA),
        wi[:, o + DECAY_LORA + AAA_LORA:], zcol(LORA_PAD - GATE_LORA)], axis=1).astype(BF16)
    mu = rwkv_mu[l]
    zv = lambda n: jnp.zeros((n,), mu.dtype)
    mu_p = jnp.concatenate([
        mu[:3 * w],
        mu[3 * w:3 * w + DECAY_LORA], zv(LORA_PAD - DECAY_LORA),
        mu[3 * w + DECAY_LORA:3 * w + DECAY_LORA + AAA_LORA], zv(LORA_PAD - AAA_LORA),
        mu[3 * w + DECAY_LORA + AAA_LORA:], zv(LORA_PAD - GATE_LORA)]).reshape(1, -1)

    head = np.arange(w) // RWKV_HEAD
    ones = jnp.asarray(head[:, None] == head[None, :], BF16)
    q, k, v, r, lw, kmod, vv, kk, bb, g, bonus = _inproj(
        x2, row1(attn_norm_w[l]), wq, wr, *tabs,
        mu_p, row1(rwkv_w0[l]), row1(rwkv_a0[l]), row1(rwkv_k_k[l]), row1(rwkv_k_a[l]),
        row1(rwkv_r_k[l]), _pad_rows(rwkv_w_up[l].astype(BF16), LORA_PAD),
        _pad_rows(rwkv_a_up[l].astype(BF16), LORA_PAD), _pad_rows(rwkv_g_up[l].astype(BF16), LORA_PAD),
        ones, seq)

    od = _attn(q.reshape(bsz, seq, -1), k.reshape(bsz, seq, -1), v.reshape(bsz, seq, -1),
               row1(diff_lambda_q1[l]), row1(diff_lambda_k1[l]), row1(diff_lambda_q2[l]),
               row1(diff_lambda_k2[l]), row1(diff_subln_w[l]), lambda_init)

    s3 = lambda a: a.reshape(bsz, seq, w)
    y = _scan(s3(r), s3(lw), s3(kmod), s3(vv), s3(kk), s3(bb)).reshape(t, w)

    n_e = router_w.shape[-1]
    rw = jnp.concatenate([router_w[l].astype(F32), jnp.zeros((d, LANES - n_e), F32)], axis=1)
    rw_hi = rw.astype(BF16)
    rw = jnp.concatenate([rw_hi, (rw - rw_hi.astype(F32)).astype(BF16)], axis=1)
    rb = jnp.concatenate([router_b[l].astype(F32), jnp.full((LANES - n_e,), -1e30, F32)]).reshape(1, -1)
    h1, xn, sel, idx_l, gate_l, cnt8 = _mix(
        od.reshape(t, -1), y, g, bonus, x2, row1(rwkv_ln_w[l]), row1(rwkv_ln_b[l]), ones,
        w_out[l].astype(BF16), row1(ffn_norm_w[l]), rw, rb)

    bm = EXPERT_BLOCK
    pc = SUBLANES
    tm = MOE_TILE
    n_tiles = t // tm
    cnt = cnt8.reshape(n_tiles, SUBLANES, LANES)[:, 0, :n_e]
    seg = (cnt + pc - 1) // pc * pc
    lend = jnp.cumsum(seg, axis=1)
    lstart = lend - seg
    rows_e = jnp.sum(seg, axis=0)
    padded = (rows_e + bm - 1) // bm * bm
    pad_ends = jnp.cumsum(padded)
    gstart = (pad_ends - padded)[None, :] + jnp.cumsum(seg, axis=0) - seg
    n_slots = tm * TOP_K + n_e * pc
    n_pieces = n_slots // pc
    n_blocks = -(-(t * TOP_K + n_tiles * n_e * (pc - 1) + n_e * (bm - pc)) // bm)
    n_rows = n_blocks * bm
    piece_row = jnp.arange(n_pieces, dtype=I32) * pc
    piece_e = jnp.minimum(jnp.sum(lend[:, None, :] <= piece_row[None, :, None], axis=-1), n_e - 1)
    pick = piece_e[:, :, None] == jnp.arange(n_e, dtype=I32)[None, None, :]
    take = lambda a: jnp.sum(jnp.where(pick, a[:, None, :], 0), axis=-1)
    gdst = ((take(gstart) + piece_row[None, :] - take(lstart)) // pc).astype(I32)
    gdst3 = jnp.clip(gdst, 0, n_rows // pc - 1).reshape(n_tiles, 1, n_pieces)
    npieces = (lend[:, -1] // pc).astype(I32)
    lstart8 = jnp.zeros((n_tiles, SUBLANES, LANES), I32).at[:, :, :n_e].set(lstart[:, None, :])
    lstart8 = lstart8.reshape(n_tiles * SUBLANES, LANES)
    gap_start = jnp.concatenate([pad_ends - padded + rows_e, pad_ends[-1:]]) // pc
    gap_len = jnp.concatenate([padded - rows_e, n_rows - pad_ends[-1:]]) // pc
    gap_end = jnp.cumsum(gap_len)
    max_fill = n_e * (bm // pc - 1) + (n_rows - t * TOP_K) // pc
    j = jnp.arange(max_fill, dtype=I32)
    gap = jnp.minimum(jnp.sum(gap_end[None, :] <= j[:, None], axis=1), n_e)
    in_gap = gap[:, None] == jnp.arange(n_e + 1, dtype=I32)[None, :]
    shift = jnp.sum(jnp.where(in_gap, (gap_start - gap_end + gap_len)[None, :], 0), axis=1)
    fill = jnp.clip(j + shift, 0, n_rows // pc - 1).astype(I32)
    nfill = gap_end[-1:].astype(I32)
    first_row = jnp.arange(n_blocks, dtype=I32) * bm
    block_e = jnp.minimum(jnp.sum(pad_ends[None, :] <= first_row[:, None], axis=1), n_e - 1).astype(I32)
    n_used = (pad_ends[-1] // bm).astype(I32).reshape(1)

    pos_l, xbuf = _dispatch(npieces, fill, nfill, gdst3, sel, idx_l, lstart8, xn, n_rows)

    ff2 = exp_w1.shape[-1]
    b1 = exp_b1[l].astype(F32).reshape(n_e, 1, ff2)
    b2 = exp_b2[l].astype(F32).reshape(n_e, 1, d)
    ybuf = _experts(block_e, padded, rows_e, n_used, xbuf, exp_w1[l].astype(F32), b1, exp_w2[l].astype(F32), b2)

    out = _combine(npieces, gdst3, pos_l, gate_l, h1, row1(final_w), ybuf)
    return out.reshape(bsz, seq, d)


def kernel(x, positions, attn_norm_w, w_in, diff_lambda_q1, diff_lambda_k1, diff_lambda_q2, diff_lambda_k2, diff_subln_w, rwkv_mu, rwkv_w0, rwkv_w_up, rwkv_a0, rwkv_a_up, rwkv_g_up, rwkv_k_k, rwkv_k_a, rwkv_r_k, rwkv_ln_w, rwkv_ln_b, w_out, ffn_norm_w, router_w, router_b, exp_w1, exp_b1, exp_w2, exp_b2, final_norm_w):
    depth = w_in.shape[0]
    assert depth == 1, "the final norm is fused into the last (only) layer's combine kernel"
    tabs = _rotary_tables(positions)
    return _layer(x, 0, tabs, attn_norm_w, w_in, diff_lambda_q1, diff_lambda_k1, diff_lambda_q2,
                  diff_lambda_k2, diff_subln_w, rwkv_mu, rwkv_w0, rwkv_w_up, rwkv_a0, rwkv_a_up, rwkv_g_up,
                  rwkv_k_k, rwkv_k_a, rwkv_r_k, rwkv_ln_w, rwkv_ln_b, w_out, ffn_norm_w, router_w, router_b,
                  exp_w1, exp_b1, exp_w2, exp_b2, final_norm_w)
```
